```python
import jax, jax.numpy as jnp
from jax import lax
import numpy as np

D_MODEL = 1024
BATCH = 8
SEQ = 4096
DEPTH = 1

M_WIDTH = D_MODEL // 2
M_HEADS = 4
M_HEAD_DIM = M_WIDTH // M_HEADS
M_CHUNK = 64
M_CONV = 4
N_WIDTH = D_MODEL - M_WIDTH
N_HEADS = 8
N_KV_GROUPS = 2
N_HEAD_DIM = N_WIDTH // N_HEADS
KV_WIDTH = N_KV_GROUPS * N_HEAD_DIM
CMP_BLOCK = 32
CMP_STRIDE = 16
CMP_HIDDEN = 2 * N_HEAD_DIM
SLC_BLOCK = 64
SLC_TOPK = 16
WINDOW = 512
NSA_Q_BLOCK = 32
ROPE_THETA = 10000.0
MOE_GROUPS = 4
EXPERTS_PER_GROUP = 8
N_EXPERTS = MOE_GROUPS * EXPERTS_PER_GROUP
TOP_K_IN_GROUP = 2
EXPERT_FF = D_MODEL // 2
MOE_ROW_BLOCK = 128
RMS_EPS = 1e-6
IN_WIDTHS = (M_WIDTH, M_WIDTH, M_WIDTH, M_WIDTH, M_HEADS, M_HEADS, N_WIDTH, KV_WIDTH, KV_WIDTH, KV_WIDTH, KV_WIDTH, KV_WIDTH, KV_WIDTH, 3 * N_HEADS)
IN_WIDTH = sum(IN_WIDTHS)

kernel_name = 'hybrid_mlstm_nsa_hmoe_block'


def rms_norm(x, w):
    xf = x.astype(jnp.float32)
    y = xf * lax.rsqrt(jnp.mean(xf * xf, axis=-1, keepdims=True) + RMS_EPS)
    return (y * w.astype(jnp.float32)).astype(x.dtype)


def split_cols(z, widths):
    outs = []
    off = 0
    for w in widths:
        outs.append(z[..., off:off + w])
        off += w
    return outs


def rope(x, positions):
    half = x.shape[-1] // 2
    inv = ROPE_THETA ** (-jnp.arange(half, dtype=jnp.float32) / half)
    ang = positions.astype(jnp.float32)[..., None] * inv
    cos = jnp.cos(ang)[:, :, None, :].astype(x.dtype)
    sin = jnp.sin(ang)[:, :, None, :].astype(x.dtype)
    x1, x2 = x[..., :half], x[..., half:]
    return jnp.concatenate([x1 * cos - x2 * sin, x2 * cos + x1 * sin], axis=-1)


def causal_conv(x, w):
    K, C = w.shape
    return lax.conv_general_dilated(x, w[:, None, :].astype(x.dtype), window_strides=(1,), padding=[(K - 1, 0)], dimension_numbers=('NWC', 'WIO', 'NWC'), feature_group_count=C)


def mlstm_chunkwise(q, k, v, i_pre, f_pre):
    B, S, NH, DH = q.shape
    L = M_CHUNK
    NC = S // L
    f32 = jnp.float32

    def chunks(t):
        return jnp.moveaxis(t, 2, 1).reshape((B, NH, NC, L) + t.shape[3:])

    q = chunks(q).astype(f32)
    k = chunks(k).astype(f32) * (DH ** -0.5)
    v = chunks(v).astype(f32)
    ig = chunks(i_pre).astype(f32)
    lf = jax.nn.log_sigmoid(chunks(f_pre).astype(f32))
    b = jnp.cumsum(lf, axis=-1)
    g = b[..., -1]
    a = g[..., None] - b + ig
    m_loc = jnp.max(a, axis=-1)
    w = jnp.exp(a - m_loc[..., None])
    kv_c = jnp.einsum('bhcl,bhcld,bhcle->bhcde', w, k, v)
    n_c = jnp.einsum('bhcl,bhcld->bhcd', w, k)

    def step(carry, xs):
        C, n, m = carry
        kv_j, n_j, m_j, g_j = xs
        m_new = jnp.maximum(g_j + m, m_j)
        a_old = jnp.exp(g_j + m - m_new)
        a_new = jnp.exp(m_j - m_new)
        C_new = a_old[..., None, None] * C + a_new[..., None, None] * kv_j
        n_new = a_old[..., None] * n + a_new[..., None] * n_j
        return (C_new, n_new, m_new), (C, n, m)

    init = (jnp.zeros((B, NH, DH, DH), f32), jnp.zeros((B, NH, DH), f32), jnp.zeros((B, NH), f32))
    xs = (jnp.moveaxis(kv_c, 2, 0), jnp.moveaxis(n_c, 2, 0), jnp.moveaxis(m_loc, 2, 0), jnp.moveaxis(g, 2, 0))
    _, (C_prev, n_prev, m_prev) = lax.scan(step, init, xs)
    C_prev = jnp.moveaxis(C_prev, 0, 2)
    n_prev = jnp.moveaxis(n_prev, 0, 2)
    m_prev = jnp.moveaxis(m_prev, 0, 2)
    causal = jnp.tril(jnp.ones((L, L), dtype=bool))
    dlog = jnp.where(causal, b[..., :, None] - b[..., None, :] + ig[..., None, :], -jnp.inf)
    inter = b + m_prev[..., None]
    m_t = jnp.maximum(inter, jnp.max(dlog, axis=-1))
    wts = jnp.exp(dlog - m_t[..., None]) * jnp.einsum('bhcld,bhcsd->bhcls', q, k)
    dec = jnp.exp(inter - m_t)
    num = jnp.einsum('bhcls,bhcse->bhcle', wts, v) + dec[..., None] * jnp.einsum('bhcld,bhcde->bhcle', q, C_prev)
    den = jnp.sum(wts, axis=-1) + dec * jnp.einsum('bhcld,bhcd->bhcl', q, n_prev)
    h = num / jnp.maximum(jnp.abs(den), jnp.exp(-m_t))[..., None]
    return jnp.moveaxis(h.reshape(B, NH, S, DH), 1, 2)


def mlstm_mixer(q_pre, k_pre, v, o_pre, i_pre, f_pre, conv_w, gate_b, head_norm_w):
    B, S, _ = q_pre.shape
    qk = jax.nn.silu(causal_conv(jnp.concatenate([q_pre, k_pre], axis=-1), conv_w))
    q, k = qk[..., :M_WIDTH], qk[..., M_WIDTH:]
    shp = (B, S, M_HEADS, M_HEAD_DIM)
    h = mlstm_chunkwise(q.reshape(shp), k.reshape(shp), v.reshape(shp), i_pre + gate_b[:M_HEADS], f_pre + gate_b[M_HEADS:])
    h = rms_norm(h, head_norm_w.reshape(M_HEADS, M_HEAD_DIM)).reshape(B, S, M_WIDTH)
    return (jax.nn.sigmoid(o_pre.astype(jnp.float32)) * h).astype(q_pre.dtype)


def masked_softmax(s, mask):
    s = jnp.where(mask, s.astype(jnp.float32), -jnp.inf)
    m = jnp.max(s, axis=-1, keepdims=True)
    m = jnp.where(jnp.isfinite(m), m, 0.0)
    p = jnp.exp(s - m)
    return p / jnp.maximum(jnp.sum(p, axis=-1, keepdims=True), 1e-30)


def compress_kv(t, pe, w1, w2):
    B, S, G, Dh = t.shape
    n_cmp = (S - CMP_BLOCK) // CMP_STRIDE + 1
    idx = jnp.arange(n_cmp)[:, None] * CMP_STRIDE + jnp.arange(CMP_BLOCK)[None, :]
    blk = t[:, idx] + pe[None, None, :, None, :]
    blk = jnp.moveaxis(blk, 3, 2).reshape(B, n_cmp, G, CMP_BLOCK * Dh)
    return jax.nn.silu(blk @ w1) @ w2


def nsa_attention(q, k_cmp, v_cmp, k_slc, v_slc, k_win, v_win, gates):
    B, S, H, Dh = q.shape
    G = k_slc.shape[2]
    HPG = H // G
    QB = NSA_Q_BLOCK
    scale = Dh ** -0.5
    n_cmp = k_cmp.shape[1]
    n_slc = S // SLC_BLOCK
    n_top = min(SLC_TOPK, n_slc)
    cmp_end = jnp.arange(n_cmp) * CMP_STRIDE + CMP_BLOCK - 1
    js = np.arange(n_slc)[:, None] * SLC_BLOCK
    cs = np.arange(n_cmp)[None, :] * CMP_STRIDE
    ov = np.clip(np.minimum(js + SLC_BLOCK, cs + CMP_BLOCK) - np.maximum(js, cs), 0, None) / CMP_STRIDE
    ov = jnp.asarray(ov.astype(np.float32))
    ks_blk = k_slc.reshape(B, n_slc, SLC_BLOCK, G, Dh).transpose(0, 3, 1, 2, 4)
    vs_blk = v_slc.reshape(B, n_slc, SLC_BLOCK, G, Dh).transpose(0, 3, 1, 2, 4)
    kw_pad = jnp.pad(k_win, ((0, 0), (WINDOW, 0), (0, 0), (0, 0)))
    vw_pad = jnp.pad(v_win, ((0, 0), (WINDOW, 0), (0, 0), (0, 0)))
    b_idx = jnp.arange(B)[:, None, None, None]
    g_idx = jnp.arange(G)[None, :, None, None]
    j_ids = jnp.arange(n_slc)
    slc_start = j_ids * SLC_BLOCK
    n_qb = S // QB

    def block(args):
        qb, gb, q0 = args
        t = q0 + jnp.arange(QB)
        qg = qb.reshape(B, QB, G, HPG, Dh)
        s = jnp.einsum('bqghd,bngd->bqghn', qg, k_cmp) * scale
        p_c = masked_softmax(s, (cmp_end[None, :] <= t[:, None])[None, :, None, None, :])
        o_c = jnp.einsum('bqghn,bngd->bqghd', p_c.astype(v_cmp.dtype), v_cmp)
        imp = jnp.einsum('bqghn,jn->bqgj', p_c, ov)
        cur = t // SLC_BLOCK
        forced = (j_ids[None, :] == 0) | (j_ids[None, :] == cur[:, None]) | (j_ids[None, :] == cur[:, None] - 1)
        future = slc_start[None, :] > t[:, None]
        score = jnp.where(forced[None, :, None, :], jnp.inf, jnp.where(future[None, :, None, :], -jnp.inf, imp))
        _, sel = lax.top_k(score, n_top)
        sel = sel.transpose(0, 2, 1, 3)
        kg = ks_blk[b_idx, g_idx, sel].reshape(B, G, QB, n_top * SLC_BLOCK, Dh)
        vg = vs_blk[b_idx, g_idx, sel].reshape(B, G, QB, n_top * SLC_BLOCK, Dh)
        kpos = (sel[..., None] * SLC_BLOCK + jnp.arange(SLC_BLOCK)).reshape(B, G, QB, n_top * SLC_BLOCK)
        m_s = (kpos <= t[None, None, :, None]).transpose(0, 2, 1, 3)[:, :, :, None, :]
        s = jnp.einsum('bqghd,bgqkd->bqghk', qg, kg) * scale
        p_s = masked_softmax(s, m_s)
        o_s = jnp.einsum('bqghk,bgqkd->bqghd', p_s.astype(vg.dtype), vg)
        kw = lax.dynamic_slice_in_dim(kw_pad, q0, WINDOW + QB, axis=1)
        vw = lax.dynamic_slice_in_dim(vw_pad, q0, WINDOW + QB, axis=1)
        wpos = q0 - WINDOW + jnp.arange(WINDOW + QB)
        m_w = (wpos[None, :] <= t[:, None]) & (wpos[None, :] > t[:, None] - WINDOW) & (wpos[None, :] >= 0)
        s = jnp.einsum('bqghd,bkgd->bqghk', qg, kw) * scale
        p_w = masked_softmax(s, m_w[None, :, None, None, :])
        o_w = jnp.einsum('bqghk,bkgd->bqghd', p_w.astype(vw.dtype), vw)
        gb = gb.reshape(B, QB, G, HPG, 3)
        o = gb[..., 0:1] * o_c + gb[..., 1:2] * o_s + gb[..., 2:3] * o_w
        return o.reshape(B, QB, H * Dh)

    qs = q.reshape(B, n_qb, QB, H, Dh).swapaxes(0, 1)
    gs = gates.reshape(B, n_qb, QB, H, 3).swapaxes(0, 1)
    q0s = jnp.arange(n_qb, dtype=jnp.int32) * QB
    out = lax.map(block, (qs, gs, q0s))
    return out.swapaxes(0, 1).reshape(B, S, H * Dh)


def grouped_expert_ffn(hf, eidx, gate, w1, w3, w2):
    T, D = hf.shape
    E = w1.shape[0]
    R = MOE_ROW_BLOCK
    A = eidx.shape[0] * eidx.shape[1]
    e = eidx.reshape(A)
    g = gate.reshape(A)
    tok = jnp.repeat(jnp.arange(T, dtype=jnp.int32), eidx.shape[1])
    order = jnp.argsort(e)
    e_s, g_s, tok_s = e[order], g[order], tok[order]
    counts = jnp.zeros((E,), jnp.int32).at[e].add(1)
    padded = (counts + R - 1) // R * R
    starts = jnp.cumsum(counts) - counts
    pstarts = jnp.cumsum(padded) - padded
    dest = pstarts[e_s] + jnp.arange(A, dtype=jnp.int32) - starts[e_s]
    P = -(-A // R) * R + E * R
    row_tok = jnp.zeros((P,), jnp.int32).at[dest].set(tok_s)
    row_gate = jnp.zeros((P,), hf.dtype).at[dest].set(g_s)
    block_start = jnp.arange(P // R, dtype=jnp.int32) * R
    block_e = jnp.minimum(jnp.sum(block_start[:, None] >= (pstarts + padded)[None, :], axis=1), E - 1)

    def run_block(args):
        toks, gts, ei = args
        xb = hf[toks]
        hb = jax.nn.silu(xb @ w1[ei]) * (xb @ w3[ei])
        return (hb @ w2[ei]) * gts[:, None]

    ys = lax.map(run_block, (row_tok.reshape(-1, R), row_gate.reshape(-1, R), block_e))
    return jax.ops.segment_sum(ys.reshape(P, D), row_tok, num_segments=T)


def hier_moe(h, wg, bg, we, be, w1, w3, w2):
    B, S, D = h.shape
    T = B * S
    hf = h.reshape(T, D)
    p_grp = jax.nn.softmax((hf @ wg + bg).astype(jnp.float32), axis=-1)
    pg_top, grp = lax.top_k(p_grp, 1)
    grp = grp[:, 0]
    le = (hf @ we + be).astype(jnp.float32).reshape(T, MOE_GROUPS, EXPERTS_PER_GROUP)
    p_in = jax.nn.softmax(le[jnp.arange(T), grp], axis=-1)
    pe_top, e_in = lax.top_k(p_in, TOP_K_IN_GROUP)
    gate = pg_top * pe_top / jnp.sum(pe_top, axis=-1, keepdims=True)
    eidx = grp[:, None] * EXPERTS_PER_GROUP + e_in
    y = grouped_expert_ffn(hf, eidx, gate.astype(h.dtype), w1, w3, w2)
    return y.reshape(B, S, D)


def setup_inputs(seed: int = 0) -> dict:
    key = jax.random.key(seed)
    ks = jax.random.split(key, 32)
    f32 = jnp.float32
    L = DEPTH
    D = D_MODEL

    def nrm(k, shape, scale):
        return jax.random.normal(k, shape, f32) * scale

    f_bias = jnp.linspace(3.0, 6.0, M_HEADS, dtype=f32)
    return {
        'x': nrm(ks[0], (BATCH, SEQ, D), 1.0),
        'c': nrm(ks[1], (BATCH, D), 1.0),
        'positions': jnp.arange(SEQ, dtype=jnp.int32)[None, :] + jax.random.randint(ks[2], (BATCH, 1), 0, 1024, dtype=jnp.int32),
        'ada_w': nrm(ks[3], (L, D, 6 * D), 0.5 * D ** -0.5),
        'ada_b': nrm(ks[4], (L, 6 * D), 0.1),
        'norm_mix_pre': 1.0 + nrm(ks[5], (L, D), 0.05),
        'norm_mix_post': 1.0 + nrm(ks[6], (L, D), 0.05),
        'norm_ffn_pre': 1.0 + nrm(ks[7], (L, D), 0.05),
        'norm_ffn_post': 1.0 + nrm(ks[8], (L, D), 0.05),
        'w_in': nrm(ks[9], (L, D, IN_WIDTH), D ** -0.5),
        'mlstm_conv_w': nrm(ks[10], (L, M_CONV, 2 * M_WIDTH), M_CONV ** -0.5),
        'mlstm_gate_b': jnp.concatenate([nrm(ks[11], (L, M_HEADS), 0.1), f_bias[None, :] + nrm(ks[12], (L, M_HEADS), 0.1)], axis=-1),
        'mlstm_head_norm': 1.0 + nrm(ks[13], (L, M_WIDTH), 0.05),
        'cmp_pe_k': nrm(ks[14], (L, CMP_BLOCK, N_HEAD_DIM), 0.1),
        'cmp_w1_k': nrm(ks[15], (L, CMP_BLOCK * N_HEAD_DIM, CMP_HIDDEN), (CMP_BLOCK * N_HEAD_DIM) ** -0.5),
        'cmp_w2_k': nrm(ks[16], (L, CMP_HIDDEN, N_HEAD_DIM), CMP_HIDDEN ** -0.5),
        'cmp_pe_v': nrm(ks[17], (L, CMP_BLOCK, N_HEAD_DIM), 0.1),
        'cmp_w1_v': nrm(ks[18], (L, CMP_BLOCK * N_HEAD_DIM, CMP_HIDDEN), (CMP_BLOCK * N_HEAD_DIM) ** -0.5),
        'cmp_w2_v': nrm(ks[19], (L, CMP_HIDDEN, N_HEAD_DIM), CMP_HIDDEN ** -0.5),
        'w_out': nrm(ks[20], (L, D, D), D ** -0.5),
        'router_grp_w': nrm(ks[21], (L, D, MOE_GROUPS), D ** -0.5),
        'router_grp_b': nrm(ks[22], (L, MOE_GROUPS), 0.01),
        'router_exp_w': nrm(ks[23], (L, D, N_EXPERTS), D ** -0.5),
        'router_exp_b': nrm(ks[24], (L, N_EXPERTS), 0.01),
        'expert_w1': nrm(ks[25], (L, N_EXPERTS, D, EXPERT_FF), D ** -0.5),
        'expert_w3': nrm(ks[26], (L, N_EXPERTS, D, EXPERT_FF), D ** -0.5),
        'expert_w2': nrm(ks[27], (L, N_EXPERTS, EXPERT_FF, D), EXPERT_FF ** -0.5),
    }


def reference(x, c, positions, ada_w, ada_b, norm_mix_pre, norm_mix_post, norm_ffn_pre, norm_ffn_post, w_in, mlstm_conv_w, mlstm_gate_b, mlstm_head_norm, cmp_pe_k, cmp_w1_k, cmp_w2_k, cmp_pe_v, cmp_w1_v, cmp_w2_v, w_out, router_grp_w, router_grp_b, router_exp_w, router_exp_b, expert_w1, expert_w3, expert_w2):
    B, S, D = x.shape
    G, Dh = N_KV_GROUPS, N_HEAD_DIM

    def as_kv(t):
        return t.reshape(B, S, G, Dh)

    for l in range(DEPTH):
        mod = jax.nn.silu(c) @ ada_w[l] + ada_b[l]
        sh_m, sc_m, g_m, sh_f, sc_f, g_f = jnp.split(mod[:, None, :], 6, axis=-1)
        h = rms_norm(x, norm_mix_pre[l]) * (1.0 + sc_m) + sh_m
        (mq, mk, mv, mo, mi, mf, nq, kc, vc, ksl, vsl, kwn, vwn, ng) = split_cols(h @ w_in[l], IN_WIDTHS)
        y_m = mlstm_mixer(mq, mk, mv, mo, mi, mf, mlstm_conv_w[l], mlstm_gate_b[l], mlstm_head_norm[l])
        q = rope(nq.reshape(B, S, N_HEADS, Dh), positions)
        k_c = compress_kv(rope(as_kv(kc), positions), cmp_pe_k[l], cmp_w1_k[l], cmp_w2_k[l])
        v_c = compress_kv(as_kv(vc), cmp_pe_v[l], cmp_w1_v[l], cmp_w2_v[l])
        y_n = nsa_attention(q, k_c, v_c, rope(as_kv(ksl), positions), as_kv(vsl), rope(as_kv(kwn), positions), as_kv(vwn), jax.nn.sigmoid(ng.reshape(B, S, N_HEADS, 3)))
        y = jnp.concatenate([y_m, y_n], axis=-1) @ w_out[l]
        x = x + g_m * rms_norm(y, norm_mix_post[l])
        h = rms_norm(x, norm_ffn_pre[l]) * (1.0 + sc_f) + sh_f
        y = hier_moe(h, router_grp_w[l], router_grp_b[l], router_exp_w[l], router_exp_b[l], expert_w1[l], expert_w3[l], expert_w2[l])
        x = x + g_f * rms_norm(y, norm_ffn_post[l])
    return x
```

```python
import functools

import jax
import jax.numpy as jnp
import numpy as np
from jax import lax
from jax.experimental import pallas as pl
from jax.experimental.pallas import tpu as pltpu

M_HEADS = 4
M_HEAD_DIM = 128
M_WIDTH = M_HEADS * M_HEAD_DIM
M_CONV = 4
N_HEADS = 8
N_KV_GROUPS = 2
HEADS_PER_GROUP = N_HEADS // N_KV_GROUPS
N_HEAD_DIM = 64
N_WIDTH = N_HEADS * N_HEAD_DIM
KV_WIDTH = N_KV_GROUPS * N_HEAD_DIM
CMP_BLOCK = 32
CMP_STRIDE = 16
CMP_HIDDEN = 2 * N_HEAD_DIM
SLC_BLOCK = 64
SLC_TOPK = 16
WINDOW = 512
ROPE_THETA = 10000.0
MOE_GROUPS = 4
EXPERTS_PER_GROUP = 8
N_EXPERTS = MOE_GROUPS * EXPERTS_PER_GROUP
RMS_EPS = 1e-6

LANE = 128
SUBLANE = 8
VMEM_LIMIT_BYTES = 56 * 1024 * 1024

TOKEN_TILE = 512
MLSTM_CHUNK = 256
NSA_Q_TILE = 128
NSA_K_TILE = 512
EXPERT_ROWS = 256
GATHER_TILE = 256

F32 = jnp.float32
BF16 = jnp.bfloat16
NEG = -1e30
HIGHEST = lax.Precision.HIGHEST
NT_DIMS = (((1,), (1,)), ((), ()))
TN_DIMS = (((0,), (0,)), ((), ()))


def _params(n_grid):
    return pltpu.CompilerParams(
        dimension_semantics=("arbitrary",) * n_grid, vmem_limit_bytes=VMEM_LIMIT_BYTES)


def _dot(a, b, **kw):
    return jnp.dot(a, b, preferred_element_type=F32, **kw)


def _dot_nt(a, b, **kw):
    return lax.dot_general(a, b, NT_DIMS, preferred_element_type=F32, **kw)


def _dot_tn(a, b, **kw):
    return lax.dot_general(a, b, TN_DIMS, preferred_element_type=F32, **kw)


def _sigmoid(x):
    return 1.0 / (1.0 + jnp.exp(-x))


def _silu(x):
    return x * _sigmoid(x)


def _rms(x, w):
    return x * lax.rsqrt(jnp.mean(x * x, axis=-1, keepdims=True) + RMS_EPS) * w


def _full(shape):
    return pl.BlockSpec(shape, lambda *_: (0,) * len(shape))


def _adaln_kernel(c_ref, w_ref, b_ref, o_ref):
    o_ref[...] = _dot(_silu(c_ref[...]), w_ref[...], precision=HIGHEST) + b_ref[...]


def _adaln(c, ada_w, ada_b):
    B, D = c.shape
    n = ada_w.shape[1] // D
    return pl.pallas_call(
        _adaln_kernel,
        grid=(n,),
        in_specs=[_full((B, D)), pl.BlockSpec((D, D), lambda j: (0, j)), pl.BlockSpec((1, D), lambda j: (0, j))],
        out_specs=pl.BlockSpec((B, D), lambda j: (0, j)),
        out_shape=jax.ShapeDtypeStruct((B, ada_w.shape[1]), F32),
        compiler_params=_params(1),
        name="adaln",
    )(c, ada_w, ada_b[None, :])


_C_M = 0
_C_Q = 4 * M_WIDTH
_C_KC = _C_Q + N_WIDTH
_C_KS = _C_KC + KV_WIDTH
_C_KW = _C_KS + KV_WIDTH
_C_VC = _C_KW + KV_WIDTH
_C_VS = _C_VC + KV_WIDTH
_C_VW = _C_VS + KV_WIDTH
_C_NG = _C_VW + KV_WIDTH
_C_END = _C_NG + LANE


def _inproj_kernel(x_ref, mod_ref, nw_ref, w_ref, wg_ref, gb_ref, pos_ref, inv_ref,
                   zm_ref, q_ref, kc_ref, ks_ref, kw_ref, vc_ref, vs_ref, vw_ref, ng_ref, gt_ref):
    mod = mod_ref[0]
    h = _rms(x_ref[...], nw_ref[...]) * (1.0 + mod[1:2]) + mod[0:1]
    z = _dot(h.astype(BF16), w_ref[...])
    zm_ref[...] = z[:, _C_M:_C_Q]
    gt_ref[...] = _dot_nt(wg_ref[...], h, precision=HIGHEST) + gb_ref[...]

    ang = pos_ref[...] * inv_ref[...]
    cos = jnp.cos(ang)
    sin = jnp.sin(ang)
    lane = lax.broadcasted_iota(jnp.int32, (1, LANE), 1)
    first = (lane % N_HEAD_DIM) < (N_HEAD_DIM // 2)
    sin_signed = jnp.where(first, -sin, sin)

    def rope(slab):
        rot = jnp.where(first, pltpu.roll(slab, LANE - N_HEAD_DIM // 2, 1), pltpu.roll(slab, N_HEAD_DIM // 2, 1))
        return slab * cos + rot * sin_signed

    scale = N_HEAD_DIM ** -0.5
    for j in range(N_WIDTH // LANE):
        q_ref[:, j * LANE:(j + 1) * LANE] = (rope(z[:, _C_Q + j * LANE:_C_Q + (j + 1) * LANE]) * scale).astype(BF16)
    kc_ref[...] = rope(z[:, _C_KC:_C_KS])
    ks_ref[...] = rope(z[:, _C_KS:_C_KW]).astype(BF16)
    kw_ref[...] = rope(z[:, _C_KW:_C_VC]).astype(BF16)
    vc_ref[...] = z[:, _C_VC:_C_VS]
    vs_ref[...] = z[:, _C_VS:_C_VW].astype(BF16)
    vw_ref[...] = z[:, _C_VW:_C_NG].astype(BF16)
    ng_ref[...] = _sigmoid(z[:, _C_NG:_C_END])


def _inproj(x2, mod3, norm_w, w_main, wg_t, gate_b, pos_col, inv_row, S):
    T, D = x2.shape
    TM = min(TOKEN_TILE, S)
    tiles_per_seq = S // TM
    row = lambda w: pl.BlockSpec((TM, w), lambda i: (i, 0))
    outs = [(4 * M_WIDTH, F32), (N_WIDTH, BF16), (KV_WIDTH, F32), (KV_WIDTH, BF16), (KV_WIDTH, BF16),
            (KV_WIDTH, F32), (KV_WIDTH, BF16), (KV_WIDTH, BF16), (LANE, F32)]
    return pl.pallas_call(
        _inproj_kernel,
        grid=(T // TM,),
        in_specs=[row(D), pl.BlockSpec((1, 6, D), lambda i: (i // tiles_per_seq, 0, 0)), _full((1, D)),
                  _full(w_main.shape), _full(wg_t.shape), _full((2 * M_HEADS, 1)), row(1), _full((1, LANE))],
        out_specs=[row(w) for w, _ in outs] + [pl.BlockSpec((2 * M_HEADS, TM), lambda i: (0, i))],
        out_shape=[jax.ShapeDtypeStruct((T, w), dt) for w, dt in outs] + [jax.ShapeDtypeStruct((2 * M_HEADS, T), F32)],
        compiler_params=_params(1),
        name="inproj",
    )(x2, mod3, norm_w[None, :], w_main, wg_t, gate_b[:, None], pos_col, inv_row)


def _mlstm_kernel(zm_ref, gt_ref, cw_ref, hn_ref, ltri_ref, eye_ref, y_ref, buf, c_s, n_s, m_s):
    L = zm_ref.shape[0]
    QK = 2 * M_WIDTH
    DH = M_HEAD_DIM

    @pl.when(pl.program_id(1) == 0)
    def _():
        buf[0:SUBLANE, :] = jnp.zeros((SUBLANE, QK), F32)
        c_s[...] = jnp.zeros_like(c_s)
        n_s[...] = jnp.zeros_like(n_s)
        m_s[...] = jnp.zeros_like(m_s)

    buf[SUBLANE:SUBLANE + L, :] = zm_ref[:, 0:QK]
    cw = cw_ref[...]
    conv = cw[M_CONV - 1:M_CONV] * buf[SUBLANE:SUBLANE + L, :]
    for j in range(M_CONV - 1):
        off = SUBLANE - (M_CONV - 1) + j
        conv = conv + cw[j:j + 1] * buf[off:off + L, :]
    buf[0:SUBLANE, :] = buf[L:L + SUBLANE, :]
    qk = _silu(conv)

    g = gt_ref[...]
    fp = g[M_HEADS:]
    lf = jnp.minimum(fp, 0.0) - jnp.log(1.0 + jnp.exp(-jnp.abs(fp)))
    lf8 = jnp.concatenate([lf, jnp.zeros_like(lf)], axis=0)
    ltri = ltri_ref[...]
    b_rows = _dot_nt(lf8, ltri, precision=HIGHEST)
    b_cols = _dot_nt(ltri, lf8, precision=HIGHEST)
    i_cols = _dot_nt(eye_ref[...], g, precision=HIGHEST)

    causal = lax.broadcasted_iota(jnp.int32, (L, L), 0) >= lax.broadcasted_iota(jnp.int32, (L, L), 1)
    for h in range(M_HEADS):
        hs = slice(h * DH, (h + 1) * DH)
        b_r, i_r = b_rows[h:h + 1], g[h:h + 1]
        b_c, i_c = b_cols[:, h:h + 1], i_cols[:, h:h + 1]
        m_prev = m_s[h:h + 1, 0:1]
        g_tot = b_r[:, L - 1:L]
        m_loc = jnp.max(g_tot - b_r + i_r, axis=1, keepdims=True)
        m_new = jnp.maximum(g_tot + m_prev, m_loc)

        q = qk[:, hs]
        k = qk[:, M_WIDTH + h * DH:M_WIDTH + (h + 1) * DH] * (DH ** -0.5)
        v = zm_ref[:, 2 * M_WIDTH + h * DH:2 * M_WIDTH + (h + 1) * DH]
        o = zm_ref[:, 3 * M_WIDTH + h * DH:3 * M_WIDTH + (h + 1) * DH]
        qb, kb, vb = q.astype(BF16), k.astype(BF16), v.astype(BF16)

        dlog = jnp.where(causal, b_c - b_r + i_r, -jnp.inf)
        inter = b_c + m_prev
        m_t = jnp.maximum(inter, jnp.max(dlog, axis=1, keepdims=True))
        wts = jnp.exp(dlog - m_t) * _dot_nt(qb, kb)
        dec = jnp.exp(inter - m_t)
        c_prev = c_s[h]
        n_prev = n_s[h:h + 1]
        num = _dot(wts.astype(BF16), vb) + dec * _dot(qb, c_prev.astype(BF16))
        den = jnp.sum(wts, axis=1, keepdims=True) + dec * jnp.sum(q * n_prev, axis=1, keepdims=True)
        hh = num / jnp.maximum(jnp.abs(den), jnp.exp(-m_t))

        kw = k * jnp.exp(g_tot - b_c + i_c - m_new)
        keep = jnp.exp(g_tot + m_prev - m_new)
        c_s[h] = keep * c_prev + _dot_tn(kw.astype(BF16), vb)
        n_s[h:h + 1] = keep * n_prev + jnp.sum(kw, axis=0, keepdims=True)
        m_s[h:h + 1] = jnp.broadcast_to(m_new, (1, LANE))

        hn = _rms(hh, hn_ref[:, hs])
        y_ref[:, hs] = (_sigmoid(o) * hn).astype(BF16)


def _mlstm(zm, gt, conv_w, head_norm, B, S):
    T = zm.shape[0]
    L = min(MLSTM_CHUNK, S)
    nc = S // L
    ltri = jnp.asarray(np.tril(np.ones((L, L), np.float32)))
    eye = jnp.asarray(np.eye(L, dtype=np.float32))
    return pl.pallas_call(
        _mlstm_kernel,
        grid=(B, nc),
        in_specs=[pl.BlockSpec((L, 4 * M_WIDTH), lambda b, c: (b * nc + c, 0)),
                  pl.BlockSpec((2 * M_HEADS, L), lambda b, c: (0, b * nc + c)),
                  _full((M_CONV, 2 * M_WIDTH)), _full((1, M_WIDTH)), _full((L, L)), _full((L, L))],
        out_specs=pl.BlockSpec((L, M_WIDTH), lambda b, c: (b * nc + c, 0)),
        out_shape=jax.ShapeDtypeStruct((T, M_WIDTH), BF16),
        scratch_shapes=[pltpu.VMEM((L + SUBLANE, 2 * M_WIDTH), F32),
                        pltpu.VMEM((M_HEADS, M_HEAD_DIM, M_HEAD_DIM), F32),
                        pltpu.VMEM((SUBLANE, M_HEAD_DIM), F32),
                        pltpu.VMEM((SUBLANE, LANE), F32)],
        compiler_params=_params(2),
        name="mlstm",
    )(zm, gt, conv_w, head_norm[None, :], ltri, eye)


def _compress_kernel(k_ref, v_ref, pak, pbk, wak, wbk, w2k, pav, pbv, wav, wbv, w2v, ko_ref, vo_ref):
    def one(x_ref, pa, pb, wa, wb, w2, o_ref):
        x = x_ref[0]
        n = x.shape[0]
        first = _dot((x + pa[...]).astype(BF16), wa[...])
        second = _dot((x + pb[...]).astype(BF16), wb[...])
        hid = _silu(first + pltpu.roll(second, n - 1, 0))
        o_ref[0] = _dot(hid.astype(BF16), w2[...]).astype(BF16)

    one(k_ref, pak, pbk, wak, wbk, w2k, ko_ref)
    one(v_ref, pav, pbv, wav, wbv, w2v, vo_ref)


def _compress_weights(pe, w1, w2):
    half = CMP_BLOCK // 2
    eye = jnp.eye(N_KV_GROUPS, dtype=F32)
    w1r = w1.reshape(CMP_BLOCK, N_HEAD_DIM, CMP_HIDDEN)
    big = lambda w: jnp.einsum("ldc,gh->lgdhc", w, eye).reshape(half * KV_WIDTH, N_KV_GROUPS * CMP_HIDDEN).astype(BF16)
    pe_row = lambda p: jnp.broadcast_to(p[:, None, :], (half, N_KV_GROUPS, N_HEAD_DIM)).reshape(1, half * KV_WIDTH)
    w2bd = jnp.einsum("cd,gh->gchd", w2, eye).reshape(N_KV_GROUPS * CMP_HIDDEN, KV_WIDTH).astype(BF16)
    return pe_row(pe[:half]), pe_row(pe[half:]), big(w1r[:half]), big(w1r[half:]), w2bd


def _compress(kc, vc, wk, wv, B, S):
    nc = S // CMP_STRIDE
    width = CMP_STRIDE * KV_WIDTH
    blk = pl.BlockSpec((1, nc, width), lambda b: (b, 0, 0))
    wspecs = [_full(w.shape) for w in wk]
    out = pl.BlockSpec((1, nc, KV_WIDTH), lambda b: (b, 0, 0))
    return pl.pallas_call(
        _compress_kernel,
        grid=(B,),
        in_specs=[blk, blk] + wspecs + wspecs,
        out_specs=[out, out],
        out_shape=[jax.ShapeDtypeStruct((B, nc, KV_WIDTH), BF16)] * 2,
        compiler_params=_params(1),
        name="compress",
    )(kc.reshape(B, nc, width), vc.reshape(B, nc, width), *wk, *wv)


def _masked_softmax(s, mask):
    s = jnp.where(mask, s, -jnp.inf)
    m = jnp.max(s, axis=-1, keepdims=True)
    m = jnp.where(jnp.isfinite(m), m, 0.0)
    p = jnp.exp(s - m)
    return p / jnp.maximum(jnp.sum(p, axis=-1, keepdims=True), 1e-30)


def _nsa_kernel(q_ref, kc_ref, vc_ref, ks_ref, vs_ref, kw_ref, vw_ref, ng_ref, ov_ref, ex_ref,
                y_ref, m_s, l_s, acc_s, *, n_top):
    TQ = q_ref.shape[0]
    TK = ex_ref.shape[2]
    NS = ov_ref.shape[0]
    G, HPG, DH = N_KV_GROUPS, HEADS_PER_GROUP, N_HEAD_DIM
    R = G * HPG * TQ
    q0 = pl.program_id(1) * TQ

    q = q_ref[...]
    zero = jnp.zeros((TQ, DH), BF16)
    parts = []
    for hd in range(N_HEADS):
        qh = q[:, hd * DH:(hd + 1) * DH]
        parts.append(jnp.concatenate([qh, zero] if hd < HPG else [zero, qh], axis=1))
    qp = jnp.concatenate(parts, axis=0)
    t_col = q0 + lax.broadcasted_iota(jnp.int32, (R, 1), 0) % TQ

    kc = kc_ref[0]
    NC = kc.shape[0]
    cmp_end = lax.broadcasted_iota(jnp.int32, (1, NC), 1) * CMP_STRIDE + (CMP_BLOCK - 1)
    p_c = _masked_softmax(_dot_nt(qp, kc), cmp_end <= t_col)
    o_c = _dot(p_c.astype(BF16), vc_ref[0])
    p_grp = jnp.sum(p_c.reshape(G, HPG, TQ, NC), axis=1).reshape(G * TQ, NC)
    imp = _dot_nt(ov_ref[...], p_grp, precision=HIGHEST)

    j_io = lax.broadcasted_iota(jnp.int32, (NS, G * TQ), 0)
    t_row = q0 + lax.broadcasted_iota(jnp.int32, (NS, G * TQ), 1) % TQ
    cur = t_row // SLC_BLOCK
    forced = (j_io == 0) | (j_io == cur) | (j_io == cur - 1)
    future = j_io * SLC_BLOCK > t_row
    score = jnp.where(forced, jnp.inf, jnp.where(future, -jnp.inf, imp))
    rank = jnp.zeros((NS, G * TQ), F32)
    for jp in range(NS):
        other = score[jp:jp + 1, :]
        rank = rank + jnp.where(j_io > jp, jnp.where(other >= score, 1.0, 0.0), jnp.where(other > score, 1.0, 0.0))
    sel_bias = jnp.where(future, NEG, jnp.where(rank < n_top, 0.0, NEG)).astype(BF16)

    m_s[...] = jnp.full_like(m_s, NEG)
    l_s[...] = jnp.zeros_like(l_s)
    acc_s[...] = jnp.zeros_like(acc_s)

    def step(kt, carry):
        k0 = pl.multiple_of(kt * TK, TK)
        bias = _dot_tn(sel_bias, ex_ref[kt])
        s = _dot_nt(qp, ks_ref[0, pl.ds(k0, TK), :])
        s = (s.reshape(G, HPG, TQ, TK) + bias.reshape(G, 1, TQ, TK)).reshape(R, TK)
        kpos = k0 + lax.broadcasted_iota(jnp.int32, (1, TK), 1)
        s = jnp.where(kpos <= t_col, s, NEG)
        m_old = m_s[...]
        m_new = jnp.maximum(m_old, jnp.max(s, axis=1, keepdims=True))
        alpha = jnp.exp(m_old - m_new)
        p = jnp.exp(s - m_new)
        l_s[...] = alpha * l_s[...] + jnp.sum(p, axis=1, keepdims=True)
        acc_s[...] = alpha * acc_s[...] + _dot(p.astype(BF16), vs_ref[0, pl.ds(k0, TK), :])
        m_s[...] = m_new
        return carry

    lax.fori_loop(0, (q0 + TQ - 1) // TK + 1, step, 0)
    o_s = acc_s[...] / l_s[...]

    WK = WINDOW + TQ
    start = pl.multiple_of(jnp.maximum(q0 - WINDOW, 0), TQ)
    wpos = start + lax.broadcasted_iota(jnp.int32, (1, WK), 1)
    p_w = _masked_softmax(_dot_nt(qp, kw_ref[0, pl.ds(start, WK), :]), (wpos <= t_col) & (wpos > t_col - WINDOW))
    o_w = _dot(p_w.astype(BF16), vw_ref[0, pl.ds(start, WK), :])

    gates = ng_ref[...]
    outs = []
    for hd in range(N_HEADS):
        rows = slice(hd * TQ, (hd + 1) * TQ)
        lanes = slice((hd // HPG) * DH, (hd // HPG + 1) * DH)
        outs.append(gates[:, 3 * hd:3 * hd + 1] * o_c[rows, lanes]
                    + gates[:, 3 * hd + 1:3 * hd + 2] * o_s[rows, lanes]
                    + gates[:, 3 * hd + 2:3 * hd + 3] * o_w[rows, lanes])
    y_ref[...] = jnp.concatenate(outs, axis=1).astype(BF16)


def _nsa(q, kcmp, vcmp, ks, vs, kw, vw, ng, B, S):
    T = q.shape[0]
    TQ = min(NSA_Q_TILE, S)
    TK = min(NSA_K_TILE, S)
    nq = S // TQ
    NS = S // SLC_BLOCK
    NC = S // CMP_STRIDE
    n_cmp = (S - CMP_BLOCK) // CMP_STRIDE + 1
    js = np.arange(NS)[:, None] * SLC_BLOCK
    cs = np.arange(NC)[None, :] * CMP_STRIDE
    ov = np.clip(np.minimum(js + SLC_BLOCK, cs + CMP_BLOCK) - np.maximum(js, cs), 0, None) / CMP_STRIDE
    ov[:, n_cmp:] = 0.0
    expand = (np.arange(S)[None, :] // SLC_BLOCK == np.arange(NS)[:, None]).astype(np.float32)
    expand = expand.reshape(NS, S // TK, TK).transpose(1, 0, 2)
    R = N_HEADS * TQ
    seq = lambda: pl.BlockSpec((1, S, KV_WIDTH), lambda b, i: (b, 0, 0))
    cmp_spec = lambda: pl.BlockSpec((1, NC, KV_WIDTH), lambda b, i: (b, 0, 0))
    return pl.pallas_call(
        functools.partial(_nsa_kernel, n_top=min(SLC_TOPK, NS)),
        grid=(B, nq),
        in_specs=[pl.BlockSpec((TQ, N_WIDTH), lambda b, i: (b * nq + i, 0)), cmp_spec(), cmp_spec(),
                  seq(), seq(), seq(), seq(), pl.BlockSpec((TQ, LANE), lambda b, i: (b * nq + i, 0)),
                  _full((NS, NC)), _full((S // TK, NS, TK))],
        out_specs=pl.BlockSpec((TQ, N_WIDTH), lambda b, i: (b * nq + i, 0)),
        out_shape=jax.ShapeDtypeStruct((T, N_WIDTH), BF16),
        scratch_shapes=[pltpu.VMEM((R, 1), F32), pltpu.VMEM((R, 1), F32), pltpu.VMEM((R, KV_WIDTH), F32)],
        compiler_params=_params(2),
        name="nsa",
    )(q, kcmp, vcmp, ks.reshape(B, S, KV_WIDTH), vs.reshape(B, S, KV_WIDTH), kw.reshape(B, S, KV_WIDTH),
      vw.reshape(B, S, KV_WIDTH), ng, jnp.asarray(ov.astype(np.float32)), jnp.asarray(expand, dtype=BF16))


def _post_kernel(ym_ref, yn_ref, wm_ref, wn_ref, x_ref, mod_ref, npost_ref, npre_ref, wr_ref, br_ref,
                 x1_ref, h2_ref, route_ref):
    TM = x_ref.shape[0]
    mod = mod_ref[0]
    y = _dot(ym_ref[...], wm_ref[...]) + _dot(yn_ref[...], wn_ref[...])
    x1 = x_ref[...] + mod[2:3] * _rms(y, npost_ref[...])
    x1_ref[...] = x1
    h2 = _rms(x1, npre_ref[...]) * (1.0 + mod[4:5]) + mod[3:4]
    for s in range(SUBLANE):
        h2_ref[pl.ds(s, TM, stride=SUBLANE), :] = h2[:, s * LANE:(s + 1) * LANE]

    logits = _dot(h2, wr_ref[...], precision=HIGHEST) + br_ref[...]
    lane = lax.broadcasted_iota(jnp.int32, (TM, LANE), 1)
    is_grp = lane < MOE_GROUPS
    lg = jnp.where(is_grp, logits, -jnp.inf)
    eg = jnp.exp(lg - jnp.max(lg, axis=1, keepdims=True))
    pg = eg / jnp.sum(eg, axis=1, keepdims=True)
    pg_top = jnp.max(pg, axis=1, keepdims=True)
    grp = jnp.min(jnp.where(is_grp & (pg == pg_top), lane, LANE), axis=1, keepdims=True)
    lo = MOE_GROUPS + EXPERTS_PER_GROUP * grp
    in_grp = (lane >= lo) & (lane < lo + EXPERTS_PER_GROUP)
    le = jnp.where(in_grp, logits, -jnp.inf)
    ee = jnp.exp(le - jnp.max(le, axis=1, keepdims=True))
    pe = jnp.where(in_grp, ee / jnp.sum(ee, axis=1, keepdims=True), -1.0)
    p1 = jnp.max(pe, axis=1, keepdims=True)
    i1 = jnp.min(jnp.where(pe == p1, lane, LANE), axis=1, keepdims=True)
    pe2 = jnp.where(lane == i1, -1.0, pe)
    p2 = jnp.max(pe2, axis=1, keepdims=True)
    i2 = jnp.min(jnp.where((pe2 == p2) & in_grp & (lane != i1), lane, LANE), axis=1, keepdims=True)
    den = p1 + p2
    e1 = (i1 - MOE_GROUPS).astype(F32)
    e2 = (i2 - MOE_GROUPS).astype(F32)
    route_ref[...] = jnp.where(lane == 0, e1, jnp.where(lane == 1, e2, jnp.where(
        lane == 2, pg_top * p1 / den, jnp.where(lane == 3, pg_top * p2 / den, 0.0))))


def _post(ym, yn, w_out, x2, mod3, norm_post, norm_pre, wr, br, S):
    T, D = x2.shape
    TM = min(TOKEN_TILE, S)
    tiles_per_seq = S // TM
    row = lambda w: pl.BlockSpec((TM, w), lambda i: (i, 0))
    return pl.pallas_call(
        _post_kernel,
        grid=(T // TM,),
        in_specs=[row(M_WIDTH), row(N_WIDTH), _full((M_WIDTH, D)), _full((N_WIDTH, D)), row(D),
                  pl.BlockSpec((1, 6, D), lambda i: (i // tiles_per_seq, 0, 0)), _full((1, D)), _full((1, D)),
                  _full((D, LANE)), _full((1, LANE))],
        out_specs=[row(D), pl.BlockSpec((TM * SUBLANE, LANE), lambda i: (i, 0)), row(LANE)],
        out_shape=[jax.ShapeDtypeStruct((T, D), F32), jax.ShapeDtypeStruct((T * SUBLANE, LANE), F32),
                   jax.ShapeDtypeStruct((T, LANE), F32)],
        compiler_params=_params(1),
        name="post_mix_router",
    )(ym, yn, w_out[:M_WIDTH].astype(BF16), w_out[M_WIDTH:].astype(BF16), x2, mod3,
      norm_post[None, :], norm_pre[None, :], wr, br)


def _one_hots(route, lane):
    return lane == route[:, 0:1].astype(jnp.int32), lane == route[:, 1:2].astype(jnp.int32)


def _rank_kernel(route_ref, ltri_ref, rank_ref, cnt_ref, carry):
    TM = route_ref.shape[0]

    @pl.when(pl.program_id(0) == 0)
    def _():
        carry[...] = jnp.zeros_like(carry)

    lane = lax.broadcasted_iota(jnp.int32, (TM, LANE), 1)
    oh0, oh1 = _one_hots(route_ref[...], lane)
    oh = jnp.where(oh0, 1.0, 0.0) + jnp.where(oh1, 1.0, 0.0)
    before = _dot(ltri_ref[...], oh.astype(BF16)) + carry[0:1, :]
    r0 = jnp.sum(jnp.where(oh0, before, 0.0), axis=1, keepdims=True)
    r1 = jnp.sum(jnp.where(oh1, before, 0.0), axis=1, keepdims=True)
    rank_ref[...] = jnp.where(lane == 0, r0, jnp.where(lane == 1, r1, 0.0))
    carry[...] = carry[...] + jnp.sum(oh, axis=0, keepdims=True)
    cnt_ref[...] = carry[...]


def _plan_kernel(cnt_ref, pstart_ref, blk_ref, tot_ref, *, rows_per_block):
    lane = lax.broadcasted_iota(jnp.int32, (SUBLANE, LANE), 1)
    nblk = (cnt_ref[...].astype(jnp.int32) + (rows_per_block - 1)) // rows_per_block
    end = nblk
    sh = 1
    while sh < N_EXPERTS:
        end = end + jnp.where(lane >= sh, pltpu.roll(end, sh, 1), 0)
        sh *= 2
    pstart_ref[...] = ((end - nblk) * rows_per_block).astype(F32)
    nbp = blk_ref.shape[0]
    blk_io = lax.broadcasted_iota(jnp.int32, (nbp, LANE), 0)
    lane_b = lax.broadcasted_iota(jnp.int32, (nbp, LANE), 1)
    passed = jnp.where((lane_b < N_EXPERTS) & (blk_io >= end[0:1, :]), 1, 0)
    blk_ref[...] = jnp.broadcast_to(jnp.minimum(jnp.sum(passed, axis=1, keepdims=True), N_EXPERTS - 1), (nbp, LANE))
    tot_ref[...] = jnp.broadcast_to(
        jnp.sum(jnp.where(lane == N_EXPERTS - 1, end, 0), axis=1, keepdims=True), (SUBLANE, LANE))


def _dest_kernel(route_ref, rank_ref, pstart_ref, dest_ref):
    TM = route_ref.shape[0]
    lane = lax.broadcasted_iota(jnp.int32, (TM, LANE), 1)
    oh0, oh1 = _one_hots(route_ref[...], lane)
    ps = pstart_ref[0:1, :]
    rank = rank_ref[...]
    d0 = jnp.sum(jnp.where(oh0, ps, 0.0), axis=1, keepdims=True) + rank[:, 0:1]
    d1 = jnp.sum(jnp.where(oh1, ps, 0.0), axis=1, keepdims=True) + rank[:, 1:2]
    dest_ref[...] = jnp.where(lane == 0, d0, jnp.where(lane == 1, d1, 0.0)).astype(jnp.int32)


def _sort_plan(route, n_blocks, S):
    T = route.shape[0]
    TM = min(TOKEN_TILE, S)
    row = pl.BlockSpec((TM, LANE), lambda i: (i, 0))
    small = pl.BlockSpec((SUBLANE, LANE), lambda i: (0, 0))
    ltri = jnp.asarray(np.tril(np.ones((TM, TM), np.float32), -1), dtype=BF16)
    rank, cnt = pl.pallas_call(
        _rank_kernel,
        grid=(T // TM,),
        in_specs=[row, _full((TM, TM))],
        out_specs=[row, small],
        out_shape=[jax.ShapeDtypeStruct((T, LANE), F32), jax.ShapeDtypeStruct((SUBLANE, LANE), F32)],
        scratch_shapes=[pltpu.VMEM((SUBLANE, LANE), F32)],
        compiler_params=_params(1),
        name="expert_rank",
    )(route, ltri)
    nbp = -(-n_blocks // SUBLANE) * SUBLANE
    pstart, blk, tot = pl.pallas_call(
        functools.partial(_plan_kernel, rows_per_block=EXPERT_ROWS),
        grid=(1,),
        in_specs=[small],
        out_specs=[small, _full((nbp, LANE)), small],
        out_shape=[jax.ShapeDtypeStruct((SUBLANE, LANE), F32), jax.ShapeDtypeStruct((nbp, LANE), jnp.int32),
                   jax.ShapeDtypeStruct((SUBLANE, LANE), jnp.int32)],
        compiler_params=_params(1),
        name="expert_plan",
    )(cnt)
    dest = pl.pallas_call(
        _dest_kernel,
        grid=(T // TM,),
        in_specs=[row, row, small],
        out_specs=row,
        out_shape=jax.ShapeDtypeStruct((T, LANE), jnp.int32),
        compiler_params=_params(1),
        name="expert_dest",
    )(route, rank, pstart)
    return dest[:, :2], blk[:n_blocks, 0], tot[0, :1]


def _row_copy(src, src_row, dst, dst_row, sem):
    return pltpu.make_async_copy(src.at[pl.ds(pl.multiple_of(src_row * SUBLANE, SUBLANE), SUBLANE)],
                                 dst.at[pl.ds(pl.multiple_of(dst_row * SUBLANE, SUBLANE), SUBLANE)], sem)


def _dispatch_kernel(dest_ref, h2_hbm, xs_in, xs_hbm, sem):
    del xs_in
    n = dest_ref.shape[2] // 2
    tok0 = pl.program_id(0) * n

    def issue(j, carry):
        for k in range(2):
            _row_copy(h2_hbm, tok0 + j, xs_hbm, dest_ref[0, 0, 2 * j + k], sem).start()
        return carry

    lax.fori_loop(0, n, issue, 0)

    def drain(j, carry):
        for k in range(2):
            _row_copy(h2_hbm, tok0 + j, xs_hbm, dest_ref[0, 0, 2 * j + k], sem).wait()
        return carry

    lax.fori_loop(0, n, drain, 0)


def _dispatch(dest3, h2t, n_rows):
    nt, _, two_n = dest3.shape
    xs0 = jnp.zeros((n_rows * SUBLANE, LANE), F32)
    return pl.pallas_call(
        _dispatch_kernel,
        grid=(nt,),
        in_specs=[pl.BlockSpec((1, 1, two_n), lambda i: (i, 0, 0), memory_space=pltpu.SMEM),
                  pl.BlockSpec(memory_space=pl.ANY), pl.BlockSpec(memory_space=pl.ANY)],
        out_specs=pl.BlockSpec(memory_space=pl.ANY),
        out_shape=jax.ShapeDtypeStruct(xs0.shape, F32),
        scratch_shapes=[pltpu.SemaphoreType.DMA(())],
        input_output_aliases={2: 0},
        compiler_params=_params(1),
        name="dispatch",
    )(dest3, h2t, xs0)


def _expert_kernel(be_ref, nb_ref, xs_ref, w1_ref, w3_ref, w2_ref, ys_ref):
    del be_ref
    RB = xs_ref.shape[0] // SUBLANE

    @pl.when(pl.program_id(0) < nb_ref[0])
    def _():
        x = jnp.concatenate([xs_ref[pl.ds(s, RB, stride=SUBLANE), :] for s in range(SUBLANE)], axis=1).astype(BF16)
        hb = _silu(_dot(x, w1_ref[0])) * _dot(x, w3_ref[0])
        y = _dot(hb.astype(BF16), w2_ref[0])
        for s in range(SUBLANE):
            ys_ref[pl.ds(s, RB, stride=SUBLANE), :] = y[:, s * LANE:(s + 1) * LANE]

    @pl.when(pl.program_id(0) >= nb_ref[0])
    def _():
        ys_ref[...] = jnp.zeros_like(ys_ref)


def _experts(block_expert, n_used, xs, w1, w3, w2):
    n_blocks = block_expert.shape[0]
    RB = EXPERT_ROWS
    D, F = w1.shape[1], w1.shape[2]
    cur = lambda i, nb: jnp.minimum(i, nb[0] - 1)
    rows = pl.BlockSpec((RB * SUBLANE, LANE), lambda i, be, nb: (cur(i, nb), 0))
    wspec = lambda a, b: pl.BlockSpec((1, a, b), lambda i, be, nb: (be[cur(i, nb)], 0, 0))
    return pl.pallas_call(
        _expert_kernel,
        grid_spec=pltpu.PrefetchScalarGridSpec(
            num_scalar_prefetch=2, grid=(n_blocks,),
            in_specs=[rows, wspec(D, F), wspec(D, F), wspec(F, D)],
            out_specs=pl.BlockSpec((RB * SUBLANE, LANE), lambda i, be, nb: (i, 0))),
        out_shape=jax.ShapeDtypeStruct(xs.shape, F32),
        compiler_params=_params(1),
        name="experts",
    )(block_expert, n_used, xs, w1, w3, w2)


def _combine_kernel(dest_ref, ys_hbm, x1_ref, route_ref, mod_ref, nw_ref, o_ref, buf, sem):
    n = x1_ref.shape[0]

    def issue(j, carry):
        for k in range(2):
            _row_copy(ys_hbm, dest_ref[0, 0, 2 * j + k], buf, k * n + j, sem).start()
        return carry

    lax.fori_loop(0, n, issue, 0)

    def drain(j, carry):
        for k in range(2):
            _row_copy(ys_hbm, dest_ref[0, 0, 2 * j + k], buf, k * n + j, sem).wait()
        return carry

    lax.fori_loop(0, n, drain, 0)

    route = route_ref[...]
    g0, g1 = route[:, 2:3], route[:, 3:4]
    y = jnp.concatenate(
        [g0 * buf[pl.ds(s, n, stride=SUBLANE), :] + g1 * buf[pl.ds(n * SUBLANE + s, n, stride=SUBLANE), :]
         for s in range(SUBLANE)], axis=1)
    o_ref[...] = x1_ref[...] + mod_ref[0][5:6] * _rms(y, nw_ref[...])


def _combine(dest3, ys, x1, route, mod3, norm_w, S):
    T, D = x1.shape
    nt, _, two_n = dest3.shape
    n = two_n // 2
    tiles_per_seq = S // n
    return pl.pallas_call(
        _combine_kernel,
        grid=(nt,),
        in_specs=[pl.BlockSpec((1, 1, two_n), lambda i: (i, 0, 0), memory_space=pltpu.SMEM),
                  pl.BlockSpec(memory_space=pl.ANY), pl.BlockSpec((n, D), lambda i: (i, 0)),
                  pl.BlockSpec((n, LANE), lambda i: (i, 0)),
                  pl.BlockSpec((1, 6, D), lambda i: (i // tiles_per_seq, 0, 0)), _full((1, D))],
        out_specs=pl.BlockSpec((n, D), lambda i: (i, 0)),
        out_shape=jax.ShapeDtypeStruct((T, D), F32),
        scratch_shapes=[pltpu.VMEM((2 * n * SUBLANE, LANE), F32), pltpu.SemaphoreType.DMA(())],
        compiler_params=_params(1),
        name="combine",
    )(dest3, ys, x1, route, mod3, norm_w[None, :])


def _layer(x, c, positions, ada_w, ada_b, norm_mix_pre, norm_mix_post, norm_ffn_pre, norm_ffn_post, w_in,
           conv_w, gate_b, head_norm, cmp_k, cmp_v, w_out, wg, bg, we, be, w1, w3, w2):
    B, S, D = x.shape
    T = B * S
    x2 = x.reshape(T, D)
    mod3 = _adaln(c, ada_w, ada_b).reshape(B, 6, D)

    o_mi = 4 * M_WIDTH
    o_nq = o_mi + 2 * M_HEADS
    o_kv = o_nq + N_WIDTH
    kv = lambda i: w_in[:, o_kv + i * KV_WIDTH:o_kv + (i + 1) * KV_WIDTH]
    o_ng = o_kv + 6 * KV_WIDTH
    w_main = jnp.concatenate(
        [w_in[:, :o_mi], w_in[:, o_nq:o_kv], kv(0), kv(2), kv(4), kv(1), kv(3), kv(5), w_in[:, o_ng:],
         jnp.zeros((D, LANE - 3 * N_HEADS), F32)], axis=1).astype(BF16)
    wg_t = w_in[:, o_mi:o_nq].T
    half = N_HEAD_DIM // 2
    inv = ROPE_THETA ** (-jnp.arange(half, dtype=F32) / half)
    inv_row = jnp.tile(inv, LANE // half)[None, :]
    pos_col = positions.astype(F32).reshape(T, 1)

    zm, q, kc, ks, kw, vc, vs, vw, ng, gt = _inproj(x2, mod3, norm_mix_pre, w_main, wg_t, gate_b, pos_col, inv_row, S)
    ym = _mlstm(zm, gt, conv_w, head_norm, B, S)
    kcmp, vcmp = _compress(kc, vc, _compress_weights(*cmp_k), _compress_weights(*cmp_v), B, S)
    yn = _nsa(q, kcmp, vcmp, ks, vs, kw, vw, ng, B, S)

    wr = jnp.concatenate([wg, we, jnp.zeros((D, LANE - MOE_GROUPS - N_EXPERTS), F32)], axis=1)
    br = jnp.concatenate([bg, be, jnp.zeros((LANE - MOE_GROUPS - N_EXPERTS,), F32)])[None, :]
    x1, h2t, route = _post(ym, yn, w_out, x2, mod3, norm_mix_post, norm_ffn_pre, wr, br, S)

    n_blocks = (2 * T) // EXPERT_ROWS + N_EXPERTS
    dest, block_expert, n_used = _sort_plan(route, n_blocks, S)
    n_tok = min(GATHER_TILE, S)
    dest3 = dest.reshape(T // n_tok, 1, 2 * n_tok)
    xs = _dispatch(dest3, h2t, n_blocks * EXPERT_ROWS)
    ys = _experts(block_expert, n_used, xs, w1.astype(BF16), w3.astype(BF16), w2.astype(BF16))
    out = _combine(dest3, ys, x1, route, mod3, norm_ffn_post, S)
    return out.reshape(B, S, D)


def kernel(x, c, positions, ada_w, ada_b, norm_mix_pre, norm_mix_post, norm_ffn_pre, norm_ffn_post, w_in, mlstm_conv_w, mlstm_gate_b, mlstm_head_norm, cmp_pe_k, cmp_w1_k, cmp_w2_k, cmp_pe_v, cmp_w1_v, cmp_w2_v, w_out, router_grp_w, router_grp_b, router_exp_w, router_exp_b, expert_w1, expert_w3, expert_w2):
    for l in range(ada_w.shape[0]):
        x = _layer(x, c, positions, ada_w[l], ada_b[l], norm_mix_pre[l], norm_mix_post[l], norm_ffn_pre[l],
                   norm_ffn_post[l], w_in[l], mlstm_conv_w[l], mlstm_gate_b[l], mlstm_head_norm[l],
                   (cmp_pe_k[l], cmp_w1_k[l], cmp_w2_k[l]), (cmp_pe_v[l], cmp_w1_v[l], cmp_w2_v[l]), w_out[l],
                   router_grp_w[l], router_grp_b[l], router_exp_w[l], router_exp_b[l],
                   expert_w1[l], expert_w3[l], expert_w2[l])
    return x
```

```python
import functools

import jax
import jax.numpy as jnp
import numpy as np
from jax import lax
from jax.experimental import pallas as pl
from jax.experimental.pallas import tpu as pltpu

M_HEADS = 4
M_HEAD_DIM = 128
M_WIDTH = M_HEADS * M_HEAD_DIM
M_CONV = 4
N_HEADS = 8
N_KV_GROUPS = 2
HEADS_PER_GROUP = N_HEADS // N_KV_GROUPS
N_HEAD_DIM = 64
N_WIDTH = N_HEADS * N_HEAD_DIM
KV_WIDTH = N_KV_GROUPS * N_HEAD_DIM
CMP_BLOCK = 32
CMP_STRIDE = 16
CMP_HIDDEN = 2 * N_HEAD_DIM
SLC_BLOCK = 64
SLC_TOPK = 16
WINDOW = 512
ROPE_THETA = 10000.0
MOE_GROUPS = 4
EXPERTS_PER_GROUP = 8
N_EXPERTS = MOE_GROUPS * EXPERTS_PER_GROUP
RMS_EPS = 1e-6

LANE = 128
SUBLANE = 8
VMEM_LIMIT_BYTES = 56 * 1024 * 1024

TOKEN_TILE = 512
MLSTM_CHUNK = 256
NSA_Q_TILE = 128
NSA_K_TILE = 512
EXPERT_ROWS = 256
GATHER_TILE = 256

F32 = jnp.float32
BF16 = jnp.bfloat16
NEG = -1e30
HIGHEST = lax.Precision.HIGHEST
NT_DIMS = (((1,), (1,)), ((), ()))
TN_DIMS = (((0,), (0,)), ((), ()))


def _params(n_grid):
    return pltpu.CompilerParams(
        dimension_semantics=("arbitrary",) * n_grid, vmem_limit_bytes=VMEM_LIMIT_BYTES)


def _dot(a, b, **kw):
    return jnp.dot(a, b, preferred_element_type=F32, **kw)


def _dot_nt(a, b, **kw):
    return lax.dot_general(a, b, NT_DIMS, preferred_element_type=F32, **kw)


def _dot_tn(a, b, **kw):
    return lax.dot_general(a, b, TN_DIMS, preferred_element_type=F32, **kw)


def _sigmoid(x):
    return 1.0 / (1.0 + jnp.exp(-x))


def _silu(x):
    return x * _sigmoid(x)


def _rms(x, w):
    return x * lax.rsqrt(jnp.mean(x * x, axis=-1, keepdims=True) + RMS_EPS) * w


def _full(shape):
    return pl.BlockSpec(shape, lambda *_: (0,) * len(shape))


def _adaln_kernel(c_ref, w_ref, b_ref, o_ref):
    o_ref[...] = _dot(_silu(c_ref[...]), w_ref[...], precision=HIGHEST) + b_ref[...]


def _adaln(c, ada_w, ada_b):
    B, D = c.shape
    n = ada_w.shape[1] // D
    return pl.pallas_call(
        _adaln_kernel,
        grid=(n,),
        in_specs=[_full((B, D)), pl.BlockSpec((D, D), lambda j: (0, j)), pl.BlockSpec((1, D), lambda j: (0, j))],
        out_specs=pl.BlockSpec((B, D), lambda j: (0, j)),
        out_shape=jax.ShapeDtypeStruct((B, ada_w.shape[1]), F32),
        compiler_params=_params(1),
        name="adaln",
    )(c, ada_w, ada_b[None, :])


_C_M = 0
_C_Q = 4 * M_WIDTH
_C_KC = _C_Q + N_WIDTH
_C_KS = _C_KC + KV_WIDTH
_C_KW = _C_KS + KV_WIDTH
_C_VC = _C_KW + KV_WIDTH
_C_VS = _C_VC + KV_WIDTH
_C_VW = _C_VS + KV_WIDTH
_C_NG = _C_VW + KV_WIDTH
_C_END = _C_NG + LANE


def _inproj_kernel(x_ref, mod_ref, nw_ref, w_ref, wg_ref, gb_ref, pos_ref, inv_ref,
                   zm_ref, q_ref, kc_ref, ks_ref, kw_ref, vc_ref, vs_ref, vw_ref, ng_ref, gt_ref):
    mod = mod_ref[0]
    h = _rms(x_ref[...], nw_ref[...]) * (1.0 + mod[1:2]) + mod[0:1]
    z = _dot(h.astype(BF16), w_ref[...])
    zm_ref[...] = z[:, _C_M:_C_Q]
    gt_ref[...] = _dot_nt(wg_ref[...], h, precision=HIGHEST) + gb_ref[...]

    ang = pos_ref[...] * inv_ref[...]
    cos = jnp.cos(ang)
    sin = jnp.sin(ang)
    lane = lax.broadcasted_iota(jnp.int32, (1, LANE), 1)
    first = (lane % N_HEAD_DIM) < (N_HEAD_DIM // 2)
    sin_signed = jnp.where(first, -sin, sin)

    def rope(slab):
        rot = jnp.where(first, pltpu.roll(slab, LANE - N_HEAD_DIM // 2, 1), pltpu.roll(slab, N_HEAD_DIM // 2, 1))
        return slab * cos + rot * sin_signed

    scale = N_HEAD_DIM ** -0.5
    for j in range(N_WIDTH // LANE):
        q_ref[:, j * LANE:(j + 1) * LANE] = (rope(z[:, _C_Q + j * LANE:_C_Q + (j + 1) * LANE]) * scale).astype(BF16)
    kc_ref[...] = rope(z[:, _C_KC:_C_KS])
    ks_ref[...] = rope(z[:, _C_KS:_C_KW]).astype(BF16)
    kw_ref[...] = rope(z[:, _C_KW:_C_VC]).astype(BF16)
    vc_ref[...] = z[:, _C_VC:_C_VS]
    vs_ref[...] = z[:, _C_VS:_C_VW].astype(BF16)
    vw_ref[...] = z[:, _C_VW:_C_NG].astype(BF16)
    ng_ref[...] = _sigmoid(z[:, _C_NG:_C_END])


def _inproj(x2, mod3, norm_w, w_main, wg_t, gate_b, pos_col, inv_row, S):
    T, D = x2.shape
    TM = min(TOKEN_TILE, S)
    tiles_per_seq = S // TM
    row = lambda w: pl.BlockSpec((TM, w), lambda i: (i, 0))
    outs = [(4 * M_WIDTH, F32), (N_WIDTH, BF16), (KV_WIDTH, F32), (KV_WIDTH, BF16), (KV_WIDTH, BF16),
            (KV_WIDTH, F32), (KV_WIDTH, BF16), (KV_WIDTH, BF16), (LANE, F32)]
    return pl.pallas_call(
        _inproj_kernel,
        grid=(T // TM,),
        in_specs=[row(D), pl.BlockSpec((1, 6, D), lambda i: (i // tiles_per_seq, 0, 0)), _full((1, D)),
                  _full(w_main.shape), _full(wg_t.shape), _full((2 * M_HEADS, 1)), row(1), _full((1, LANE))],
        out_specs=[row(w) for w, _ in outs] + [pl.BlockSpec((2 * M_HEADS, TM), lambda i: (0, i))],
        out_shape=[jax.ShapeDtypeStruct((T, w), dt) for w, dt in outs] + [jax.ShapeDtypeStruct((2 * M_HEADS, T), F32)],
        compiler_params=_params(1),
        name="inproj",
    )(x2, mod3, norm_w[None, :], w_main, wg_t, gate_b[:, None], pos_col, inv_row)


def _mlstm_kernel(zm_ref, gt_ref, cw_ref, hn_ref, ltri_ref, eye_ref, y_ref, buf, c_s, n_s, m_s):
    L = zm_ref.shape[0]
    QK = 2 * M_WIDTH
    DH = M_HEAD_DIM

    @pl.when(pl.program_id(1) == 0)
    def _():
        buf[0:SUBLANE, :] = jnp.zeros((SUBLANE, QK), F32)
        c_s[...] = jnp.zeros_like(c_s)
        n_s[...] = jnp.zeros_like(n_s)
        m_s[...] = jnp.zeros_like(m_s)

    buf[SUBLANE:SUBLANE + L, :] = zm_ref[:, 0:QK]
    cw = cw_ref[...]
    conv = cw[M_CONV - 1:M_CONV] * buf[SUBLANE:SUBLANE + L, :]
    for j in range(M_CONV - 1):
        off = SUBLANE - (M_CONV - 1) + j
        conv = conv + cw[j:j + 1] * buf[off:off + L, :]
    buf[0:SUBLANE, :] = buf[L:L + SUBLANE, :]
    qk = _silu(conv)

    g = gt_ref[...]
    fp = g[M_HEADS:]
    lf = jnp.minimum(fp, 0.0) - jnp.log(1.0 + jnp.exp(-jnp.abs(fp)))
    lf8 = jnp.concatenate([lf, jnp.zeros_like(lf)], axis=0)
    ltri = ltri_ref[...]
    b_rows = _dot_nt(lf8, ltri, precision=HIGHEST)
    b_cols = _dot_nt(ltri, lf8, precision=HIGHEST)
    i_cols = _dot_nt(eye_ref[...], g, precision=HIGHEST)

    causal = lax.broadcasted_iota(jnp.int32, (L, L), 0) >= lax.broadcasted_iota(jnp.int32, (L, L), 1)
    for h in range(M_HEADS):
        hs = slice(h * DH, (h + 1) * DH)
        b_r, i_r = b_rows[h:h + 1], g[h:h + 1]
        b_c, i_c = b_cols[:, h:h + 1], i_cols[:, h:h + 1]
        m_prev = m_s[h:h + 1, 0:1]
        g_tot = b_r[:, L - 1:L]
        m_loc = jnp.max(g_tot - b_r + i_r, axis=1, keepdims=True)
        m_new = jnp.maximum(g_tot + m_prev, m_loc)

        q = qk[:, hs]
        k = qk[:, M_WIDTH + h * DH:M_WIDTH + (h + 1) * DH] * (DH ** -0.5)
        v = zm_ref[:, 2 * M_WIDTH + h * DH:2 * M_WIDTH + (h + 1) * DH]
        o = zm_ref[:, 3 * M_WIDTH + h * DH:3 * M_WIDTH + (h + 1) * DH]
        qb, kb, vb = q.astype(BF16), k.astype(BF16), v.astype(BF16)

        dlog = jnp.where(causal, b_c - b_r + i_r, -jnp.inf)
        inter = b_c + m_prev
        m_t = jnp.maximum(inter, jnp.max(dlog, axis=1, keepdims=True))
        wts = jnp.exp(dlog - m_t) * _dot_nt(qb, kb)
        dec = jnp.exp(inter - m_t)
        c_prev = c_s[h]
        n_prev = n_s[h:h + 1]
        num = _dot(wts.astype(BF16), vb) + dec * _dot(qb, c_prev.astype(BF16))
        den = jnp.sum(wts, axis=1, keepdims=True) + dec * jnp.sum(q * n_prev, axis=1, keepdims=True)
        hh = num / jnp.maximum(jnp.abs(den), jnp.exp(-m_t))

        kw = k * jnp.exp(g_tot - b_c + i_c - m_new)
        keep = jnp.exp(g_tot + m_prev - m_new)
        c_s[h] = keep * c_prev + _dot_tn(kw.astype(BF16), vb)
        n_s[h:h + 1] = keep * n_prev + jnp.sum(kw, axis=0, keepdims=True)
        m_s[h:h + 1] = jnp.broadcast_to(m_new, (1, LANE))

        hn = _rms(hh, hn_ref[:, hs])
        y_ref[:, hs] = (_sigmoid(o) * hn).astype(BF16)


def _mlstm(zm, gt, conv_w, head_norm, B, S):
    T = zm.shape[0]
    L = min(MLSTM_CHUNK, S)
    nc = S // L
    ltri = jnp.asarray(np.tril(np.ones((L, L), np.float32)))
    eye = jnp.asarray(np.eye(L, dtype=np.float32))
    return pl.pallas_call(
        _mlstm_kernel,
        grid=(B, nc),
        in_specs=[pl.BlockSpec((L, 4 * M_WIDTH), lambda b, c: (b * nc + c, 0)),
                  pl.BlockSpec((2 * M_HEADS, L), lambda b, c: (0, b * nc + c)),
                  _full((M_CONV, 2 * M_WIDTH)), _full((1, M_WIDTH)), _full((L, L)), _full((L, L))],
        out_specs=pl.BlockSpec((L, M_WIDTH), lambda b, c: (b * nc + c, 0)),
        out_shape=jax.ShapeDtypeStruct((T, M_WIDTH), BF16),
        scratch_shapes=[pltpu.VMEM((L + SUBLANE, 2 * M_WIDTH), F32),
                        pltpu.VMEM((M_HEADS, M_HEAD_DIM, M_HEAD_DIM), F32),
                        pltpu.VMEM((SUBLANE, M_HEAD_DIM), F32),
                        pltpu.VMEM((SUBLANE, LANE), F32)],
        compiler_params=_params(2),
        name="mlstm",
    )(zm, gt, conv_w, head_norm[None, :], ltri, eye)


def _compress_kernel(k_ref, v_ref, pak, pbk, wak, wbk, w2k, pav, pbv, wav, wbv, w2v, ko_ref, vo_ref):
    def one(x_ref, pa, pb, wa, wb, w2, o_ref):
        x = x_ref[0]
        n = x.shape[0]
        first = _dot((x + pa[...]).astype(BF16), wa[...])
        second = _dot((x + pb[...]).astype(BF16), wb[...])
        hid = _silu(first + pltpu.roll(second, n - 1, 0))
        o_ref[0] = _dot(hid.astype(BF16), w2[...]).astype(BF16)

    one(k_ref, pak, pbk, wak, wbk, w2k, ko_ref)
    one(v_ref, pav, pbv, wav, wbv, w2v, vo_ref)


def _compress_weights(pe, w1, w2):
    half = CMP_BLOCK // 2
    eye = jnp.eye(N_KV_GROUPS, dtype=F32)
    w1r = w1.reshape(CMP_BLOCK, N_HEAD_DIM, CMP_HIDDEN)
    big = lambda w: jnp.einsum("ldc,gh->lgdhc", w, eye).reshape(half * KV_WIDTH, N_KV_GROUPS * CMP_HIDDEN).astype(BF16)
    pe_row = lambda p: jnp.broadcast_to(p[:, None, :], (half, N_KV_GROUPS, N_HEAD_DIM)).reshape(1, half * KV_WIDTH)
    w2bd = jnp.einsum("cd,gh->gchd", w2, eye).reshape(N_KV_GROUPS * CMP_HIDDEN, KV_WIDTH).astype(BF16)
    return pe_row(pe[:half]), pe_row(pe[half:]), big(w1r[:half]), big(w1r[half:]), w2bd


def _compress(kc, vc, wk, wv, B, S):
    nc = S // CMP_STRIDE
    width = CMP_STRIDE * KV_WIDTH
    blk = pl.BlockSpec((1, nc, width), lambda b: (b, 0, 0))
    wspecs = [_full(w.shape) for w in wk]
    out = pl.BlockSpec((1, nc, KV_WIDTH), lambda b: (b, 0, 0))
    return pl.pallas_call(
        _compress_kernel,
        grid=(B,),
        in_specs=[blk, blk] + wspecs + wspecs,
        out_specs=[out, out],
        out_shape=[jax.ShapeDtypeStruct((B, nc, KV_WIDTH), BF16)] * 2,
        compiler_params=_params(1),
        name="compress",
    )(kc.reshape(B, nc, width), vc.reshape(B, nc, width), *wk, *wv)


def _masked_softmax(s, mask):
    s = jnp.where(mask, s, -jnp.inf)
    m = jnp.max(s, axis=-1, keepdims=True)
    m = jnp.where(jnp.isfinite(m), m, 0.0)
    p = jnp.exp(s - m)
    return p / jnp.maximum(jnp.sum(p, axis=-1, keepdims=True), 1e-30)


def _nsa_kernel(q_ref, kc_ref, vc_ref, ks_ref, vs_ref, kw_ref, vw_ref, ng_ref, ov_ref, ek_ref, eye_ref,
                y_ref, m_s, l_s, acc_s, *, n_top, TK):
    TQ = q_ref.shape[0]
    NS = ov_ref.shape[0]
    G, HPG, DH = N_KV_GROUPS, HEADS_PER_GROUP, N_HEAD_DIM
    R = G * HPG * TQ
    q0 = pl.program_id(1) * TQ

    q = q_ref[...]
    zero = jnp.zeros((TQ, DH), BF16)
    parts = []
    for hd in range(N_HEADS):
        qh = q[:, hd * DH:(hd + 1) * DH]
        parts.append(jnp.concatenate([qh, zero] if hd < HPG else [zero, qh], axis=1))
    qp = jnp.concatenate(parts, axis=0)
    t_col = q0 + lax.broadcasted_iota(jnp.int32, (R, 1), 0) % TQ

    kc = kc_ref[0]
    NC = kc.shape[0]
    cmp_end = lax.broadcasted_iota(jnp.int32, (1, NC), 1) * CMP_STRIDE + (CMP_BLOCK - 1)
    p_c = _masked_softmax(_dot_nt(qp, kc), cmp_end <= t_col)
    o_c = _dot(p_c.astype(BF16), vc_ref[0])
    p_grp = jnp.sum(p_c.reshape(G, HPG, TQ, NC), axis=1).reshape(G * TQ, NC)
    imp = _dot_nt(ov_ref[...], p_grp, precision=HIGHEST)

    j_io = lax.broadcasted_iota(jnp.int32, (NS, G * TQ), 0)
    t_row = q0 + lax.broadcasted_iota(jnp.int32, (NS, G * TQ), 1) % TQ
    cur = t_row // SLC_BLOCK
    forced = (j_io == 0) | (j_io == cur) | (j_io == cur - 1)
    future = j_io * SLC_BLOCK > t_row
    score = jnp.where(forced, jnp.inf, jnp.where(future, -jnp.inf, imp))
    rank = jnp.zeros((NS, G * TQ), F32)
    for jp in range(NS):
        other = score[jp:jp + 1, :]
        rank = rank + jnp.where(j_io > jp, jnp.where(other >= score, 1.0, 0.0), jnp.where(other > score, 1.0, 0.0))
    sel_bias = jnp.where(future, NEG, jnp.where(rank < n_top, 0.0, NEG)).astype(BF16)
    sel_rows = _dot_tn(sel_bias, eye_ref[...]).astype(BF16)
    sel_rows = jnp.broadcast_to(sel_rows.reshape(G, 1, TQ, LANE), (G, HPG, TQ, LANE)).reshape(R, LANE)
    qa = jnp.concatenate([qp, sel_rows], axis=1)

    m_s[...] = jnp.full_like(m_s, NEG)
    l_s[...] = jnp.zeros_like(l_s)
    acc_s[...] = jnp.zeros_like(acc_s)

    def step(kt, causal):
        k0 = pl.multiple_of(kt * TK, TK)
        ka = jnp.concatenate([ks_ref[0, pl.ds(k0, TK), :], ek_ref[pl.ds(k0, TK), :]], axis=1)
        s = _dot_nt(qa, ka)
        if causal:
            kpos = k0 + lax.broadcasted_iota(jnp.int32, (1, TK), 1)
            s = jnp.where(kpos <= t_col, s, NEG)
        m_old = m_s[...]
        m_new = jnp.maximum(m_old, jnp.max(s, axis=1, keepdims=True))
        alpha = jnp.exp(m_old - m_new)
        p = jnp.exp(s - m_new)
        l_s[...] = alpha * l_s[...] + jnp.sum(p, axis=1, keepdims=True)
        acc_s[...] = alpha * acc_s[...] + _dot(p.astype(BF16), vs_ref[0, pl.ds(k0, TK), :])
        m_s[...] = m_new

    def full_tile(kt, carry):
        step(kt, False)
        return carry

    last = q0 // TK
    lax.fori_loop(0, last, full_tile, 0)
    step(last, True)
    o_s = acc_s[...] / l_s[...]

    WK = WINDOW + TQ
    start = pl.multiple_of(jnp.maximum(q0 - WINDOW, 0), TQ)
    wpos = start + lax.broadcasted_iota(jnp.int32, (1, WK), 1)
    p_w = _masked_softmax(_dot_nt(qp, kw_ref[0, pl.ds(start, WK), :]), (wpos <= t_col) & (wpos > t_col - WINDOW))
    o_w = _dot(p_w.astype(BF16), vw_ref[0, pl.ds(start, WK), :])

    gates = ng_ref[...]
    outs = []
    for hd in range(N_HEADS):
        rows = slice(hd * TQ, (hd + 1) * TQ)
        lanes = slice((hd // HPG) * DH, (hd // HPG + 1) * DH)
        outs.append(gates[:, 3 * hd:3 * hd + 1] * o_c[rows, lanes]
                    + gates[:, 3 * hd + 1:3 * hd + 2] * o_s[rows, lanes]
                    + gates[:, 3 * hd + 2:3 * hd + 3] * o_w[rows, lanes])
    y_ref[...] = jnp.concatenate(outs, axis=1).astype(BF16)


def _nsa(q, kcmp, vcmp, ks, vs, kw, vw, ng, B, S):
    T = q.shape[0]
    TQ = min(NSA_Q_TILE, S)
    TK = min(NSA_K_TILE, S)
    nq = S // TQ
    NS = S // SLC_BLOCK
    NC = S // CMP_STRIDE
    n_cmp = (S - CMP_BLOCK) // CMP_STRIDE + 1
    js = np.arange(NS)[:, None] * SLC_BLOCK
    cs = np.arange(NC)[None, :] * CMP_STRIDE
    ov = np.clip(np.minimum(js + SLC_BLOCK, cs + CMP_BLOCK) - np.maximum(js, cs), 0, None) / CMP_STRIDE
    ov[:, n_cmp:] = 0.0
    assert NS <= LANE and TK % TQ == 0
    block_of_key = (np.arange(S)[:, None] // SLC_BLOCK == np.arange(LANE)[None, :]).astype(np.float32)
    eye_pad = np.eye(NS, LANE, dtype=np.float32)
    R = N_HEADS * TQ
    seq = lambda: pl.BlockSpec((1, S, KV_WIDTH), lambda b, i: (b, 0, 0))
    cmp_spec = lambda: pl.BlockSpec((1, NC, KV_WIDTH), lambda b, i: (b, 0, 0))
    return pl.pallas_call(
        functools.partial(_nsa_kernel, n_top=min(SLC_TOPK, NS), TK=TK),
        grid=(B, nq),
        in_specs=[pl.BlockSpec((TQ, N_WIDTH), lambda b, i: (b * nq + i, 0)), cmp_spec(), cmp_spec(),
                  seq(), seq(), seq(), seq(), pl.BlockSpec((TQ, LANE), lambda b, i: (b * nq + i, 0)),
                  _full((NS, NC)), _full((S, LANE)), _full((NS, LANE))],
        out_specs=pl.BlockSpec((TQ, N_WIDTH), lambda b, i: (b * nq + i, 0)),
        out_shape=jax.ShapeDtypeStruct((T, N_WIDTH), BF16),
        scratch_shapes=[pltpu.VMEM((R, 1), F32), pltpu.VMEM((R, 1), F32), pltpu.VMEM((R, KV_WIDTH), F32)],
        compiler_params=_params(2),
        name="nsa",
    )(q, kcmp, vcmp, ks.reshape(B, S, KV_WIDTH), vs.reshape(B, S, KV_WIDTH), kw.reshape(B, S, KV_WIDTH),
      vw.reshape(B, S, KV_WIDTH), ng, jnp.asarray(ov.astype(np.float32)), jnp.asarray(block_of_key, dtype=BF16), jnp.asarray(eye_pad, dtype=BF16))


def _post_kernel(ym_ref, yn_ref, wm_ref, wn_ref, x_ref, mod_ref, npost_ref, npre_ref, wr_ref, br_ref,
                 x1_ref, h2_ref, route_ref):
    TM = x_ref.shape[0]
    mod = mod_ref[0]
    y = _dot(ym_ref[...], wm_ref[...]) + _dot(yn_ref[...], wn_ref[...])
    x1 = x_ref[...] + mod[2:3] * _rms(y, npost_ref[...])
    x1_ref[...] = x1
    h2 = _rms(x1, npre_ref[...]) * (1.0 + mod[4:5]) + mod[3:4]
    for s in range(SUBLANE):
        h2_ref[pl.ds(s, TM, stride=SUBLANE), :] = h2[:, s * LANE:(s + 1) * LANE]

    logits = _dot(h2, wr_ref[...], precision=HIGHEST) + br_ref[...]
    lane = lax.broadcasted_iota(jnp.int32, (TM, LANE), 1)
    is_grp = lane < MOE_GROUPS
    lg = jnp.where(is_grp, logits, -jnp.inf)
    eg = jnp.exp(lg - jnp.max(lg, axis=1, keepdims=True))
    pg = eg / jnp.sum(eg, axis=1, keepdims=True)
    pg_top = jnp.max(pg, axis=1, keepdims=True)
    grp = jnp.min(jnp.where(is_grp & (pg == pg_top), lane, LANE), axis=1, keepdims=True)
    lo = MOE_GROUPS + EXPERTS_PER_GROUP * grp
    in_grp = (lane >= lo) & (lane < lo + EXPERTS_PER_GROUP)
    le = jnp.where(in_grp, logits, -jnp.inf)
    ee = jnp.exp(le - jnp.max(le, axis=1, keepdims=True))
    pe = jnp.where(in_grp, ee / jnp.sum(ee, axis=1, keepdims=True), -1.0)
    p1 = jnp.max(pe, axis=1, keepdims=True)
    i1 = jnp.min(jnp.where(pe == p1, lane, LANE), axis=1, keepdims=True)
    pe2 = jnp.where(lane == i1, -1.0, pe)
    p2 = jnp.max(pe2, axis=1, keepdims=True)
    i2 = jnp.min(jnp.where((pe2 == p2) & in_grp & (lane != i1), lane, LANE), axis=1, keepdims=True)
    den = p1 + p2
    e1 = (i1 - MOE_GROUPS).astype(F32)
    e2 = (i2 - MOE_GROUPS).astype(F32)
    route_ref[...] = jnp.where(lane == 0, e1, jnp.where(lane == 1, e2, jnp.where(
        lane == 2, pg_top * p1 / den, jnp.where(lane == 3, pg_top * p2 / den, 0.0))))


def _post(ym, yn, w_out, x2, mod3, norm_post, norm_pre, wr, br, S):
    T, D = x2.shape
    TM = min(TOKEN_TILE, S)
    tiles_per_seq = S // TM
    row = lambda w: pl.BlockSpec((TM, w), lambda i: (i, 0))
    return pl.pallas_call(
        _post_kernel,
        grid=(T // TM,),
        in_specs=[row(M_WIDTH), row(N_WIDTH), _full((M_WIDTH, D)), _full((N_WIDTH, D)), row(D),
                  pl.BlockSpec((1, 6, D), lambda i: (i // tiles_per_seq, 0, 0)), _full((1, D)), _full((1, D)),
                  _full((D, LANE)), _full((1, LANE))],
        out_specs=[row(D), pl.BlockSpec((TM * SUBLANE, LANE), lambda i: (i, 0)), row(LANE)],
        out_shape=[jax.ShapeDtypeStruct((T, D), F32), jax.ShapeDtypeStruct((T * SUBLANE, LANE), F32),
                   jax.ShapeDtypeStruct((T, LANE), F32)],
        compiler_params=_params(1),
        name="post_mix_router",
    )(ym, yn, w_out[:M_WIDTH].astype(BF16), w_out[M_WIDTH:].astype(BF16), x2, mod3,
      norm_post[None, :], norm_pre[None, :], wr, br)


def _one_hots(route, lane):
    return lane == route[:, 0:1].astype(jnp.int32), lane == route[:, 1:2].astype(jnp.int32)


def _rank_kernel(route_ref, ltri_ref, rank_ref, cnt_ref, carry):
    TM = route_ref.shape[0]

    @pl.when(pl.program_id(0) == 0)
    def _():
        carry[...] = jnp.zeros_like(carry)

    lane = lax.broadcasted_iota(jnp.int32, (TM, LANE), 1)
    oh0, oh1 = _one_hots(route_ref[...], lane)
    oh = jnp.where(oh0, 1.0, 0.0) + jnp.where(oh1, 1.0, 0.0)
    before = _dot(ltri_ref[...], oh.astype(BF16)) + carry[0:1, :]
    r0 = jnp.sum(jnp.where(oh0, before, 0.0), axis=1, keepdims=True)
    r1 = jnp.sum(jnp.where(oh1, before, 0.0), axis=1, keepdims=True)
    rank_ref[...] = jnp.where(lane == 0, r0, jnp.where(lane == 1, r1, 0.0))
    carry[...] = carry[...] + jnp.sum(oh, axis=0, keepdims=True)
    cnt_ref[...] = carry[...]


def _plan_kernel(cnt_ref, pstart_ref, blk_ref, tot_ref, *, rows_per_block):
    lane = lax.broadcasted_iota(jnp.int32, (SUBLANE, LANE), 1)
    nblk = (cnt_ref[...].astype(jnp.int32) + (rows_per_block - 1)) // rows_per_block
    end = nblk
    sh = 1
    while sh < N_EXPERTS:
        end = end + jnp.where(lane >= sh, pltpu.roll(end, sh, 1), 0)
        sh *= 2
    pstart_ref[...] = ((end - nblk) * rows_per_block).astype(F32)
    nbp = blk_ref.shape[0]
    blk_io = lax.broadcasted_iota(jnp.int32, (nbp, LANE), 0)
    lane_b = lax.broadcasted_iota(jnp.int32, (nbp, LANE), 1)
    passed = jnp.where((lane_b < N_EXPERTS) & (blk_io >= end[0:1, :]), 1, 0)
    blk_ref[...] = jnp.broadcast_to(jnp.minimum(jnp.sum(passed, axis=1, keepdims=True), N_EXPERTS - 1), (nbp, LANE))
    tot_ref[...] = jnp.broadcast_to(
        jnp.sum(jnp.where(lane == N_EXPERTS - 1, end, 0), axis=1, keepdims=True), (SUBLANE, LANE))


def _dest_kernel(route_ref, rank_ref, pstart_ref, dest_ref):
    TM = route_ref.shape[0]
    lane = lax.broadcasted_iota(jnp.int32, (TM, LANE), 1)
    oh0, oh1 = _one_hots(route_ref[...], lane)
    ps = pstart_ref[0:1, :]
    rank = rank_ref[...]
    d0 = jnp.sum(jnp.where(oh0, ps, 0.0), axis=1, keepdims=True) + rank[:, 0:1]
    d1 = jnp.sum(jnp.where(oh1, ps, 0.0), axis=1, keepdims=True) + rank[:, 1:2]
    dest_ref[...] = jnp.where(lane == 0, d0, jnp.where(lane == 1, d1, 0.0)).astype(jnp.int32)


def _sort_plan(route, n_blocks, S):
    T = route.shape[0]
    TM = min(TOKEN_TILE, S)
    row = pl.BlockSpec((TM, LANE), lambda i: (i, 0))
    small = pl.BlockSpec((SUBLANE, LANE), lambda i: (0, 0))
    ltri = jnp.asarray(np.tril(np.ones((TM, TM), np.float32), -1), dtype=BF16)
    rank, cnt = pl.pallas_call(
        _rank_kernel,
        grid=(T // TM,),
        in_specs=[row, _full((TM, TM))],
        out_specs=[row, small],
        out_shape=[jax.ShapeDtypeStruct((T, LANE), F32), jax.ShapeDtypeStruct((SUBLANE, LANE), F32)],
        scratch_shapes=[pltpu.VMEM((SUBLANE, LANE), F32)],
        compiler_params=_params(1),
        name="expert_rank",
    )(route, ltri)
    nbp = -(-n_blocks // SUBLANE) * SUBLANE
    pstart, blk, tot = pl.pallas_call(
        functools.partial(_plan_kernel, rows_per_block=EXPERT_ROWS),
        grid=(1,),
        in_specs=[small],
        out_specs=[small, _full((nbp, LANE)), small],
        out_shape=[jax.ShapeDtypeStruct((SUBLANE, LANE), F32), jax.ShapeDtypeStruct((nbp, LANE), jnp.int32),
                   jax.ShapeDtypeStruct((SUBLANE, LANE), jnp.int32)],
        compiler_params=_params(1),
        name="expert_plan",
    )(cnt)
    dest = pl.pallas_call(
        _dest_kernel,
        grid=(T // TM,),
        in_specs=[row, row, small],
        out_specs=row,
        out_shape=jax.ShapeDtypeStruct((T, LANE), jnp.int32),
        compiler_params=_params(1),
        name="expert_dest",
    )(route, rank, pstart)
    return dest[:, :2], blk[:n_blocks, 0], tot[0, :1]


def _row_copy(src, src_row, dst, dst_row, sem):
    return pltpu.make_async_copy(src.at[pl.ds(pl.multiple_of(src_row * SUBLANE, SUBLANE), SUBLANE)],
                                 dst.at[pl.ds(pl.multiple_of(dst_row * SUBLANE, SUBLANE), SUBLANE)], sem)


def _dispatch_kernel(dest_ref, h2_ref, xs_in, xs_hbm, sem):
    del xs_in
    n = dest_ref.shape[2] // 2

    def issue(j, carry):
        for k in range(2):
            _row_copy(h2_ref, j, xs_hbm, dest_ref[0, 0, 2 * j + k], sem).start()
        return carry

    lax.fori_loop(0, n, issue, 0)

    def drain(j, carry):
        for k in range(2):
            _row_copy(h2_ref, j, xs_hbm, dest_ref[0, 0, 2 * j + k], sem).wait()
        return carry

    lax.fori_loop(0, n, drain, 0)


def _dispatch(dest3, h2t, n_rows):
    nt, _, two_n = dest3.shape
    xs0 = jnp.zeros((n_rows * SUBLANE, LANE), F32)
    return pl.pallas_call(
        _dispatch_kernel,
        grid=(nt,),
        in_specs=[pl.BlockSpec((1, 1, two_n), lambda i: (i, 0, 0), memory_space=pltpu.SMEM),
                  pl.BlockSpec((two_n // 2 * SUBLANE, LANE), lambda i: (i, 0)), pl.BlockSpec(memory_space=pl.ANY)],
        out_specs=pl.BlockSpec(memory_space=pl.ANY),
        out_shape=jax.ShapeDtypeStruct(xs0.shape, F32),
        scratch_shapes=[pltpu.SemaphoreType.DMA(())],
        input_output_aliases={2: 0},
        compiler_params=_params(1),
        name="dispatch",
    )(dest3, h2t, xs0)


def _expert_kernel(be_ref, nb_ref, xs_ref, w1_ref, w3_ref, w2_ref, ys_ref):
    del be_ref
    RB = xs_ref.shape[0] // SUBLANE

    @pl.when(pl.program_id(0) < nb_ref[0])
    def _():
        x = jnp.concatenate([xs_ref[pl.ds(s, RB, stride=SUBLANE), :] for s in range(SUBLANE)], axis=1).astype(BF16)
        hb = _silu(_dot(x, w1_ref[0])) * _dot(x, w3_ref[0])
        y = _dot(hb.astype(BF16), w2_ref[0])
        for s in range(SUBLANE):
            ys_ref[pl.ds(s, RB, stride=SUBLANE), :] = y[:, s * LANE:(s + 1) * LANE]

    @pl.when(pl.program_id(0) >= nb_ref[0])
    def _():
        ys_ref[...] = jnp.zeros_like(ys_ref)


def _experts(block_expert, n_used, xs, w1, w3, w2):
    n_blocks = block_expert.shape[0]
    RB = EXPERT_ROWS
    D, F = w1.shape[1], w1.shape[2]
    cur = lambda i, nb: jnp.minimum(i, nb[0] - 1)
    rows = pl.BlockSpec((RB * SUBLANE, LANE), lambda i, be, nb: (cur(i, nb), 0))
    wspec = lambda a, b: pl.BlockSpec((1, a, b), lambda i, be, nb: (be[cur(i, nb)], 0, 0))
    return pl.pallas_call(
        _expert_kernel,
        grid_spec=pltpu.PrefetchScalarGridSpec(
            num_scalar_prefetch=2, grid=(n_blocks,),
            in_specs=[rows, wspec(D, F), wspec(D, F), wspec(F, D)],
            out_specs=pl.BlockSpec((RB * SUBLANE, LANE), lambda i, be, nb: (i, 0))),
        out_shape=jax.ShapeDtypeStruct(xs.shape, F32),
        compiler_params=_params(1),
        name="experts",
    )(block_expert, n_used, xs, w1, w3, w2)


def _combine_kernel(dest_ref, ys_hbm, x1_ref, route_ref, mod_ref, nw_ref, o_ref, buf, sem):
    n = x1_ref.shape[0]

    def issue(j, carry):
        for k in range(2):
            _row_copy(ys_hbm, dest_ref[0, 0, 2 * j + k], buf, k * n + j, sem).start()
        return carry

    lax.fori_loop(0, n, issue, 0)

    def drain(j, carry):
        for k in range(2):
            _row_copy(ys_hbm, dest_ref[0, 0, 2 * j + k], buf, k * n + j, sem).wait()
        return carry

    lax.fori_loop(0, n, drain, 0)

    route = route_ref[...]
    g0, g1 = route[:, 2:3], route[:, 3:4]
    y = jnp.concatenate(
        [g0 * buf[pl.ds(s, n, stride=SUBLANE), :] + g1 * buf[pl.ds(n * SUBLANE + s, n, stride=SUBLANE), :]
         for s in range(SUBLANE)], axis=1)
    o_ref[...] = x1_ref[...] + mod_ref[0][5:6] * _rms(y, nw_ref[...])


def _combine(dest3, ys, x1, route, mod3, norm_w, S):
    T, D = x1.shape
    nt, _, two_n = dest3.shape
    n = two_n // 2
    tiles_per_seq = S // n
    return pl.pallas_call(
        _combine_kernel,
        grid=(nt,),
        in_specs=[pl.BlockSpec((1, 1, two_n), lambda i: (i, 0, 0), memory_space=pltpu.SMEM),
                  pl.BlockSpec(memory_space=pl.ANY), pl.BlockSpec((n, D), lambda i: (i, 0)),
                  pl.BlockSpec((n, LANE), lambda i: (i, 0)),
                  pl.BlockSpec((1, 6, D), lambda i: (i // tiles_per_seq, 0, 0)), _full((1, D))],
        out_specs=pl.BlockSpec((n, D), lambda i: (i, 0)),
        out_shape=jax.ShapeDtypeStruct((T, D), F32),
        scratch_shapes=[pltpu.VMEM((2 * n * SUBLANE, LANE), F32), pltpu.SemaphoreType.DMA(())],
        compiler_params=_params(1),
        name="combine",
    )(dest3, ys, x1, route, mod3, norm_w[None, :])


def _layer(x, c, positions, ada_w, ada_b, norm_mix_pre, norm_mix_post, norm_ffn_pre, norm_ffn_post, w_in,
           conv_w, gate_b, head_norm, cmp_k, cmp_v, w_out, wg, bg, we, be, w1, w3, w2):
    B, S, D = x.shape
    T = B * S
    x2 = x.reshape(T, D)
    mod3 = _adaln(c, ada_w, ada_b).reshape(B, 6, D)

    o_mi = 4 * M_WIDTH
    o_nq = o_mi + 2 * M_HEADS
    o_kv = o_nq + N_WIDTH
    kv = lambda i: w_in[:, o_kv + i * KV_WIDTH:o_kv + (i + 1) * KV_WIDTH]
    o_ng = o_kv + 6 * KV_WIDTH
    w_main = jnp.concatenate(
        [w_in[:, :o_mi], w_in[:, o_nq:o_kv], kv(0), kv(2), kv(4), kv(1), kv(3), kv(5), w_in[:, o_ng:],
         jnp.zeros((D, LANE - 3 * N_HEADS), F32)], axis=1).astype(BF16)
    wg_t = w_in[:, o_mi:o_nq].T
    half = N_HEAD_DIM // 2
    inv = ROPE_THETA ** (-jnp.arange(half, dtype=F32) / half)
    inv_row = jnp.tile(inv, LANE // half)[None, :]
    pos_col = positions.astype(F32).reshape(T, 1)

    zm, q, kc, ks, kw, vc, vs, vw, ng, gt = _inproj(x2, mod3, norm_mix_pre, w_main, wg_t, gate_b, pos_col, inv_row, S)
    ym = _mlstm(zm, gt, conv_w, head_norm, B, S)
    kcmp, vcmp = _compress(kc, vc, _compress_weights(*cmp_k), _compress_weights(*cmp_v), B, S)
    yn = _nsa(q, kcmp, vcmp, ks, vs, kw, vw, ng, B, S)

    wr = jnp.concatenate([wg, we, jnp.zeros((D, LANE - MOE_GROUPS - N_EXPERTS), F32)], axis=1)
    br = jnp.concatenate([bg, be, jnp.zeros((LANE - MOE_GROUPS - N_EXPERTS,), F32)])[None, :]
    x1, h2t, route = _post(ym, yn, w_out, x2, mod3, norm_mix_post, norm_ffn_pre, wr, br, S)

    n_blocks = (2 * T) // EXPERT_ROWS + N_EXPERTS
    dest, block_expert, n_used = _sort_plan(route, n_blocks, S)
    n_tok = min(GATHER_TILE, S)
    dest3 = dest.reshape(T // n_tok, 1, 2 * n_tok)
    xs = _dispatch(dest3, h2t, n_blocks * EXPERT_ROWS)
    ys = _experts(block_expert, n_used, xs, w1.astype(BF16), w3.astype(BF16), w2.astype(BF16))
    out = _combine(dest3, ys, x1, route, mod3, norm_ffn_post, S)
    return out.reshape(B, S, D)


def kernel(x, c, positions, ada_w, ada_b, norm_mix_pre, norm_mix_post, norm_ffn_pre, norm_ffn_post, w_in, mlstm_conv_w, mlstm_gate_b, mlstm_head_norm, cmp_pe_k, cmp_w1_k, cmp_w2_k, cmp_pe_v, cmp_w1_v, cmp_w2_v, w_out, router_grp_w, router_grp_b, router_exp_w, router_exp_b, expert_w1, expert_w3, expert_w2):
    for l in range(ada_w.shape[0]):
        x = _layer(x, c, positions, ada_w[l], ada_b[l], norm_mix_pre[l], norm_mix_post[l], norm_ffn_pre[l],
                   norm_ffn_post[l], w_in[l], mlstm_conv_w[l], mlstm_gate_b[l], mlstm_head_norm[l],
                   (cmp_pe_k[l], cmp_w1_k[l], cmp_w2_k[l]), (cmp_pe_v[l], cmp_w1_v[l], cmp_w2_v[l]), w_out[l],
                   router_grp_w[l], router_grp_b[l], router_exp_w[l], router_exp_b[l],
                   expert_w1[l], expert_w3[l], expert_w2[l])
    return x
```

```python
import functools

import jax
import jax.numpy as jnp
import numpy as np
from jax import lax
from jax.experimental import pallas as pl
from jax.experimental.pallas import tpu as pltpu

M_HEADS = 4
M_HEAD_DIM = 128
M_WIDTH = M_HEADS * M_HEAD_DIM
M_CONV = 4
N_HEADS = 8
N_KV_GROUPS = 2
HEADS_PER_GROUP = N_HEADS // N_KV_GROUPS
N_HEAD_DIM = 64
N_WIDTH = N_HEADS * N_HEAD_DIM
KV_WIDTH = N_KV_GROUPS * N_HEAD_DIM
CMP_BLOCK = 32
CMP_STRIDE = 16
CMP_HIDDEN = 2 * N_HEAD_DIM
SLC_BLOCK = 64
SLC_TOPK = 16
WINDOW = 512
ROPE_THETA = 10000.0
MOE_GROUPS = 4
EXPERTS_PER_GROUP = 8
N_EXPERTS = MOE_GROUPS * EXPERTS_PER_GROUP
RMS_EPS = 1e-6

LANE = 128
SUBLANE = 8
VMEM_LIMIT_BYTES = 56 * 1024 * 1024

TOKEN_TILE = 512
MLSTM_CHUNK = 256
NSA_Q_TILE = 128
NSA_K_TILE = 512
NSA_COL_BLOCK = 256
EXPERT_ROWS = 256
GATHER_TILE = 256

F32 = jnp.float32
BF16 = jnp.bfloat16
NEG = -1e30
HIGHEST = lax.Precision.HIGHEST
NT_DIMS = (((1,), (1,)), ((), ()))
TN_DIMS = (((0,), (0,)), ((), ()))


def _params(n_grid):
    return pltpu.CompilerParams(
        dimension_semantics=("arbitrary",) * n_grid, vmem_limit_bytes=VMEM_LIMIT_BYTES)


def _dot(a, b, **kw):
    return jnp.dot(a, b, preferred_element_type=F32, **kw)


def _dot_nt(a, b, **kw):
    return lax.dot_general(a, b, NT_DIMS, preferred_element_type=F32, **kw)


def _dot_tn(a, b, **kw):
    return lax.dot_general(a, b, TN_DIMS, preferred_element_type=F32, **kw)


def _sigmoid(x):
    return 1.0 / (1.0 + jnp.exp(-x))


def _silu(x):
    return x * _sigmoid(x)


def _rms(x, w):
    return x * lax.rsqrt(jnp.mean(x * x, axis=-1, keepdims=True) + RMS_EPS) * w


def _full(shape):
    return pl.BlockSpec(shape, lambda *_: (0,) * len(shape))


def _adaln_kernel(c_ref, w_ref, b_ref, o_ref):
    o_ref[...] = _dot(_silu(c_ref[...]), w_ref[...], precision=HIGHEST) + b_ref[...]


def _adaln(c, ada_w, ada_b):
    B, D = c.shape
    n = ada_w.shape[1] // D
    return pl.pallas_call(
        _adaln_kernel,
        grid=(n,),
        in_specs=[_full((B, D)), pl.BlockSpec((D, D), lambda j: (0, j)), pl.BlockSpec((1, D), lambda j: (0, j))],
        out_specs=pl.BlockSpec((B, D), lambda j: (0, j)),
        out_shape=jax.ShapeDtypeStruct((B, ada_w.shape[1]), F32),
        compiler_params=_params(1),
        name="adaln",
    )(c, ada_w, ada_b[None, :])


_C_M = 0
_C_Q = 4 * M_WIDTH
_C_KC = _C_Q + N_WIDTH
_C_KS = _C_KC + KV_WIDTH
_C_KW = _C_KS + KV_WIDTH
_C_VC = _C_KW + KV_WIDTH
_C_END = _C_VC + KV_WIDTH
_R_VS = 0
_R_VW = KV_WIDTH
_R_NG = 2 * KV_WIDTH
_R_END = _R_NG + 4 * SUBLANE


def _inproj_kernel(x_ref, mod_ref, nw_ref, w_ref, wt_ref, wg_ref, gb_ref, pos_ref, inv_ref,
                   zm_ref, q_ref, kc_ref, ks_ref, kw_ref, vc_ref, vst_ref, vwt_ref, ngt_ref, gt_ref):
    mod = mod_ref[0]
    h = _rms(x_ref[...], nw_ref[...]) * (1.0 + mod[1:2]) + mod[0:1]
    hb = h.astype(BF16)
    z = _dot(hb, w_ref[...])
    zm_ref[...] = z[:, _C_M:_C_Q]
    gt_ref[...] = _dot_nt(wg_ref[...], h, precision=HIGHEST) + gb_ref[...]
    zt = _dot_nt(wt_ref[...], hb)
    vst_ref[...] = zt[_R_VS:_R_VW].astype(BF16)
    vwt_ref[...] = zt[_R_VW:_R_NG].astype(BF16)
    ngt_ref[...] = _sigmoid(zt[_R_NG:_R_END])

    ang = pos_ref[...] * inv_ref[...]
    cos = jnp.cos(ang)
    sin = jnp.sin(ang)
    lane = lax.broadcasted_iota(jnp.int32, (1, LANE), 1)
    first = (lane % N_HEAD_DIM) < (N_HEAD_DIM // 2)
    sin_signed = jnp.where(first, -sin, sin)

    def rope(slab):
        rot = jnp.where(first, pltpu.roll(slab, LANE - N_HEAD_DIM // 2, 1), pltpu.roll(slab, N_HEAD_DIM // 2, 1))
        return slab * cos + rot * sin_signed

    scale = N_HEAD_DIM ** -0.5
    for j in range(N_WIDTH // LANE):
        q_ref[:, j * LANE:(j + 1) * LANE] = (rope(z[:, _C_Q + j * LANE:_C_Q + (j + 1) * LANE]) * scale).astype(BF16)
    kc_ref[...] = rope(z[:, _C_KC:_C_KS])
    ks_ref[...] = rope(z[:, _C_KS:_C_KW]).astype(BF16)
    kw_ref[...] = rope(z[:, _C_KW:_C_VC]).astype(BF16)
    vc_ref[...] = z[:, _C_VC:_C_END]


def _inproj(x2, mod3, norm_w, w_main, w_t, wg_t, gate_b, pos_col, inv_row, S):
    T, D = x2.shape
    TM = min(TOKEN_TILE, S)
    tiles_per_seq = S // TM
    row = lambda w: pl.BlockSpec((TM, w), lambda i: (i, 0))
    col = lambda r: pl.BlockSpec((r, TM), lambda i: (0, i))
    outs = [(4 * M_WIDTH, F32), (N_WIDTH, BF16), (KV_WIDTH, F32), (KV_WIDTH, BF16), (KV_WIDTH, BF16), (KV_WIDTH, F32)]
    outs_t = [(KV_WIDTH, BF16), (KV_WIDTH, BF16), (_R_END - _R_NG, F32), (2 * M_HEADS, F32)]
    return pl.pallas_call(
        _inproj_kernel,
        grid=(T // TM,),
        in_specs=[row(D), pl.BlockSpec((1, 6, D), lambda i: (i // tiles_per_seq, 0, 0)), _full((1, D)),
                  _full(w_main.shape), _full(w_t.shape), _full(wg_t.shape), _full((2 * M_HEADS, 1)), row(1),
                  _full((1, LANE))],
        out_specs=[row(w) for w, _ in outs] + [col(r) for r, _ in outs_t],
        out_shape=[jax.ShapeDtypeStruct((T, w), dt) for w, dt in outs]
        + [jax.ShapeDtypeStruct((r, T), dt) for r, dt in outs_t],
        compiler_params=_params(1),
        name="inproj",
    )(x2, mod3, norm_w[None, :], w_main, w_t, wg_t, gate_b[:, None], pos_col, inv_row)


def _mlstm_kernel(zm_ref, gt_ref, cw_ref, hn_ref, ltri_ref, eye_ref, y_ref, buf, c_s, n_s, m_s):
    L = zm_ref.shape[0]
    QK = 2 * M_WIDTH
    DH = M_HEAD_DIM

    @pl.when(pl.program_id(1) == 0)
    def _():
        buf[0:SUBLANE, :] = jnp.zeros((SUBLANE, QK), F32)
        c_s[...] = jnp.zeros_like(c_s)
        n_s[...] = jnp.zeros_like(n_s)
        m_s[...] = jnp.zeros_like(m_s)

    buf[SUBLANE:SUBLANE + L, :] = zm_ref[:, 0:QK]
    cw = cw_ref[...]
    conv = cw[M_CONV - 1:M_CONV] * buf[SUBLANE:SUBLANE + L, :]
    for j in range(M_CONV - 1):
        off = SUBLANE - (M_CONV - 1) + j
        conv = conv + cw[j:j + 1] * buf[off:off + L, :]
    buf[0:SUBLANE, :] = buf[L:L + SUBLANE, :]
    qk = _silu(conv)

    g = gt_ref[...]
    fp = g[M_HEADS:]
    lf = jnp.minimum(fp, 0.0) - jnp.log(1.0 + jnp.exp(-jnp.abs(fp)))
    lf8 = jnp.concatenate([lf, jnp.zeros_like(lf)], axis=0)
    ltri = ltri_ref[...]
    b_rows = _dot_nt(lf8, ltri, precision=HIGHEST)
    b_cols = _dot_nt(ltri, lf8, precision=HIGHEST)
    i_cols = _dot_nt(eye_ref[...], g, precision=HIGHEST)

    causal = lax.broadcasted_iota(jnp.int32, (L, L), 0) >= lax.broadcasted_iota(jnp.int32, (L, L), 1)
    for h in range(M_HEADS):
        hs = slice(h * DH, (h + 1) * DH)
        b_r, i_r = b_rows[h:h + 1], g[h:h + 1]
        b_c, i_c = b_cols[:, h:h + 1], i_cols[:, h:h + 1]
        m_prev = m_s[h:h + 1, 0:1]
        g_tot = b_r[:, L - 1:L]
        m_loc = jnp.max(g_tot - b_r + i_r, axis=1, keepdims=True)
        m_new = jnp.maximum(g_tot + m_prev, m_loc)

        q = qk[:, hs]
        k = qk[:, M_WIDTH + h * DH:M_WIDTH + (h + 1) * DH] * (DH ** -0.5)
        v = zm_ref[:, 2 * M_WIDTH + h * DH:2 * M_WIDTH + (h + 1) * DH]
        o = zm_ref[:, 3 * M_WIDTH + h * DH:3 * M_WIDTH + (h + 1) * DH]
        qb, kb, vb = q.astype(BF16), k.astype(BF16), v.astype(BF16)

        dlog = jnp.where(causal, b_c - b_r + i_r, -jnp.inf)
        inter = b_c + m_prev
        m_t = jnp.maximum(inter, jnp.max(dlog, axis=1, keepdims=True))
        wts = jnp.exp(dlog - m_t) * _dot_nt(qb, kb)
        dec = jnp.exp(inter - m_t)
        c_prev = c_s[h]
        n_prev = n_s[h:h + 1]
        num = _dot(wts.astype(BF16), vb) + dec * _dot(qb, c_prev.astype(BF16))
        den = jnp.sum(wts, axis=1, keepdims=True) + dec * jnp.sum(q * n_prev, axis=1, keepdims=True)
        hh = num / jnp.maximum(jnp.abs(den), jnp.exp(-m_t))

        kw = k * jnp.exp(g_tot - b_c + i_c - m_new)
        keep = jnp.exp(g_tot + m_prev - m_new)
        c_s[h] = keep * c_prev + _dot_tn(kw.astype(BF16), vb)
        n_s[h:h + 1] = keep * n_prev + jnp.sum(kw, axis=0, keepdims=True)
        m_s[h:h + 1] = jnp.broadcast_to(m_new, (1, LANE))

        hn = _rms(hh, hn_ref[:, hs])
        y_ref[:, hs] = (_sigmoid(o) * hn).astype(BF16)


def _mlstm(zm, gt, conv_w, head_norm, B, S):
    T = zm.shape[0]
    L = min(MLSTM_CHUNK, S)
    nc = S // L
    ltri = jnp.asarray(np.tril(np.ones((L, L), np.float32)))
    eye = jnp.asarray(np.eye(L, dtype=np.float32))
    return pl.pallas_call(
        _mlstm_kernel,
        grid=(B, nc),
        in_specs=[pl.BlockSpec((L, 4 * M_WIDTH), lambda b, c: (b * nc + c, 0)),
                  pl.BlockSpec((2 * M_HEADS, L), lambda b, c: (0, b * nc + c)),
                  _full((M_CONV, 2 * M_WIDTH)), _full((1, M_WIDTH)), _full((L, L)), _full((L, L))],
        out_specs=pl.BlockSpec((L, M_WIDTH), lambda b, c: (b * nc + c, 0)),
        out_shape=jax.ShapeDtypeStruct((T, M_WIDTH), BF16),
        scratch_shapes=[pltpu.VMEM((L + SUBLANE, 2 * M_WIDTH), F32),
                        pltpu.VMEM((M_HEADS, M_HEAD_DIM, M_HEAD_DIM), F32),
                        pltpu.VMEM((SUBLANE, M_HEAD_DIM), F32),
                        pltpu.VMEM((SUBLANE, LANE), F32)],
        compiler_params=_params(2),
        name="mlstm",
    )(zm, gt, conv_w, head_norm[None, :], ltri, eye)


def _compress_kernel(k_ref, v_ref, pak, pbk, wak, wbk, w2k, pav, pbv, wav, wbv, w2v, ko_ref, vo_ref):
    def one(x_ref, pa, pb, wa, wb, w2, o_ref, transposed):
        x = x_ref[0]
        n = x.shape[0]
        first = _dot((x + pa[...]).astype(BF16), wa[...])
        second = _dot((x + pb[...]).astype(BF16), wb[...])
        hid = _silu(first + pltpu.roll(second, n - 1, 0)).astype(BF16)
        o_ref[0] = (_dot_nt(w2[...], hid) if transposed else _dot(hid, w2[...])).astype(BF16)

    one(k_ref, pak, pbk, wak, wbk, w2k, ko_ref, False)
    one(v_ref, pav, pbv, wav, wbv, w2v, vo_ref, True)


def _compress_weights(pe, w1, w2):
    half = CMP_BLOCK // 2
    eye = jnp.eye(N_KV_GROUPS, dtype=F32)
    w1r = w1.reshape(CMP_BLOCK, N_HEAD_DIM, CMP_HIDDEN)
    big = lambda w: jnp.einsum("ldc,gh->lgdhc", w, eye).reshape(half * KV_WIDTH, N_KV_GROUPS * CMP_HIDDEN).astype(BF16)
    pe_row = lambda p: jnp.broadcast_to(p[:, None, :], (half, N_KV_GROUPS, N_HEAD_DIM)).reshape(1, half * KV_WIDTH)
    w2bd = jnp.einsum("cd,gh->gchd", w2, eye).reshape(N_KV_GROUPS * CMP_HIDDEN, KV_WIDTH).astype(BF16)
    return pe_row(pe[:half]), pe_row(pe[half:]), big(w1r[:half]), big(w1r[half:]), w2bd


def _compress(kc, vc, wk, wv, B, S):
    nc = S // CMP_STRIDE
    width = CMP_STRIDE * KV_WIDTH
    blk = pl.BlockSpec((1, nc, width), lambda b: (b, 0, 0))
    wv = wv[:4] + (wv[4].T,)
    return pl.pallas_call(
        _compress_kernel,
        grid=(B,),
        in_specs=[blk, blk] + [_full(w.shape) for w in wk] + [_full(w.shape) for w in wv],
        out_specs=[pl.BlockSpec((1, nc, KV_WIDTH), lambda b: (b, 0, 0)),
                   pl.BlockSpec((1, KV_WIDTH, nc), lambda b: (b, 0, 0))],
        out_shape=[jax.ShapeDtypeStruct((B, nc, KV_WIDTH), BF16), jax.ShapeDtypeStruct((B, KV_WIDTH, nc), BF16)],
        compiler_params=_params(1),
        name="compress",
    )(kc.reshape(B, nc, width), vc.reshape(B, nc, width), *wk, *wv)


def _masked_softmax_keys(s, mask):
    s = jnp.where(mask, s, -jnp.inf)
    m = jnp.max(s, axis=0, keepdims=True)
    m = jnp.where(jnp.isfinite(m), m, 0.0)
    e = jnp.exp(s - m)
    return e, 1.0 / jnp.maximum(jnp.sum(e, axis=0, keepdims=True), 1e-30)


def _nsa_kernel(q_ref, kc_ref, vct_ref, ks_ref, vst_ref, kw_ref, vwt_ref, ngt_ref, ov_ref, ek_ref, eye_ref,
                y_ref, m_s, l_s, acc_s, *, n_top, TK):
    TQ = q_ref.shape[0]
    NS = ov_ref.shape[0]
    G, HPG, DH = N_KV_GROUPS, HEADS_PER_GROUP, N_HEAD_DIM
    R = G * HPG * TQ
    q0 = pl.program_id(1) * TQ

    q = q_ref[...]
    zero = jnp.zeros((TQ, DH), BF16)
    parts = []
    for hd in range(N_HEADS):
        qh = q[:, hd * DH:(hd + 1) * DH]
        parts.append(jnp.concatenate([qh, zero] if hd < HPG else [zero, qh], axis=1))
    qp = jnp.concatenate(parts, axis=0)
    t_q = q0 + lax.broadcasted_iota(jnp.int32, (1, R), 1) % TQ

    kc = kc_ref[0]
    NC = kc.shape[0]
    cmp_end = lax.broadcasted_iota(jnp.int32, (NC, 1), 0) * CMP_STRIDE + (CMP_BLOCK - 1)
    e_c, inv_c = _masked_softmax_keys(_dot_nt(kc, qp), cmp_end <= t_q)
    p_c = e_c * inv_c
    o_c = _dot(vct_ref[0], p_c.astype(BF16))
    p_grp = jnp.concatenate(
        [sum(p_c[:, (g * HPG + h) * TQ:(g * HPG + h + 1) * TQ] for h in range(HPG)) for g in range(G)], axis=1)
    imp = _dot(ov_ref[...], p_grp, precision=HIGHEST)

    j_io = lax.broadcasted_iota(jnp.int32, (NS, G * TQ), 0)
    t_row = q0 + lax.broadcasted_iota(jnp.int32, (NS, G * TQ), 1) % TQ
    cur = t_row // SLC_BLOCK
    forced = (j_io == 0) | (j_io == cur) | (j_io == cur - 1)
    future = j_io * SLC_BLOCK > t_row
    score = jnp.where(forced, jnp.inf, jnp.where(future, -jnp.inf, imp))
    rank = jnp.zeros((NS, G * TQ), F32)
    for jp in range(NS):
        other = score[jp:jp + 1, :]
        rank = rank + jnp.where(j_io > jp, jnp.where(other >= score, 1.0, 0.0), jnp.where(other > score, 1.0, 0.0))
    sel_bias = jnp.where(future, NEG, jnp.where(rank < n_top, 0.0, NEG)).astype(BF16)
    sel_rows = _dot_tn(sel_bias, eye_ref[...]).astype(BF16)
    sel_rows = jnp.broadcast_to(sel_rows.reshape(G, 1, TQ, LANE), (G, HPG, TQ, LANE)).reshape(R, LANE)
    qa = jnp.concatenate([qp, sel_rows], axis=1)

    m_s[...] = jnp.full_like(m_s, NEG)
    l_s[...] = jnp.zeros_like(l_s)
    acc_s[...] = jnp.zeros_like(acc_s)

    CB = NSA_COL_BLOCK

    def step(kt, causal):
        k0 = pl.multiple_of(kt * TK, TK)
        ka = jnp.concatenate([ks_ref[0, pl.ds(k0, TK), :], ek_ref[pl.ds(k0, TK), :]], axis=1)
        vt = vst_ref[:, pl.ds(k0, TK)]
        kpos = k0 + lax.broadcasted_iota(jnp.int32, (TK, 1), 0)
        scores = [_dot_nt(ka, qa[j * CB:(j + 1) * CB]) for j in range(R // CB)]
        probs, alphas = [], []
        for j in range(R // CB):
            cols = slice(j * CB, (j + 1) * CB)
            s = scores[j]
            if causal:
                s = jnp.where(kpos <= t_q[:, cols], s, NEG)
            m_old = m_s[:, cols]
            m_new = jnp.maximum(m_old, jnp.max(s, axis=0, keepdims=True))
            alpha = jnp.exp(m_old - m_new)
            p = jnp.exp(s - m_new)
            l_s[:, cols] = alpha * l_s[:, cols] + jnp.sum(p, axis=0, keepdims=True)
            m_s[:, cols] = m_new
            probs.append(p.astype(BF16))
            alphas.append(alpha)
        for j in range(R // CB):
            cols = slice(j * CB, (j + 1) * CB)
            acc_s[:, cols] = alphas[j] * acc_s[:, cols] + _dot(vt, probs[j])

    def full_tile(kt, carry):
        step(kt, False)
        return carry

    last = q0 // TK
    lax.fori_loop(0, last, full_tile, 0)
    step(last, True)
    o_s = acc_s[...] * (1.0 / l_s[...])

    WK = WINDOW + TQ
    start = pl.multiple_of(jnp.maximum(q0 - WINDOW, 0), TQ)
    wpos = start + lax.broadcasted_iota(jnp.int32, (WK, 1), 0)
    e_w, inv_w = _masked_softmax_keys(_dot_nt(kw_ref[0, pl.ds(start, WK), :], qp),
                                      (wpos <= t_q) & (wpos > t_q - WINDOW))
    o_w = _dot(vwt_ref[:, pl.ds(start, WK)], e_w.astype(BF16)) * inv_w

    gates = ngt_ref[...]
    outs = []
    for hd in range(N_HEADS):
        cols = slice(hd * TQ, (hd + 1) * TQ)
        rows = slice((hd // HPG) * DH, (hd // HPG + 1) * DH)
        outs.append(gates[3 * hd:3 * hd + 1] * o_c[rows, cols]
                    + gates[3 * hd + 1:3 * hd + 2] * o_s[rows, cols]
                    + gates[3 * hd + 2:3 * hd + 3] * o_w[rows, cols])
    y_ref[...] = jnp.concatenate(outs, axis=0).T.astype(BF16)


def _nsa(q, kcmp, vcmp_t, ks, vs_t, kw, vw_t, ng_t, B, S):
    T = q.shape[0]
    TQ = min(NSA_Q_TILE, S)
    TK = min(NSA_K_TILE, S)
    nq = S // TQ
    NS = S // SLC_BLOCK
    NC = S // CMP_STRIDE
    n_cmp = (S - CMP_BLOCK) // CMP_STRIDE + 1
    js = np.arange(NS)[:, None] * SLC_BLOCK
    cs = np.arange(NC)[None, :] * CMP_STRIDE
    ov = np.clip(np.minimum(js + SLC_BLOCK, cs + CMP_BLOCK) - np.maximum(js, cs), 0, None) / CMP_STRIDE
    ov[:, n_cmp:] = 0.0
    assert NS <= LANE and TK % TQ == 0
    block_of_key = (np.arange(S)[:, None] // SLC_BLOCK == np.arange(LANE)[None, :]).astype(np.float32)
    eye_pad = np.eye(NS, LANE, dtype=np.float32)
    R = N_HEADS * TQ
    assert R % NSA_COL_BLOCK == 0
    seq = lambda: pl.BlockSpec((1, S, KV_WIDTH), lambda b, i: (b, 0, 0))
    seq_t = lambda: pl.BlockSpec((KV_WIDTH, S), lambda b, i: (0, b))
    return pl.pallas_call(
        functools.partial(_nsa_kernel, n_top=min(SLC_TOPK, NS), TK=TK),
        grid=(B, nq),
        in_specs=[pl.BlockSpec((TQ, N_WIDTH), lambda b, i: (b * nq + i, 0)),
                  pl.BlockSpec((1, NC, KV_WIDTH), lambda b, i: (b, 0, 0)),
                  pl.BlockSpec((1, KV_WIDTH, NC), lambda b, i: (b, 0, 0)),
                  seq(), seq_t(), seq(), seq_t(), pl.BlockSpec((ng_t.shape[0], TQ), lambda b, i: (0, b * nq + i)),
                  _full((NS, NC)), _full((S, LANE)), _full((NS, LANE))],
        out_specs=pl.BlockSpec((TQ, N_WIDTH), lambda b, i: (b * nq + i, 0)),
        out_shape=jax.ShapeDtypeStruct((T, N_WIDTH), BF16),
        scratch_shapes=[pltpu.VMEM((1, R), F32), pltpu.VMEM((1, R), F32), pltpu.VMEM((KV_WIDTH, R), F32)],
        compiler_params=_params(2),
        name="nsa",
    )(q, kcmp, vcmp_t, ks.reshape(B, S, KV_WIDTH), vs_t, kw.reshape(B, S, KV_WIDTH), vw_t, ng_t,
      jnp.asarray(ov.astype(np.float32)), jnp.asarray(block_of_key, dtype=BF16), jnp.asarray(eye_pad, dtype=BF16))


def _post_kernel(ym_ref, yn_ref, wm_ref, wn_ref, x_ref, mod_ref, npost_ref, npre_ref, wr_ref, br_ref,
                 x1_ref, h2_ref, route_ref):
    TM = x_ref.shape[0]
    mod = mod_ref[0]
    y = _dot(ym_ref[...], wm_ref[...]) + _dot(yn_ref[...], wn_ref[...])
    x1 = x_ref[...] + mod[2:3] * _rms(y, npost_ref[...])
    x1_ref[...] = x1
    h2 = _rms(x1, npre_ref[...]) * (1.0 + mod[4:5]) + mod[3:4]
    for s in range(SUBLANE):
        h2_ref[pl.ds(s, TM, stride=SUBLANE), :] = h2[:, s * LANE:(s + 1) * LANE]

    logits = _dot(h2, wr_ref[...], precision=HIGHEST) + br_ref[...]
    lane = lax.broadcasted_iota(jnp.int32, (TM, LANE), 1)
    is_grp = lane < MOE_GROUPS
    lg = jnp.where(is_grp, logits, -jnp.inf)
    eg = jnp.exp(lg - jnp.max(lg, axis=1, keepdims=True))
    pg = eg / jnp.sum(eg, axis=1, keepdims=True)
    pg_top = jnp.max(pg, axis=1, keepdims=True)
    grp = jnp.min(jnp.where(is_grp & (pg == pg_top), lane, LANE), axis=1, keepdims=True)
    lo = MOE_GROUPS + EXPERTS_PER_GROUP * grp
    in_grp = (lane >= lo) & (lane < lo + EXPERTS_PER_GROUP)
    le = jnp.where(in_grp, logits, -jnp.inf)
    ee = jnp.exp(le - jnp.max(le, axis=1, keepdims=True))
    pe = jnp.where(in_grp, ee / jnp.sum(ee, axis=1, keepdims=True), -1.0)
    p1 = jnp.max(pe, axis=1, keepdims=True)
    i1 = jnp.min(jnp.where(pe == p1, lane, LANE), axis=1, keepdims=True)
    pe2 = jnp.where(lane == i1, -1.0, pe)
    p2 = jnp.max(pe2, axis=1, keepdims=True)
    i2 = jnp.min(jnp.where((pe2 == p2) & in_grp & (lane != i1), lane, LANE), axis=1, keepdims=True)
    den = p1 + p2
    e1 = (i1 - MOE_GROUPS).astype(F32)
    e2 = (i2 - MOE_GROUPS).astype(F32)
    route_ref[...] = jnp.where(lane == 0, e1, jnp.where(lane == 1, e2, jnp.where(
        lane == 2, pg_top * p1 / den, jnp.where(lane == 3, pg_top * p2 / den, 0.0))))


def _post(ym, yn, w_out, x2, mod3, norm_post, norm_pre, wr, br, S):
    T, D = x2.shape
    TM = min(TOKEN_TILE, S)
    tiles_per_seq = S // TM
    row = lambda w: pl.BlockSpec((TM, w), lambda i: (i, 0))
    return pl.pallas_call(
        _post_kernel,
        grid=(T // TM,),
        in_specs=[row(M_WIDTH), row(N_WIDTH), _full((M_WIDTH, D)), _full((N_WIDTH, D)), row(D),
                  pl.BlockSpec((1, 6, D), lambda i: (i // tiles_per_seq, 0, 0)), _full((1, D)), _full((1, D)),
                  _full((D, LANE)), _full((1, LANE))],
        out_specs=[row(D), pl.BlockSpec((TM * SUBLANE, LANE), lambda i: (i, 0)), row(LANE)],
        out_shape=[jax.ShapeDtypeStruct((T, D), F32), jax.ShapeDtypeStruct((T * SUBLANE, LANE), F32),
                   jax.ShapeDtypeStruct((T, LANE), F32)],
        compiler_params=_params(1),
        name="post_mix_router",
    )(ym, yn, w_out[:M_WIDTH].astype(BF16), w_out[M_WIDTH:].astype(BF16), x2, mod3,
      norm_post[None, :], norm_pre[None, :], wr, br)


def _one_hots(route, lane):
    return lane == route[:, 0:1].astype(jnp.int32), lane == route[:, 1:2].astype(jnp.int32)


def _rank_kernel(route_ref, ltri_ref, rank_ref, cnt_ref, carry):
    TM = route_ref.shape[0]

    @pl.when(pl.program_id(0) == 0)
    def _():
        carry[...] = jnp.zeros_like(carry)

    lane = lax.broadcasted_iota(jnp.int32, (TM, LANE), 1)
    oh0, oh1 = _one_hots(route_ref[...], lane)
    oh = jnp.where(oh0, 1.0, 0.0) + jnp.where(oh1, 1.0, 0.0)
    before = _dot(ltri_ref[...], oh.astype(BF16)) + carry[0:1, :]
    r0 = jnp.sum(jnp.where(oh0, before, 0.0), axis=1, keepdims=True)
    r1 = jnp.sum(jnp.where(oh1, before, 0.0), axis=1, keepdims=True)
    rank_ref[...] = jnp.where(lane == 0, r0, jnp.where(lane == 1, r1, 0.0))
    carry[...] = carry[...] + jnp.sum(oh, axis=0, keepdims=True)
    cnt_ref[...] = carry[...]


def _plan_kernel(cnt_ref, pstart_ref, blk_ref, tot_ref, *, rows_per_block):
    lane = lax.broadcasted_iota(jnp.int32, (SUBLANE, LANE), 1)
    nblk = (cnt_ref[...].astype(jnp.int32) + (rows_per_block - 1)) // rows_per_block
    end = nblk
    sh = 1
    while sh < N_EXPERTS:
        end = end + jnp.where(lane >= sh, pltpu.roll(end, sh, 1), 0)
        sh *= 2
    pstart_ref[...] = ((end - nblk) * rows_per_block).astype(F32)
    nbp = blk_ref.shape[0]
    blk_io = lax.broadcasted_iota(jnp.int32, (nbp, LANE), 0)
    lane_b = lax.broadcasted_iota(jnp.int32, (nbp, LANE), 1)
    passed = jnp.where((lane_b < N_EXPERTS) & (blk_io >= end[0:1, :]), 1, 0)
    blk_ref[...] = jnp.broadcast_to(jnp.minimum(jnp.sum(passed, axis=1, keepdims=True), N_EXPERTS - 1), (nbp, LANE))
    tot_ref[...] = jnp.broadcast_to(
        jnp.sum(jnp.where(lane == N_EXPERTS - 1, end, 0), axis=1, keepdims=True), (SUBLANE, LANE))


def _dest_kernel(route_ref, rank_ref, pstart_ref, dest_ref):
    TM = route_ref.shape[0]
    lane = lax.broadcasted_iota(jnp.int32, (TM, LANE), 1)
    oh0, oh1 = _one_hots(route_ref[...], lane)
    ps = pstart_ref[0:1, :]
    rank = rank_ref[...]
    d0 = jnp.sum(jnp.where(oh0, ps, 0.0), axis=1, keepdims=True) + rank[:, 0:1]
    d1 = jnp.sum(jnp.where(oh1, ps, 0.0), axis=1, keepdims=True) + rank[:, 1:2]
    dest_ref[...] = jnp.where(lane == 0, d0, jnp.where(lane == 1, d1, 0.0)).astype(jnp.int32)


def _sort_plan(route, n_blocks, S):
    T = route.shape[0]
    TM = min(TOKEN_TILE, S)
    row = pl.BlockSpec((TM, LANE), lambda i: (i, 0))
    small = pl.BlockSpec((SUBLANE, LANE), lambda i: (0, 0))
    ltri = jnp.asarray(np.tril(np.ones((TM, TM), np.float32), -1), dtype=BF16)
    rank, cnt = pl.pallas_call(
        _rank_kernel,
        grid=(T // TM,),
        in_specs=[row, _full((TM, TM))],
        out_specs=[row, small],
        out_shape=[jax.ShapeDtypeStruct((T, LANE), F32), jax.ShapeDtypeStruct((SUBLANE, LANE), F32)],
        scratch_shapes=[pltpu.VMEM((SUBLANE, LANE), F32)],
        compiler_params=_params(1),
        name="expert_rank",
    )(route, ltri)
    nbp = -(-n_blocks // SUBLANE) * SUBLANE
    pstart, blk, tot = pl.pallas_call(
        functools.partial(_plan_kernel, rows_per_block=EXPERT_ROWS),
        grid=(1,),
        in_specs=[small],
        out_specs=[small, _full((nbp, LANE)), small],
        out_shape=[jax.ShapeDtypeStruct((SUBLANE, LANE), F32), jax.ShapeDtypeStruct((nbp, LANE), jnp.int32),
                   jax.ShapeDtypeStruct((SUBLANE, LANE), jnp.int32)],
        compiler_params=_params(1),
        name="expert_plan",
    )(cnt)
    dest = pl.pallas_call(
        _dest_kernel,
        grid=(T // TM,),
        in_specs=[row, row, small],
        out_specs=row,
        out_shape=jax.ShapeDtypeStruct((T, LANE), jnp.int32),
        compiler_params=_params(1),
        name="expert_dest",
    )(route, rank, pstart)
    return dest[:, :2], blk[:n_blocks, 0], tot[0, :1]


def _row_copy(src, src_row, dst, dst_row, sem):
    return pltpu.make_async_copy(src.at[pl.ds(pl.multiple_of(src_row * SUBLANE, SUBLANE), SUBLANE)],
                                 dst.at[pl.ds(pl.multiple_of(dst_row * SUBLANE, SUBLANE), SUBLANE)], sem)


def _dispatch_kernel(dest_ref, h2_ref, xs_in, xs_hbm, sem):
    del xs_in
    n = dest_ref.shape[2] // 2

    def issue(j, carry):
        for k in range(2):
            _row_copy(h2_ref, j, xs_hbm, dest_ref[0, 0, 2 * j + k], sem).start()
        return carry

    lax.fori_loop(0, n, issue, 0)

    def drain(j, carry):
        for k in range(2):
            _row_copy(h2_ref, j, xs_hbm, dest_ref[0, 0, 2 * j + k], sem).wait()
        return carry

    lax.fori_loop(0, n, drain, 0)


def _dispatch(dest3, h2t, n_rows):
    nt, _, two_n = dest3.shape
    xs0 = jnp.zeros((n_rows * SUBLANE, LANE), F32)
    return pl.pallas_call(
        _dispatch_kernel,
        grid=(nt,),
        in_specs=[pl.BlockSpec((1, 1, two_n), lambda i: (i, 0, 0), memory_space=pltpu.SMEM),
                  pl.BlockSpec((two_n // 2 * SUBLANE, LANE), lambda i: (i, 0)), pl.BlockSpec(memory_space=pl.ANY)],
        out_specs=pl.BlockSpec(memory_space=pl.ANY),
        out_shape=jax.ShapeDtypeStruct(xs0.shape, F32),
        scratch_shapes=[pltpu.SemaphoreType.DMA(())],
        input_output_aliases={2: 0},
        compiler_params=_params(1),
        name="dispatch",
    )(dest3, h2t, xs0)


def _expert_kernel(be_ref, nb_ref, xs_ref, w1_ref, w3_ref, w2_ref, ys_ref):
    del be_ref
    RB = xs_ref.shape[0] // SUBLANE

    @pl.when(pl.program_id(0) < nb_ref[0])
    def _():
        x = jnp.concatenate([xs_ref[pl.ds(s, RB, stride=SUBLANE), :] for s in range(SUBLANE)], axis=1).astype(BF16)
        hb = _silu(_dot(x, w1_ref[0])) * _dot(x, w3_ref[0])
        y = _dot(hb.astype(BF16), w2_ref[0])
        for s in range(SUBLANE):
            ys_ref[pl.ds(s, RB, stride=SUBLANE), :] = y[:, s * LANE:(s + 1) * LANE]

    @pl.when(pl.program_id(0) >= nb_ref[0])
    def _():
        ys_ref[...] = jnp.zeros_like(ys_ref)


def _experts(block_expert, n_used, xs, w1, w3, w2):
    n_blocks = block_expert.shape[0]
    RB = EXPERT_ROWS
    D, F = w1.shape[1], w1.shape[2]
    cur = lambda i, nb: jnp.minimum(i, nb[0] - 1)
    rows = pl.BlockSpec((RB * SUBLANE, LANE), lambda i, be, nb: (cur(i, nb), 0))
    wspec = lambda a, b: pl.BlockSpec((1, a, b), lambda i, be, nb: (be[cur(i, nb)], 0, 0))
    return pl.pallas_call(
        _expert_kernel,
        grid_spec=pltpu.PrefetchScalarGridSpec(
            num_scalar_prefetch=2, grid=(n_blocks,),
            in_specs=[rows, wspec(D, F), wspec(D, F), wspec(F, D)],
            out_specs=pl.BlockSpec((RB * SUBLANE, LANE), lambda i, be, nb: (i, 0))),
        out_shape=jax.ShapeDtypeStruct(xs.shape, F32),
        compiler_params=_params(1),
        name="experts",
    )(block_expert, n_used, xs, w1, w3, w2)


def _combine_kernel(dest_ref, ys_hbm, x1_ref, route_ref, mod_ref, nw_ref, o_ref, buf, sem):
    n = x1_ref.shape[0]

    def issue(j, carry):
        for k in range(2):
            _row_copy(ys_hbm, dest_ref[0, 0, 2 * j + k], buf, k * n + j, sem).start()
        return carry

    lax.fori_loop(0, n, issue, 0)

    def drain(j, carry):
        for k in range(2):
            _row_copy(ys_hbm, dest_ref[0, 0, 2 * j + k], buf, k * n + j, sem).wait()
        return carry

    lax.fori_loop(0, n, drain, 0)

    route = route_ref[...]
    g0, g1 = route[:, 2:3], route[:, 3:4]
    y = jnp.concatenate(
        [g0 * buf[pl.ds(s, n, stride=SUBLANE), :] + g1 * buf[pl.ds(n * SUBLANE + s, n, stride=SUBLANE), :]
         for s in range(SUBLANE)], axis=1)
    o_ref[...] = x1_ref[...] + mod_ref[0][5:6] * _rms(y, nw_ref[...])


def _combine(dest3, ys, x1, route, mod3, norm_w, S):
    T, D = x1.shape
    nt, _, two_n = dest3.shape
    n = two_n // 2
    tiles_per_seq = S // n
    return pl.pallas_call(
        _combine_kernel,
        grid=(nt,),
        in_specs=[pl.BlockSpec((1, 1, two_n), lambda i: (i, 0, 0), memory_space=pltpu.SMEM),
                  pl.BlockSpec(memory_space=pl.ANY), pl.BlockSpec((n, D), lambda i: (i, 0)),
                  pl.BlockSpec((n, LANE), lambda i: (i, 0)),
                  pl.BlockSpec((1, 6, D), lambda i: (i // tiles_per_seq, 0, 0)), _full((1, D))],
        out_specs=pl.BlockSpec((n, D), lambda i: (i, 0)),
        out_shape=jax.ShapeDtypeStruct((T, D), F32),
        scratch_shapes=[pltpu.VMEM((2 * n * SUBLANE, LANE), F32), pltpu.SemaphoreType.DMA(())],
        compiler_params=_params(1),
        name="combine",
    )(dest3, ys, x1, route, mod3, norm_w[None, :])


def _layer(x, c, positions, ada_w, ada_b, norm_mix_pre, norm_mix_post, norm_ffn_pre, norm_ffn_post, w_in,
           conv_w, gate_b, head_norm, cmp_k, cmp_v, w_out, wg, bg, we, be, w1, w3, w2):
    B, S, D = x.shape
    T = B * S
    x2 = x.reshape(T, D)
    mod3 = _adaln(c, ada_w, ada_b).reshape(B, 6, D)

    o_mi = 4 * M_WIDTH
    o_nq = o_mi + 2 * M_HEADS
    o_kv = o_nq + N_WIDTH
    kv = lambda i: w_in[:, o_kv + i * KV_WIDTH:o_kv + (i + 1) * KV_WIDTH]
    o_ng = o_kv + 6 * KV_WIDTH
    w_main = jnp.concatenate([w_in[:, :o_mi], w_in[:, o_nq:o_kv], kv(0), kv(2), kv(4), kv(1)], axis=1).astype(BF16)
    w_t = jnp.concatenate([kv(3), kv(5), w_in[:, o_ng:], jnp.zeros((D, _R_END - _R_NG - 3 * N_HEADS), F32)],
                          axis=1).T.astype(BF16)
    wg_t = w_in[:, o_mi:o_nq].T
    half = N_HEAD_DIM // 2
    inv = ROPE_THETA ** (-jnp.arange(half, dtype=F32) / half)
    inv_row = jnp.tile(inv, LANE // half)[None, :]
    pos_col = positions.astype(F32).reshape(T, 1)

    zm, q, kc, ks, kw, vc, vs_t, vw_t, ng_t, gt = _inproj(
        x2, mod3, norm_mix_pre, w_main, w_t, wg_t, gate_b, pos_col, inv_row, S)
    ym = _mlstm(zm, gt, conv_w, head_norm, B, S)
    kcmp, vcmp_t = _compress(kc, vc, _compress_weights(*cmp_k), _compress_weights(*cmp_v), B, S)
    yn = _nsa(q, kcmp, vcmp_t, ks, vs_t, kw, vw_t, ng_t, B, S)

    wr = jnp.concatenate([wg, we, jnp.zeros((D, LANE - MOE_GROUPS - N_EXPERTS), F32)], axis=1)
    br = jnp.concatenate([bg, be, jnp.zeros((LANE - MOE_GROUPS - N_EXPERTS,), F32)])[None, :]
    x1, h2t, route = _post(ym, yn, w_out, x2, mod3, norm_mix_post, norm_ffn_pre, wr, br, S)

    n_blocks = (2 * T) // EXPERT_ROWS + N_EXPERTS
    dest, block_expert, n_used = _sort_plan(route, n_blocks, S)
    n_tok = min(GATHER_TILE, S)
    dest3 = dest.reshape(T // n_tok, 1, 2 * n_tok)
    xs = _dispatch(dest3, h2t, n_blocks * EXPERT_ROWS)
    ys = _experts(block_expert, n_used, xs, w1.astype(BF16), w3.astype(BF16), w2.astype(BF16))
    out = _combine(dest3, ys, x1, route, mod3, norm_ffn_post, S)
    return out.reshape(B, S, D)


def kernel(x, c, positions, ada_w, ada_b, norm_mix_pre, norm_mix_post, norm_ffn_pre, norm_ffn_post, w_in, mlstm_conv_w, mlstm_gate_b, mlstm_head_norm, cmp_pe_k, cmp_w1_k, cmp_w2_k, cmp_pe_v, cmp_w1_v, cmp_w2_v, w_out, router_grp_w, router_grp_b, router_exp_w, router_exp_b, expert_w1, expert_w3, expert_w2):
    for l in range(ada_w.shape[0]):
        x = _layer(x, c, positions, ada_w[l], ada_b[l], norm_mix_pre[l], norm_mix_post[l], norm_ffn_pre[l],
                   norm_ffn_post[l], w_in[l], mlstm_conv_w[l], mlstm_gate_b[l], mlstm_head_norm[l],
                   (cmp_pe_k[l], cmp_w1_k[l], cmp_w2_k[l]), (cmp_pe_v[l], cmp_w1_v[l], cmp_w2_v[l]), w_out[l],
                   router_grp_w[l], router_grp_b[l], router_exp_w[l], router_exp_b[l],
                   expert_w1[l], expert_w3[l], expert_w2[l])
    return x
```

```python
import functools

import jax
import jax.numpy as jnp
import numpy as np
from jax import lax
from jax.experimental import pallas as pl
from jax.experimental.pallas import tpu as pltpu

M_HEADS = 4
M_HEAD_DIM = 128
M_WIDTH = M_HEADS * M_HEAD_DIM
M_CONV = 4
N_HEADS = 8
N_KV_GROUPS = 2
HEADS_PER_GROUP = N_HEADS // N_KV_GROUPS
N_HEAD_DIM = 64
N_WIDTH = N_HEADS * N_HEAD_DIM
KV_WIDTH = N_KV_GROUPS * N_HEAD_DIM
CMP_BLOCK = 32
CMP_STRIDE = 16
CMP_HIDDEN = 2 * N_HEAD_DIM
SLC_BLOCK = 64
SLC_TOPK = 16
WINDOW = 512
ROPE_THETA = 10000.0
MOE_GROUPS = 4
EXPERTS_PER_GROUP = 8
N_EXPERTS = MOE_GROUPS * EXPERTS_PER_GROUP
RMS_EPS = 1e-6

LANE = 128
SUBLANE = 8
VMEM_LIMIT_BYTES = 56 * 1024 * 1024

TOKEN_TILE = 512
MLSTM_CHUNK = 256
NSA_Q_TILE = 256
NSA_K_TILE = 512
NSA_COL_BLOCK = 256
EXPERT_ROWS = 512
GATHER_TILE = 256

F32 = jnp.float32
BF16 = jnp.bfloat16
NEG = -1e30
HIGHEST = lax.Precision.HIGHEST
NT_DIMS = (((1,), (1,)), ((), ()))
TN_DIMS = (((0,), (0,)), ((), ()))


def _params(n_grid):
    return pltpu.CompilerParams(
        dimension_semantics=("arbitrary",) * n_grid, vmem_limit_bytes=VMEM_LIMIT_BYTES)


def _dot(a, b, **kw):
    return jnp.dot(a, b, preferred_element_type=F32, **kw)


def _dot_nt(a, b, **kw):
    return lax.dot_general(a, b, NT_DIMS, preferred_element_type=F32, **kw)


def _dot_tn(a, b, **kw):
    return lax.dot_general(a, b, TN_DIMS, preferred_element_type=F32, **kw)


def _sigmoid(x):
    return 1.0 / (1.0 + jnp.exp(-x))


def _silu(x):
    return x * _sigmoid(x)


def _rms(x, w):
    return x * lax.rsqrt(jnp.mean(x * x, axis=-1, keepdims=True) + RMS_EPS) * w


def _full(shape):
    return pl.BlockSpec(shape, lambda *_: (0,) * len(shape))


def _adaln_kernel(c_ref, w_ref, b_ref, o_ref):
    o_ref[...] = _dot(_silu(c_ref[...]), w_ref[...], precision=HIGHEST) + b_ref[...]


def _adaln(c, ada_w, ada_b):
    B, D = c.shape
    n = ada_w.shape[1] // D
    return pl.pallas_call(
        _adaln_kernel,
        grid=(n,),
        in_specs=[_full((B, D)), pl.BlockSpec((D, D), lambda j: (0, j)), pl.BlockSpec((1, D), lambda j: (0, j))],
        out_specs=pl.BlockSpec((B, D), lambda j: (0, j)),
        out_shape=jax.ShapeDtypeStruct((B, ada_w.shape[1]), F32),
        compiler_params=_params(1),
        name="adaln",
    )(c, ada_w, ada_b[None, :])


_C_M = 0
_C_Q = 4 * M_WIDTH
_C_KC = _C_Q + N_WIDTH
_C_KS = _C_KC + KV_WIDTH
_C_KW = _C_KS + KV_WIDTH
_C_VC = _C_KW + KV_WIDTH
_C_END = _C_VC + KV_WIDTH
_R_VS = 0
_R_VW = KV_WIDTH
_R_NG = 2 * KV_WIDTH
_R_END = _R_NG + 4 * SUBLANE


def _inproj_kernel(x_ref, mod_ref, nw_ref, w_ref, wt_ref, wg_ref, gb_ref, pos_ref, inv_ref,
                   zm_ref, q_ref, kc_ref, ks_ref, kw_ref, vc_ref, vst_ref, vwt_ref, ngt_ref, gt_ref):
    mod = mod_ref[0]
    h = _rms(x_ref[...], nw_ref[...]) * (1.0 + mod[1:2]) + mod[0:1]
    hb = h.astype(BF16)
    z = _dot(hb, w_ref[...])
    zm_ref[...] = z[:, _C_M:_C_Q]
    gt_ref[...] = _dot_nt(wg_ref[...], h, precision=HIGHEST) + gb_ref[...]
    zt = _dot_nt(wt_ref[...], hb)
    vst_ref[...] = zt[_R_VS:_R_VW].astype(BF16)
    vwt_ref[...] = zt[_R_VW:_R_NG].astype(BF16)
    ngt_ref[...] = _sigmoid(zt[_R_NG:_R_END])

    ang = pos_ref[...] * inv_ref[...]
    cos = jnp.cos(ang)
    sin = jnp.sin(ang)
    lane = lax.broadcasted_iota(jnp.int32, (1, LANE), 1)
    first = (lane % N_HEAD_DIM) < (N_HEAD_DIM // 2)
    sin_signed = jnp.where(first, -sin, sin)

    def rope(slab):
        rot = jnp.where(first, pltpu.roll(slab, LANE - N_HEAD_DIM // 2, 1), pltpu.roll(slab, N_HEAD_DIM // 2, 1))
        return slab * cos + rot * sin_signed

    scale = N_HEAD_DIM ** -0.5
    for j in range(N_WIDTH // LANE):
        q_ref[:, j * LANE:(j + 1) * LANE] = (rope(z[:, _C_Q + j * LANE:_C_Q + (j + 1) * LANE]) * scale).astype(BF16)
    kc_ref[...] = rope(z[:, _C_KC:_C_KS])
    ks_ref[...] = rope(z[:, _C_KS:_C_KW]).astype(BF16)
    kw_ref[...] = rope(z[:, _C_KW:_C_VC]).astype(BF16)
    vc_ref[...] = z[:, _C_VC:_C_END]


def _inproj(x2, mod3, norm_w, w_main, w_t, wg_t, gate_b, pos_col, inv_row, S):
    T, D = x2.shape
    TM = min(TOKEN_TILE, S)
    tiles_per_seq = S // TM
    row = lambda w: pl.BlockSpec((TM, w), lambda i: (i, 0))
    col = lambda r: pl.BlockSpec((r, TM), lambda i: (0, i))
    outs = [(4 * M_WIDTH, F32), (N_WIDTH, BF16), (KV_WIDTH, F32), (KV_WIDTH, BF16), (KV_WIDTH, BF16), (KV_WIDTH, F32)]
    outs_t = [(KV_WIDTH, BF16), (KV_WIDTH, BF16), (_R_END - _R_NG, F32), (2 * M_HEADS, F32)]
    return pl.pallas_call(
        _inproj_kernel,
        grid=(T // TM,),
        in_specs=[row(D), pl.BlockSpec((1, 6, D), lambda i: (i // tiles_per_seq, 0, 0)), _full((1, D)),
                  _full(w_main.shape), _full(w_t.shape), _full(wg_t.shape), _full((2 * M_HEADS, 1)), row(1),
                  _full((1, LANE))],
        out_specs=[row(w) for w, _ in outs] + [col(r) for r, _ in outs_t],
        out_shape=[jax.ShapeDtypeStruct((T, w), dt) for w, dt in outs]
        + [jax.ShapeDtypeStruct((r, T), dt) for r, dt in outs_t],
        compiler_params=_params(1),
        name="inproj",
    )(x2, mod3, norm_w[None, :], w_main, w_t, wg_t, gate_b[:, None], pos_col, inv_row)


def _mlstm_kernel(zm_ref, gt_ref, cw_ref, hn_ref, ltri_ref, eye_ref, y_ref, buf, c_s, n_s, m_s):
    L = zm_ref.shape[0]
    QK = 2 * M_WIDTH
    DH = M_HEAD_DIM

    @pl.when(pl.program_id(1) == 0)
    def _():
        buf[0:SUBLANE, :] = jnp.zeros((SUBLANE, QK), F32)
        c_s[...] = jnp.zeros_like(c_s)
        n_s[...] = jnp.zeros_like(n_s)
        m_s[...] = jnp.zeros_like(m_s)

    buf[SUBLANE:SUBLANE + L, :] = zm_ref[:, 0:QK]
    cw = cw_ref[...]
    conv = cw[M_CONV - 1:M_CONV] * buf[SUBLANE:SUBLANE + L, :]
    for j in range(M_CONV - 1):
        off = SUBLANE - (M_CONV - 1) + j
        conv = conv + cw[j:j + 1] * buf[off:off + L, :]
    buf[0:SUBLANE, :] = buf[L:L + SUBLANE, :]
    qk = _silu(conv)

    g = gt_ref[...]
    fp = g[M_HEADS:]
    lf = jnp.minimum(fp, 0.0) - jnp.log(1.0 + jnp.exp(-jnp.abs(fp)))
    lf8 = jnp.concatenate([lf, jnp.zeros_like(lf)], axis=0)
    ltri = ltri_ref[...]
    b_rows = _dot_nt(lf8, ltri, precision=HIGHEST)
    b_cols = _dot_nt(ltri, lf8, precision=HIGHEST)
    i_cols = _dot_nt(eye_ref[...], g, precision=HIGHEST)

    causal = lax.broadcasted_iota(jnp.int32, (L, L), 0) >= lax.broadcasted_iota(jnp.int32, (L, L), 1)
    for h in range(M_HEADS):
        hs = slice(h * DH, (h + 1) * DH)
        b_r, i_r = b_rows[h:h + 1], g[h:h + 1]
        b_c, i_c = b_cols[:, h:h + 1], i_cols[:, h:h + 1]
        m_prev = m_s[h:h + 1, 0:1]
        g_tot = b_r[:, L - 1:L]
        m_loc = jnp.max(g_tot - b_r + i_r, axis=1, keepdims=True)
        m_new = jnp.maximum(g_tot + m_prev, m_loc)

        q = qk[:, hs]
        k = qk[:, M_WIDTH + h * DH:M_WIDTH + (h + 1) * DH] * (DH ** -0.5)
        v = zm_ref[:, 2 * M_WIDTH + h * DH:2 * M_WIDTH + (h + 1) * DH]
        o = zm_ref[:, 3 * M_WIDTH + h * DH:3 * M_WIDTH + (h + 1) * DH]
        qb, kb, vb = q.astype(BF16), k.astype(BF16), v.astype(BF16)

        dlog = jnp.where(causal, b_c - b_r + i_r, -jnp.inf)
        inter = b_c + m_prev
        m_t = jnp.maximum(inter, jnp.max(dlog, axis=1, keepdims=True))
        wts = jnp.exp(dlog - m_t) * _dot_nt(qb, kb)
        dec = jnp.exp(inter - m_t)
        c_prev = c_s[h]
        n_prev = n_s[h:h + 1]
        num = _dot(wts.astype(BF16), vb) + dec * _dot(qb, c_prev.astype(BF16))
        den = jnp.sum(wts, axis=1, keepdims=True) + dec * jnp.sum(q * n_prev, axis=1, keepdims=True)
        hh = num / jnp.maximum(jnp.abs(den), jnp.exp(-m_t))

        kw = k * jnp.exp(g_tot - b_c + i_c - m_new)
        keep = jnp.exp(g_tot + m_prev - m_new)
        c_s[h] = keep * c_prev + _dot_tn(kw.astype(BF16), vb)
        n_s[h:h + 1] = keep * n_prev + jnp.sum(kw, axis=0, keepdims=True)
        m_s[h:h + 1] = jnp.broadcast_to(m_new, (1, LANE))

        hn = _rms(hh, hn_ref[:, hs])
        y_ref[:, hs] = (_sigmoid(o) * hn).astype(BF16)


def _mlstm(zm, gt, conv_w, head_norm, B, S):
    T = zm.shape[0]
    L = min(MLSTM_CHUNK, S)
    nc = S // L
    ltri = jnp.asarray(np.tril(np.ones((L, L), np.float32)))
    eye = jnp.asarray(np.eye(L, dtype=np.float32))
    return pl.pallas_call(
        _mlstm_kernel,
        grid=(B, nc),
        in_specs=[pl.BlockSpec((L, 4 * M_WIDTH), lambda b, c: (b * nc + c, 0)),
                  pl.BlockSpec((2 * M_HEADS, L), lambda b, c: (0, b * nc + c)),
                  _full((M_CONV, 2 * M_WIDTH)), _full((1, M_WIDTH)), _full((L, L)), _full((L, L))],
        out_specs=pl.BlockSpec((L, M_WIDTH), lambda b, c: (b * nc + c, 0)),
        out_shape=jax.ShapeDtypeStruct((T, M_WIDTH), BF16),
        scratch_shapes=[pltpu.VMEM((L + SUBLANE, 2 * M_WIDTH), F32),
                        pltpu.VMEM((M_HEADS, M_HEAD_DIM, M_HEAD_DIM), F32),
                        pltpu.VMEM((SUBLANE, M_HEAD_DIM), F32),
                        pltpu.VMEM((SUBLANE, LANE), F32)],
        compiler_params=_params(2),
        name="mlstm",
    )(zm, gt, conv_w, head_norm[None, :], ltri, eye)


def _compress_kernel(k_ref, v_ref, pak, pbk, wak, wbk, w2k, pav, pbv, wav, wbv, w2v, ko_ref, vo_ref):
    def one(x_ref, pa, pb, wa, wb, w2, o_ref, transposed):
        x = x_ref[0]
        n = x.shape[0]
        first = _dot((x + pa[...]).astype(BF16), wa[...])
        second = _dot((x + pb[...]).astype(BF16), wb[...])
        hid = _silu(first + pltpu.roll(second, n - 1, 0)).astype(BF16)
        o_ref[0] = (_dot_nt(w2[...], hid) if transposed else _dot(hid, w2[...])).astype(BF16)

    one(k_ref, pak, pbk, wak, wbk, w2k, ko_ref, False)
    one(v_ref, pav, pbv, wav, wbv, w2v, vo_ref, True)


def _compress_weights(pe, w1, w2):
    half = CMP_BLOCK // 2
    eye = jnp.eye(N_KV_GROUPS, dtype=F32)
    w1r = w1.reshape(CMP_BLOCK, N_HEAD_DIM, CMP_HIDDEN)
    big = lambda w: jnp.einsum("ldc,gh->lgdhc", w, eye).reshape(half * KV_WIDTH, N_KV_GROUPS * CMP_HIDDEN).astype(BF16)
    pe_row = lambda p: jnp.broadcast_to(p[:, None, :], (half, N_KV_GROUPS, N_HEAD_DIM)).reshape(1, half * KV_WIDTH)
    w2bd = jnp.einsum("cd,gh->gchd", w2, eye).reshape(N_KV_GROUPS * CMP_HIDDEN, KV_WIDTH).astype(BF16)
    return pe_row(pe[:half]), pe_row(pe[half:]), big(w1r[:half]), big(w1r[half:]), w2bd


def _compress(kc, vc, wk, wv, B, S):
    nc = S // CMP_STRIDE
    width = CMP_STRIDE * KV_WIDTH
    blk = pl.BlockSpec((1, nc, width), lambda b: (b, 0, 0))
    wv = wv[:4] + (wv[4].T,)
    return pl.pallas_call(
        _compress_kernel,
        grid=(B,),
        in_specs=[blk, blk] + [_full(w.shape) for w in wk] + [_full(w.shape) for w in wv],
        out_specs=[pl.BlockSpec((1, nc, KV_WIDTH), lambda b: (b, 0, 0)),
                   pl.BlockSpec((1, KV_WIDTH, nc), lambda b: (b, 0, 0))],
        out_shape=[jax.ShapeDtypeStruct((B, nc, KV_WIDTH), BF16), jax.ShapeDtypeStruct((B, KV_WIDTH, nc), BF16)],
        compiler_params=_params(1),
        name="compress",
    )(kc.reshape(B, nc, width), vc.reshape(B, nc, width), *wk, *wv)


def _masked_softmax_keys(s, mask):
    s = jnp.where(mask, s, -jnp.inf)
    m = jnp.max(s, axis=0, keepdims=True)
    m = jnp.where(jnp.isfinite(m), m, 0.0)
    e = jnp.exp(s - m)
    return e, 1.0 / jnp.maximum(jnp.sum(e, axis=0, keepdims=True), 1e-30)


def _nsa_kernel(q_ref, kc_ref, vct_ref, ks_ref, vst_ref, kw_ref, vwt_ref, ngt_ref, ov_ref, ek_ref, eye_ref,
                y_ref, m_s, l_s, acc_s, *, n_top, TK):
    TQ = q_ref.shape[0]
    NS = ov_ref.shape[0]
    G, HPG, DH = N_KV_GROUPS, HEADS_PER_GROUP, N_HEAD_DIM
    R = G * HPG * TQ
    q0 = pl.program_id(1) * TQ

    q = q_ref[...]
    zero = jnp.zeros((TQ, DH), BF16)
    parts = []
    for hd in range(N_HEADS):
        qh = q[:, hd * DH:(hd + 1) * DH]
        parts.append(jnp.concatenate([qh, zero] if hd < HPG else [zero, qh], axis=1))
    qp = jnp.concatenate(parts, axis=0)
    t_q = q0 + lax.broadcasted_iota(jnp.int32, (1, R), 1) % TQ

    kc = kc_ref[0]
    NC = kc.shape[0]
    cmp_end = lax.broadcasted_iota(jnp.int32, (NC, 1), 0) * CMP_STRIDE + (CMP_BLOCK - 1)
    e_c, inv_c = _masked_softmax_keys(_dot_nt(kc, qp), cmp_end <= t_q)
    p_c = e_c * inv_c
    o_c = _dot(vct_ref[0], p_c.astype(BF16))
    p_grp = jnp.concatenate(
        [sum(p_c[:, (g * HPG + h) * TQ:(g * HPG + h + 1) * TQ] for h in range(HPG)) for g in range(G)], axis=1)
    imp = _dot(ov_ref[...], p_grp, precision=HIGHEST)

    j_io = lax.broadcasted_iota(jnp.int32, (NS, G * TQ), 0)
    t_row = q0 + lax.broadcasted_iota(jnp.int32, (NS, G * TQ), 1) % TQ
    cur = t_row // SLC_BLOCK
    forced = (j_io == 0) | (j_io == cur) | (j_io == cur - 1)
    future = j_io * SLC_BLOCK > t_row
    score = jnp.where(forced, jnp.inf, jnp.where(future, -jnp.inf, imp))
    rank = jnp.zeros((NS, G * TQ), F32)
    for jp in range(NS):
        other = score[jp:jp + 1, :]
        rank = rank + jnp.where(j_io > jp, jnp.where(other >= score, 1.0, 0.0), jnp.where(other > score, 1.0, 0.0))
    sel_bias = jnp.where(future, NEG, jnp.where(rank < n_top, 0.0, NEG)).astype(BF16)
    sel_rows = _dot_tn(sel_bias, eye_ref[...]).astype(BF16)
    sel_rows = jnp.broadcast_to(sel_rows.reshape(G, 1, TQ, LANE), (G, HPG, TQ, LANE)).reshape(R, LANE)
    qa = jnp.concatenate([qp, sel_rows], axis=1)

    m_s[...] = jnp.full_like(m_s, NEG)
    l_s[...] = jnp.zeros_like(l_s)
    acc_s[...] = jnp.zeros_like(acc_s)

    CB = NSA_COL_BLOCK

    def step(kt, causal):
        k0 = pl.multiple_of(kt * TK, TK)
        ka = jnp.concatenate([ks_ref[0, pl.ds(k0, TK), :], ek_ref[pl.ds(k0, TK), :]], axis=1)
        vt = vst_ref[:, pl.ds(k0, TK)]
        kpos = k0 + lax.broadcasted_iota(jnp.int32, (TK, 1), 0)
        scores = [_dot_nt(ka, qa[j * CB:(j + 1) * CB]) for j in range(R // CB)]
        probs, alphas = [], []
        for j in range(R // CB):
            cols = slice(j * CB, (j + 1) * CB)
            s = scores[j]
            if causal:
                s = jnp.where(kpos <= t_q[:, cols], s, NEG)
            m_old = m_s[:, cols]
            m_new = jnp.maximum(m_old, jnp.max(s, axis=0, keepdims=True))
            alpha = jnp.exp(m_old - m_new)
            p = jnp.exp(s - m_new)
            l_s[:, cols] = alpha * l_s[:, cols] + jnp.sum(p, axis=0, keepdims=True)
            m_s[:, cols] = m_new
            probs.append(p.astype(BF16))
            alphas.append(alpha)
        for j in range(R // CB):
            cols = slice(j * CB, (j + 1) * CB)
            acc_s[:, cols] = alphas[j] * acc_s[:, cols] + _dot(vt, probs[j])

    def full_tile(kt, carry):
        step(kt, False)
        return carry

    last = q0 // TK
    lax.fori_loop(0, last, full_tile, 0)
    step(last, True)
    o_s = acc_s[...] * (1.0 / l_s[...])

    WK = WINDOW + TQ
    start = pl.multiple_of(jnp.maximum(q0 - WINDOW, 0), TQ)
    wpos = start + lax.broadcasted_iota(jnp.int32, (WK, 1), 0)
    e_w, inv_w = _masked_softmax_keys(_dot_nt(kw_ref[0, pl.ds(start, WK), :], qp),
                                      (wpos <= t_q) & (wpos > t_q - WINDOW))
    o_w = _dot(vwt_ref[:, pl.ds(start, WK)], e_w.astype(BF16)) * inv_w

    gates = ngt_ref[...]
    outs = []
    for hd in range(N_HEADS):
        cols = slice(hd * TQ, (hd + 1) * TQ)
        rows = slice((hd // HPG) * DH, (hd // HPG + 1) * DH)
        outs.append(gates[3 * hd:3 * hd + 1] * o_c[rows, cols]
                    + gates[3 * hd + 1:3 * hd + 2] * o_s[rows, cols]
                    + gates[3 * hd + 2:3 * hd + 3] * o_w[rows, cols])
    y_ref[...] = jnp.concatenate(outs, axis=0).T.astype(BF16)


def _nsa(q, kcmp, vcmp_t, ks, vs_t, kw, vw_t, ng_t, B, S):
    T = q.shape[0]
    TQ = min(NSA_Q_TILE, S)
    TK = min(NSA_K_TILE, S)
    nq = S // TQ
    NS = S // SLC_BLOCK
    NC = S // CMP_STRIDE
    n_cmp = (S - CMP_BLOCK) // CMP_STRIDE + 1
    js = np.arange(NS)[:, None] * SLC_BLOCK
    cs = np.arange(NC)[None, :] * CMP_STRIDE
    ov = np.clip(np.minimum(js + SLC_BLOCK, cs + CMP_BLOCK) - np.maximum(js, cs), 0, None) / CMP_STRIDE
    ov[:, n_cmp:] = 0.0
    assert NS <= LANE and TK % TQ == 0
    block_of_key = (np.arange(S)[:, None] // SLC_BLOCK == np.arange(LANE)[None, :]).astype(np.float32)
    eye_pad = np.eye(NS, LANE, dtype=np.float32)
    R = N_HEADS * TQ
    assert R % NSA_COL_BLOCK == 0
    seq = lambda: pl.BlockSpec((1, S, KV_WIDTH), lambda b, i: (b, 0, 0))
    seq_t = lambda: pl.BlockSpec((KV_WIDTH, S), lambda b, i: (0, b))
    return pl.pallas_call(
        functools.partial(_nsa_kernel, n_top=min(SLC_TOPK, NS), TK=TK),
        grid=(B, nq),
        in_specs=[pl.BlockSpec((TQ, N_WIDTH), lambda b, i: (b * nq + i, 0)),
                  pl.BlockSpec((1, NC, KV_WIDTH), lambda b, i: (b, 0, 0)),
                  pl.BlockSpec((1, KV_WIDTH, NC), lambda b, i: (b, 0, 0)),
                  seq(), seq_t(), seq(), seq_t(), pl.BlockSpec((ng_t.shape[0], TQ), lambda b, i: (0, b * nq + i)),
                  _full((NS, NC)), _full((S, LANE)), _full((NS, LANE))],
        out_specs=pl.BlockSpec((TQ, N_WIDTH), lambda b, i: (b * nq + i, 0)),
        out_shape=jax.ShapeDtypeStruct((T, N_WIDTH), BF16),
        scratch_shapes=[pltpu.VMEM((1, R), F32), pltpu.VMEM((1, R), F32), pltpu.VMEM((KV_WIDTH, R), F32)],
        compiler_params=_params(2),
        name="nsa",
    )(q, kcmp, vcmp_t, ks.reshape(B, S, KV_WIDTH), vs_t, kw.reshape(B, S, KV_WIDTH), vw_t, ng_t,
      jnp.asarray(ov.astype(np.float32)), jnp.asarray(block_of_key, dtype=BF16), jnp.asarray(eye_pad, dtype=BF16))


def _post_kernel(ym_ref, yn_ref, wm_ref, wn_ref, x_ref, mod_ref, npost_ref, npre_ref, wrh_ref, wrl_ref, br_ref,
                 x1_ref, h2_ref, route_ref):
    TM = x_ref.shape[0]
    mod = mod_ref[0]
    y = _dot(ym_ref[...], wm_ref[...]) + _dot(yn_ref[...], wn_ref[...])
    x1 = x_ref[...] + mod[2:3] * _rms(y, npost_ref[...])
    x1_ref[...] = x1
    h2 = _rms(x1, npre_ref[...]) * (1.0 + mod[4:5]) + mod[3:4]
    for s in range(SUBLANE):
        h2_ref[pl.ds(s, TM, stride=SUBLANE), :] = h2[:, s * LANE:(s + 1) * LANE]

    h_hi = h2.astype(BF16)
    h_lo = (h2 - h_hi.astype(F32)).astype(BF16)
    logits = _dot(h_hi, wrh_ref[...]) + _dot(h_lo, wrh_ref[...]) + _dot(h_hi, wrl_ref[...]) + br_ref[...]
    lane = lax.broadcasted_iota(jnp.int32, (TM, LANE), 1)
    is_grp = lane < MOE_GROUPS
    lg = jnp.where(is_grp, logits, -jnp.inf)
    eg = jnp.exp(lg - jnp.max(lg, axis=1, keepdims=True))
    pg = eg / jnp.sum(eg, axis=1, keepdims=True)
    pg_top = jnp.max(pg, axis=1, keepdims=True)
    grp = jnp.min(jnp.where(is_grp & (pg == pg_top), lane, LANE), axis=1, keepdims=True)
    lo = MOE_GROUPS + EXPERTS_PER_GROUP * grp
    in_grp = (lane >= lo) & (lane < lo + EXPERTS_PER_GROUP)
    le = jnp.where(in_grp, logits, -jnp.inf)
    ee = jnp.exp(le - jnp.max(le, axis=1, keepdims=True))
    pe = jnp.where(in_grp, ee / jnp.sum(ee, axis=1, keepdims=True), -1.0)
    p1 = jnp.max(pe, axis=1, keepdims=True)
    i1 = jnp.min(jnp.where(pe == p1, lane, LANE), axis=1, keepdims=True)
    pe2 = jnp.where(lane == i1, -1.0, pe)
    p2 = jnp.max(pe2, axis=1, keepdims=True)
    i2 = jnp.min(jnp.where((pe2 == p2) & in_grp & (lane != i1), lane, LANE), axis=1, keepdims=True)
    den = p1 + p2
    e1 = (i1 - MOE_GROUPS).astype(F32)
    e2 = (i2 - MOE_GROUPS).astype(F32)
    route_ref[...] = jnp.where(lane == 0, e1, jnp.where(lane == 1, e2, jnp.where(
        lane == 2, pg_top * p1 / den, jnp.where(lane == 3, pg_top * p2 / den, 0.0))))


def _post(ym, yn, w_out, x2, mod3, norm_post, norm_pre, wr, br, S):
    T, D = x2.shape
    TM = min(TOKEN_TILE, S)
    tiles_per_seq = S // TM
    row = lambda w: pl.BlockSpec((TM, w), lambda i: (i, 0))
    return pl.pallas_call(
        _post_kernel,
        grid=(T // TM,),
        in_specs=[row(M_WIDTH), row(N_WIDTH), _full((M_WIDTH, D)), _full((N_WIDTH, D)), row(D),
                  pl.BlockSpec((1, 6, D), lambda i: (i // tiles_per_seq, 0, 0)), _full((1, D)), _full((1, D)),
                  _full((D, LANE)), _full((D, LANE)), _full((1, LANE))],
        out_specs=[row(D), pl.BlockSpec((TM * SUBLANE, LANE), lambda i: (i, 0)), row(LANE)],
        out_shape=[jax.ShapeDtypeStruct((T, D), F32), jax.ShapeDtypeStruct((T * SUBLANE, LANE), F32),
                   jax.ShapeDtypeStruct((T, LANE), F32)],
        compiler_params=_params(1),
        name="post_mix_router",
    )(ym, yn, w_out[:M_WIDTH].astype(BF16), w_out[M_WIDTH:].astype(BF16), x2, mod3,
      norm_post[None, :], norm_pre[None, :], wr.astype(BF16), (wr - wr.astype(BF16).astype(F32)).astype(BF16), br)


def _one_hots(route, lane):
    return lane == route[:, 0:1].astype(jnp.int32), lane == route[:, 1:2].astype(jnp.int32)


def _rank_kernel(route_ref, ltri_ref, rank_ref, cnt_ref, carry):
    TM = route_ref.shape[0]

    @pl.when(pl.program_id(0) == 0)
    def _():
        carry[...] = jnp.zeros_like(carry)

    lane = lax.broadcasted_iota(jnp.int32, (TM, LANE), 1)
    oh0, oh1 = _one_hots(route_ref[...], lane)
    oh = jnp.where(oh0, 1.0, 0.0) + jnp.where(oh1, 1.0, 0.0)
    before = _dot(ltri_ref[...], oh.astype(BF16)) + carry[0:1, :]
    r0 = jnp.sum(jnp.where(oh0, before, 0.0), axis=1, keepdims=True)
    r1 = jnp.sum(jnp.where(oh1, before, 0.0), axis=1, keepdims=True)
    rank_ref[...] = jnp.where(lane == 0, r0, jnp.where(lane == 1, r1, 0.0))
    carry[...] = carry[...] + jnp.sum(oh, axis=0, keepdims=True)
    cnt_ref[...] = carry[...]


def _plan_kernel(cnt_ref, pstart_ref, blk_ref, tot_ref, *, rows_per_block):
    lane = lax.broadcasted_iota(jnp.int32, (SUBLANE, LANE), 1)
    nblk = (cnt_ref[...].astype(jnp.int32) + (rows_per_block - 1)) // rows_per_block
    end = nblk
    sh = 1
    while sh < N_EXPERTS:
        end = end + jnp.where(lane >= sh, pltpu.roll(end, sh, 1), 0)
        sh *= 2
    pstart_ref[...] = ((end - nblk) * rows_per_block).astype(F32)
    nbp = blk_ref.shape[0]
    blk_io = lax.broadcasted_iota(jnp.int32, (nbp, LANE), 0)
    lane_b = lax.broadcasted_iota(jnp.int32, (nbp, LANE), 1)
    passed = jnp.where((lane_b < N_EXPERTS) & (blk_io >= end[0:1, :]), 1, 0)
    blk_ref[...] = jnp.broadcast_to(jnp.minimum(jnp.sum(passed, axis=1, keepdims=True), N_EXPERTS - 1), (nbp, LANE))
    tot_ref[...] = jnp.broadcast_to(
        jnp.sum(jnp.where(lane == N_EXPERTS - 1, end, 0), axis=1, keepdims=True), (SUBLANE, LANE))


def _dest_kernel(route_ref, rank_ref, pstart_ref, dest_ref):
    TM = route_ref.shape[0]
    lane = lax.broadcasted_iota(jnp.int32, (TM, LANE), 1)
    oh0, oh1 = _one_hots(route_ref[...], lane)
    ps = pstart_ref[0:1, :]
    rank = rank_ref[...]
    d0 = jnp.sum(jnp.where(oh0, ps, 0.0), axis=1, keepdims=True) + rank[:, 0:1]
    d1 = jnp.sum(jnp.where(oh1, ps, 0.0), axis=1, keepdims=True) + rank[:, 1:2]
    dest_ref[...] = jnp.where(lane == 0, d0, jnp.where(lane == 1, d1, 0.0)).astype(jnp.int32)


def _sort_plan(route, n_blocks, S):
    T = route.shape[0]
    TM = min(TOKEN_TILE, S)
    row = pl.BlockSpec((TM, LANE), lambda i: (i, 0))
    small = pl.BlockSpec((SUBLANE, LANE), lambda i: (0, 0))
    ltri = jnp.asarray(np.tril(np.ones((TM, TM), np.float32), -1), dtype=BF16)
    rank, cnt = pl.pallas_call(
        _rank_kernel,
        grid=(T // TM,),
        in_specs=[row, _full((TM, TM))],
        out_specs=[row, small],
        out_shape=[jax.ShapeDtypeStruct((T, LANE), F32), jax.ShapeDtypeStruct((SUBLANE, LANE), F32)],
        scratch_shapes=[pltpu.VMEM((SUBLANE, LANE), F32)],
        compiler_params=_params(1),
        name="expert_rank",
    )(route, ltri)
    nbp = -(-n_blocks // SUBLANE) * SUBLANE
    pstart, blk, tot = pl.pallas_call(
        functools.partial(_plan_kernel, rows_per_block=EXPERT_ROWS),
        grid=(1,),
        in_specs=[small],
        out_specs=[small, _full((nbp, LANE)), small],
        out_shape=[jax.ShapeDtypeStruct((SUBLANE, LANE), F32), jax.ShapeDtypeStruct((nbp, LANE), jnp.int32),
                   jax.ShapeDtypeStruct((SUBLANE, LANE), jnp.int32)],
        compiler_params=_params(1),
        name="expert_plan",
    )(cnt)
    dest = pl.pallas_call(
        _dest_kernel,
        grid=(T // TM,),
        in_specs=[row, row, small],
        out_specs=row,
        out_shape=jax.ShapeDtypeStruct((T, LANE), jnp.int32),
        compiler_params=_params(1),
        name="expert_dest",
    )(route, rank, pstart)
    return dest[:, :2], blk[:n_blocks, 0], tot[0, :1]


def _row_copy(src, src_row, dst, dst_row, sem):
    return pltpu.make_async_copy(src.at[pl.ds(pl.multiple_of(src_row * SUBLANE, SUBLANE), SUBLANE)],
                                 dst.at[pl.ds(pl.multiple_of(dst_row * SUBLANE, SUBLANE), SUBLANE)], sem)


def _dispatch_kernel(dest_ref, h2_ref, xs_in, xs_hbm, sem):
    del xs_in
    n = dest_ref.shape[2] // 2

    def issue(j, carry):
        for k in range(2):
            _row_copy(h2_ref, j, xs_hbm, dest_ref[0, 0, 2 * j + k], sem).start()
        return carry

    lax.fori_loop(0, n, issue, 0)
    whole = xs_hbm.at[pl.ds(0, 2 * n * SUBLANE)]
    pltpu.make_async_copy(whole, whole, sem).wait()


def _dispatch(dest3, h2t, n_rows):
    nt, _, two_n = dest3.shape
    xs0 = jnp.zeros((n_rows * SUBLANE, LANE), F32)
    return pl.pallas_call(
        _dispatch_kernel,
        grid=(nt,),
        in_specs=[pl.BlockSpec((1, 1, two_n), lambda i: (i, 0, 0), memory_space=pltpu.SMEM),
                  pl.BlockSpec((two_n // 2 * SUBLANE, LANE), lambda i: (i, 0)), pl.BlockSpec(memory_space=pl.ANY)],
        out_specs=pl.BlockSpec(memory_space=pl.ANY),
        out_shape=jax.ShapeDtypeStruct(xs0.shape, F32),
        scratch_shapes=[pltpu.SemaphoreType.DMA(())],
        input_output_aliases={2: 0},
        compiler_params=_params(1),
        name="dispatch",
    )(dest3, h2t, xs0)


def _expert_kernel(be_ref, nb_ref, xs_ref, w1_ref, w3_ref, w2_ref, ys_ref, w1b, w3b, w2b):
    RB = xs_ref.shape[0] // SUBLANE
    i = pl.program_id(0)

    @pl.when(i < nb_ref[0])
    def _():
        @pl.when((i == 0) | (be_ref[i] != be_ref[jnp.maximum(i - 1, 0)]))
        def _():
            w1b[...] = w1_ref[0].astype(BF16)
            w3b[...] = w3_ref[0].astype(BF16)
            w2b[...] = w2_ref[0].astype(BF16)

        x = jnp.concatenate([xs_ref[pl.ds(s, RB, stride=SUBLANE), :] for s in range(SUBLANE)], axis=1).astype(BF16)
        hb = _silu(_dot(x, w1b[...])) * _dot(x, w3b[...])
        y = _dot(hb.astype(BF16), w2b[...])
        for s in range(SUBLANE):
            ys_ref[pl.ds(s, RB, stride=SUBLANE), :] = y[:, s * LANE:(s + 1) * LANE]

    @pl.when(i >= nb_ref[0])
    def _():
        ys_ref[...] = jnp.zeros_like(ys_ref)


def _experts(block_expert, n_used, xs, w1, w3, w2):
    n_blocks = block_expert.shape[0]
    RB = EXPERT_ROWS
    D, F = w1.shape[1], w1.shape[2]
    cur = lambda i, nb: jnp.minimum(i, nb[0] - 1)
    rows = pl.BlockSpec((RB * SUBLANE, LANE), lambda i, be, nb: (cur(i, nb), 0))
    wspec = lambda a, b: pl.BlockSpec((1, a, b), lambda i, be, nb: (be[cur(i, nb)], 0, 0))
    return pl.pallas_call(
        _expert_kernel,
        grid_spec=pltpu.PrefetchScalarGridSpec(
            num_scalar_prefetch=2, grid=(n_blocks,),
            in_specs=[rows, wspec(D, F), wspec(D, F), wspec(F, D)],
            out_specs=pl.BlockSpec((RB * SUBLANE, LANE), lambda i, be, nb: (i, 0)),
            scratch_shapes=[pltpu.VMEM((D, F), BF16), pltpu.VMEM((D, F), BF16), pltpu.VMEM((F, D), BF16)]),
        out_shape=jax.ShapeDtypeStruct(xs.shape, F32),
        compiler_params=_params(1),
        name="experts",
    )(block_expert, n_used, xs, w1, w3, w2)


def _combine_kernel(dest_ref, dest_next_ref, ys_hbm, x1_ref, route_ref, mod_ref, nw_ref, o_ref, buf, sems):
    n = x1_ref.shape[0]
    i = pl.program_id(0)
    slot = i % 2

    def gather(dref, into):
        def issue(j, carry):
            for k in range(2):
                _row_copy(ys_hbm, dref[0, 0, 2 * j + k], buf.at[into], k * n + j, sems.at[into]).start()
            return carry

        lax.fori_loop(0, n, issue, 0)

    @pl.when(i == 0)
    def _():
        gather(dest_ref, 0)

    @pl.when(i + 1 < pl.num_programs(0))
    def _():
        gather(dest_next_ref, 1 - slot)

    pltpu.make_async_copy(ys_hbm.at[pl.ds(0, 2 * n * SUBLANE)], buf.at[slot], sems.at[slot]).wait()

    route = route_ref[...]
    g0, g1 = route[:, 2:3], route[:, 3:4]
    y = jnp.concatenate(
        [g0 * buf[slot, pl.ds(s, n, stride=SUBLANE), :] + g1 * buf[slot, pl.ds(n * SUBLANE + s, n, stride=SUBLANE), :]
         for s in range(SUBLANE)], axis=1)
    o_ref[...] = x1_ref[...] + mod_ref[0][5:6] * _rms(y, nw_ref[...])


def _combine(dest3, ys, x1, route, mod3, norm_w, S):
    T, D = x1.shape
    nt, _, two_n = dest3.shape
    n = two_n // 2
    tiles_per_seq = S // n
    return pl.pallas_call(
        _combine_kernel,
        grid=(nt,),
        in_specs=[pl.BlockSpec((1, 1, two_n), lambda i: (i, 0, 0), memory_space=pltpu.SMEM),
                  pl.BlockSpec((1, 1, two_n), lambda i: (jnp.minimum(i + 1, nt - 1), 0, 0), memory_space=pltpu.SMEM),
                  pl.BlockSpec(memory_space=pl.ANY), pl.BlockSpec((n, D), lambda i: (i, 0)),
                  pl.BlockSpec((n, LANE), lambda i: (i, 0)),
                  pl.BlockSpec((1, 6, D), lambda i: (i // tiles_per_seq, 0, 0)), _full((1, D))],
        out_specs=pl.BlockSpec((n, D), lambda i: (i, 0)),
        out_shape=jax.ShapeDtypeStruct((T, D), F32),
        scratch_shapes=[pltpu.VMEM((2, 2 * n * SUBLANE, LANE), F32), pltpu.SemaphoreType.DMA((2,))],
        compiler_params=_params(1),
        name="combine",
    )(dest3, dest3, ys, x1, route, mod3, norm_w[None, :])


def _layer(x, c, positions, ada_w, ada_b, norm_mix_pre, norm_mix_post, norm_ffn_pre, norm_ffn_post, w_in,
           conv_w, gate_b, head_norm, cmp_k, cmp_v, w_out, wg, bg, we, be, w1, w3, w2):
    B, S, D = x.shape
    T = B * S
    x2 = x.reshape(T, D)
    mod3 = _adaln(c, ada_w, ada_b).reshape(B, 6, D)

    o_mi = 4 * M_WIDTH
    o_nq = o_mi + 2 * M_HEADS
    o_kv = o_nq + N_WIDTH
    kv = lambda i: w_in[:, o_kv + i * KV_WIDTH:o_kv + (i + 1) * KV_WIDTH]
    o_ng = o_kv + 6 * KV_WIDTH
    w_main = jnp.concatenate([w_in[:, :o_mi], w_in[:, o_nq:o_kv], kv(0), kv(2), kv(4), kv(1)], axis=1).astype(BF16)
    w_t = jnp.concatenate([kv(3), kv(5), w_in[:, o_ng:], jnp.zeros((D, _R_END - _R_NG - 3 * N_HEADS), F32)],
                          axis=1).T.astype(BF16)
    wg_t = w_in[:, o_mi:o_nq].T
    half = N_HEAD_DIM // 2
    inv = ROPE_THETA ** (-jnp.arange(half, dtype=F32) / half)
    inv_row = jnp.tile(inv, LANE // half)[None, :]
    pos_col = positions.astype(F32).reshape(T, 1)

    zm, q, kc, ks, kw, vc, vs_t, vw_t, ng_t, gt = _inproj(
        x2, mod3, norm_mix_pre, w_main, w_t, wg_t, gate_b, pos_col, inv_row, S)
    ym = _mlstm(zm, gt, conv_w, head_norm, B, S)
    kcmp, vcmp_t = _compress(kc, vc, _compress_weights(*cmp_k), _compress_weights(*cmp_v), B, S)
    yn = _nsa(q, kcmp, vcmp_t, ks, vs_t, kw, vw_t, ng_t, B, S)

    wr = jnp.concatenate([wg, we, jnp.zeros((D, LANE - MOE_GROUPS - N_EXPERTS), F32)], axis=1)
    br = jnp.concatenate([bg, be, jnp.zeros((LANE - MOE_GROUPS - N_EXPERTS,), F32)])[None, :]
    x1, h2t, route = _post(ym, yn, w_out, x2, mod3, norm_mix_post, norm_ffn_pre, wr, br, S)

    n_blocks = (2 * T) // EXPERT_ROWS + N_EXPERTS
    dest, block_expert, n_used = _sort_plan(route, n_blocks, S)
    n_tok = min(GATHER_TILE, S)
    dest3 = dest.reshape(T // n_tok, 1, 2 * n_tok)
    xs = _dispatch(dest3, h2t, n_blocks * EXPERT_ROWS)
    ys = _experts(block_expert, n_used, xs, w1, w3, w2)
    out = _combine(dest3, ys, x1, route, mod3, norm_ffn_post, S)
    return out.reshape(B, S, D)


def kernel(x, c, positions, ada_w, ada_b, norm_mix_pre, norm_mix_post, norm_ffn_pre, norm_ffn_post, w_in, mlstm_conv_w, mlstm_gate_b, mlstm_head_norm, cmp_pe_k, cmp_w1_k, cmp_w2_k, cmp_pe_v, cmp_w1_v, cmp_w2_v, w_out, router_grp_w, router_grp_b, router_exp_w, router_exp_b, expert_w1, expert_w3, expert_w2):
    for l in range(ada_w.shape[0]):
        x = _layer(x, c, positions, ada_w[l], ada_b[l], norm_mix_pre[l], norm_mix_post[l], norm_ffn_pre[l],
                   norm_ffn_post[l], w_in[l], mlstm_conv_w[l], mlstm_gate_b[l], mlstm_head_norm[l],
                   (cmp_pe_k[l], cmp_w1_k[l], cmp_w2_k[l]), (cmp_pe_v[l], cmp_w1_v[l], cmp_w2_v[l]), w_out[l],
                   router_grp_w[l], router_grp_b[l], router_exp_w[l], router_exp_b[l],
                   expert_w1[l], expert_w3[l], expert_w2[l])
    return x
```

```python
import functools

import jax
import jax.numpy as jnp
import numpy as np
from jax import lax
from jax.experimental import pallas as pl
from jax.experimental.pallas import tpu as pltpu

M_HEADS = 4
M_HEAD_DIM = 128
M_WIDTH = M_HEADS * M_HEAD_DIM
M_CONV = 4
N_HEADS = 8
N_KV_GROUPS = 2
HEADS_PER_GROUP = N_HEADS // N_KV_GROUPS
N_HEAD_DIM = 64
N_WIDTH = N_HEADS * N_HEAD_DIM
KV_WIDTH = N_KV_GROUPS * N_HEAD_DIM
CMP_BLOCK = 32
CMP_STRIDE = 16
CMP_HIDDEN = 2 * N_HEAD_DIM
SLC_BLOCK = 64
SLC_TOPK = 16
WINDOW = 512
ROPE_THETA = 10000.0
MOE_GROUPS = 4
EXPERTS_PER_GROUP = 8
N_EXPERTS = MOE_GROUPS * EXPERTS_PER_GROUP
RMS_EPS = 1e-6

LANE = 128
SUBLANE = 8
VMEM_LIMIT_BYTES = 56 * 1024 * 1024

TOKEN_TILE = 512
MLSTM_CHUNK = 256
NSA_Q_TILE = 256
NSA_K_TILE = 512
NSA_COL_BLOCK = 256
EXPERT_ROWS = 512
GATHER_TILE = 256

F32 = jnp.float32
BF16 = jnp.bfloat16
NEG = -1e30
HIGHEST = lax.Precision.HIGHEST
NT_DIMS = (((1,), (1,)), ((), ()))
TN_DIMS = (((0,), (0,)), ((), ()))


def _params(n_grid):
    return pltpu.CompilerParams(
        dimension_semantics=("arbitrary",) * n_grid, vmem_limit_bytes=VMEM_LIMIT_BYTES)


def _dot(a, b, **kw):
    return jnp.dot(a, b, preferred_element_type=F32, **kw)


def _dot_nt(a, b, **kw):
    return lax.dot_general(a, b, NT_DIMS, preferred_element_type=F32, **kw)


def _dot_tn(a, b, **kw):
    return lax.dot_general(a, b, TN_DIMS, preferred_element_type=F32, **kw)


def _sigmoid(x):
    return 1.0 / (1.0 + jnp.exp(-x))


def _silu(x):
    return x * _sigmoid(x)


def _rms(x, w):
    return x * lax.rsqrt(jnp.mean(x * x, axis=-1, keepdims=True) + RMS_EPS) * w


def _full(shape):
    return pl.BlockSpec(shape, lambda *_: (0,) * len(shape))


def _adaln_kernel(c_ref, w_ref, b_ref, o_ref):
    o_ref[...] = _dot(_silu(c_ref[...]), w_ref[...], precision=HIGHEST) + b_ref[...]


def _adaln(c, ada_w, ada_b):
    B, D = c.shape
    n = ada_w.shape[1] // D
    return pl.pallas_call(
        _adaln_kernel,
        grid=(n,),
        in_specs=[_full((B, D)), pl.BlockSpec((D, D), lambda j: (0, j)), pl.BlockSpec((1, D), lambda j: (0, j))],
        out_specs=pl.BlockSpec((B, D), lambda j: (0, j)),
        out_shape=jax.ShapeDtypeStruct((B, ada_w.shape[1]), F32),
        compiler_params=_params(1),
        name="adaln",
    )(c, ada_w, ada_b[None, :])


_C_M = 0
_C_Q = 4 * M_WIDTH
_C_KC = _C_Q + N_WIDTH
_C_KS = _C_KC + KV_WIDTH
_C_KW = _C_KS + KV_WIDTH
_C_VC = _C_KW + KV_WIDTH
_C_END = _C_VC + KV_WIDTH
_R_VS = 0
_R_VW = KV_WIDTH
_R_NG = 2 * KV_WIDTH
_R_END = _R_NG + 4 * SUBLANE


def _inproj_kernel(x_ref, mod_ref, nw_ref, w_ref, wt_ref, wg_ref, gb_ref, pos_ref, inv_ref,
                   zm_ref, q_ref, kc_ref, ks_ref, kw_ref, vc_ref, vst_ref, vwt_ref, ngt_ref, gt_ref):
    mod = mod_ref[0]
    h = _rms(x_ref[...], nw_ref[...]) * (1.0 + mod[1:2]) + mod[0:1]
    hb = h.astype(BF16)
    z = _dot(hb, w_ref[...])
    zm_ref[...] = z[:, _C_M:_C_Q]
    gt_ref[...] = _dot_nt(wg_ref[...], h, precision=HIGHEST) + gb_ref[...]
    zt = _dot_nt(wt_ref[...], hb)
    vst_ref[...] = zt[_R_VS:_R_VW].astype(BF16)
    vwt_ref[...] = zt[_R_VW:_R_NG].astype(BF16)
    ngt_ref[...] = _sigmoid(zt[_R_NG:_R_END])

    ang = pos_ref[...] * inv_ref[...]
    cos = jnp.cos(ang)
    sin = jnp.sin(ang)
    lane = lax.broadcasted_iota(jnp.int32, (1, LANE), 1)
    first = (lane % N_HEAD_DIM) < (N_HEAD_DIM // 2)
    sin_signed = jnp.where(first, -sin, sin)

    def rope(slab):
        rot = jnp.where(first, pltpu.roll(slab, LANE - N_HEAD_DIM // 2, 1), pltpu.roll(slab, N_HEAD_DIM // 2, 1))
        return slab * cos + rot * sin_signed

    scale = N_HEAD_DIM ** -0.5
    for j in range(N_WIDTH // LANE):
        q_ref[:, j * LANE:(j + 1) * LANE] = (rope(z[:, _C_Q + j * LANE:_C_Q + (j + 1) * LANE]) * scale).astype(BF16)
    kc_ref[...] = rope(z[:, _C_KC:_C_KS])
    ks_ref[...] = rope(z[:, _C_KS:_C_KW]).astype(BF16)
    kw_ref[...] = rope(z[:, _C_KW:_C_VC]).astype(BF16)
    vc_ref[...] = z[:, _C_VC:_C_END]


def _inproj(x2, mod3, norm_w, w_main, w_t, wg_t, gate_b, pos_col, inv_row, S):
    T, D = x2.shape
    TM = min(TOKEN_TILE, S)
    tiles_per_seq = S // TM
    row = lambda w: pl.BlockSpec((TM, w), lambda i: (i, 0))
    col = lambda r: pl.BlockSpec((r, TM), lambda i: (0, i))
    outs = [(4 * M_WIDTH, F32), (N_WIDTH, BF16), (KV_WIDTH, F32), (KV_WIDTH, BF16), (KV_WIDTH, BF16), (KV_WIDTH, F32)]
    outs_t = [(KV_WIDTH, BF16), (KV_WIDTH, BF16), (_R_END - _R_NG, F32), (2 * M_HEADS, F32)]
    return pl.pallas_call(
        _inproj_kernel,
        grid=(T // TM,),
        in_specs=[row(D), pl.BlockSpec((1, 6, D), lambda i: (i // tiles_per_seq, 0, 0)), _full((1, D)),
                  _full(w_main.shape), _full(w_t.shape), _full(wg_t.shape), _full((2 * M_HEADS, 1)), row(1),
                  _full((1, LANE))],
        out_specs=[row(w) for w, _ in outs] + [col(r) for r, _ in outs_t],
        out_shape=[jax.ShapeDtypeStruct((T, w), dt) for w, dt in outs]
        + [jax.ShapeDtypeStruct((r, T), dt) for r, dt in outs_t],
        compiler_params=_params(1),
        name="inproj",
    )(x2, mod3, norm_w[None, :], w_main, w_t, wg_t, gate_b[:, None], pos_col, inv_row)


def _mlstm_kernel(zm_ref, gt_ref, cw_ref, hn_ref, ltri_ref, eye_ref, y_ref, buf, c_s, n_s, m_s):
    L = zm_ref.shape[0]
    QK = 2 * M_WIDTH
    DH = M_HEAD_DIM

    @pl.when(pl.program_id(1) == 0)
    def _():
        buf[0:SUBLANE, :] = jnp.zeros((SUBLANE, QK), F32)
        c_s[...] = jnp.zeros_like(c_s)
        n_s[...] = jnp.zeros_like(n_s)
        m_s[...] = jnp.zeros_like(m_s)

    buf[SUBLANE:SUBLANE + L, :] = zm_ref[:, 0:QK]
    cw = cw_ref[...]
    conv = cw[M_CONV - 1:M_CONV] * buf[SUBLANE:SUBLANE + L, :]
    for j in range(M_CONV - 1):
        off = SUBLANE - (M_CONV - 1) + j
        conv = conv + cw[j:j + 1] * buf[off:off + L, :]
    buf[0:SUBLANE, :] = buf[L:L + SUBLANE, :]
    qk = _silu(conv)

    g = gt_ref[...]
    fp = g[M_HEADS:]
    lf = jnp.minimum(fp, 0.0) - jnp.log(1.0 + jnp.exp(-jnp.abs(fp)))
    lf8 = jnp.concatenate([lf, jnp.zeros_like(lf)], axis=0)
    ltri = ltri_ref[...]
    b_rows = _dot_nt(lf8, ltri, precision=HIGHEST)
    b_cols = _dot_nt(ltri, lf8, precision=HIGHEST)
    i_cols = _dot_nt(eye_ref[...], g, precision=HIGHEST)

    causal = lax.broadcasted_iota(jnp.int32, (L, L), 0) >= lax.broadcasted_iota(jnp.int32, (L, L), 1)
    for h in range(M_HEADS):
        hs = slice(h * DH, (h + 1) * DH)
        b_r, i_r = b_rows[h:h + 1], g[h:h + 1]
        b_c, i_c = b_cols[:, h:h + 1], i_cols[:, h:h + 1]
        m_prev = m_s[h:h + 1, 0:1]
        g_tot = b_r[:, L - 1:L]
        m_loc = jnp.max(g_tot - b_r + i_r, axis=1, keepdims=True)
        m_new = jnp.maximum(g_tot + m_prev, m_loc)

        q = qk[:, hs]
        k = qk[:, M_WIDTH + h * DH:M_WIDTH + (h + 1) * DH] * (DH ** -0.5)
        v = zm_ref[:, 2 * M_WIDTH + h * DH:2 * M_WIDTH + (h + 1) * DH]
        o = zm_ref[:, 3 * M_WIDTH + h * DH:3 * M_WIDTH + (h + 1) * DH]
        qb, kb, vb = q.astype(BF16), k.astype(BF16), v.astype(BF16)

        dlog = jnp.where(causal, b_c - b_r + i_r, -jnp.inf)
        inter = b_c + m_prev
        m_t = jnp.maximum(inter, jnp.max(dlog, axis=1, keepdims=True))
        wts = jnp.exp(dlog - m_t) * _dot_nt(qb, kb)
        dec = jnp.exp(inter - m_t)
        c_prev = c_s[h]
        n_prev = n_s[h:h + 1]
        num = _dot(wts.astype(BF16), vb) + dec * _dot(qb, c_prev.astype(BF16))
        den = jnp.sum(wts, axis=1, keepdims=True) + dec * jnp.sum(q * n_prev, axis=1, keepdims=True)
        hh = num / jnp.maximum(jnp.abs(den), jnp.exp(-m_t))

        kw = k * jnp.exp(g_tot - b_c + i_c - m_new)
        keep = jnp.exp(g_tot + m_prev - m_new)
        c_s[h] = keep * c_prev + _dot_tn(kw.astype(BF16), vb)
        n_s[h:h + 1] = keep * n_prev + jnp.sum(kw, axis=0, keepdims=True)
        m_s[h:h + 1] = jnp.broadcast_to(m_new, (1, LANE))

        hn = _rms(hh, hn_ref[:, hs])
        y_ref[:, hs] = (_sigmoid(o) * hn).astype(BF16)


def _mlstm(zm, gt, conv_w, head_norm, B, S):
    T = zm.shape[0]
    L = min(MLSTM_CHUNK, S)
    nc = S // L
    ltri = jnp.asarray(np.tril(np.ones((L, L), np.float32)))
    eye = jnp.asarray(np.eye(L, dtype=np.float32))
    return pl.pallas_call(
        _mlstm_kernel,
        grid=(B, nc),
        in_specs=[pl.BlockSpec((L, 4 * M_WIDTH), lambda b, c: (b * nc + c, 0)),
                  pl.BlockSpec((2 * M_HEADS, L), lambda b, c: (0, b * nc + c)),
                  _full((M_CONV, 2 * M_WIDTH)), _full((1, M_WIDTH)), _full((L, L)), _full((L, L))],
        out_specs=pl.BlockSpec((L, M_WIDTH), lambda b, c: (b * nc + c, 0)),
        out_shape=jax.ShapeDtypeStruct((T, M_WIDTH), BF16),
        scratch_shapes=[pltpu.VMEM((L + SUBLANE, 2 * M_WIDTH), F32),
                        pltpu.VMEM((M_HEADS, M_HEAD_DIM, M_HEAD_DIM), F32),
                        pltpu.VMEM((SUBLANE, M_HEAD_DIM), F32),
                        pltpu.VMEM((SUBLANE, LANE), F32)],
        compiler_params=_params(2),
        name="mlstm",
    )(zm, gt, conv_w, head_norm[None, :], ltri, eye)


def _compress_kernel(k_ref, v_ref, pak, pbk, wak, wbk, w2k, pav, pbv, wav, wbv, w2v, ko_ref, vo_ref):
    def one(x_ref, pa, pb, wa, wb, w2, o_ref, transposed):
        n = x_ref.shape[0] // CMP_STRIDE
        x = jnp.concatenate([x_ref[pl.ds(t, n, stride=CMP_STRIDE), :] for t in range(CMP_STRIDE)], axis=1)
        first = _dot((x + pa[...]).astype(BF16), wa[...])
        second = _dot((x + pb[...]).astype(BF16), wb[...])
        hid = _silu(first + pltpu.roll(second, n - 1, 0)).astype(BF16)
        o_ref[0] = (_dot_nt(w2[...], hid) if transposed else _dot(hid, w2[...])).astype(BF16)

    one(k_ref, pak, pbk, wak, wbk, w2k, ko_ref, False)
    one(v_ref, pav, pbv, wav, wbv, w2v, vo_ref, True)


def _compress_weights(pe, w1, w2):
    half = CMP_BLOCK // 2
    eye = jnp.eye(N_KV_GROUPS, dtype=F32)
    w1r = w1.reshape(CMP_BLOCK, N_HEAD_DIM, CMP_HIDDEN)
    big = lambda w: jnp.einsum("ldc,gh->lgdhc", w, eye).reshape(half * KV_WIDTH, N_KV_GROUPS * CMP_HIDDEN).astype(BF16)
    pe_row = lambda p: jnp.broadcast_to(p[:, None, :], (half, N_KV_GROUPS, N_HEAD_DIM)).reshape(1, half * KV_WIDTH)
    w2bd = jnp.einsum("cd,gh->gchd", w2, eye).reshape(N_KV_GROUPS * CMP_HIDDEN, KV_WIDTH).astype(BF16)
    return pe_row(pe[:half]), pe_row(pe[half:]), big(w1r[:half]), big(w1r[half:]), w2bd


def _compress(kc, vc, wk, wv, B, S):
    nc = S // CMP_STRIDE
    blk = pl.BlockSpec((S, KV_WIDTH), lambda b: (b, 0))
    wv = wv[:4] + (wv[4].T,)
    return pl.pallas_call(
        _compress_kernel,
        grid=(B,),
        in_specs=[blk, blk] + [_full(w.shape) for w in wk] + [_full(w.shape) for w in wv],
        out_specs=[pl.BlockSpec((1, nc, KV_WIDTH), lambda b: (b, 0, 0)),
                   pl.BlockSpec((1, KV_WIDTH, nc), lambda b: (b, 0, 0))],
        out_shape=[jax.ShapeDtypeStruct((B, nc, KV_WIDTH), BF16), jax.ShapeDtypeStruct((B, KV_WIDTH, nc), BF16)],
        compiler_params=_params(1),
        name="compress",
    )(kc, vc, *wk, *wv)


def _masked_softmax_keys(s, mask):
    s = jnp.where(mask, s, -jnp.inf)
    m = jnp.max(s, axis=0, keepdims=True)
    m = jnp.where(jnp.isfinite(m), m, 0.0)
    e = jnp.exp(s - m)
    return e, 1.0 / jnp.maximum(jnp.sum(e, axis=0, keepdims=True), 1e-30)


def _nsa_kernel(q_ref, kc_ref, vct_ref, ks_ref, vst_ref, kw_ref, vwt_ref, ngt_ref, ov_ref, ek_ref, eye_ref,
                y_ref, m_s, l_s, acc_s, *, n_top, TK):
    TQ = q_ref.shape[0]
    NS = ov_ref.shape[0]
    G, HPG, DH = N_KV_GROUPS, HEADS_PER_GROUP, N_HEAD_DIM
    R = G * HPG * TQ
    q0 = pl.program_id(1) * TQ

    q = q_ref[...]
    zero = jnp.zeros((TQ, DH), BF16)
    parts = []
    for hd in range(N_HEADS):
        qh = q[:, hd * DH:(hd + 1) * DH]
        parts.append(jnp.concatenate([qh, zero] if hd < HPG else [zero, qh], axis=1))
    qp = jnp.concatenate(parts, axis=0)
    t_q = q0 + lax.broadcasted_iota(jnp.int32, (1, R), 1) % TQ

    kc = kc_ref[0]
    NC = kc.shape[0]
    cmp_end = lax.broadcasted_iota(jnp.int32, (NC, 1), 0) * CMP_STRIDE + (CMP_BLOCK - 1)
    e_c, inv_c = _masked_softmax_keys(_dot_nt(kc, qp), cmp_end <= t_q)
    p_c = e_c * inv_c
    o_c = _dot(vct_ref[0], p_c.astype(BF16))
    p_grp = jnp.concatenate(
        [sum(p_c[:, (g * HPG + h) * TQ:(g * HPG + h + 1) * TQ] for h in range(HPG)) for g in range(G)], axis=1)
    imp = _dot(ov_ref[...], p_grp, precision=HIGHEST)

    j_io = lax.broadcasted_iota(jnp.int32, (NS, G * TQ), 0)
    t_row = q0 + lax.broadcasted_iota(jnp.int32, (NS, G * TQ), 1) % TQ
    cur = t_row // SLC_BLOCK
    forced = (j_io == 0) | (j_io == cur) | (j_io == cur - 1)
    future = j_io * SLC_BLOCK > t_row
    score = jnp.where(forced, jnp.inf, jnp.where(future, -jnp.inf, imp))
    rank = jnp.zeros((NS, G * TQ), F32)
    for jp in range(NS):
        other = score[jp:jp + 1, :]
        rank = rank + jnp.where(j_io > jp, jnp.where(other >= score, 1.0, 0.0), jnp.where(other > score, 1.0, 0.0))
    sel_bias = jnp.where(future, NEG, jnp.where(rank < n_top, 0.0, NEG)).astype(BF16)
    sel_rows = _dot_tn(sel_bias, eye_ref[...]).astype(BF16)
    sel_rows = jnp.broadcast_to(sel_rows.reshape(G, 1, TQ, LANE), (G, HPG, TQ, LANE)).reshape(R, LANE)
    qa = jnp.concatenate([qp, sel_rows], axis=1)

    m_s[...] = jnp.full_like(m_s, NEG)
    l_s[...] = jnp.zeros_like(l_s)
    acc_s[...] = jnp.zeros_like(acc_s)

    CB = NSA_COL_BLOCK

    def step(kt, causal):
        k0 = pl.multiple_of(kt * TK, TK)
        ka = jnp.concatenate([ks_ref[0, pl.ds(k0, TK), :], ek_ref[pl.ds(k0, TK), :]], axis=1)
        vt = vst_ref[:, pl.ds(k0, TK)]
        kpos = k0 + lax.broadcasted_iota(jnp.int32, (TK, 1), 0)
        scores = [_dot_nt(ka, qa[j * CB:(j + 1) * CB]) for j in range(R // CB)]
        probs, alphas = [], []
        for j in range(R // CB):
            cols = slice(j * CB, (j + 1) * CB)
            s = scores[j]
            if causal:
                s = jnp.where(kpos <= t_q[:, cols], s, NEG)
            m_old = m_s[:, cols]
            m_new = jnp.maximum(m_old, jnp.max(s, axis=0, keepdims=True))
            alpha = jnp.exp(m_old - m_new)
            p = jnp.exp(s - m_new)
            l_s[:, cols] = alpha * l_s[:, cols] + jnp.sum(p, axis=0, keepdims=True)
            m_s[:, cols] = m_new
            probs.append(p.astype(BF16))
            alphas.append(alpha)
        for j in range(R // CB):
            cols = slice(j * CB, (j + 1) * CB)
            acc_s[:, cols] = alphas[j] * acc_s[:, cols] + _dot(vt, probs[j])

    def full_tile(kt, carry):
        step(kt, False)
        return carry

    last = q0 // TK
    lax.fori_loop(0, last, full_tile, 0)
    step(last, True)
    o_s = acc_s[...] * (1.0 / l_s[...])

    WK = WINDOW + TQ
    start = pl.multiple_of(jnp.maximum(q0 - WINDOW, 0), TQ)
    wpos = start + lax.broadcasted_iota(jnp.int32, (WK, 1), 0)
    e_w, inv_w = _masked_softmax_keys(_dot_nt(kw_ref[0, pl.ds(start, WK), :], qp),
                                      (wpos <= t_q) & (wpos > t_q - WINDOW))
    o_w = _dot(vwt_ref[:, pl.ds(start, WK)], e_w.astype(BF16)) * inv_w

    gates = ngt_ref[...]
    outs = []
    for hd in range(N_HEADS):
        cols = slice(hd * TQ, (hd + 1) * TQ)
        rows = slice((hd // HPG) * DH, (hd // HPG + 1) * DH)
        outs.append(gates[3 * hd:3 * hd + 1] * o_c[rows, cols]
                    + gates[3 * hd + 1:3 * hd + 2] * o_s[rows, cols]
                    + gates[3 * hd + 2:3 * hd + 3] * o_w[rows, cols])
    y_ref[...] = jnp.concatenate(outs, axis=0).T.astype(BF16)


def _nsa(q, kcmp, vcmp_t, ks, vs_t, kw, vw_t, ng_t, B, S):
    T = q.shape[0]
    TQ = min(NSA_Q_TILE, S)
    TK = min(NSA_K_TILE, S)
    nq = S // TQ
    NS = S // SLC_BLOCK
    NC = S // CMP_STRIDE
    n_cmp = (S - CMP_BLOCK) // CMP_STRIDE + 1
    js = np.arange(NS)[:, None] * SLC_BLOCK
    cs = np.arange(NC)[None, :] * CMP_STRIDE
    ov = np.clip(np.minimum(js + SLC_BLOCK, cs + CMP_BLOCK) - np.maximum(js, cs), 0, None) / CMP_STRIDE
    ov[:, n_cmp:] = 0.0
    assert NS <= LANE and TK % TQ == 0
    block_of_key = (np.arange(S)[:, None] // SLC_BLOCK == np.arange(LANE)[None, :]).astype(np.float32)
    eye_pad = np.eye(NS, LANE, dtype=np.float32)
    R = N_HEADS * TQ
    assert R % NSA_COL_BLOCK == 0
    seq = lambda: pl.BlockSpec((1, S, KV_WIDTH), lambda b, i: (b, 0, 0))
    seq_t = lambda: pl.BlockSpec((KV_WIDTH, S), lambda b, i: (0, b))
    return pl.pallas_call(
        functools.partial(_nsa_kernel, n_top=min(SLC_TOPK, NS), TK=TK),
        grid=(B, nq),
        in_specs=[pl.BlockSpec((TQ, N_WIDTH), lambda b, i: (b * nq + i, 0)),
                  pl.BlockSpec((1, NC, KV_WIDTH), lambda b, i: (b, 0, 0)),
                  pl.BlockSpec((1, KV_WIDTH, NC), lambda b, i: (b, 0, 0)),
                  seq(), seq_t(), seq(), seq_t(), pl.BlockSpec((ng_t.shape[0], TQ), lambda b, i: (0, b * nq + i)),
                  _full((NS, NC)), _full((S, LANE)), _full((NS, LANE))],
        out_specs=pl.BlockSpec((TQ, N_WIDTH), lambda b, i: (b * nq + i, 0)),
        out_shape=jax.ShapeDtypeStruct((T, N_WIDTH), BF16),
        scratch_shapes=[pltpu.VMEM((1, R), F32), pltpu.VMEM((1, R), F32), pltpu.VMEM((KV_WIDTH, R), F32)],
        compiler_params=_params(2),
        name="nsa",
    )(q, kcmp, vcmp_t, ks.reshape(B, S, KV_WIDTH), vs_t, kw.reshape(B, S, KV_WIDTH), vw_t, ng_t,
      jnp.asarray(ov.astype(np.float32)), jnp.asarray(block_of_key, dtype=BF16), jnp.asarray(eye_pad, dtype=BF16))


def _post_kernel(ym_ref, yn_ref, wm_ref, wn_ref, x_ref, mod_ref, npost_ref, npre_ref, wrh_ref, wrl_ref, br_ref,
                 ltri_ref, x1_ref, h2_ref, route_ref, cnt_ref, carry):
    TM = x_ref.shape[0]

    @pl.when(pl.program_id(0) == 0)
    def _():
        carry[...] = jnp.zeros_like(carry)

    mod = mod_ref[0]
    y = _dot(ym_ref[...], wm_ref[...]) + _dot(yn_ref[...], wn_ref[...])
    x1 = x_ref[...] + mod[2:3] * _rms(y, npost_ref[...])
    x1_ref[...] = x1
    h2 = _rms(x1, npre_ref[...]) * (1.0 + mod[4:5]) + mod[3:4]
    for s in range(SUBLANE):
        h2_ref[pl.ds(s, TM, stride=SUBLANE), :] = h2[:, s * LANE:(s + 1) * LANE]

    h_hi = h2.astype(BF16)
    h_lo = (h2 - h_hi.astype(F32)).astype(BF16)
    logits = _dot(h_hi, wrh_ref[...]) + _dot(h_lo, wrh_ref[...]) + _dot(h_hi, wrl_ref[...]) + br_ref[...]
    lane = lax.broadcasted_iota(jnp.int32, (TM, LANE), 1)
    is_grp = lane < MOE_GROUPS
    lg = jnp.where(is_grp, logits, -jnp.inf)
    eg = jnp.exp(lg - jnp.max(lg, axis=1, keepdims=True))
    pg = eg / jnp.sum(eg, axis=1, keepdims=True)
    pg_top = jnp.max(pg, axis=1, keepdims=True)
    grp = jnp.min(jnp.where(is_grp & (pg == pg_top), lane, LANE), axis=1, keepdims=True)
    lo = MOE_GROUPS + EXPERTS_PER_GROUP * grp
    in_grp = (lane >= lo) & (lane < lo + EXPERTS_PER_GROUP)
    le = jnp.where(in_grp, logits, -jnp.inf)
    ee = jnp.exp(le - jnp.max(le, axis=1, keepdims=True))
    pe = jnp.where(in_grp, ee / jnp.sum(ee, axis=1, keepdims=True), -1.0)
    p1 = jnp.max(pe, axis=1, keepdims=True)
    i1 = jnp.min(jnp.where(pe == p1, lane, LANE), axis=1, keepdims=True)
    pe2 = jnp.where(lane == i1, -1.0, pe)
    p2 = jnp.max(pe2, axis=1, keepdims=True)
    i2 = jnp.min(jnp.where((pe2 == p2) & in_grp & (lane != i1), lane, LANE), axis=1, keepdims=True)
    den = p1 + p2
    e1 = i1 - MOE_GROUPS
    e2 = i2 - MOE_GROUPS
    oh1, oh2 = lane == e1, lane == e2
    oh = jnp.where(oh1, 1.0, 0.0) + jnp.where(oh2, 1.0, 0.0)
    before = _dot(ltri_ref[...], oh.astype(BF16)) + carry[0:1, :]
    r1 = jnp.sum(jnp.where(oh1, before, 0.0), axis=1, keepdims=True)
    r2 = jnp.sum(jnp.where(oh2, before, 0.0), axis=1, keepdims=True)
    carry[...] = carry[...] + jnp.sum(oh, axis=0, keepdims=True)
    cnt_ref[...] = carry[...]
    cols = (e1.astype(F32), e2.astype(F32), pg_top * p1 / den, pg_top * p2 / den, r1, r2)
    route = jnp.zeros((TM, LANE), F32)
    for k, col in enumerate(cols):
        route = jnp.where(lane == k, col, route)
    route_ref[...] = route


def _post(ym, yn, w_out, x2, mod3, norm_post, norm_pre, wr, br, S):
    T, D = x2.shape
    TM = min(TOKEN_TILE, S)
    tiles_per_seq = S // TM
    row = lambda w: pl.BlockSpec((TM, w), lambda i: (i, 0))
    small = pl.BlockSpec((SUBLANE, LANE), lambda i: (0, 0))
    ltri = jnp.asarray(np.tril(np.ones((TM, TM), np.float32), -1), dtype=BF16)
    return pl.pallas_call(
        _post_kernel,
        grid=(T // TM,),
        in_specs=[row(M_WIDTH), row(N_WIDTH), _full((M_WIDTH, D)), _full((N_WIDTH, D)), row(D),
                  pl.BlockSpec((1, 6, D), lambda i: (i // tiles_per_seq, 0, 0)), _full((1, D)), _full((1, D)),
                  _full((D, LANE)), _full((D, LANE)), _full((1, LANE)), _full((TM, TM))],
        out_specs=[row(D), pl.BlockSpec((TM * SUBLANE, LANE), lambda i: (i, 0)), row(LANE), small],
        out_shape=[jax.ShapeDtypeStruct((T, D), F32), jax.ShapeDtypeStruct((T * SUBLANE, LANE), F32),
                   jax.ShapeDtypeStruct((T, LANE), F32), jax.ShapeDtypeStruct((SUBLANE, LANE), F32)],
        scratch_shapes=[pltpu.VMEM((SUBLANE, LANE), F32)],
        compiler_params=_params(1),
        name="post_mix_router",
    )(ym, yn, w_out[:M_WIDTH].astype(BF16), w_out[M_WIDTH:].astype(BF16), x2, mod3,
      norm_post[None, :], norm_pre[None, :], wr.astype(BF16), (wr - wr.astype(BF16).astype(F32)).astype(BF16), br,
      ltri)


def _plan_kernel(cnt_ref, pstart_ref, blk_ref, tot_ref, *, rows_per_block):
    lane = lax.broadcasted_iota(jnp.int32, (SUBLANE, LANE), 1)
    nblk = (cnt_ref[...].astype(jnp.int32) + (rows_per_block - 1)) // rows_per_block
    end = nblk
    sh = 1
    while sh < N_EXPERTS:
        end = end + jnp.where(lane >= sh, pltpu.roll(end, sh, 1), 0)
        sh *= 2
    pstart_ref[...] = (end - nblk) * rows_per_block
    nbp = blk_ref.shape[0]
    blk_io = lax.broadcasted_iota(jnp.int32, (nbp, LANE), 0)
    lane_b = lax.broadcasted_iota(jnp.int32, (nbp, LANE), 1)
    passed = jnp.where((lane_b < N_EXPERTS) & (blk_io >= end[0:1, :]), 1, 0)
    blk_ref[...] = jnp.broadcast_to(jnp.minimum(jnp.sum(passed, axis=1, keepdims=True), N_EXPERTS - 1), (nbp, LANE))
    tot_ref[...] = jnp.broadcast_to(
        jnp.sum(jnp.where(lane == N_EXPERTS - 1, end, 0), axis=1, keepdims=True), (SUBLANE, LANE))


def _sort_plan(cnt, n_blocks):
    small = pl.BlockSpec((SUBLANE, LANE), lambda i: (0, 0))
    nbp = -(-n_blocks // SUBLANE) * SUBLANE
    pstart, blk, tot = pl.pallas_call(
        functools.partial(_plan_kernel, rows_per_block=EXPERT_ROWS),
        grid=(1,),
        in_specs=[small],
        out_specs=[small, _full((nbp, LANE)), small],
        out_shape=[jax.ShapeDtypeStruct((SUBLANE, LANE), jnp.int32), jax.ShapeDtypeStruct((nbp, LANE), jnp.int32),
                   jax.ShapeDtypeStruct((SUBLANE, LANE), jnp.int32)],
        compiler_params=_params(1),
        name="expert_plan",
    )(cnt)
    return pstart[0, :N_EXPERTS], blk[:n_blocks, 0], tot[0, :1]


_ASG = 4


def _dest_row(pstart_ref, asg_ref, j, k):
    return pstart_ref[asg_ref[0, 0, _ASG * j + k]] + asg_ref[0, 0, _ASG * j + 2 + k]


def _row_copy(src, src_row, dst, dst_row, sem):
    return pltpu.make_async_copy(src.at[pl.ds(pl.multiple_of(src_row * SUBLANE, SUBLANE), SUBLANE)],
                                 dst.at[pl.ds(pl.multiple_of(dst_row * SUBLANE, SUBLANE), SUBLANE)], sem)


def _dispatch_kernel(pstart_ref, asg_ref, h2_ref, xs_in, xs_hbm, sem):
    del xs_in
    n = asg_ref.shape[2] // _ASG

    def issue(j, carry):
        for k in range(2):
            _row_copy(h2_ref, j, xs_hbm, _dest_row(pstart_ref, asg_ref, j, k), sem).start()
        return carry

    lax.fori_loop(0, n, issue, 0)
    whole = xs_hbm.at[pl.ds(0, 2 * n * SUBLANE)]
    pltpu.make_async_copy(whole, whole, sem).wait()


def _dispatch(pstart, asg3, h2t, n_rows):
    nt, _, width = asg3.shape
    n = width // _ASG
    xs0 = jnp.zeros((n_rows * SUBLANE, LANE), F32)
    return pl.pallas_call(
        _dispatch_kernel,
        grid=(nt,),
        in_specs=[pl.BlockSpec(memory_space=pltpu.SMEM),
                  pl.BlockSpec((1, 1, width), lambda i: (i, 0, 0), memory_space=pltpu.SMEM),
                  pl.BlockSpec((n * SUBLANE, LANE), lambda i: (i, 0)), pl.BlockSpec(memory_space=pl.ANY)],
        out_specs=pl.BlockSpec(memory_space=pl.ANY),
        out_shape=jax.ShapeDtypeStruct(xs0.shape, F32),
        scratch_shapes=[pltpu.SemaphoreType.DMA(())],
        input_output_aliases={3: 0},
        compiler_params=_params(1),
        name="dispatch",
    )(pstart, asg3, h2t, xs0)


def _expert_kernel(be_ref, nb_ref, xs_ref, w1_ref, w3_ref, w2_ref, ys_ref, w1b, w3b, w2b):
    RB = xs_ref.shape[0] // SUBLANE
    i = pl.program_id(0)

    @pl.when(i < nb_ref[0])
    def _():
        @pl.when((i == 0) | (be_ref[i] != be_ref[jnp.maximum(i - 1, 0)]))
        def _():
            w1b[...] = w1_ref[0].astype(BF16)
            w3b[...] = w3_ref[0].astype(BF16)
            w2b[...] = w2_ref[0].astype(BF16)

        x = jnp.concatenate([xs_ref[pl.ds(s, RB, stride=SUBLANE), :] for s in range(SUBLANE)], axis=1).astype(BF16)
        hb = _silu(_dot(x, w1b[...])) * _dot(x, w3b[...])
        y = _dot(hb.astype(BF16), w2b[...])
        for s in range(SUBLANE):
            ys_ref[pl.ds(s, RB, stride=SUBLANE), :] = y[:, s * LANE:(s + 1) * LANE]

    @pl.when(i >= nb_ref[0])
    def _():
        ys_ref[...] = jnp.zeros_like(ys_ref)


def _experts(block_expert, n_used, xs, w1, w3, w2):
    n_blocks = block_expert.shape[0]
    RB = EXPERT_ROWS
    D, F = w1.shape[1], w1.shape[2]
    cur = lambda i, nb: jnp.minimum(i, nb[0] - 1)
    rows = pl.BlockSpec((RB * SUBLANE, LANE), lambda i, be, nb: (cur(i, nb), 0))
    wspec = lambda a, b: pl.BlockSpec((1, a, b), lambda i, be, nb: (be[cur(i, nb)], 0, 0))
    return pl.pallas_call(
        _expert_kernel,
        grid_spec=pltpu.PrefetchScalarGridSpec(
            num_scalar_prefetch=2, grid=(n_blocks,),
            in_specs=[rows, wspec(D, F), wspec(D, F), wspec(F, D)],
            out_specs=pl.BlockSpec((RB * SUBLANE, LANE), lambda i, be, nb: (i, 0)),
            scratch_shapes=[pltpu.VMEM((D, F), BF16), pltpu.VMEM((D, F), BF16), pltpu.VMEM((F, D), BF16)]),
        out_shape=jax.ShapeDtypeStruct(xs.shape, F32),
        compiler_params=_params(1),
        name="experts",
    )(block_expert, n_used, xs, w1, w3, w2)


def _combine_kernel(pstart_ref, asg_ref, asg_next_ref, ys_hbm, x1_ref, route_ref, mod_ref, nw_ref, o_ref, buf, sems):
    n = x1_ref.shape[0]
    i = pl.program_id(0)
    slot = i % 2

    def gather(dref, into):
        def issue(j, carry):
            for k in range(2):
                _row_copy(ys_hbm, _dest_row(pstart_ref, dref, j, k), buf.at[into], k * n + j, sems.at[into]).start()
            return carry

        lax.fori_loop(0, n, issue, 0)

    @pl.when(i == 0)
    def _():
        gather(asg_ref, 0)

    @pl.when(i + 1 < pl.num_programs(0))
    def _():
        gather(asg_next_ref, 1 - slot)

    pltpu.make_async_copy(ys_hbm.at[pl.ds(0, 2 * n * SUBLANE)], buf.at[slot], sems.at[slot]).wait()

    route = route_ref[...]
    g0, g1 = route[:, 2:3], route[:, 3:4]
    y = jnp.concatenate(
        [g0 * buf[slot, pl.ds(s, n, stride=SUBLANE), :] + g1 * buf[slot, pl.ds(n * SUBLANE + s, n, stride=SUBLANE), :]
         for s in range(SUBLANE)], axis=1)
    o_ref[...] = x1_ref[...] + mod_ref[0][5:6] * _rms(y, nw_ref[...])


def _combine(pstart, asg3, ys, x1, route, mod3, norm_w, S):
    T, D = x1.shape
    nt, _, width = asg3.shape
    n = width // _ASG
    tiles_per_seq = S // n
    return pl.pallas_call(
        _combine_kernel,
        grid=(nt,),
        in_specs=[pl.BlockSpec(memory_space=pltpu.SMEM),
                  pl.BlockSpec((1, 1, width), lambda i: (i, 0, 0), memory_space=pltpu.SMEM),
                  pl.BlockSpec((1, 1, width), lambda i: (jnp.minimum(i + 1, nt - 1), 0, 0), memory_space=pltpu.SMEM),
                  pl.BlockSpec(memory_space=pl.ANY), pl.BlockSpec((n, D), lambda i: (i, 0)),
                  pl.BlockSpec((n, LANE), lambda i: (i, 0)),
                  pl.BlockSpec((1, 6, D), lambda i: (i // tiles_per_seq, 0, 0)), _full((1, D))],
        out_specs=pl.BlockSpec((n, D), lambda i: (i, 0)),
        out_shape=jax.ShapeDtypeStruct((T, D), F32),
        scratch_shapes=[pltpu.VMEM((2, 2 * n * SUBLANE, LANE), F32), pltpu.SemaphoreType.DMA((2,))],
        compiler_params=_params(1),
        name="combine",
    )(pstart, asg3, asg3, ys, x1, route, mod3, norm_w[None, :])


def _layer(x, c, positions, ada_w, ada_b, norm_mix_pre, norm_mix_post, norm_ffn_pre, norm_ffn_post, w_in,
           conv_w, gate_b, head_norm, cmp_k, cmp_v, w_out, wg, bg, we, be, w1, w3, w2):
    B, S, D = x.shape
    T = B * S
    x2 = x.reshape(T, D)
    mod3 = _adaln(c, ada_w, ada_b).reshape(B, 6, D)

    o_mi = 4 * M_WIDTH
    o_nq = o_mi + 2 * M_HEADS
    o_kv = o_nq + N_WIDTH
    kv = lambda i: w_in[:, o_kv + i * KV_WIDTH:o_kv + (i + 1) * KV_WIDTH]
    o_ng = o_kv + 6 * KV_WIDTH
    w_main = jnp.concatenate([w_in[:, :o_mi], w_in[:, o_nq:o_kv], kv(0), kv(2), kv(4), kv(1)], axis=1).astype(BF16)
    w_t = jnp.concatenate([kv(3), kv(5), w_in[:, o_ng:], jnp.zeros((D, _R_END - _R_NG - 3 * N_HEADS), F32)],
                          axis=1).T.astype(BF16)
    wg_t = w_in[:, o_mi:o_nq].T
    half = N_HEAD_DIM // 2
    inv = ROPE_THETA ** (-jnp.arange(half, dtype=F32) / half)
    inv_row = jnp.tile(inv, LANE // half)[None, :]
    pos_col = positions.astype(F32).reshape(T, 1)

    zm, q, kc, ks, kw, vc, vs_t, vw_t, ng_t, gt = _inproj(
        x2, mod3, norm_mix_pre, w_main, w_t, wg_t, gate_b, pos_col, inv_row, S)
    ym = _mlstm(zm, gt, conv_w, head_norm, B, S)
    kcmp, vcmp_t = _compress(kc, vc, _compress_weights(*cmp_k), _compress_weights(*cmp_v), B, S)
    yn = _nsa(q, kcmp, vcmp_t, ks, vs_t, kw, vw_t, ng_t, B, S)

    wr = jnp.concatenate([wg, we, jnp.zeros((D, LANE - MOE_GROUPS - N_EXPERTS), F32)], axis=1)
    br = jnp.concatenate([bg, be, jnp.zeros((LANE - MOE_GROUPS - N_EXPERTS,), F32)])[None, :]
    x1, h2t, route, cnt = _post(ym, yn, w_out, x2, mod3, norm_mix_post, norm_ffn_pre, wr, br, S)

    n_blocks = (2 * T) // EXPERT_ROWS + N_EXPERTS
    pstart, block_expert, n_used = _sort_plan(cnt, n_blocks)
    n_tok = min(GATHER_TILE, S)
    asg = jnp.concatenate([route[:, 0:2], route[:, 4:6]], axis=1).astype(jnp.int32)
    asg3 = asg.reshape(T // n_tok, 1, _ASG * n_tok)
    xs = _dispatch(pstart, asg3, h2t, n_blocks * EXPERT_ROWS)
    ys = _experts(block_expert, n_used, xs, w1, w3, w2)
    out = _combine(pstart, asg3, ys, x1, route, mod3, norm_ffn_post, S)
    return out.reshape(B, S, D)


def kernel(x, c, positions, ada_w, ada_b, norm_mix_pre, norm_mix_post, norm_ffn_pre, norm_ffn_post, w_in, mlstm_conv_w, mlstm_gate_b, mlstm_head_norm, cmp_pe_k, cmp_w1_k, cmp_w2_k, cmp_pe_v, cmp_w1_v, cmp_w2_v, w_out, router_grp_w, router_grp_b, router_exp_w, router_exp_b, expert_w1, expert_w3, expert_w2):
    for l in range(ada_w.shape[0]):
        x = _layer(x, c, positions, ada_w[l], ada_b[l], norm_mix_pre[l], norm_mix_post[l], norm_ffn_pre[l],
                   norm_ffn_post[l], w_in[l], mlstm_conv_w[l], mlstm_gate_b[l], mlstm_head_norm[l],
                   (cmp_pe_k[l], cmp_w1_k[l], cmp_w2_k[l]), (cmp_pe_v[l], cmp_w1_v[l], cmp_w2_v[l]), w_out[l],
                   router_grp_w[l], router_grp_b[l], router_exp_w[l], router_exp_b[l],
                   expert_w1[l], expert_w3[l], expert_w2[l])
    return x
```

```python
import functools

import jax
import jax.numpy as jnp
import numpy as np
from jax import lax
from jax.experimental import pallas as pl
from jax.experimental.pallas import tpu as pltpu

M_HEADS = 4
M_HEAD_DIM = 128
M_WIDTH = M_HEADS * M_HEAD_DIM
M_CONV = 4
N_HEADS = 8
N_KV_GROUPS = 2
HEADS_PER_GROUP = N_HEADS // N_KV_GROUPS
N_HEAD_DIM = 64
N_WIDTH = N_HEADS * N_HEAD_DIM
KV_WIDTH = N_KV_GROUPS * N_HEAD_DIM
CMP_BLOCK = 32
CMP_STRIDE = 16
CMP_HIDDEN = 2 * N_HEAD_DIM
SLC_BLOCK = 64
SLC_TOPK = 16
WINDOW = 512
ROPE_THETA = 10000.0
MOE_GROUPS = 4
EXPERTS_PER_GROUP = 8
N_EXPERTS = MOE_GROUPS * EXPERTS_PER_GROUP
RMS_EPS = 1e-6

LANE = 128
SUBLANE = 8
VMEM_LIMIT_BYTES = 56 * 1024 * 1024

TOKEN_TILE = 512
MLSTM_CHUNK = 256
NSA_Q_TILE = 256
NSA_K_TILE = 512
NSA_COL_BLOCK = 256
EXPERT_ROWS = 512
GATHER_TILE = 256
ISSUE_UNROLL = 8

F32 = jnp.float32
BF16 = jnp.bfloat16
NEG = -1e30
HIGHEST = lax.Precision.HIGHEST
NT_DIMS = (((1,), (1,)), ((), ()))
TN_DIMS = (((0,), (0,)), ((), ()))


def _params(n_grid):
    return pltpu.CompilerParams(
        dimension_semantics=("arbitrary",) * n_grid, vmem_limit_bytes=VMEM_LIMIT_BYTES)


def _dot(a, b, **kw):
    return jnp.dot(a, b, preferred_element_type=F32, **kw)


def _dot_nt(a, b, **kw):
    return lax.dot_general(a, b, NT_DIMS, preferred_element_type=F32, **kw)


def _dot_tn(a, b, **kw):
    return lax.dot_general(a, b, TN_DIMS, preferred_element_type=F32, **kw)


def _sigmoid(x):
    return 1.0 / (1.0 + jnp.exp(-x))


def _silu(x):
    return x * _sigmoid(x)


def _rms(x, w):
    return x * lax.rsqrt(jnp.mean(x * x, axis=-1, keepdims=True) + RMS_EPS) * w


def _full(shape):
    return pl.BlockSpec(shape, lambda *_: (0,) * len(shape))


def _adaln_kernel(c_ref, w_ref, b_ref, o_ref):
    o_ref[...] = _dot(_silu(c_ref[...]), w_ref[...], precision=HIGHEST) + b_ref[...]


def _adaln(c, ada_w, ada_b):
    B, D = c.shape
    n = ada_w.shape[1] // D
    return pl.pallas_call(
        _adaln_kernel,
        grid=(n,),
        in_specs=[_full((B, D)), pl.BlockSpec((D, D), lambda j: (0, j)), pl.BlockSpec((1, D), lambda j: (0, j))],
        out_specs=pl.BlockSpec((B, D), lambda j: (0, j)),
        out_shape=jax.ShapeDtypeStruct((B, ada_w.shape[1]), F32),
        compiler_params=_params(1),
        name="adaln",
    )(c, ada_w, ada_b[None, :])


_C_M = 0
_C_Q = 2 * M_WIDTH
_C_KC = _C_Q + N_WIDTH
_C_KS = _C_KC + KV_WIDTH
_C_KW = _C_KS + KV_WIDTH
_C_VC = _C_KW + KV_WIDTH
_C_END = _C_VC + KV_WIDTH
_R_VS = 0
_R_VW = KV_WIDTH
_R_NG = 2 * KV_WIDTH
_R_MV = _R_NG + 4 * SUBLANE
_R_MO = _R_MV + M_WIDTH
_R_END = _R_MO + M_WIDTH


def _inproj_kernel(x_ref, mod_ref, nw_ref, w_ref, wt_ref, wg_ref, gb_ref, pos_ref, inv_ref,
                   zm_ref, q_ref, kc_ref, ks_ref, kw_ref, vc_ref, vst_ref, vwt_ref, ngt_ref, mvt_ref, mot_ref, gt_ref):
    mod = mod_ref[0]
    h = _rms(x_ref[...], nw_ref[...]) * (1.0 + mod[1:2]) + mod[0:1]
    hb = h.astype(BF16)
    z = _dot(hb, w_ref[...])
    zm_ref[...] = z[:, _C_M:_C_Q]
    gt_ref[...] = _dot_nt(wg_ref[...], h, precision=HIGHEST) + gb_ref[...]
    zt = _dot_nt(wt_ref[...], hb)
    vst_ref[...] = zt[_R_VS:_R_VW].astype(BF16)
    vwt_ref[...] = zt[_R_VW:_R_NG].astype(BF16)
    ngt_ref[...] = _sigmoid(zt[_R_NG:_R_MV])
    mvt_ref[...] = zt[_R_MV:_R_MO].astype(BF16)
    mot_ref[...] = zt[_R_MO:_R_END]

    ang = pos_ref[...] * inv_ref[...]
    cos = jnp.cos(ang)
    sin = jnp.sin(ang)
    lane = lax.broadcasted_iota(jnp.int32, (1, LANE), 1)
    first = (lane % N_HEAD_DIM) < (N_HEAD_DIM // 2)
    sin_signed = jnp.where(first, -sin, sin)

    def rope(slab):
        rot = jnp.where(first, pltpu.roll(slab, LANE - N_HEAD_DIM // 2, 1), pltpu.roll(slab, N_HEAD_DIM // 2, 1))
        return slab * cos + rot * sin_signed

    scale = N_HEAD_DIM ** -0.5
    for j in range(N_WIDTH // LANE):
        q_ref[:, j * LANE:(j + 1) * LANE] = (rope(z[:, _C_Q + j * LANE:_C_Q + (j + 1) * LANE]) * scale).astype(BF16)
    kc_ref[...] = rope(z[:, _C_KC:_C_KS])
    ks_ref[...] = rope(z[:, _C_KS:_C_KW]).astype(BF16)
    kw_ref[...] = rope(z[:, _C_KW:_C_VC]).astype(BF16)
    vc_ref[...] = z[:, _C_VC:_C_END]


def _inproj(x2, mod3, norm_w, w_main, w_t, wg_t, gate_b, pos_col, inv_row, S):
    T, D = x2.shape
    TM = min(TOKEN_TILE, S)
    tiles_per_seq = S // TM
    row = lambda w: pl.BlockSpec((TM, w), lambda i: (i, 0))
    col = lambda r: pl.BlockSpec((r, TM), lambda i: (0, i))
    outs = [(2 * M_WIDTH, F32), (N_WIDTH, BF16), (KV_WIDTH, F32), (KV_WIDTH, BF16), (KV_WIDTH, BF16), (KV_WIDTH, F32)]
    outs_t = [(KV_WIDTH, BF16), (KV_WIDTH, BF16), (_R_MV - _R_NG, F32), (M_WIDTH, BF16), (M_WIDTH, F32),
              (2 * M_HEADS, F32)]
    return pl.pallas_call(
        _inproj_kernel,
        grid=(T // TM,),
        in_specs=[row(D), pl.BlockSpec((1, 6, D), lambda i: (i // tiles_per_seq, 0, 0)), _full((1, D)),
                  _full(w_main.shape), _full(w_t.shape), _full(wg_t.shape), _full((2 * M_HEADS, 1)), row(1),
                  _full((1, LANE))],
        out_specs=[row(w) for w, _ in outs] + [col(r) for r, _ in outs_t],
        out_shape=[jax.ShapeDtypeStruct((T, w), dt) for w, dt in outs]
        + [jax.ShapeDtypeStruct((r, T), dt) for r, dt in outs_t],
        compiler_params=_params(1),
        name="inproj",
    )(x2, mod3, norm_w[None, :], w_main, w_t, wg_t, gate_b[:, None], pos_col, inv_row)


def _mlstm_kernel(zm_ref, vt_ref, ot_ref, gt_ref, cw_ref, hn_ref, ltri_ref, eye_ref, y_ref, buf, c_s, n_s, m_s):
    L = zm_ref.shape[0]
    QK = 2 * M_WIDTH
    DH = M_HEAD_DIM

    @pl.when(pl.program_id(1) == 0)
    def _():
        buf[0:SUBLANE, :] = jnp.zeros((SUBLANE, QK), F32)
        c_s[...] = jnp.zeros_like(c_s)
        n_s[...] = jnp.zeros_like(n_s)
        m_s[...] = jnp.zeros_like(m_s)

    buf[SUBLANE:SUBLANE + L, :] = zm_ref[:, 0:QK]
    cw = cw_ref[...]
    conv = cw[M_CONV - 1:M_CONV] * buf[SUBLANE:SUBLANE + L, :]
    for j in range(M_CONV - 1):
        off = SUBLANE - (M_CONV - 1) + j
        conv = conv + cw[j:j + 1] * buf[off:off + L, :]
    buf[0:SUBLANE, :] = buf[L:L + SUBLANE, :]
    qk = _silu(conv)

    g = gt_ref[...]
    fp = g[M_HEADS:]
    lf = jnp.minimum(fp, 0.0) - jnp.log(1.0 + jnp.exp(-jnp.abs(fp)))
    lf8 = jnp.concatenate([lf, jnp.zeros_like(lf)], axis=0)
    ltri = ltri_ref[...]
    b_rows = _dot_nt(lf8, ltri, precision=HIGHEST)
    b_cols = _dot_nt(ltri, lf8, precision=HIGHEST)
    i_cols = _dot_nt(eye_ref[...], g, precision=HIGHEST)

    causal = lax.broadcasted_iota(jnp.int32, (L, L), 0) <= lax.broadcasted_iota(jnp.int32, (L, L), 1)
    ones_rows = jnp.ones((2 * SUBLANE, L), F32)
    outs = []
    for h in range(M_HEADS):
        hs = slice(h * DH, (h + 1) * DH)
        b_r, i_r = b_rows[h:h + 1], g[h:h + 1]
        src_c = b_cols[:, h:h + 1] - i_cols[:, h:h + 1]
        m_prev = m_s[h:h + 1, 0:1]
        g_tot = b_r[:, L - 1:L]
        a_r = g_tot - b_r + i_r
        m_new = jnp.maximum(g_tot + m_prev, jnp.max(a_r, axis=1, keepdims=True))

        qb = qk[:, hs].astype(BF16)
        kb = (qk[:, M_WIDTH + h * DH:M_WIDTH + (h + 1) * DH] * (DH ** -0.5)).astype(BF16)
        vt = vt_ref[hs, :]

        dlog = jnp.where(causal, b_r - src_c, -jnp.inf)
        inter = b_r + m_prev
        m_t = jnp.maximum(inter, jnp.max(dlog, axis=0, keepdims=True))
        wts = jnp.exp(dlog - m_t) * _dot_nt(kb, qb)
        dec = jnp.exp(inter - m_t)
        ct_prev = c_s[h]
        n_prev = n_s[...]
        num = _dot(vt, wts.astype(BF16)) + dec * _dot_nt(ct_prev.astype(BF16), qb)
        qn = _dot_nt(n_prev.astype(BF16), qb)[h:h + 1]
        den = jnp.sum(wts, axis=0, keepdims=True) + dec * qn
        hh = num * (1.0 / jnp.maximum(jnp.abs(den), jnp.exp(-m_t)))

        w_r = jnp.exp(a_r - m_new)
        keep = jnp.exp(g_tot + m_prev - m_new)
        lhs = jnp.concatenate([vt.astype(F32), ones_rows], axis=0) * w_r
        upd = _dot(lhs.astype(BF16), kb)
        c_s[h] = keep * ct_prev + upd[:DH]
        n_s[h:h + 1] = keep * n_prev[h:h + 1] + upd[DH:DH + 1]
        m_s[h:h + 1] = jnp.broadcast_to(m_new, (1, LANE))

        hn = hh * lax.rsqrt(jnp.mean(hh * hh, axis=0, keepdims=True) + RMS_EPS)
        hn = hn * jnp.concatenate([hn_ref[hs, :]] * (L // LANE), axis=1)
        outs.append(_sigmoid(ot_ref[hs, :]) * hn)
    y_ref[...] = jnp.concatenate(outs, axis=0).T.astype(BF16)


def _mlstm(zqk, v_t, o_t, gt, conv_w, head_norm, B, S):
    T = zqk.shape[0]
    L = min(MLSTM_CHUNK, S)
    assert L % LANE == 0
    nc = S // L
    ltri = jnp.asarray(np.tril(np.ones((L, L), np.float32)))
    eye = jnp.asarray(np.eye(L, dtype=np.float32))
    col = lambda r: pl.BlockSpec((r, L), lambda b, c: (0, b * nc + c))
    return pl.pallas_call(
        _mlstm_kernel,
        grid=(B, nc),
        in_specs=[pl.BlockSpec((L, 2 * M_WIDTH), lambda b, c: (b * nc + c, 0)), col(M_WIDTH), col(M_WIDTH),
                  col(2 * M_HEADS), _full((M_CONV, 2 * M_WIDTH)), _full((M_WIDTH, LANE)), _full((L, L)), _full((L, L))],
        out_specs=pl.BlockSpec((L, M_WIDTH), lambda b, c: (b * nc + c, 0)),
        out_shape=jax.ShapeDtypeStruct((T, M_WIDTH), BF16),
        scratch_shapes=[pltpu.VMEM((L + SUBLANE, 2 * M_WIDTH), F32),
                        pltpu.VMEM((M_HEADS, M_HEAD_DIM, M_HEAD_DIM), F32),
                        pltpu.VMEM((SUBLANE, M_HEAD_DIM), F32),
                        pltpu.VMEM((SUBLANE, LANE), F32)],
        compiler_params=_params(2),
        name="mlstm",
    )(zqk, v_t, o_t, gt, conv_w, jnp.broadcast_to(head_norm[:, None], (M_WIDTH, LANE)), ltri, eye)


def _compress_kernel(k_ref, v_ref, pak, pbk, wak, wbk, w2k, pav, pbv, wav, wbv, w2v, ko_ref, vo_ref):
    def one(x_ref, pa, pb, wa, wb, w2, o_ref, transposed):
        n = x_ref.shape[0] // CMP_STRIDE
        x = jnp.concatenate([x_ref[pl.ds(t, n, stride=CMP_STRIDE), :] for t in range(CMP_STRIDE)], axis=1)
        first = _dot((x + pa[...]).astype(BF16), wa[...])
        second = _dot((x + pb[...]).astype(BF16), wb[...])
        hid = _silu(first + pltpu.roll(second, n - 1, 0)).astype(BF16)
        o_ref[0] = (_dot_nt(w2[...], hid) if transposed else _dot(hid, w2[...])).astype(BF16)

    one(k_ref, pak, pbk, wak, wbk, w2k, ko_ref, False)
    one(v_ref, pav, pbv, wav, wbv, w2v, vo_ref, True)


def _compress_weights(pe, w1, w2):
    half = CMP_BLOCK // 2
    eye = jnp.eye(N_KV_GROUPS, dtype=F32)
    w1r = w1.reshape(CMP_BLOCK, N_HEAD_DIM, CMP_HIDDEN)
    big = lambda w: jnp.einsum("ldc,gh->lgdhc", w, eye).reshape(half * KV_WIDTH, N_KV_GROUPS * CMP_HIDDEN).astype(BF16)
    pe_row = lambda p: jnp.broadcast_to(p[:, None, :], (half, N_KV_GROUPS, N_HEAD_DIM)).reshape(1, half * KV_WIDTH)
    w2bd = jnp.einsum("cd,gh->gchd", w2, eye).reshape(N_KV_GROUPS * CMP_HIDDEN, KV_WIDTH).astype(BF16)
    return pe_row(pe[:half]), pe_row(pe[half:]), big(w1r[:half]), big(w1r[half:]), w2bd


def _compress(kc, vc, wk, wv, B, S):
    nc = S // CMP_STRIDE
    blk = pl.BlockSpec((S, KV_WIDTH), lambda b: (b, 0))
    wv = wv[:4] + (wv[4].T,)
    return pl.pallas_call(
        _compress_kernel,
        grid=(B,),
        in_specs=[blk, blk] + [_full(w.shape) for w in wk] + [_full(w.shape) for w in wv],
        out_specs=[pl.BlockSpec((1, nc, KV_WIDTH), lambda b: (b, 0, 0)),
                   pl.BlockSpec((1, KV_WIDTH, nc), lambda b: (b, 0, 0))],
        out_shape=[jax.ShapeDtypeStruct((B, nc, KV_WIDTH), BF16), jax.ShapeDtypeStruct((B, KV_WIDTH, nc), BF16)],
        compiler_params=_params(1),
        name="compress",
    )(kc, vc, *wk, *wv)


def _masked_softmax_keys(s, mask):
    s = jnp.where(mask, s, -jnp.inf)
    m = jnp.max(s, axis=0, keepdims=True)
    m = jnp.where(jnp.isfinite(m), m, 0.0)
    e = jnp.exp(s - m)
    return e, 1.0 / jnp.maximum(jnp.sum(e, axis=0, keepdims=True), 1e-30)


def _nsa_kernel(q_ref, kc_ref, vct_ref, ks_ref, vst_ref, kw_ref, vwt_ref, ngt_ref, ov_ref, ek_ref, eye_ref,
                y_ref, m_s, l_s, acc_s, *, n_top, TK):
    TQ = q_ref.shape[0]
    NS = ov_ref.shape[0]
    G, HPG, DH = N_KV_GROUPS, HEADS_PER_GROUP, N_HEAD_DIM
    R = G * HPG * TQ
    q0 = pl.program_id(1) * TQ

    q = q_ref[...]
    zero = jnp.zeros((TQ, DH), BF16)
    parts = []
    for hd in range(N_HEADS):
        qh = q[:, hd * DH:(hd + 1) * DH]
        parts.append(jnp.concatenate([qh, zero] if hd < HPG else [zero, qh], axis=1))
    qp = jnp.concatenate(parts, axis=0)
    t_q = q0 + lax.broadcasted_iota(jnp.int32, (1, R), 1) % TQ

    kc = kc_ref[0]
    NC = kc.shape[0]
    cmp_end = lax.broadcasted_iota(jnp.int32, (NC, 1), 0) * CMP_STRIDE + (CMP_BLOCK - 1)
    e_c, inv_c = _masked_softmax_keys(_dot_nt(kc, qp), cmp_end <= t_q)
    p_c = e_c * inv_c
    o_c = _dot(vct_ref[0], p_c.astype(BF16))
    p_grp = jnp.concatenate(
        [sum(p_c[:, (g * HPG + h) * TQ:(g * HPG + h + 1) * TQ] for h in range(HPG)) for g in range(G)], axis=1)
    imp = _dot(ov_ref[...], p_grp, precision=HIGHEST)

    j_io = lax.broadcasted_iota(jnp.int32, (NS, G * TQ), 0)
    t_row = q0 + lax.broadcasted_iota(jnp.int32, (NS, G * TQ), 1) % TQ
    cur = t_row // SLC_BLOCK
    forced = (j_io == 0) | (j_io == cur) | (j_io == cur - 1)
    future = j_io * SLC_BLOCK > t_row
    score = jnp.where(forced, jnp.inf, jnp.where(future, -jnp.inf, imp))
    rank = jnp.zeros((NS, G * TQ), F32)
    for jp in range(NS):
        other = score[jp:jp + 1, :]
        rank = rank + jnp.where(j_io > jp, jnp.where(other >= score, 1.0, 0.0), jnp.where(other > score, 1.0, 0.0))
    sel_bias = jnp.where(future, NEG, jnp.where(rank < n_top, 0.0, NEG)).astype(BF16)
    sel_rows = _dot_tn(sel_bias, eye_ref[...]).astype(BF16)
    sel_rows = jnp.broadcast_to(sel_rows.reshape(G, 1, TQ, LANE), (G, HPG, TQ, LANE)).reshape(R, LANE)
    qa = jnp.concatenate([qp, sel_rows], axis=1)

    m_s[...] = jnp.full_like(m_s, NEG)
    l_s[...] = jnp.zeros_like(l_s)
    acc_s[...] = jnp.zeros_like(acc_s)

    CB = NSA_COL_BLOCK

    def step(kt, causal):
        k0 = pl.multiple_of(kt * TK, TK)
        ka = jnp.concatenate([ks_ref[0, pl.ds(k0, TK), :], ek_ref[pl.ds(k0, TK), :]], axis=1)
        vt = vst_ref[:, pl.ds(k0, TK)]
        kpos = k0 + lax.broadcasted_iota(jnp.int32, (TK, 1), 0)
        scores = [_dot_nt(ka, qa[j * CB:(j + 1) * CB]) for j in range(R // CB)]
        probs, alphas = [], []
        for j in range(R // CB):
            cols = slice(j * CB, (j + 1) * CB)
            s = scores[j]
            if causal:
                s = jnp.where(kpos <= t_q[:, cols], s, NEG)
            m_old = m_s[:, cols]
            m_new = jnp.maximum(m_old, jnp.max(s, axis=0, keepdims=True))
            alpha = jnp.exp(m_old - m_new)
            p = jnp.exp(s - m_new)
            l_s[:, cols] = alpha * l_s[:, cols] + jnp.sum(p, axis=0, keepdims=True)
            m_s[:, cols] = m_new
            probs.append(p.astype(BF16))
            alphas.append(alpha)
        for j in range(R // CB):
            cols = slice(j * CB, (j + 1) * CB)
            acc_s[:, cols] = alphas[j] * acc_s[:, cols] + _dot(vt, probs[j])

    def full_tile(kt, carry):
        step(kt, False)
        return carry

    last = q0 // TK
    lax.fori_loop(0, last, full_tile, 0)
    step(last, True)
    o_s = acc_s[...] * (1.0 / l_s[...])

    WK = WINDOW + TQ
    start = pl.multiple_of(jnp.maximum(q0 - WINDOW, 0), TQ)
    wpos = start + lax.broadcasted_iota(jnp.int32, (WK, 1), 0)
    e_w, inv_w = _masked_softmax_keys(_dot_nt(kw_ref[0, pl.ds(start, WK), :], qp),
                                      (wpos <= t_q) & (wpos > t_q - WINDOW))
    o_w = _dot(vwt_ref[:, pl.ds(start, WK)], e_w.astype(BF16)) * inv_w

    gates = ngt_ref[...]
    outs = []
    for hd in range(N_HEADS):
        cols = slice(hd * TQ, (hd + 1) * TQ)
        rows = slice((hd // HPG) * DH, (hd // HPG + 1) * DH)
        outs.append(gates[3 * hd:3 * hd + 1] * o_c[rows, cols]
                    + gates[3 * hd + 1:3 * hd + 2] * o_s[rows, cols]
                    + gates[3 * hd + 2:3 * hd + 3] * o_w[rows, cols])
    y_ref[...] = jnp.concatenate(outs, axis=0).T.astype(BF16)


def _nsa(q, kcmp, vcmp_t, ks, vs_t, kw, vw_t, ng_t, B, S):
    T = q.shape[0]
    TQ = min(NSA_Q_TILE, S)
    TK = min(NSA_K_TILE, S)
    nq = S // TQ
    NS = S // SLC_BLOCK
    NC = S // CMP_STRIDE
    n_cmp = (S - CMP_BLOCK) // CMP_STRIDE + 1
    js = np.arange(NS)[:, None] * SLC_BLOCK
    cs = np.arange(NC)[None, :] * CMP_STRIDE
    ov = np.clip(np.minimum(js + SLC_BLOCK, cs + CMP_BLOCK) - np.maximum(js, cs), 0, None) / CMP_STRIDE
    ov[:, n_cmp:] = 0.0
    assert NS <= LANE and TK % TQ == 0
    block_of_key = (np.arange(S)[:, None] // SLC_BLOCK == np.arange(LANE)[None, :]).astype(np.float32)
    eye_pad = np.eye(NS, LANE, dtype=np.float32)
    R = N_HEADS * TQ
    assert R % NSA_COL_BLOCK == 0
    seq = lambda: pl.BlockSpec((1, S, KV_WIDTH), lambda b, i: (b, 0, 0))
    seq_t = lambda: pl.BlockSpec((KV_WIDTH, S), lambda b, i: (0, b))
    return pl.pallas_call(
        functools.partial(_nsa_kernel, n_top=min(SLC_TOPK, NS), TK=TK),
        grid=(B, nq),
        in_specs=[pl.BlockSpec((TQ, N_WIDTH), lambda b, i: (b * nq + i, 0)),
                  pl.BlockSpec((1, NC, KV_WIDTH), lambda b, i: (b, 0, 0)),
                  pl.BlockSpec((1, KV_WIDTH, NC), lambda b, i: (b, 0, 0)),
                  seq(), seq_t(), seq(), seq_t(), pl.BlockSpec((ng_t.shape[0], TQ), lambda b, i: (0, b * nq + i)),
                  _full((NS, NC)), _full((S, LANE)), _full((NS, LANE))],
        out_specs=pl.BlockSpec((TQ, N_WIDTH), lambda b, i: (b * nq + i, 0)),
        out_shape=jax.ShapeDtypeStruct((T, N_WIDTH), BF16),
        scratch_shapes=[pltpu.VMEM((1, R), F32), pltpu.VMEM((1, R), F32), pltpu.VMEM((KV_WIDTH, R), F32)],
        compiler_params=_params(2),
        name="nsa",
    )(q, kcmp, vcmp_t, ks.reshape(B, S, KV_WIDTH), vs_t, kw.reshape(B, S, KV_WIDTH), vw_t, ng_t,
      jnp.asarray(ov.astype(np.float32)), jnp.asarray(block_of_key, dtype=BF16), jnp.asarray(eye_pad, dtype=BF16))


def _post_kernel(ym_ref, yn_ref, wm_ref, wn_ref, x_ref, mod_ref, npost_ref, npre_ref, wrh_ref, wrl_ref, br_ref,
                 ltri_ref, x1_ref, h2_ref, route_ref, cnt_ref, carry):
    TM = x_ref.shape[0]

    @pl.when(pl.program_id(0) == 0)
    def _():
        carry[...] = jnp.zeros_like(carry)

    mod = mod_ref[0]
    y = _dot(ym_ref[...], wm_ref[...]) + _dot(yn_ref[...], wn_ref[...])
    x1 = x_ref[...] + mod[2:3] * _rms(y, npost_ref[...])
    x1_ref[...] = x1
    h2 = _rms(x1, npre_ref[...]) * (1.0 + mod[4:5]) + mod[3:4]
    for s in range(SUBLANE):
        h2_ref[pl.ds(s, TM, stride=SUBLANE), :] = h2[:, s * LANE:(s + 1) * LANE]

    h_hi = h2.astype(BF16)
    h_lo = (h2 - h_hi.astype(F32)).astype(BF16)
    logits = _dot(h_hi, wrh_ref[...]) + _dot(h_lo, wrh_ref[...]) + _dot(h_hi, wrl_ref[...]) + br_ref[...]
    lane = lax.broadcasted_iota(jnp.int32, (TM, LANE), 1)
    is_grp = lane < MOE_GROUPS
    lg = jnp.where(is_grp, logits, -jnp.inf)
    eg = jnp.exp(lg - jnp.max(lg, axis=1, keepdims=True))
    pg = eg / jnp.sum(eg, axis=1, keepdims=True)
    pg_top = jnp.max(pg, axis=1, keepdims=True)
    grp = jnp.min(jnp.where(is_grp & (pg == pg_top), lane, LANE), axis=1, keepdims=True)
    lo = MOE_GROUPS + EXPERTS_PER_GROUP * grp
    in_grp = (lane >= lo) & (lane < lo + EXPERTS_PER_GROUP)
    le = jnp.where(in_grp, logits, -jnp.inf)
    ee = jnp.exp(le - jnp.max(le, axis=1, keepdims=True))
    pe = jnp.where(in_grp, ee / jnp.sum(ee, axis=1, keepdims=True), -1.0)
    p1 = jnp.max(pe, axis=1, keepdims=True)
    i1 = jnp.min(jnp.where(pe == p1, lane, LANE), axis=1, keepdims=True)
    pe2 = jnp.where(lane == i1, -1.0, pe)
    p2 = jnp.max(pe2, axis=1, keepdims=True)
    i2 = jnp.min(jnp.where((pe2 == p2) & in_grp & (lane != i1), lane, LANE), axis=1, keepdims=True)
    den = p1 + p2
    e1 = i1 - MOE_GROUPS
    e2 = i2 - MOE_GROUPS
    oh1, oh2 = lane == e1, lane == e2
    oh = jnp.where(oh1, 1.0, 0.0) + jnp.where(oh2, 1.0, 0.0)
    before = _dot(ltri_ref[...], oh.astype(BF16)) + carry[0:1, :]
    r1 = jnp.sum(jnp.where(oh1, before, 0.0), axis=1, keepdims=True)
    r2 = jnp.sum(jnp.where(oh2, before, 0.0), axis=1, keepdims=True)
    carry[...] = carry[...] + jnp.sum(oh, axis=0, keepdims=True)
    cnt_ref[...] = carry[...]
    cols = (e1.astype(F32), e2.astype(F32), pg_top * p1 / den, pg_top * p2 / den, r1, r2)
    route = jnp.zeros((TM, LANE), F32)
    for k, col in enumerate(cols):
        route = jnp.where(lane == k, col, route)
    route_ref[...] = route


def _post(ym, yn, w_out, x2, mod3, norm_post, norm_pre, wr, br, S):
    T, D = x2.shape
    TM = min(TOKEN_TILE, S)
    tiles_per_seq = S // TM
    row = lambda w: pl.BlockSpec((TM, w), lambda i: (i, 0))
    small = pl.BlockSpec((SUBLANE, LANE), lambda i: (0, 0))
    ltri = jnp.asarray(np.tril(np.ones((TM, TM), np.float32), -1), dtype=BF16)
    return pl.pallas_call(
        _post_kernel,
        grid=(T // TM,),
        in_specs=[row(M_WIDTH), row(N_WIDTH), _full((M_WIDTH, D)), _full((N_WIDTH, D)), row(D),
                  pl.BlockSpec((1, 6, D), lambda i: (i // tiles_per_seq, 0, 0)), _full((1, D)), _full((1, D)),
                  _full((D, LANE)), _full((D, LANE)), _full((1, LANE)), _full((TM, TM))],
        out_specs=[row(D), pl.BlockSpec((TM * SUBLANE, LANE), lambda i: (i, 0)), row(LANE), small],
        out_shape=[jax.ShapeDtypeStruct((T, D), F32), jax.ShapeDtypeStruct((T * SUBLANE, LANE), F32),
                   jax.ShapeDtypeStruct((T, LANE), F32), jax.ShapeDtypeStruct((SUBLANE, LANE), F32)],
        scratch_shapes=[pltpu.VMEM((SUBLANE, LANE), F32)],
        compiler_params=_params(1),
        name="post_mix_router",
    )(ym, yn, w_out[:M_WIDTH].astype(BF16), w_out[M_WIDTH:].astype(BF16), x2, mod3,
      norm_post[None, :], norm_pre[None, :], wr.astype(BF16), (wr - wr.astype(BF16).astype(F32)).astype(BF16), br,
      ltri)


def _plan_kernel(cnt_ref, pstart_ref, blk_ref, tot_ref, *, rows_per_block):
    lane = lax.broadcasted_iota(jnp.int32, (SUBLANE, LANE), 1)
    nblk = (cnt_ref[...].astype(jnp.int32) + (rows_per_block - 1)) // rows_per_block
    end = nblk
    sh = 1
    while sh < N_EXPERTS:
        end = end + jnp.where(lane >= sh, pltpu.roll(end, sh, 1), 0)
        sh *= 2
    pstart_ref[...] = (end - nblk) * rows_per_block
    nbp = blk_ref.shape[0]
    blk_io = lax.broadcasted_iota(jnp.int32, (nbp, LANE), 0)
    lane_b = lax.broadcasted_iota(jnp.int32, (nbp, LANE), 1)
    passed = jnp.where((lane_b < N_EXPERTS) & (blk_io >= end[0:1, :]), 1, 0)
    blk_ref[...] = jnp.broadcast_to(jnp.minimum(jnp.sum(passed, axis=1, keepdims=True), N_EXPERTS - 1), (nbp, LANE))
    tot_ref[...] = jnp.broadcast_to(
        jnp.sum(jnp.where(lane == N_EXPERTS - 1, end, 0), axis=1, keepdims=True), (SUBLANE, LANE))


def _sort_plan(cnt, n_blocks):
    small = pl.BlockSpec((SUBLANE, LANE), lambda i: (0, 0))
    nbp = -(-n_blocks // SUBLANE) * SUBLANE
    pstart, blk, tot = pl.pallas_call(
        functools.partial(_plan_kernel, rows_per_block=EXPERT_ROWS),
        grid=(1,),
        in_specs=[small],
        out_specs=[small, _full((nbp, LANE)), small],
        out_shape=[jax.ShapeDtypeStruct((SUBLANE, LANE), jnp.int32), jax.ShapeDtypeStruct((nbp, LANE), jnp.int32),
                   jax.ShapeDtypeStruct((SUBLANE, LANE), jnp.int32)],
        compiler_params=_params(1),
        name="expert_plan",
    )(cnt)
    return pstart[0, :N_EXPERTS], blk[:n_blocks, 0], tot[0, :1]


_ASG = 4


def _dest_row(pstart_ref, asg_ref, j, k):
    return pstart_ref[asg_ref[0, 0, _ASG * j + k]] + asg_ref[0, 0, _ASG * j + 2 + k]


def _row_copy(src, src_row, dst, dst_row, sem):
    return pltpu.make_async_copy(src.at[pl.ds(pl.multiple_of(src_row * SUBLANE, SUBLANE), SUBLANE)],
                                 dst.at[pl.ds(pl.multiple_of(dst_row * SUBLANE, SUBLANE), SUBLANE)], sem)


def _dispatch_kernel(pstart_ref, asg_ref, h2_ref, xs_in, xs_hbm, sem):
    del xs_in
    n = asg_ref.shape[2] // _ASG

    def issue(jj, carry):
        for u in range(ISSUE_UNROLL):
            j = jj * ISSUE_UNROLL + u
            for k in range(2):
                _row_copy(h2_ref, j, xs_hbm, _dest_row(pstart_ref, asg_ref, j, k), sem).start(priority=k)
        return carry

    lax.fori_loop(0, n // ISSUE_UNROLL, issue, 0)
    whole = xs_hbm.at[pl.ds(0, 2 * n * SUBLANE)]
    pltpu.make_async_copy(whole, whole, sem).wait()


def _dispatch(pstart, asg3, h2t, n_rows):
    nt, _, width = asg3.shape
    n = width // _ASG
    xs0 = jnp.zeros((n_rows * SUBLANE, LANE), F32)
    return pl.pallas_call(
        _dispatch_kernel,
        grid=(nt,),
        in_specs=[pl.BlockSpec(memory_space=pltpu.SMEM),
                  pl.BlockSpec((1, 1, width), lambda i: (i, 0, 0), memory_space=pltpu.SMEM),
                  pl.BlockSpec((n * SUBLANE, LANE), lambda i: (i, 0)), pl.BlockSpec(memory_space=pl.ANY)],
        out_specs=pl.BlockSpec(memory_space=pl.ANY),
        out_shape=jax.ShapeDtypeStruct(xs0.shape, F32),
        scratch_shapes=[pltpu.SemaphoreType.DMA(())],
        input_output_aliases={3: 0},
        compiler_params=_params(1),
        name="dispatch",
    )(pstart, asg3, h2t, xs0)


def _expert_kernel(be_ref, nb_ref, xs_ref, w1_ref, w3_ref, w2_ref, ys_ref, w1b, w3b, w2b):
    RB = xs_ref.shape[0] // SUBLANE
    i = pl.program_id(0)

    @pl.when(i < nb_ref[0])
    def _():
        @pl.when((i == 0) | (be_ref[i] != be_ref[jnp.maximum(i - 1, 0)]))
        def _():
            w1b[...] = w1_ref[0].astype(BF16)
            w3b[...] = w3_ref[0].astype(BF16)
            w2b[...] = w2_ref[0].astype(BF16)

        x = jnp.concatenate([xs_ref[pl.ds(s, RB, stride=SUBLANE), :] for s in range(SUBLANE)], axis=1).astype(BF16)
        hb = _silu(_dot(x, w1b[...])) * _dot(x, w3b[...])
        y = _dot(hb.astype(BF16), w2b[...])
        for s in range(SUBLANE):
            ys_ref[pl.ds(s, RB, stride=SUBLANE), :] = y[:, s * LANE:(s + 1) * LANE]

    @pl.when(i >= nb_ref[0])
    def _():
        ys_ref[...] = jnp.zeros_like(ys_ref)


def _experts(block_expert, n_used, xs, w1, w3, w2):
    n_blocks = block_expert.shape[0]
    RB = EXPERT_ROWS
    D, F = w1.shape[1], w1.shape[2]
    cur = lambda i, nb: jnp.minimum(i, nb[0] - 1)
    rows = pl.BlockSpec((RB * SUBLANE, LANE), lambda i, be, nb: (cur(i, nb), 0))
    wspec = lambda a, b: pl.BlockSpec((1, a, b), lambda i, be, nb: (be[cur(i, nb)], 0, 0))
    return pl.pallas_call(
        _expert_kernel,
        grid_spec=pltpu.PrefetchScalarGridSpec(
            num_scalar_prefetch=2, grid=(n_blocks,),
            in_specs=[rows, wspec(D, F), wspec(D, F), wspec(F, D)],
            out_specs=pl.BlockSpec((RB * SUBLANE, LANE), lambda i, be, nb: (i, 0)),
            scratch_shapes=[pltpu.VMEM((D, F), BF16), pltpu.VMEM((D, F), BF16), pltpu.VMEM((F, D), BF16)]),
        out_shape=jax.ShapeDtypeStruct(xs.shape, F32),
        compiler_params=_params(1),
        name="experts",
    )(block_expert, n_used, xs, w1, w3, w2)


def _combine_kernel(pstart_ref, asg_ref, asg_next_ref, ys_hbm, x1_ref, route_ref, mod_ref, nw_ref, o_ref, buf, sems):
    n = x1_ref.shape[0]
    i = pl.program_id(0)
    slot = i % 2

    def gather(dref, into):
        def issue(jj, carry):
            for u in range(ISSUE_UNROLL):
                j = jj * ISSUE_UNROLL + u
                for k in range(2):
                    _row_copy(ys_hbm, _dest_row(pstart_ref, dref, j, k), buf.at[into], k * n + j,
                              sems.at[into]).start(priority=k)
            return carry

        lax.fori_loop(0, n // ISSUE_UNROLL, issue, 0)

    @pl.when(i == 0)
    def _():
        gather(asg_ref, 0)

    @pl.when(i + 1 < pl.num_programs(0))
    def _():
        gather(asg_next_ref, 1 - slot)

    pltpu.make_async_copy(ys_hbm.at[pl.ds(0, 2 * n * SUBLANE)], buf.at[slot], sems.at[slot]).wait()

    route = route_ref[...]
    g0, g1 = route[:, 2:3], route[:, 3:4]
    y = jnp.concatenate(
        [g0 * buf[slot, pl.ds(s, n, stride=SUBLANE), :] + g1 * buf[slot, pl.ds(n * SUBLANE + s, n, stride=SUBLANE), :]
         for s in range(SUBLANE)], axis=1)
    o_ref[...] = x1_ref[...] + mod_ref[0][5:6] * _rms(y, nw_ref[...])


def _combine(pstart, asg3, ys, x1, route, mod3, norm_w, S):
    T, D = x1.shape
    nt, _, width = asg3.shape
    n = width // _ASG
    tiles_per_seq = S // n
    return pl.pallas_call(
        _combine_kernel,
        grid=(nt,),
        in_specs=[pl.BlockSpec(memory_space=pltpu.SMEM),
                  pl.BlockSpec((1, 1, width), lambda i: (i, 0, 0), memory_space=pltpu.SMEM),
                  pl.BlockSpec((1, 1, width), lambda i: (jnp.minimum(i + 1, nt - 1), 0, 0), memory_space=pltpu.SMEM),
                  pl.BlockSpec(memory_space=pl.ANY), pl.BlockSpec((n, D), lambda i: (i, 0)),
                  pl.BlockSpec((n, LANE), lambda i: (i, 0)),
                  pl.BlockSpec((1, 6, D), lambda i: (i // tiles_per_seq, 0, 0)), _full((1, D))],
        out_specs=pl.BlockSpec((n, D), lambda i: (i, 0)),
        out_shape=jax.ShapeDtypeStruct((T, D), F32),
        scratch_shapes=[pltpu.VMEM((2, 2 * n * SUBLANE, LANE), F32), pltpu.SemaphoreType.DMA((2,))],
        compiler_params=_params(1),
        name="combine",
    )(pstart, asg3, asg3, ys, x1, route, mod3, norm_w[None, :])


def _layer(x, c, positions, ada_w, ada_b, norm_mix_pre, norm_mix_post, norm_ffn_pre, norm_ffn_post, w_in,
           conv_w, gate_b, head_norm, cmp_k, cmp_v, w_out, wg, bg, we, be, w1, w3, w2):
    B, S, D = x.shape
    T = B * S
    x2 = x.reshape(T, D)
    mod3 = _adaln(c, ada_w, ada_b).reshape(B, 6, D)

    o_mi = 4 * M_WIDTH
    o_nq = o_mi + 2 * M_HEADS
    o_kv = o_nq + N_WIDTH
    kv = lambda i: w_in[:, o_kv + i * KV_WIDTH:o_kv + (i + 1) * KV_WIDTH]
    o_ng = o_kv + 6 * KV_WIDTH
    w_main = jnp.concatenate([w_in[:, :2 * M_WIDTH], w_in[:, o_nq:o_kv], kv(0), kv(2), kv(4), kv(1)],
                             axis=1).astype(BF16)
    w_t = jnp.concatenate([kv(3), kv(5), w_in[:, o_ng:], jnp.zeros((D, _R_MV - _R_NG - 3 * N_HEADS), F32),
                           w_in[:, 2 * M_WIDTH:o_mi]], axis=1).T.astype(BF16)
    wg_t = w_in[:, o_mi:o_nq].T
    half = N_HEAD_DIM // 2
    inv = ROPE_THETA ** (-jnp.arange(half, dtype=F32) / half)
    inv_row = jnp.tile(inv, LANE // half)[None, :]
    pos_col = positions.astype(F32).reshape(T, 1)

    zqk, q, kc, ks, kw, vc, vs_t, vw_t, ng_t, mv_t, mo_t, gt = _inproj(
        x2, mod3, norm_mix_pre, w_main, w_t, wg_t, gate_b, pos_col, inv_row, S)
    ym = _mlstm(zqk, mv_t, mo_t, gt, conv_w, head_norm, B, S)
    kcmp, vcmp_t = _compress(kc, vc, _compress_weights(*cmp_k), _compress_weights(*cmp_v), B, S)
    yn = _nsa(q, kcmp, vcmp_t, ks, vs_t, kw, vw_t, ng_t, B, S)

    wr = jnp.concatenate([wg, we, jnp.zeros((D, LANE - MOE_GROUPS - N_EXPERTS), F32)], axis=1)
    br = jnp.concatenate([bg, be, jnp.zeros((LANE - MOE_GROUPS - N_EXPERTS,), F32)])[None, :]
    x1, h2t, route, cnt = _post(ym, yn, w_out, x2, mod3, norm_mix_post, norm_ffn_pre, wr, br, S)

    n_blocks = (2 * T) // EXPERT_ROWS + N_EXPERTS
    pstart, block_expert, n_used = _sort_plan(cnt, n_blocks)
    n_tok = min(GATHER_TILE, S)
    asg = jnp.concatenate([route[:, 0:2], route[:, 4:6]], axis=1).astype(jnp.int32)
    asg3 = asg.reshape(T // n_tok, 1, _ASG * n_tok)
    xs = _dispatch(pstart, asg3, h2t, n_blocks * EXPERT_ROWS)
    ys = _experts(block_expert, n_used, xs, w1, w3, w2)
    out = _combine(pstart, asg3, ys, x1, route, mod3, norm_ffn_post, S)
    return out.reshape(B, S, D)


def kernel(x, c, positions, ada_w, ada_b, norm_mix_pre, norm_mix_post, norm_ffn_pre, norm_ffn_post, w_in, mlstm_conv_w, mlstm_gate_b, mlstm_head_norm, cmp_pe_k, cmp_w1_k, cmp_w2_k, cmp_pe_v, cmp_w1_v, cmp_w2_v, w_out, router_grp_w, router_grp_b, router_exp_w, router_exp_b, expert_w1, expert_w3, expert_w2):
    for l in range(ada_w.shape[0]):
        x = _layer(x, c, positions, ada_w[l], ada_b[l], norm_mix_pre[l], norm_mix_post[l], norm_ffn_pre[l],
                   norm_ffn_post[l], w_in[l], mlstm_conv_w[l], mlstm_gate_b[l], mlstm_head_norm[l],
                   (cmp_pe_k[l], cmp_w1_k[l], cmp_w2_k[l]), (cmp_pe_v[l], cmp_w1_v[l], cmp_w2_v[l]), w_out[l],
                   router_grp_w[l], router_grp_b[l], router_exp_w[l], router_exp_b[l],
                   expert_w1[l], expert_w3[l], expert_w2[l])
    return x
```

```python
import functools

import jax
import jax.numpy as jnp
import numpy as np
from jax import lax
from jax.experimental import pallas as pl
from jax.experimental.pallas import tpu as pltpu

M_HEADS = 4
M_HEAD_DIM = 128
M_WIDTH = M_HEADS * M_HEAD_DIM
M_CONV = 4
N_HEADS = 8
N_KV_GROUPS = 2
HEADS_PER_GROUP = N_HEADS // N_KV_GROUPS
N_HEAD_DIM = 64
N_WIDTH = N_HEADS * N_HEAD_DIM
KV_WIDTH = N_KV_GROUPS * N_HEAD_DIM
CMP_BLOCK = 32
CMP_STRIDE = 16
CMP_HIDDEN = 2 * N_HEAD_DIM
SLC_BLOCK = 64
SLC_TOPK = 16
WINDOW = 512
ROPE_THETA = 10000.0
MOE_GROUPS = 4
EXPERTS_PER_GROUP = 8
N_EXPERTS = MOE_GROUPS * EXPERTS_PER_GROUP
RMS_EPS = 1e-6

LANE = 128
SUBLANE = 8
VMEM_LIMIT_BYTES = 56 * 1024 * 1024

TOKEN_TILE = 512
MLSTM_CHUNK = 256
NSA_Q_TILE = 256
NSA_K_TILE = 512
NSA_COL_BLOCK = 256
EXPERT_ROWS = 512
GATHER_TILE = 256
ISSUE_UNROLL = 8

F32 = jnp.float32
BF16 = jnp.bfloat16
NEG = -1e30
HIGHEST = lax.Precision.HIGHEST
NT_DIMS = (((1,), (1,)), ((), ()))
TN_DIMS = (((0,), (0,)), ((), ()))


def _params(n_grid):
    return pltpu.CompilerParams(
        dimension_semantics=("arbitrary",) * n_grid, vmem_limit_bytes=VMEM_LIMIT_BYTES)


def _dot(a, b, **kw):
    return jnp.dot(a, b, preferred_element_type=F32, **kw)


def _dot_nt(a, b, **kw):
    return lax.dot_general(a, b, NT_DIMS, preferred_element_type=F32, **kw)


def _dot_tn(a, b, **kw):
    return lax.dot_general(a, b, TN_DIMS, preferred_element_type=F32, **kw)


def _sigmoid(x):
    return 1.0 / (1.0 + jnp.exp(-x))


def _silu(x):
    return x * _sigmoid(x)


def _rms(x, w):
    return x * lax.rsqrt(jnp.mean(x * x, axis=-1, keepdims=True) + RMS_EPS) * w


def _full(shape):
    return pl.BlockSpec(shape, lambda *_: (0,) * len(shape))


def _adaln_kernel(c_ref, w_ref, b_ref, o_ref):
    o_ref[...] = _dot(_silu(c_ref[...]), w_ref[...], precision=HIGHEST) + b_ref[...]


def _adaln(c, ada_w, ada_b):
    B, D = c.shape
    n = ada_w.shape[1] // D
    return pl.pallas_call(
        _adaln_kernel,
        grid=(n,),
        in_specs=[_full((B, D)), pl.BlockSpec((D, D), lambda j: (0, j)), pl.BlockSpec((1, D), lambda j: (0, j))],
        out_specs=pl.BlockSpec((B, D), lambda j: (0, j)),
        out_shape=jax.ShapeDtypeStruct((B, ada_w.shape[1]), F32),
        compiler_params=_params(1),
        name="adaln",
    )(c, ada_w, ada_b[None, :])


_C_M = 0
_C_Q = 2 * M_WIDTH
_C_KC = _C_Q + N_WIDTH
_C_KS = _C_KC + KV_WIDTH
_C_KW = _C_KS + KV_WIDTH
_C_VC = _C_KW + KV_WIDTH
_C_END = _C_VC + KV_WIDTH
_R_VS = 0
_R_VW = KV_WIDTH
_R_NG = 2 * KV_WIDTH
_R_MV = _R_NG + 4 * SUBLANE
_R_MO = _R_MV + M_WIDTH
_R_GH = _R_MO + M_WIDTH
_R_GL = _R_GH + 2 * M_HEADS
_R_END = _R_GL + 2 * M_HEADS


def _inproj_kernel(x_ref, mod_ref, nw_ref, w_ref, wt_ref, wg_ref, gb_ref, pos_ref, inv_ref,
                   zm_ref, q_ref, kc_ref, ks_ref, kw_ref, vc_ref, vst_ref, vwt_ref, ngt_ref, mvt_ref, mot_ref, gt_ref):
    mod = mod_ref[0]
    h = _rms(x_ref[...], nw_ref[...]) * (1.0 + mod[1:2]) + mod[0:1]
    hb = h.astype(BF16)
    z = _dot(hb, w_ref[...])
    zm_ref[...] = z[:, _C_M:_C_Q]
    zt = _dot_nt(wt_ref[...], hb)
    h_lo = (h - hb.astype(F32)).astype(BF16)
    gt_ref[...] = (zt[_R_GH:_R_GL] + zt[_R_GL:_R_END] + _dot_nt(wg_ref[...], h_lo)) + gb_ref[...]
    vst_ref[...] = zt[_R_VS:_R_VW].astype(BF16)
    vwt_ref[...] = zt[_R_VW:_R_NG].astype(BF16)
    ngt_ref[...] = _sigmoid(zt[_R_NG:_R_MV])
    mvt_ref[...] = zt[_R_MV:_R_MO].astype(BF16)
    mot_ref[...] = zt[_R_MO:_R_GH]

    ang = pos_ref[...] * inv_ref[...]
    cos = jnp.cos(ang)
    sin = jnp.sin(ang)
    lane = lax.broadcasted_iota(jnp.int32, (1, LANE), 1)
    first = (lane % N_HEAD_DIM) < (N_HEAD_DIM // 2)
    sin_signed = jnp.where(first, -sin, sin)

    def rope(slab):
        rot = jnp.where(first, pltpu.roll(slab, LANE - N_HEAD_DIM // 2, 1), pltpu.roll(slab, N_HEAD_DIM // 2, 1))
        return slab * cos + rot * sin_signed

    scale = N_HEAD_DIM ** -0.5
    for j in range(N_WIDTH // LANE):
        q_ref[:, j * LANE:(j + 1) * LANE] = (rope(z[:, _C_Q + j * LANE:_C_Q + (j + 1) * LANE]) * scale).astype(BF16)
    kc_ref[...] = rope(z[:, _C_KC:_C_KS])
    ks_ref[...] = rope(z[:, _C_KS:_C_KW]).astype(BF16)
    kw_ref[...] = rope(z[:, _C_KW:_C_VC]).astype(BF16)
    vc_ref[...] = z[:, _C_VC:_C_END]


def _inproj(x2, mod3, norm_w, w_main, w_t, wg_t, gate_b, pos_col, inv_row, S):
    T, D = x2.shape
    TM = min(TOKEN_TILE, S)
    tiles_per_seq = S // TM
    row = lambda w: pl.BlockSpec((TM, w), lambda i: (i, 0))
    col = lambda r: pl.BlockSpec((r, TM), lambda i: (0, i))
    outs = [(2 * M_WIDTH, F32), (N_WIDTH, BF16), (KV_WIDTH, F32), (KV_WIDTH, BF16), (KV_WIDTH, BF16), (KV_WIDTH, F32)]
    outs_t = [(KV_WIDTH, BF16), (KV_WIDTH, BF16), (_R_MV - _R_NG, F32), (M_WIDTH, BF16), (M_WIDTH, F32),
              (2 * M_HEADS, F32)]
    return pl.pallas_call(
        _inproj_kernel,
        grid=(T // TM,),
        in_specs=[row(D), pl.BlockSpec((1, 6, D), lambda i: (i // tiles_per_seq, 0, 0)), _full((1, D)),
                  _full(w_main.shape), _full(w_t.shape), _full(wg_t.shape), _full((2 * M_HEADS, 1)), row(1),
                  _full((1, LANE))],
        out_specs=[row(w) for w, _ in outs] + [col(r) for r, _ in outs_t],
        out_shape=[jax.ShapeDtypeStruct((T, w), dt) for w, dt in outs]
        + [jax.ShapeDtypeStruct((r, T), dt) for r, dt in outs_t],
        compiler_params=_params(1),
        name="inproj",
    )(x2, mod3, norm_w[None, :], w_main, w_t, wg_t, gate_b[:, None], pos_col, inv_row)


def _mlstm_kernel(zm_ref, vt_ref, ot_ref, gt_ref, cw_ref, hn_ref, ltri_ref, eye_ref, y_ref, buf, c_s, n_s, m_s):
    L = zm_ref.shape[0]
    QK = 2 * M_WIDTH
    DH = M_HEAD_DIM

    @pl.when(pl.program_id(1) == 0)
    def _():
        buf[0:SUBLANE, :] = jnp.zeros((SUBLANE, QK), F32)
        c_s[...] = jnp.zeros_like(c_s)
        n_s[...] = jnp.zeros_like(n_s)
        m_s[...] = jnp.zeros_like(m_s)

    buf[SUBLANE:SUBLANE + L, :] = zm_ref[:, 0:QK]
    cw = cw_ref[...]
    conv = cw[M_CONV - 1:M_CONV] * buf[SUBLANE:SUBLANE + L, :]
    for j in range(M_CONV - 1):
        off = SUBLANE - (M_CONV - 1) + j
        conv = conv + cw[j:j + 1] * buf[off:off + L, :]
    buf[0:SUBLANE, :] = buf[L:L + SUBLANE, :]
    qk = _silu(conv)

    g = gt_ref[...]
    fp = g[M_HEADS:]
    lf = jnp.minimum(fp, 0.0) - jnp.log(1.0 + jnp.exp(-jnp.abs(fp)))
    lf8 = jnp.concatenate([lf, jnp.zeros_like(lf)], axis=0)
    ltri = ltri_ref[...]
    b_rows = _dot_nt(lf8, ltri, precision=HIGHEST)
    b_cols = _dot_nt(ltri, lf8, precision=HIGHEST)
    i_cols = _dot_nt(eye_ref[...], g, precision=HIGHEST)

    causal = lax.broadcasted_iota(jnp.int32, (L, L), 0) <= lax.broadcasted_iota(jnp.int32, (L, L), 1)
    ones_rows = jnp.ones((2 * SUBLANE, L), F32)
    outs = []
    for h in range(M_HEADS):
        hs = slice(h * DH, (h + 1) * DH)
        b_r, i_r = b_rows[h:h + 1], g[h:h + 1]
        src_c = b_cols[:, h:h + 1] - i_cols[:, h:h + 1]
        m_prev = m_s[h:h + 1, 0:1]
        g_tot = b_r[:, L - 1:L]
        a_r = g_tot - b_r + i_r
        m_new = jnp.maximum(g_tot + m_prev, jnp.max(a_r, axis=1, keepdims=True))

        qb = qk[:, hs].astype(BF16)
        kb = (qk[:, M_WIDTH + h * DH:M_WIDTH + (h + 1) * DH] * (DH ** -0.5)).astype(BF16)
        vt = vt_ref[hs, :]

        dlog = jnp.where(causal, b_r - src_c, -jnp.inf)
        inter = b_r + m_prev
        m_t = jnp.maximum(inter, jnp.max(dlog, axis=0, keepdims=True))
        wts = jnp.exp(dlog - m_t) * _dot_nt(kb, qb)
        dec = jnp.exp(inter - m_t)
        ct_prev = c_s[h]
        n_prev = n_s[...]
        num = _dot(vt, wts.astype(BF16)) + dec * _dot_nt(ct_prev.astype(BF16), qb)
        qn = _dot_nt(n_prev.astype(BF16), qb)[h:h + 1]
        den = jnp.sum(wts, axis=0, keepdims=True) + dec * qn
        hh = num * (1.0 / jnp.maximum(jnp.abs(den), jnp.exp(-m_t)))

        w_r = jnp.exp(a_r - m_new)
        keep = jnp.exp(g_tot + m_prev - m_new)
        lhs = jnp.concatenate([vt.astype(F32), ones_rows], axis=0) * w_r
        upd = _dot(lhs.astype(BF16), kb)
        c_s[h] = keep * ct_prev + upd[:DH]
        n_s[h:h + 1] = keep * n_prev[h:h + 1] + upd[DH:DH + 1]
        m_s[h:h + 1] = jnp.broadcast_to(m_new, (1, LANE))

        hn = hh * lax.rsqrt(jnp.mean(hh * hh, axis=0, keepdims=True) + RMS_EPS)
        hn = hn * jnp.concatenate([hn_ref[hs, :]] * (L // LANE), axis=1)
        outs.append(_sigmoid(ot_ref[hs, :]) * hn)
    y_ref[...] = jnp.concatenate(outs, axis=0).T.astype(BF16)


def _mlstm(zqk, v_t, o_t, gt, conv_w, head_norm, B, S):
    T = zqk.shape[0]
    L = min(MLSTM_CHUNK, S)
    assert L % LANE == 0
    nc = S // L
    ltri = jnp.asarray(np.tril(np.ones((L, L), np.float32)))
    eye = jnp.asarray(np.eye(L, dtype=np.float32))
    col = lambda r: pl.BlockSpec((r, L), lambda b, c: (0, b * nc + c))
    return pl.pallas_call(
        _mlstm_kernel,
        grid=(B, nc),
        in_specs=[pl.BlockSpec((L, 2 * M_WIDTH), lambda b, c: (b * nc + c, 0)), col(M_WIDTH), col(M_WIDTH),
                  col(2 * M_HEADS), _full((M_CONV, 2 * M_WIDTH)), _full((M_WIDTH, LANE)), _full((L, L)), _full((L, L))],
        out_specs=pl.BlockSpec((L, M_WIDTH), lambda b, c: (b * nc + c, 0)),
        out_shape=jax.ShapeDtypeStruct((T, M_WIDTH), BF16),
        scratch_shapes=[pltpu.VMEM((L + SUBLANE, 2 * M_WIDTH), F32),
                        pltpu.VMEM((M_HEADS, M_HEAD_DIM, M_HEAD_DIM), F32),
                        pltpu.VMEM((SUBLANE, M_HEAD_DIM), F32),
                        pltpu.VMEM((SUBLANE, LANE), F32)],
        compiler_params=_params(2),
        name="mlstm",
    )(zqk, v_t, o_t, gt, conv_w, jnp.broadcast_to(head_norm[:, None], (M_WIDTH, LANE)), ltri, eye)


def _compress_kernel(k_ref, v_ref, pak, pbk, wak, wbk, w2k, pav, pbv, wav, wbv, w2v, ko_ref, vo_ref):
    def one(x_ref, pa, pb, wa, wb, w2, o_ref, transposed):
        n = x_ref.shape[0] // CMP_STRIDE
        x = jnp.concatenate([x_ref[pl.ds(t, n, stride=CMP_STRIDE), :] for t in range(CMP_STRIDE)], axis=1)
        first = _dot((x + pa[...]).astype(BF16), wa[...])
        second = _dot((x + pb[...]).astype(BF16), wb[...])
        hid = _silu(first + pltpu.roll(second, n - 1, 0)).astype(BF16)
        o_ref[0] = (_dot_nt(w2[...], hid) if transposed else _dot(hid, w2[...])).astype(BF16)

    one(k_ref, pak, pbk, wak, wbk, w2k, ko_ref, False)
    one(v_ref, pav, pbv, wav, wbv, w2v, vo_ref, True)


def _compress_weights(pe, w1, w2):
    half = CMP_BLOCK // 2
    eye = jnp.eye(N_KV_GROUPS, dtype=F32)
    w1r = w1.reshape(CMP_BLOCK, N_HEAD_DIM, CMP_HIDDEN)
    big = lambda w: jnp.einsum("ldc,gh->lgdhc", w, eye).reshape(half * KV_WIDTH, N_KV_GROUPS * CMP_HIDDEN).astype(BF16)
    pe_row = lambda p: jnp.broadcast_to(p[:, None, :], (half, N_KV_GROUPS, N_HEAD_DIM)).reshape(1, half * KV_WIDTH)
    w2bd = jnp.einsum("cd,gh->gchd", w2, eye).reshape(N_KV_GROUPS * CMP_HIDDEN, KV_WIDTH).astype(BF16)
    return pe_row(pe[:half]), pe_row(pe[half:]), big(w1r[:half]), big(w1r[half:]), w2bd


def _compress(kc, vc, wk, wv, B, S):
    nc = S // CMP_STRIDE
    blk = pl.BlockSpec((S, KV_WIDTH), lambda b: (b, 0))
    wv = wv[:4] + (wv[4].T,)
    return pl.pallas_call(
        _compress_kernel,
        grid=(B,),
        in_specs=[blk, blk] + [_full(w.shape) for w in wk] + [_full(w.shape) for w in wv],
        out_specs=[pl.BlockSpec((1, nc, KV_WIDTH), lambda b: (b, 0, 0)),
                   pl.BlockSpec((1, KV_WIDTH, nc), lambda b: (b, 0, 0))],
        out_shape=[jax.ShapeDtypeStruct((B, nc, KV_WIDTH), BF16), jax.ShapeDtypeStruct((B, KV_WIDTH, nc), BF16)],
        compiler_params=_params(1),
        name="compress",
    )(kc, vc, *wk, *wv)


def _masked_softmax_keys(s, mask):
    s = jnp.where(mask, s, -jnp.inf)
    m = jnp.max(s, axis=0, keepdims=True)
    m = jnp.where(jnp.isfinite(m), m, 0.0)
    e = jnp.exp(s - m)
    return e, 1.0 / jnp.maximum(jnp.sum(e, axis=0, keepdims=True), 1e-30)


def _nsa_kernel(q_ref, kc_ref, vct_ref, ks_ref, vst_ref, kw_ref, vwt_ref, ngt_ref, ov_ref, ek_ref, eye_ref,
                y_ref, m_s, l_s, acc_s, *, n_top, TK):
    TQ = q_ref.shape[0]
    NS = ov_ref.shape[0]
    G, HPG, DH = N_KV_GROUPS, HEADS_PER_GROUP, N_HEAD_DIM
    R = G * HPG * TQ
    q0 = pl.program_id(1) * TQ

    q = q_ref[...]
    zero = jnp.zeros((TQ, DH), BF16)
    parts = []
    for hd in range(N_HEADS):
        qh = q[:, hd * DH:(hd + 1) * DH]
        parts.append(jnp.concatenate([qh, zero] if hd < HPG else [zero, qh], axis=1))
    qp = jnp.concatenate(parts, axis=0)
    t_q = q0 + lax.broadcasted_iota(jnp.int32, (1, R), 1) % TQ

    kc = kc_ref[0]
    NC = kc.shape[0]
    cmp_end = lax.broadcasted_iota(jnp.int32, (NC, 1), 0) * CMP_STRIDE + (CMP_BLOCK - 1)
    e_c, inv_c = _masked_softmax_keys(_dot_nt(kc, qp), cmp_end <= t_q)
    p_c = e_c * inv_c
    o_c = _dot(vct_ref[0], p_c.astype(BF16))
    p_grp = jnp.concatenate(
        [sum(p_c[:, (g * HPG + h) * TQ:(g * HPG + h + 1) * TQ] for h in range(HPG)) for g in range(G)], axis=1)
    imp = _dot(ov_ref[...], p_grp, precision=HIGHEST)

    j_io = lax.broadcasted_iota(jnp.int32, (NS, G * TQ), 0)
    t_row = q0 + lax.broadcasted_iota(jnp.int32, (NS, G * TQ), 1) % TQ
    cur = t_row // SLC_BLOCK
    forced = (j_io == 0) | (j_io == cur) | (j_io == cur - 1)
    future = j_io * SLC_BLOCK > t_row
    score = jnp.where(forced, jnp.inf, jnp.where(future, -jnp.inf, imp))
    rank = jnp.zeros((NS, G * TQ), F32)
    for jp in range(NS):
        other = score[jp:jp + 1, :]
        rank = rank + jnp.where(j_io > jp, jnp.where(other >= score, 1.0, 0.0), jnp.where(other > score, 1.0, 0.0))
    sel_bias = jnp.where(future, NEG, jnp.where(rank < n_top, 0.0, NEG)).astype(BF16)
    sel_rows = _dot_tn(sel_bias, eye_ref[...]).astype(BF16)
    sel_rows = jnp.broadcast_to(sel_rows.reshape(G, 1, TQ, LANE), (G, HPG, TQ, LANE)).reshape(R, LANE)
    qa = jnp.concatenate([qp, sel_rows], axis=1)

    m_s[...] = jnp.full_like(m_s, NEG)
    l_s[...] = jnp.zeros_like(l_s)
    acc_s[...] = jnp.zeros_like(acc_s)

    CB = NSA_COL_BLOCK

    def step(kt, causal):
        k0 = pl.multiple_of(kt * TK, TK)
        ka = jnp.concatenate([ks_ref[0, pl.ds(k0, TK), :], ek_ref[pl.ds(k0, TK), :]], axis=1)
        vt = vst_ref[:, pl.ds(k0, TK)]
        kpos = k0 + lax.broadcasted_iota(jnp.int32, (TK, 1), 0)
        scores = [_dot_nt(ka, qa[j * CB:(j + 1) * CB]) for j in range(R // CB)]
        probs, alphas = [], []
        for j in range(R // CB):
            cols = slice(j * CB, (j + 1) * CB)
            s = scores[j]
            if causal:
                s = jnp.where(kpos <= t_q[:, cols], s, NEG)
            m_old = m_s[:, cols]
            m_new = jnp.maximum(m_old, jnp.max(s, axis=0, keepdims=True))
            alpha = jnp.exp(m_old - m_new)
            p = jnp.exp(s - m_new)
            l_s[:, cols] = alpha * l_s[:, cols] + jnp.sum(p, axis=0, keepdims=True)
            m_s[:, cols] = m_new
            probs.append(p.astype(BF16))
            alphas.append(alpha)
        for j in range(R // CB):
            cols = slice(j * CB, (j + 1) * CB)
            acc_s[:, cols] = alphas[j] * acc_s[:, cols] + _dot(vt, probs[j])

    def full_tile(kt, carry):
        step(kt, False)
        return carry

    last = q0 // TK
    lax.fori_loop(0, last, full_tile, 0)
    step(last, True)
    o_s = acc_s[...] * (1.0 / l_s[...])

    WK = WINDOW + TQ
    start = pl.multiple_of(jnp.maximum(q0 - WINDOW, 0), TQ)
    wpos = start + lax.broadcasted_iota(jnp.int32, (WK, 1), 0)
    e_w, inv_w = _masked_softmax_keys(_dot_nt(kw_ref[0, pl.ds(start, WK), :], qp),
                                      (t_q - wpos).astype(jnp.uint32) < WINDOW)
    o_w =_dot(vwt_ref[:, pl.ds(start, WK)], e_w.astype(BF16)) * inv_w

    gates = ngt_ref[...]
    outs = []
    for hd in range(N_HEADS):
        cols = slice(hd * TQ, (hd + 1) * TQ)
        rows = slice((hd // HPG) * DH, (hd // HPG + 1) * DH)
        outs.append(gates[3 * hd:3 * hd + 1] * o_c[rows, cols]
                    + gates[3 * hd + 1:3 * hd + 2] * o_s[rows, cols]
                    + gates[3 * hd + 2:3 * hd + 3] * o_w[rows, cols])
    y_ref[...] = jnp.concatenate(outs, axis=0).T.astype(BF16)


def _nsa(q, kcmp, vcmp_t, ks, vs_t, kw, vw_t, ng_t, B, S):
    T = q.shape[0]
    TQ = min(NSA_Q_TILE, S)
    TK = min(NSA_K_TILE, S)
    nq = S // TQ
    NS = S // SLC_BLOCK
    NC = S // CMP_STRIDE
    n_cmp = (S - CMP_BLOCK) // CMP_STRIDE + 1
    js = np.arange(NS)[:, None] * SLC_BLOCK
    cs = np.arange(NC)[None, :] * CMP_STRIDE
    ov = np.clip(np.minimum(js + SLC_BLOCK, cs + CMP_BLOCK) - np.maximum(js, cs), 0, None) / CMP_STRIDE
    ov[:, n_cmp:] = 0.0
    assert NS <= LANE and TK % TQ == 0
    block_of_key = (np.arange(S)[:, None] // SLC_BLOCK == np.arange(LANE)[None, :]).astype(np.float32)
    eye_pad = np.eye(NS, LANE, dtype=np.float32)
    R = N_HEADS * TQ
    assert R % NSA_COL_BLOCK == 0
    seq = lambda: pl.BlockSpec((1, S, KV_WIDTH), lambda b, i: (b, 0, 0))
    seq_t = lambda: pl.BlockSpec((KV_WIDTH, S), lambda b, i: (0, b))
    return pl.pallas_call(
        functools.partial(_nsa_kernel, n_top=min(SLC_TOPK, NS), TK=TK),
        grid=(B, nq),
        in_specs=[pl.BlockSpec((TQ, N_WIDTH), lambda b, i: (b * nq + i, 0)),
                  pl.BlockSpec((1, NC, KV_WIDTH), lambda b, i: (b, 0, 0)),
                  pl.BlockSpec((1, KV_WIDTH, NC), lambda b, i: (b, 0, 0)),
                  seq(), seq_t(), seq(), seq_t(), pl.BlockSpec((ng_t.shape[0], TQ), lambda b, i: (0, b * nq + i)),
                  _full((NS, NC)), _full((S, LANE)), _full((NS, LANE))],
        out_specs=pl.BlockSpec((TQ, N_WIDTH), lambda b, i: (b * nq + i, 0)),
        out_shape=jax.ShapeDtypeStruct((T, N_WIDTH), BF16),
        scratch_shapes=[pltpu.VMEM((1, R), F32), pltpu.VMEM((1, R), F32), pltpu.VMEM((KV_WIDTH, R), F32)],
        compiler_params=_params(2),
        name="nsa",
    )(q, kcmp, vcmp_t, ks.reshape(B, S, KV_WIDTH), vs_t, kw.reshape(B, S, KV_WIDTH), vw_t, ng_t,
      jnp.asarray(ov.astype(np.float32)), jnp.asarray(block_of_key, dtype=BF16), jnp.asarray(eye_pad, dtype=BF16))


def _post_kernel(ym_ref, yn_ref, wm_ref, wn_ref, x_ref, mod_ref, npost_ref, npre_ref, wrh_ref, wrl_ref, br_ref,
                 ltri_ref, x1_ref, h2_ref, route_ref, cnt_ref, carry):
    TM = x_ref.shape[0]

    @pl.when(pl.program_id(0) == 0)
    def _():
        carry[...] = jnp.zeros_like(carry)

    mod = mod_ref[0]
    y = _dot(ym_ref[...], wm_ref[...]) + _dot(yn_ref[...], wn_ref[...])
    x1 = x_ref[...] + mod[2:3] * _rms(y, npost_ref[...])
    x1_ref[...] = x1
    h2 = _rms(x1, npre_ref[...]) * (1.0 + mod[4:5]) + mod[3:4]
    for s in range(SUBLANE):
        h2_ref[pl.ds(s, TM, stride=SUBLANE), :] = h2[:, s * LANE:(s + 1) * LANE]

    h_hi = h2.astype(BF16)
    h_lo = (h2 - h_hi.astype(F32)).astype(BF16)
    logits = _dot(h_hi, wrh_ref[...]) + _dot(h_lo, wrh_ref[...]) + _dot(h_hi, wrl_ref[...]) + br_ref[...]
    lane = lax.broadcasted_iota(jnp.int32, (TM, LANE), 1)
    is_grp = lane < MOE_GROUPS
    lg = jnp.where(is_grp, logits, -jnp.inf)
    eg = jnp.exp(lg - jnp.max(lg, axis=1, keepdims=True))
    pg = eg / jnp.sum(eg, axis=1, keepdims=True)
    pg_top = jnp.max(pg, axis=1, keepdims=True)
    grp = jnp.min(jnp.where(is_grp & (pg == pg_top), lane, LANE), axis=1, keepdims=True)
    lo = MOE_GROUPS + EXPERTS_PER_GROUP * grp
    in_grp = (lane >= lo) & (lane < lo + EXPERTS_PER_GROUP)
    le = jnp.where(in_grp, logits, -jnp.inf)
    ee = jnp.exp(le - jnp.max(le, axis=1, keepdims=True))
    pe = jnp.where(in_grp, ee / jnp.sum(ee, axis=1, keepdims=True), -1.0)
    p1 = jnp.max(pe, axis=1, keepdims=True)
    i1 = jnp.min(jnp.where(pe == p1, lane, LANE), axis=1, keepdims=True)
    pe2 = jnp.where(lane == i1, -1.0, pe)
    p2 = jnp.max(pe2, axis=1, keepdims=True)
    i2 = jnp.min(jnp.where((pe2 == p2) & in_grp & (lane != i1), lane, LANE), axis=1, keepdims=True)
    den = p1 + p2
    e1 = i1 - MOE_GROUPS
    e2 = i2 - MOE_GROUPS
    oh1, oh2 = lane == e1, lane == e2
    oh = jnp.where(oh1, 1.0, 0.0) + jnp.where(oh2, 1.0, 0.0)
    before = _dot(ltri_ref[...], oh.astype(BF16)) + carry[0:1, :]
    r1 = jnp.sum(jnp.where(oh1, before, 0.0), axis=1, keepdims=True)
    r2 = jnp.sum(jnp.where(oh2, before, 0.0), axis=1, keepdims=True)
    carry[...] = carry[...] + jnp.sum(oh, axis=0, keepdims=True)
    cnt_ref[...] = carry[...]
    cols = (e1.astype(F32), e2.astype(F32), pg_top * p1 / den, pg_top * p2 / den, r1, r2)
    route = jnp.zeros((TM, LANE), F32)
    for k, col in enumerate(cols):
        route = jnp.where(lane == k, col, route)
    route_ref[...] = route


def _post(ym, yn, w_out, x2, mod3, norm_post, norm_pre, wr, br, S):
    T, D = x2.shape
    TM = min(TOKEN_TILE, S)
    tiles_per_seq = S // TM
    row = lambda w: pl.BlockSpec((TM, w), lambda i: (i, 0))
    small = pl.BlockSpec((SUBLANE, LANE), lambda i: (0, 0))
    ltri = jnp.asarray(np.tril(np.ones((TM, TM), np.float32), -1), dtype=BF16)
    return pl.pallas_call(
        _post_kernel,
        grid=(T // TM,),
        in_specs=[row(M_WIDTH), row(N_WIDTH), _full((M_WIDTH, D)), _full((N_WIDTH, D)), row(D),
                  pl.BlockSpec((1, 6, D), lambda i: (i // tiles_per_seq, 0, 0)), _full((1, D)), _full((1, D)),
                  _full((D, LANE)), _full((D, LANE)), _full((1, LANE)), _full((TM, TM))],
        out_specs=[row(D), pl.BlockSpec((TM * SUBLANE, LANE), lambda i: (i, 0)), row(LANE), small],
        out_shape=[jax.ShapeDtypeStruct((T, D), F32), jax.ShapeDtypeStruct((T * SUBLANE, LANE), F32),
                   jax.ShapeDtypeStruct((T, LANE), F32), jax.ShapeDtypeStruct((SUBLANE, LANE), F32)],
        scratch_shapes=[pltpu.VMEM((SUBLANE, LANE), F32)],
        compiler_params=_params(1),
        name="post_mix_router",
    )(ym, yn, w_out[:M_WIDTH].astype(BF16), w_out[M_WIDTH:].astype(BF16), x2, mod3,
      norm_post[None, :], norm_pre[None, :], wr.astype(BF16), (wr - wr.astype(BF16).astype(F32)).astype(BF16), br,
      ltri)


def _plan_kernel(cnt_ref, pstart_ref, blk_ref, tot_ref, *, rows_per_block):
    lane = lax.broadcasted_iota(jnp.int32, (SUBLANE, LANE), 1)
    nblk = (cnt_ref[...].astype(jnp.int32) + (rows_per_block - 1)) // rows_per_block
    end = nblk
    sh = 1
    while sh < 2 * N_EXPERTS:
        end = end + jnp.where(lane >= sh, pltpu.roll(end, sh, 1), 0)
        sh *= 2
    pstart_ref[...] = (end - nblk) * rows_per_block
    nbp = blk_ref.shape[0]
    blk_io = lax.broadcasted_iota(jnp.int32, (nbp, LANE), 0)
    lane_b = lax.broadcasted_iota(jnp.int32, (nbp, LANE), 1)
    passed = jnp.where((lane_b < N_EXPERTS) & (blk_io >= end[0:1, :]), 1, 0)
    blk_ref[...] = jnp.broadcast_to(jnp.minimum(jnp.sum(passed, axis=1, keepdims=True), N_EXPERTS - 1), (nbp, LANE))
    tot_ref[...] = jnp.broadcast_to(
        jnp.sum(jnp.where(lane == N_EXPERTS - 1, end, 0), axis=1, keepdims=True), (SUBLANE, LANE))


def _sort_plan(cnt, n_blocks):
    small = pl.BlockSpec((SUBLANE, LANE), lambda i: (0, 0))
    nbp = -(-n_blocks // SUBLANE) * SUBLANE
    pstart, blk, tot = pl.pallas_call(
        functools.partial(_plan_kernel, rows_per_block=EXPERT_ROWS),
        grid=(1,),
        in_specs=[small],
        out_specs=[small, _full((nbp, LANE)), small],
        out_shape=[jax.ShapeDtypeStruct((SUBLANE, LANE), jnp.int32), jax.ShapeDtypeStruct((nbp, LANE), jnp.int32),
                   jax.ShapeDtypeStruct((SUBLANE, LANE), jnp.int32)],
        compiler_params=_params(1),
        name="expert_plan",
    )(cnt)
    return pstart[0, :N_EXPERTS + 1], blk[:n_blocks, 0], tot[0, :1]


_ASG = 4


def _dest_row(pstart_ref, asg_ref, j, k):
    return pstart_ref[asg_ref[0, 0, _ASG * j + k]] + asg_ref[0, 0, _ASG * j + 2 + k]


def _row_copy(src, src_row, dst, dst_row, sem):
    return pltpu.make_async_copy(src.at[pl.ds(pl.multiple_of(src_row * SUBLANE, SUBLANE), SUBLANE)],
                                 dst.at[pl.ds(pl.multiple_of(dst_row * SUBLANE, SUBLANE), SUBLANE)], sem)


def _dispatch_kernel(pstart_ref, asg_ref, h2_ref, xs_hbm, zeros, sem, zsem, *, n_blocks):
    n = asg_ref.shape[2] // _ASG
    RB = zeros.shape[0] // SUBLANE

    @pl.when(pl.program_id(0) == 0)
    def _():
        zeros[...] = jnp.zeros_like(zeros)

        def fill(row0):
            return pltpu.make_async_copy(
                zeros, xs_hbm.at[pl.ds(pl.multiple_of(row0 * SUBLANE, SUBLANE), RB * SUBLANE)], zsem)

        def experts(act):
            def body(e, carry):
                @pl.when(pstart_ref[e + 1] > pstart_ref[e])
                def _():
                    act(fill(pstart_ref[e + 1] - RB))
                return carry
            lax.fori_loop(0, N_EXPERTS, body, 0)

        def trailing(act):
            def body(b, carry):
                act(fill(b * RB))
                return carry
            lax.fori_loop(pstart_ref[N_EXPERTS] // RB, n_blocks, body, 0)

        experts(lambda c: c.start())
        trailing(lambda c: c.start())
        experts(lambda c: c.wait())
        trailing(lambda c: c.wait())

    def issue(jj, carry):
        for u in range(ISSUE_UNROLL):
            j = jj * ISSUE_UNROLL + u
            for k in range(2):
                _row_copy(h2_ref, j, xs_hbm, _dest_row(pstart_ref, asg_ref, j, k), sem).start(priority=k)
        return carry

    lax.fori_loop(0, n // ISSUE_UNROLL, issue, 0)
    whole = xs_hbm.at[pl.ds(0, 2 * n * SUBLANE)]
    pltpu.make_async_copy(whole, whole, sem).wait()


def _dispatch(pstart, asg3, h2t, n_blocks):
    nt, _, width = asg3.shape
    n = width // _ASG
    return pl.pallas_call(
        functools.partial(_dispatch_kernel, n_blocks=n_blocks),
        grid=(nt,),
        in_specs=[pl.BlockSpec(memory_space=pltpu.SMEM),
                  pl.BlockSpec((1, 1, width), lambda i: (i, 0, 0), memory_space=pltpu.SMEM),
                  pl.BlockSpec((n * SUBLANE, LANE), lambda i: (i, 0))],
        out_specs=pl.BlockSpec(memory_space=pl.ANY),
        out_shape=jax.ShapeDtypeStruct((n_blocks * EXPERT_ROWS * SUBLANE, LANE), F32),
        scratch_shapes=[pltpu.VMEM((EXPERT_ROWS * SUBLANE, LANE), F32), pltpu.SemaphoreType.DMA(()),
                        pltpu.SemaphoreType.DMA(())],
        compiler_params=_params(1),
        name="dispatch",
    )(pstart, asg3, h2t)


def _expert_kernel(be_ref, nb_ref, xs_ref, w1_ref, w3_ref, w2_ref, ys_ref, w1b, w3b, w2b):
    RB = xs_ref.shape[0] // SUBLANE
    i = pl.program_id(0)

    @pl.when(i < nb_ref[0])
    def _():
        @pl.when((i == 0) | (be_ref[i] != be_ref[jnp.maximum(i - 1, 0)]))
        def _():
            w1b[...] = w1_ref[0].astype(BF16)
            w3b[...] = w3_ref[0].astype(BF16)
            w2b[...] = w2_ref[0].astype(BF16)

        x = jnp.concatenate([xs_ref[pl.ds(s, RB, stride=SUBLANE), :] for s in range(SUBLANE)], axis=1).astype(BF16)
        hb = _silu(_dot(x, w1b[...])) * _dot(x, w3b[...])
        y = _dot(hb.astype(BF16), w2b[...])
        for s in range(SUBLANE):
            ys_ref[pl.ds(s, RB, stride=SUBLANE), :] = y[:, s * LANE:(s + 1) * LANE]

    @pl.when(i >= nb_ref[0])
    def _():
        ys_ref[...] = jnp.zeros_like(ys_ref)


def _experts(block_expert, n_used, xs, w1, w3, w2):
    n_blocks = block_expert.shape[0]
    RB = EXPERT_ROWS
    D, F = w1.shape[1], w1.shape[2]
    cur = lambda i, nb: jnp.minimum(i, nb[0] - 1)
    rows = pl.BlockSpec((RB * SUBLANE, LANE), lambda i, be, nb: (cur(i, nb), 0))
    wspec = lambda a, b: pl.BlockSpec((1, a, b), lambda i, be, nb: (be[cur(i, nb)], 0, 0))
    return pl.pallas_call(
        _expert_kernel,
        grid_spec=pltpu.PrefetchScalarGridSpec(
            num_scalar_prefetch=2, grid=(n_blocks,),
            in_specs=[rows, wspec(D, F), wspec(D, F), wspec(F, D)],
            out_specs=pl.BlockSpec((RB * SUBLANE, LANE), lambda i, be, nb: (i, 0)),
            scratch_shapes=[pltpu.VMEM((D, F), BF16), pltpu.VMEM((D, F), BF16), pltpu.VMEM((F, D), BF16)]),
        out_shape=jax.ShapeDtypeStruct(xs.shape, F32),
        compiler_params=_params(1),
        name="experts",
    )(block_expert, n_used, xs, w1, w3, w2)


def _combine_kernel(pstart_ref, asg_ref, asg_next_ref, ys_hbm, x1_ref, route_ref, mod_ref, nw_ref, o_ref, buf, sems):
    n = x1_ref.shape[0]
    i = pl.program_id(0)
    slot = i % 2

    def gather(dref, into):
        def issue(jj, carry):
            for u in range(ISSUE_UNROLL):
                j = jj * ISSUE_UNROLL + u
                for k in range(2):
                    _row_copy(ys_hbm, _dest_row(pstart_ref, dref, j, k), buf.at[into], k * n + j,
                              sems.at[into]).start(priority=k)
            return carry

        lax.fori_loop(0, n // ISSUE_UNROLL, issue, 0)

    @pl.when(i == 0)
    def _():
        gather(asg_ref, 0)

    @pl.when(i + 1 < pl.num_programs(0))
    def _():
        gather(asg_next_ref, 1 - slot)

    pltpu.make_async_copy(ys_hbm.at[pl.ds(0, 2 * n * SUBLANE)], buf.at[slot], sems.at[slot]).wait()

    route = route_ref[...]
    g0, g1 = route[:, 2:3], route[:, 3:4]
    y = jnp.concatenate(
        [g0 * buf[slot, pl.ds(s, n, stride=SUBLANE), :] + g1 * buf[slot, pl.ds(n * SUBLANE + s, n, stride=SUBLANE), :]
         for s in range(SUBLANE)], axis=1)
    o_ref[...] = x1_ref[...] + mod_ref[0][5:6] * _rms(y, nw_ref[...])


def _combine(pstart, asg3, ys, x1, route, mod3, norm_w, S):
    T, D = x1.shape
    nt, _, width = asg3.shape
    n = width // _ASG
    tiles_per_seq = S // n
    return pl.pallas_call(
        _combine_kernel,
        grid=(nt,),
        in_specs=[pl.BlockSpec(memory_space=pltpu.SMEM),
                  pl.BlockSpec((1, 1, width), lambda i: (i, 0, 0), memory_space=pltpu.SMEM),
                  pl.BlockSpec((1, 1, width), lambda i: (jnp.minimum(i + 1, nt - 1), 0, 0), memory_space=pltpu.SMEM),
                  pl.BlockSpec(memory_space=pl.ANY), pl.BlockSpec((n, D), lambda i: (i, 0)),
                  pl.BlockSpec((n, LANE), lambda i: (i, 0)),
                  pl.BlockSpec((1, 6, D), lambda i: (i // tiles_per_seq, 0, 0)), _full((1, D))],
        out_specs=pl.BlockSpec((n, D), lambda i: (i, 0)),
        out_shape=jax.ShapeDtypeStruct((T, D), F32),
        scratch_shapes=[pltpu.VMEM((2, 2 * n * SUBLANE, LANE), F32), pltpu.SemaphoreType.DMA((2,))],
        compiler_params=_params(1),
        name="combine",
    )(pstart, asg3, asg3, ys, x1, route, mod3, norm_w[None, :])


def _layer(x, c, positions, ada_w, ada_b, norm_mix_pre, norm_mix_post, norm_ffn_pre, norm_ffn_post, w_in,
           conv_w, gate_b, head_norm, cmp_k, cmp_v, w_out, wg, bg, we, be, w1, w3, w2):
    B, S, D = x.shape
    T = B * S
    x2 = x.reshape(T, D)
    mod3 = _adaln(c, ada_w, ada_b).reshape(B, 6, D)

    o_mi = 4 * M_WIDTH
    o_nq = o_mi + 2 * M_HEADS
    o_kv = o_nq + N_WIDTH
    kv = lambda i: w_in[:, o_kv + i * KV_WIDTH:o_kv + (i + 1) * KV_WIDTH]
    o_ng = o_kv + 6 * KV_WIDTH
    w_main = jnp.concatenate([w_in[:, :2 * M_WIDTH], w_in[:, o_nq:o_kv], kv(0), kv(2), kv(4), kv(1)],
                             axis=1).astype(BF16)
    w_t = jnp.concatenate([kv(3), kv(5), w_in[:, o_ng:], jnp.zeros((D, _R_MV - _R_NG - 3 * N_HEADS), F32),
                           w_in[:, 2 * M_WIDTH:o_mi]], axis=1).T.astype(BF16)
    w_gate = w_in[:, o_mi:o_nq].T
    wg_t = w_gate.astype(BF16)
    w_t = jnp.concatenate([w_t, wg_t, (w_gate - wg_t.astype(F32)).astype(BF16)], axis=0)
    half = N_HEAD_DIM // 2
    inv = ROPE_THETA ** (-jnp.arange(half, dtype=F32) / half)
    inv_row = jnp.tile(inv, LANE // half)[None, :]
    pos_col = positions.astype(F32).reshape(T, 1)

    zqk, q, kc, ks, kw, vc, vs_t, vw_t, ng_t, mv_t, mo_t, gt = _inproj(
        x2, mod3, norm_mix_pre, w_main, w_t, wg_t, gate_b, pos_col, inv_row, S)
    ym = _mlstm(zqk, mv_t, mo_t, gt, conv_w, head_norm, B, S)
    kcmp, vcmp_t = _compress(kc, vc, _compress_weights(*cmp_k), _compress_weights(*cmp_v), B, S)
    yn = _nsa(q, kcmp, vcmp_t, ks, vs_t, kw, vw_t, ng_t, B, S)

    wr = jnp.concatenate([wg, we, jnp.zeros((D, LANE - MOE_GROUPS - N_EXPERTS), F32)], axis=1)
    br = jnp.concatenate([bg, be, jnp.zeros((LANE - MOE_GROUPS - N_EXPERTS,), F32)])[None, :]
    x1, h2t, route, cnt = _post(ym, yn, w_out, x2, mod3, norm_mix_post, norm_ffn_pre, wr, br, S)

    n_blocks = (2 * T) // EXPERT_ROWS + N_EXPERTS
    pstart, block_expert, n_used = _sort_plan(cnt, n_blocks)
    n_tok = min(GATHER_TILE, S)
    asg = jnp.concatenate([route[:, 0:2], route[:, 4:6]], axis=1).astype(jnp.int32)
    asg3 = asg.reshape(T // n_tok, 1, _ASG * n_tok)
    xs = _dispatch(pstart, asg3, h2t, n_blocks)
    ys = _experts(block_expert, n_used, xs, w1, w3, w2)
    out = _combine(pstart, asg3, ys, x1, route, mod3, norm_ffn_post, S)
    return out.reshape(B, S, D)


def kernel(x, c, positions, ada_w, ada_b, norm_mix_pre, norm_mix_post, norm_ffn_pre, norm_ffn_post, w_in, mlstm_conv_w, mlstm_gate_b, mlstm_head_norm, cmp_pe_k, cmp_w1_k, cmp_w2_k, cmp_pe_v, cmp_w1_v, cmp_w2_v, w_out, router_grp_w, router_grp_b, router_exp_w, router_exp_b, expert_w1, expert_w3, expert_w2):
    for l in range(ada_w.shape[0]):
        x = _layer(x, c, positions, ada_w[l], ada_b[l], norm_mix_pre[l], norm_mix_post[l], norm_ffn_pre[l],
                   norm_ffn_post[l], w_in[l], mlstm_conv_w[l], mlstm_gate_b[l], mlstm_head_norm[l],
                   (cmp_pe_k[l], cmp_w1_k[l], cmp_w2_k[l]), (cmp_pe_v[l], cmp_w1_v[l], cmp_w2_v[l]), w_out[l],
                   router_grp_w[l], router_grp_b[l], router_exp_w[l], router_exp_b[l],
                   expert_w1[l], expert_w3[l], expert_w2[l])
    return x
```

```python
import functools

import jax
import jax.numpy as jnp
import numpy as np
from jax import lax
from jax.experimental import pallas as pl
from jax.experimental.pallas import tpu as pltpu

M_HEADS = 4
M_HEAD_DIM = 128
M_WIDTH = M_HEADS * M_HEAD_DIM
M_CONV = 4
N_HEADS = 8
N_KV_GROUPS = 2
HEADS_PER_GROUP = N_HEADS // N_KV_GROUPS
N_HEAD_DIM = 64
N_WIDTH = N_HEADS * N_HEAD_DIM
KV_WIDTH = N_KV_GROUPS * N_HEAD_DIM
CMP_BLOCK = 32
CMP_STRIDE = 16
CMP_HIDDEN = 2 * N_HEAD_DIM
SLC_BLOCK = 64
SLC_TOPK = 16
WINDOW = 512
ROPE_THETA = 10000.0
MOE_GROUPS = 4
EXPERTS_PER_GROUP = 8
N_EXPERTS = MOE_GROUPS * EXPERTS_PER_GROUP
RMS_EPS = 1e-6
LOG2_E = 1.4426950408889634

LANE = 128
SUBLANE = 8
VMEM_LIMIT_BYTES = 56 * 1024 * 1024

TOKEN_TILE = 512
MLSTM_CHUNK = 256
NSA_Q_TILE = 256
NSA_K_TILE = 512
NSA_COL_BLOCK = 256
EXPERT_ROWS = 512
GATHER_TILE = 256
ISSUE_UNROLL = 8

F32 = jnp.float32
BF16 = jnp.bfloat16
NEG = -1e30
HIGHEST = lax.Precision.HIGHEST
NT_DIMS = (((1,), (1,)), ((), ()))
TN_DIMS = (((0,), (0,)), ((), ()))


def _params(n_grid):
    return pltpu.CompilerParams(
        dimension_semantics=("arbitrary",) * n_grid, vmem_limit_bytes=VMEM_LIMIT_BYTES)


def _dot(a, b, **kw):
    return jnp.dot(a, b, preferred_element_type=F32, **kw)


def _dot_nt(a, b, **kw):
    return lax.dot_general(a, b, NT_DIMS, preferred_element_type=F32, **kw)


def _dot_tn(a, b, **kw):
    return lax.dot_general(a, b, TN_DIMS, preferred_element_type=F32, **kw)


def _sigmoid(x):
    return 1.0 / (1.0 + jnp.exp(-x))


def _silu(x):
    return x * _sigmoid(x)


def _rms(x, w):
    return x * lax.rsqrt(jnp.mean(x * x, axis=-1, keepdims=True) + RMS_EPS) * w


def _full(shape):
    return pl.BlockSpec(shape, lambda *_: (0,) * len(shape))


def _adaln_kernel(c_ref, w_ref, b_ref, o_ref):
    o_ref[...] = _dot(_silu(c_ref[...]), w_ref[...], precision=HIGHEST) + b_ref[...]


def _adaln(c, ada_w, ada_b):
    B, D = c.shape
    n = ada_w.shape[1] // D
    return pl.pallas_call(
        _adaln_kernel,
        grid=(n,),
        in_specs=[_full((B, D)), pl.BlockSpec((D, D), lambda j: (0, j)), pl.BlockSpec((1, D), lambda j: (0, j))],
        out_specs=pl.BlockSpec((B, D), lambda j: (0, j)),
        out_shape=jax.ShapeDtypeStruct((B, ada_w.shape[1]), F32),
        compiler_params=_params(1),
        name="adaln",
    )(c, ada_w, ada_b[None, :])


_C_M = 0
_C_Q = 2 * M_WIDTH
_C_KC = _C_Q + N_WIDTH
_C_KS = _C_KC + KV_WIDTH
_C_KW = _C_KS + KV_WIDTH
_C_VC = _C_KW + KV_WIDTH
_C_END = _C_VC + KV_WIDTH
_R_VS = 0
_R_VW = KV_WIDTH
_R_NG = 2 * KV_WIDTH
_R_MV = _R_NG + 4 * SUBLANE
_R_MO = _R_MV + M_WIDTH
_R_GH = _R_MO + M_WIDTH
_R_GL = _R_GH + 2 * M_HEADS
_R_END = _R_GL + 2 * M_HEADS


def _inproj_kernel(x_ref, mod_ref, nw_ref, w_ref, wt_ref, wg_ref, gb_ref, pos_ref, inv_ref,
                   zm_ref, q_ref, kc_ref, ks_ref, kw_ref, vc_ref, vst_ref, vwt_ref, ngt_ref, mvt_ref, mot_ref, gt_ref):
    mod = mod_ref[0]
    h = _rms(x_ref[...], nw_ref[...]) * (1.0 + mod[1:2]) + mod[0:1]
    hb = h.astype(BF16)
    z = _dot(hb, w_ref[...])
    zm_ref[...] = z[:, _C_M:_C_Q]
    zt = _dot_nt(wt_ref[...], hb)
    h_lo = (h - hb.astype(F32)).astype(BF16)
    gt_ref[...] = (zt[_R_GH:_R_GL] + zt[_R_GL:_R_END] + _dot_nt(wg_ref[...], h_lo)) + gb_ref[...]
    vst_ref[...] = zt[_R_VS:_R_VW].astype(BF16)
    vwt_ref[...] = zt[_R_VW:_R_NG].astype(BF16)
    ngt_ref[...] = _sigmoid(zt[_R_NG:_R_MV])
    mvt_ref[...] = zt[_R_MV:_R_MO].astype(BF16)
    mot_ref[...] = zt[_R_MO:_R_GH]

    ang = pos_ref[...] * inv_ref[...]
    cos = jnp.cos(ang)
    sin = jnp.sin(ang)
    lane = lax.broadcasted_iota(jnp.int32, (1, LANE), 1)
    first = (lane % N_HEAD_DIM) < (N_HEAD_DIM // 2)
    sin_signed = jnp.where(first, -sin, sin)

    def rope(slab):
        rot = jnp.where(first, pltpu.roll(slab, LANE - N_HEAD_DIM // 2, 1), pltpu.roll(slab, N_HEAD_DIM // 2, 1))
        return slab * cos + rot * sin_signed

    scale = N_HEAD_DIM ** -0.5 * LOG2_E
    for j in range(N_WIDTH // LANE):
        q_ref[:, j * LANE:(j + 1) * LANE] = (rope(z[:, _C_Q + j * LANE:_C_Q + (j + 1) * LANE]) * scale).astype(BF16)
    kc_ref[...] = rope(z[:, _C_KC:_C_KS])
    ks_ref[...] = rope(z[:, _C_KS:_C_KW]).astype(BF16)
    kw_ref[...] = rope(z[:, _C_KW:_C_VC]).astype(BF16)
    vc_ref[...] = z[:, _C_VC:_C_END]


def _inproj(x2, mod3, norm_w, w_main, w_t, wg_t, gate_b, pos_col, inv_row, S):
    T, D = x2.shape
    TM = min(TOKEN_TILE, S)
    tiles_per_seq = S // TM
    row = lambda w: pl.BlockSpec((TM, w), lambda i: (i, 0))
    col = lambda r: pl.BlockSpec((r, TM), lambda i: (0, i))
    outs = [(2 * M_WIDTH, F32), (N_WIDTH, BF16), (KV_WIDTH, F32), (KV_WIDTH, BF16), (KV_WIDTH, BF16), (KV_WIDTH, F32)]
    outs_t = [(KV_WIDTH, BF16), (KV_WIDTH, BF16), (_R_MV - _R_NG, F32), (M_WIDTH, BF16), (M_WIDTH, F32),
              (2 * M_HEADS, F32)]
    return pl.pallas_call(
        _inproj_kernel,
        grid=(T // TM,),
        in_specs=[row(D), pl.BlockSpec((1, 6, D), lambda i: (i // tiles_per_seq, 0, 0)), _full((1, D)),
                  _full(w_main.shape), _full(w_t.shape), _full(wg_t.shape), _full((2 * M_HEADS, 1)), row(1),
                  _full((1, LANE))],
        out_specs=[row(w) for w, _ in outs] + [col(r) for r, _ in outs_t],
        out_shape=[jax.ShapeDtypeStruct((T, w), dt) for w, dt in outs]
        + [jax.ShapeDtypeStruct((r, T), dt) for r, dt in outs_t],
        compiler_params=_params(1),
        name="inproj",
    )(x2, mod3, norm_w[None, :], w_main, w_t, wg_t, gate_b[:, None], pos_col, inv_row)


def _mlstm_kernel(zm_ref, vt_ref, ot_ref, gt_ref, cw_ref, hn_ref, ltri_ref, eye_ref, y_ref, buf, c_s, n_s, m_s):
    L = zm_ref.shape[0]
    QK = 2 * M_WIDTH
    DH = M_HEAD_DIM

    @pl.when(pl.program_id(1) == 0)
    def _():
        buf[0:SUBLANE, :] = jnp.zeros((SUBLANE, QK), F32)
        c_s[...] = jnp.zeros_like(c_s)
        n_s[...] = jnp.zeros_like(n_s)
        m_s[...] = jnp.zeros_like(m_s)

    buf[SUBLANE:SUBLANE + L, :] = zm_ref[:, 0:QK]
    cw = cw_ref[...]
    conv = cw[M_CONV - 1:M_CONV] * buf[SUBLANE:SUBLANE + L, :]
    for j in range(M_CONV - 1):
        off = SUBLANE - (M_CONV - 1) + j
        conv = conv + cw[j:j + 1] * buf[off:off + L, :]
    buf[0:SUBLANE, :] = buf[L:L + SUBLANE, :]
    qk = _silu(conv)

    g = gt_ref[...]
    fp = g[M_HEADS:]
    lf = jnp.minimum(fp, 0.0) - jnp.log(1.0 + jnp.exp(-jnp.abs(fp)))
    lf8 = jnp.concatenate([lf, jnp.zeros_like(lf)], axis=0)
    ltri = ltri_ref[...]
    b_rows = _dot_nt(lf8, ltri, precision=HIGHEST)
    b_cols = _dot_nt(ltri, lf8, precision=HIGHEST)
    i_cols = _dot_nt(eye_ref[...], g, precision=HIGHEST)

    causal = lax.broadcasted_iota(jnp.int32, (L, L), 0) <= lax.broadcasted_iota(jnp.int32, (L, L), 1)
    ones_rows = jnp.ones((2 * SUBLANE, L), F32)
    outs = []
    for h in range(M_HEADS):
        hs = slice(h * DH, (h + 1) * DH)
        b_r, i_r = b_rows[h:h + 1], g[h:h + 1]
        src_c = b_cols[:, h:h + 1] - i_cols[:, h:h + 1]
        m_prev = m_s[h:h + 1, 0:1]
        g_tot = b_r[:, L - 1:L]
        a_r = g_tot - b_r + i_r
        m_new = jnp.maximum(g_tot + m_prev, jnp.max(a_r, axis=1, keepdims=True))

        qb = qk[:, hs].astype(BF16)
        kb = (qk[:, M_WIDTH + h * DH:M_WIDTH + (h + 1) * DH] * (DH ** -0.5)).astype(BF16)
        vt = vt_ref[hs, :]

        dlog = jnp.where(causal, b_r - src_c, -jnp.inf)
        inter = b_r + m_prev
        m_t = jnp.maximum(inter, jnp.max(dlog, axis=0, keepdims=True))
        wts = jnp.exp(dlog - m_t) * _dot_nt(kb, qb)
        dec = jnp.exp(inter - m_t)
        ct_prev = c_s[h]
        n_prev = n_s[...]
        num = _dot(vt, wts.astype(BF16)) + dec * _dot_nt(ct_prev.astype(BF16), qb)
        qn = _dot_nt(n_prev.astype(BF16), qb)[h:h + 1]
        den = jnp.sum(wts, axis=0, keepdims=True) + dec * qn
        hh = num * (1.0 / jnp.maximum(jnp.abs(den), jnp.exp(-m_t)))

        w_r = jnp.exp(a_r - m_new)
        keep = jnp.exp(g_tot + m_prev - m_new)
        lhs = jnp.concatenate([vt.astype(F32), ones_rows], axis=0) * w_r
        upd = _dot(lhs.astype(BF16), kb)
        c_s[h] = keep * ct_prev + upd[:DH]
        n_s[h:h + 1] = keep * n_prev[h:h + 1] + upd[DH:DH + 1]
        m_s[h:h + 1] = jnp.broadcast_to(m_new, (1, LANE))

        hn = hh * lax.rsqrt(jnp.mean(hh * hh, axis=0, keepdims=True) + RMS_EPS)
        hn = hn * jnp.concatenate([hn_ref[hs, :]] * (L // LANE), axis=1)
        outs.append(_sigmoid(ot_ref[hs, :]) * hn)
    y_ref[...] = jnp.concatenate(outs, axis=0).T.astype(BF16)


def _mlstm(zqk, v_t, o_t, gt, conv_w, head_norm, B, S):
    T = zqk.shape[0]
    L = min(MLSTM_CHUNK, S)
    assert L % LANE == 0
    nc = S // L
    ltri = jnp.asarray(np.tril(np.ones((L, L), np.float32)))
    eye = jnp.asarray(np.eye(L, dtype=np.float32))
    col = lambda r: pl.BlockSpec((r, L), lambda b, c: (0, b * nc + c))
    return pl.pallas_call(
        _mlstm_kernel,
        grid=(B, nc),
        in_specs=[pl.BlockSpec((L, 2 * M_WIDTH), lambda b, c: (b * nc + c, 0)), col(M_WIDTH), col(M_WIDTH),
                  col(2 * M_HEADS), _full((M_CONV, 2 * M_WIDTH)), _full((M_WIDTH, LANE)), _full((L, L)), _full((L, L))],
        out_specs=pl.BlockSpec((L, M_WIDTH), lambda b, c: (b * nc + c, 0)),
        out_shape=jax.ShapeDtypeStruct((T, M_WIDTH), BF16),
        scratch_shapes=[pltpu.VMEM((L + SUBLANE, 2 * M_WIDTH), F32),
                        pltpu.VMEM((M_HEADS, M_HEAD_DIM, M_HEAD_DIM), F32),
                        pltpu.VMEM((SUBLANE, M_HEAD_DIM), F32),
                        pltpu.VMEM((SUBLANE, LANE), F32)],
        compiler_params=_params(2),
        name="mlstm",
    )(zqk, v_t, o_t, gt, conv_w, jnp.broadcast_to(head_norm[:, None], (M_WIDTH, LANE)), ltri, eye)


def _compress_kernel(k_ref, v_ref, pak, pbk, wak, wbk, w2k, pav, pbv, wav, wbv, w2v, ko_ref, vo_ref):
    def one(x_ref, pa, pb, wa, wb, w2, o_ref, transposed):
        n = x_ref.shape[0] // CMP_STRIDE
        x = jnp.concatenate([x_ref[pl.ds(t, n, stride=CMP_STRIDE), :] for t in range(CMP_STRIDE)], axis=1)
        first = _dot((x + pa[...]).astype(BF16), wa[...])
        second = _dot((x + pb[...]).astype(BF16), wb[...])
        hid = _silu(first + pltpu.roll(second, n - 1, 0)).astype(BF16)
        o_ref[0] = (_dot_nt(w2[...], hid) if transposed else _dot(hid, w2[...])).astype(BF16)

    one(k_ref, pak, pbk, wak, wbk, w2k, ko_ref, False)
    one(v_ref, pav, pbv, wav, wbv, w2v, vo_ref, True)


def _compress_weights(pe, w1, w2):
    half = CMP_BLOCK // 2
    eye = jnp.eye(N_KV_GROUPS, dtype=F32)
    w1r = w1.reshape(CMP_BLOCK, N_HEAD_DIM, CMP_HIDDEN)
    big = lambda w: jnp.einsum("ldc,gh->lgdhc", w, eye).reshape(half * KV_WIDTH, N_KV_GROUPS * CMP_HIDDEN).astype(BF16)
    pe_row = lambda p: jnp.broadcast_to(p[:, None, :], (half, N_KV_GROUPS, N_HEAD_DIM)).reshape(1, half * KV_WIDTH)
    w2bd = jnp.einsum("cd,gh->gchd", w2, eye).reshape(N_KV_GROUPS * CMP_HIDDEN, KV_WIDTH).astype(BF16)
    return pe_row(pe[:half]), pe_row(pe[half:]), big(w1r[:half]), big(w1r[half:]), w2bd


def _compress(kc, vc, wk, wv, B, S):
    nc = S // CMP_STRIDE
    blk = pl.BlockSpec((S, KV_WIDTH), lambda b: (b, 0))
    wv = wv[:4] + (wv[4].T,)
    return pl.pallas_call(
        _compress_kernel,
        grid=(B,),
        in_specs=[blk, blk] + [_full(w.shape) for w in wk] + [_full(w.shape) for w in wv],
        out_specs=[pl.BlockSpec((1, nc, KV_WIDTH), lambda b: (b, 0, 0)),
                   pl.BlockSpec((1, KV_WIDTH, nc), lambda b: (b, 0, 0))],
        out_shape=[jax.ShapeDtypeStruct((B, nc, KV_WIDTH), BF16), jax.ShapeDtypeStruct((B, KV_WIDTH, nc), BF16)],
        compiler_params=_params(1),
        name="compress",
    )(kc, vc, *wk, *wv)


def _masked_softmax_keys(s, mask):
    s = jnp.where(mask, s, -jnp.inf)
    m = jnp.max(s, axis=0, keepdims=True)
    m = jnp.where(jnp.isfinite(m), m, 0.0)
    e = jnp.exp2(s - m)
    return e, 1.0 / jnp.maximum(jnp.sum(e, axis=0, keepdims=True), 1e-30)


def _nsa_kernel(q_ref, kc_ref, vct_ref, ks_ref, vst_ref, kw_ref, vwt_ref, ngt_ref, ov_ref, ek_ref, eye_ref,
                y_ref, m_s, l_s, acc_s, *, n_top, TK):
    TQ = q_ref.shape[0]
    NS = ov_ref.shape[0]
    G, HPG, DH = N_KV_GROUPS, HEADS_PER_GROUP, N_HEAD_DIM
    R = G * HPG * TQ
    q0 = pl.program_id(1) * TQ

    q = q_ref[...]
    zero = jnp.zeros((TQ, DH), BF16)
    parts = []
    for hd in range(N_HEADS):
        qh = q[:, hd * DH:(hd + 1) * DH]
        parts.append(jnp.concatenate([qh, zero] if hd < HPG else [zero, qh], axis=1))
    qp = jnp.concatenate(parts, axis=0)
    t_q = q0 + lax.broadcasted_iota(jnp.int32, (1, R), 1) % TQ

    kc = kc_ref[0]
    NC = kc.shape[0]
    cmp_end = lax.broadcasted_iota(jnp.int32, (NC, 1), 0) * CMP_STRIDE + (CMP_BLOCK - 1)
    e_c, inv_c = _masked_softmax_keys(_dot_nt(kc, qp), cmp_end <= t_q)
    p_c = e_c * inv_c
    o_c = _dot(vct_ref[0], p_c.astype(BF16))
    p_grp = jnp.concatenate(
        [sum(p_c[:, (g * HPG + h) * TQ:(g * HPG + h + 1) * TQ] for h in range(HPG)) for g in range(G)], axis=1)
    imp = _dot(ov_ref[...], p_grp, precision=HIGHEST)

    j_io = lax.broadcasted_iota(jnp.int32, (NS, G * TQ), 0)
    t_row = q0 + lax.broadcasted_iota(jnp.int32, (NS, G * TQ), 1) % TQ
    cur = t_row // SLC_BLOCK
    forced = (j_io == 0) | (j_io == cur) | (j_io == cur - 1)
    future = j_io * SLC_BLOCK > t_row
    score = jnp.where(forced, jnp.inf, jnp.where(future, -jnp.inf, imp))
    rank = jnp.zeros((NS, G * TQ), F32)
    for jp in range(NS):
        other = score[jp:jp + 1, :]
        rank = rank + jnp.where(j_io > jp, jnp.where(other >= score, 1.0, 0.0), jnp.where(other > score, 1.0, 0.0))
    sel_bias = jnp.where(future, NEG, jnp.where(rank < n_top, 0.0, NEG)).astype(BF16)
    sel_rows = _dot_tn(sel_bias, eye_ref[...]).astype(BF16)
    sel_rows = jnp.broadcast_to(sel_rows.reshape(G, 1, TQ, LANE), (G, HPG, TQ, LANE)).reshape(R, LANE)
    qa = jnp.concatenate([qp, sel_rows], axis=1)

    m_s[...] = jnp.full_like(m_s, NEG)
    l_s[...] = jnp.zeros_like(l_s)
    acc_s[...] = jnp.zeros_like(acc_s)

    CB = NSA_COL_BLOCK

    def step(kt, causal):
        k0 = pl.multiple_of(kt * TK, TK)
        ka = jnp.concatenate([ks_ref[0, pl.ds(k0, TK), :], ek_ref[pl.ds(k0, TK), :]], axis=1)
        vt = vst_ref[:, pl.ds(k0, TK)]
        kpos = k0 + lax.broadcasted_iota(jnp.int32, (TK, 1), 0)
        scores = [_dot_nt(ka, qa[j * CB:(j + 1) * CB]) for j in range(R // CB)]
        probs, alphas = [], []
        for j in range(R // CB):
            cols = slice(j * CB, (j + 1) * CB)
            s = scores[j]
            if causal:
                s = jnp.where(kpos <= t_q[:, cols], s, NEG)
            m_old = m_s[:, cols]
            m_new = jnp.maximum(m_old, jnp.max(s, axis=0, keepdims=True))
            alpha = jnp.exp2(m_old - m_new)
            p = jnp.exp2(s - m_new)
            l_s[:, cols] = alpha * l_s[:, cols] + jnp.sum(p, axis=0, keepdims=True)
            m_s[:, cols] = m_new
            probs.append(p.astype(BF16))
            alphas.append(alpha)
        for j in range(R // CB):
            cols = slice(j * CB, (j + 1) * CB)
            acc_s[:, cols] = alphas[j] * acc_s[:, cols] + _dot(vt, probs[j])

    def full_tile(kt, carry):
        step(kt, False)
        return carry

    last = q0 // TK
    lax.fori_loop(0, last, full_tile, 0)
    step(last, True)
    o_s = acc_s[...] * (1.0 / l_s[...])

    WK = WINDOW + TQ
    start = pl.multiple_of(jnp.maximum(q0 - WINDOW, 0), TQ)
    wpos = start + lax.broadcasted_iota(jnp.int32, (WK, 1), 0)
    e_w, inv_w = _masked_softmax_keys(_dot_nt(kw_ref[0, pl.ds(start, WK), :], qp),
                                      (t_q - wpos).astype(jnp.uint32) < WINDOW)
    o_w =_dot(vwt_ref[:, pl.ds(start, WK)], e_w.astype(BF16)) * inv_w

    gates = ngt_ref[...]
    outs = []
    for hd in range(N_HEADS):
        cols = slice(hd * TQ, (hd + 1) * TQ)
        rows = slice((hd // HPG) * DH, (hd // HPG + 1) * DH)
        outs.append(gates[3 * hd:3 * hd + 1] * o_c[rows, cols]
                    + gates[3 * hd + 1:3 * hd + 2] * o_s[rows, cols]
                    + gates[3 * hd + 2:3 * hd + 3] * o_w[rows, cols])
    y_ref[...] = jnp.concatenate(outs, axis=0).T.astype(BF16)


def _nsa(q, kcmp, vcmp_t, ks, vs_t, kw, vw_t, ng_t, B, S):
    T = q.shape[0]
    TQ = min(NSA_Q_TILE, S)
    TK = min(NSA_K_TILE, S)
    nq = S // TQ
    NS = S // SLC_BLOCK
    NC = S // CMP_STRIDE
    n_cmp = (S - CMP_BLOCK) // CMP_STRIDE + 1
    js = np.arange(NS)[:, None] * SLC_BLOCK
    cs = np.arange(NC)[None, :] * CMP_STRIDE
    ov = np.clip(np.minimum(js + SLC_BLOCK, cs + CMP_BLOCK) - np.maximum(js, cs), 0, None) / CMP_STRIDE
    ov[:, n_cmp:] = 0.0
    assert NS <= LANE and TK % TQ == 0
    block_of_key = (np.arange(S)[:, None] // SLC_BLOCK == np.arange(LANE)[None, :]).astype(np.float32)
    eye_pad = np.eye(NS, LANE, dtype=np.float32)
    R = N_HEADS * TQ
    assert R % NSA_COL_BLOCK == 0
    seq = lambda: pl.BlockSpec((1, S, KV_WIDTH), lambda b, i: (b, 0, 0))
    seq_t = lambda: pl.BlockSpec((KV_WIDTH, S), lambda b, i: (0, b))
    return pl.pallas_call(
        functools.partial(_nsa_kernel, n_top=min(SLC_TOPK, NS), TK=TK),
        grid=(B, nq),
        in_specs=[pl.BlockSpec((TQ, N_WIDTH), lambda b, i: (b * nq + i, 0)),
                  pl.BlockSpec((1, NC, KV_WIDTH), lambda b, i: (b, 0, 0)),
                  pl.BlockSpec((1, KV_WIDTH, NC), lambda b, i: (b, 0, 0)),
                  seq(), seq_t(), seq(), seq_t(), pl.BlockSpec((ng_t.shape[0], TQ), lambda b, i: (0, b * nq + i)),
                  _full((NS, NC)), _full((S, LANE)), _full((NS, LANE))],
        out_specs=pl.BlockSpec((TQ, N_WIDTH), lambda b, i: (b * nq + i, 0)),
        out_shape=jax.ShapeDtypeStruct((T, N_WIDTH), BF16),
        scratch_shapes=[pltpu.VMEM((1, R), F32), pltpu.VMEM((1, R), F32), pltpu.VMEM((KV_WIDTH, R), F32)],
        compiler_params=_params(2),
        name="nsa",
    )(q, kcmp, vcmp_t, ks.reshape(B, S, KV_WIDTH), vs_t, kw.reshape(B, S, KV_WIDTH), vw_t, ng_t,
      jnp.asarray(ov.astype(np.float32)), jnp.asarray(block_of_key, dtype=BF16), jnp.asarray(eye_pad, dtype=BF16))


def _post_kernel(ym_ref, yn_ref, wm_ref, wn_ref, x_ref, mod_ref, npost_ref, npre_ref, wrh_ref, wrl_ref, br_ref,
                 ltri_ref, x1_ref, h2_ref, route_ref, cnt_ref, carry):
    TM = x_ref.shape[0]

    @pl.when(pl.program_id(0) == 0)
    def _():
        carry[...] = jnp.zeros_like(carry)

    mod = mod_ref[0]
    y = _dot(ym_ref[...], wm_ref[...]) + _dot(yn_ref[...], wn_ref[...])
    x1 = x_ref[...] + mod[2:3] * _rms(y, npost_ref[...])
    x1_ref[...] = x1
    h2 = _rms(x1, npre_ref[...]) * (1.0 + mod[4:5]) + mod[3:4]
    for s in range(SUBLANE):
        h2_ref[pl.ds(s, TM, stride=SUBLANE), :] = h2[:, s * LANE:(s + 1) * LANE]

    h_hi = h2.astype(BF16)
    h_lo = (h2 - h_hi.astype(F32)).astype(BF16)
    logits = _dot(h_hi, wrh_ref[...]) + _dot(h_lo, wrh_ref[...]) + _dot(h_hi, wrl_ref[...]) + br_ref[...]
    lane = lax.broadcasted_iota(jnp.int32, (TM, LANE), 1)
    is_grp = lane < MOE_GROUPS
    lg = jnp.where(is_grp, logits, -jnp.inf)
    eg = jnp.exp(lg - jnp.max(lg, axis=1, keepdims=True))
    pg = eg / jnp.sum(eg, axis=1, keepdims=True)
    pg_top = jnp.max(pg, axis=1, keepdims=True)
    grp = jnp.min(jnp.where(is_grp & (pg == pg_top), lane, LANE), axis=1, keepdims=True)
    lo = MOE_GROUPS + EXPERTS_PER_GROUP * grp
    in_grp = (lane >= lo) & (lane < lo + EXPERTS_PER_GROUP)
    le = jnp.where(in_grp, logits, -jnp.inf)
    ee = jnp.exp(le - jnp.max(le, axis=1, keepdims=True))
    pe = jnp.where(in_grp, ee / jnp.sum(ee, axis=1, keepdims=True), -1.0)
    p1 = jnp.max(pe, axis=1, keepdims=True)
    i1 = jnp.min(jnp.where(pe == p1, lane, LANE), axis=1, keepdims=True)
    pe2 = jnp.where(lane == i1, -1.0, pe)
    p2 = jnp.max(pe2, axis=1, keepdims=True)
    i2 = jnp.min(jnp.where((pe2 == p2) & in_grp & (lane != i1), lane, LANE), axis=1, keepdims=True)
    den = p1 + p2
    e1 = i1 - MOE_GROUPS
    e2 = i2 - MOE_GROUPS
    oh1, oh2 = lane == e1, lane == e2
    oh = jnp.where(oh1, 1.0, 0.0) + jnp.where(oh2, 1.0, 0.0)
    before = _dot(ltri_ref[...], oh.astype(BF16)) + carry[0:1, :]
    r1 = jnp.sum(jnp.where(oh1, before, 0.0), axis=1, keepdims=True)
    r2 = jnp.sum(jnp.where(oh2, before, 0.0), axis=1, keepdims=True)
    carry[...] = carry[...] + jnp.sum(oh, axis=0, keepdims=True)
    cnt_ref[...] = carry[...]
    cols = (e1.astype(F32), e2.astype(F32), pg_top * p1 / den, pg_top * p2 / den, r1, r2)
    route = jnp.zeros((TM, LANE), F32)
    for k, col in enumerate(cols):
        route = jnp.where(lane == k, col, route)
    route_ref[...] = route


def _post(ym, yn, w_out, x2, mod3, norm_post, norm_pre, wr, br, S):
    T, D = x2.shape
    TM = min(TOKEN_TILE, S)
    tiles_per_seq = S // TM
    row = lambda w: pl.BlockSpec((TM, w), lambda i: (i, 0))
    small = pl.BlockSpec((SUBLANE, LANE), lambda i: (0, 0))
    ltri = jnp.asarray(np.tril(np.ones((TM, TM), np.float32), -1), dtype=BF16)
    return pl.pallas_call(
        _post_kernel,
        grid=(T // TM,),
        in_specs=[row(M_WIDTH), row(N_WIDTH), _full((M_WIDTH, D)), _full((N_WIDTH, D)), row(D),
                  pl.BlockSpec((1, 6, D), lambda i: (i // tiles_per_seq, 0, 0)), _full((1, D)), _full((1, D)),
                  _full((D, LANE)), _full((D, LANE)), _full((1, LANE)), _full((TM, TM))],
        out_specs=[row(D), pl.BlockSpec((TM * SUBLANE, LANE), lambda i: (i, 0)), row(LANE), small],
        out_shape=[jax.ShapeDtypeStruct((T, D), F32), jax.ShapeDtypeStruct((T * SUBLANE, LANE), F32),
                   jax.ShapeDtypeStruct((T, LANE), F32), jax.ShapeDtypeStruct((SUBLANE, LANE), F32)],
        scratch_shapes=[pltpu.VMEM((SUBLANE, LANE), F32)],
        compiler_params=_params(1),
        name="post_mix_router",
    )(ym, yn, w_out[:M_WIDTH].astype(BF16), w_out[M_WIDTH:].astype(BF16), x2, mod3,
      norm_post[None, :], norm_pre[None, :], wr.astype(BF16), (wr - wr.astype(BF16).astype(F32)).astype(BF16), br,
      ltri)


def _plan_kernel(cnt_ref, pstart_ref, blk_ref, tot_ref, *, rows_per_block):
    lane = lax.broadcasted_iota(jnp.int32, (SUBLANE, LANE), 1)
    nblk = (cnt_ref[...].astype(jnp.int32) + (rows_per_block - 1)) // rows_per_block
    end = nblk
    sh = 1
    while sh < 2 * N_EXPERTS:
        end = end + jnp.where(lane >= sh, pltpu.roll(end, sh, 1), 0)
        sh *= 2
    pstart_ref[...] = (end - nblk) * rows_per_block
    nbp = blk_ref.shape[0]
    blk_io = lax.broadcasted_iota(jnp.int32, (nbp, LANE), 0)
    lane_b = lax.broadcasted_iota(jnp.int32, (nbp, LANE), 1)
    passed = jnp.where((lane_b < N_EXPERTS) & (blk_io >= end[0:1, :]), 1, 0)
    blk_ref[...] = jnp.broadcast_to(jnp.minimum(jnp.sum(passed, axis=1, keepdims=True), N_EXPERTS - 1), (nbp, LANE))
    tot_ref[...] = jnp.broadcast_to(
        jnp.sum(jnp.where(lane == N_EXPERTS - 1, end, 0), axis=1, keepdims=True), (SUBLANE, LANE))


def _sort_plan(cnt, n_blocks):
    small = pl.BlockSpec((SUBLANE, LANE), lambda i: (0, 0))
    nbp = -(-n_blocks // SUBLANE) * SUBLANE
    pstart, blk, tot = pl.pallas_call(
        functools.partial(_plan_kernel, rows_per_block=EXPERT_ROWS),
        grid=(1,),
        in_specs=[small],
        out_specs=[small, _full((nbp, LANE)), small],
        out_shape=[jax.ShapeDtypeStruct((SUBLANE, LANE), jnp.int32), jax.ShapeDtypeStruct((nbp, LANE), jnp.int32),
                   jax.ShapeDtypeStruct((SUBLANE, LANE), jnp.int32)],
        compiler_params=_params(1),
        name="expert_plan",
    )(cnt)
    return pstart[0, :N_EXPERTS + 1], blk[:n_blocks, 0], tot[0, :1]


_ASG = 4


def _dest_row(pstart_ref, asg_ref, j, k):
    return pstart_ref[asg_ref[0, 0, _ASG * j + k]] + asg_ref[0, 0, _ASG * j + 2 + k]


def _row_copy(src, src_row, dst, dst_row, sem):
    return pltpu.make_async_copy(src.at[pl.ds(pl.multiple_of(src_row * SUBLANE, SUBLANE), SUBLANE)],
                                 dst.at[pl.ds(pl.multiple_of(dst_row * SUBLANE, SUBLANE), SUBLANE)], sem)


def _dispatch_kernel(pstart_ref, asg_ref, h2_ref, xs_hbm, zeros, sem, zsem, *, n_blocks):
    n = asg_ref.shape[2] // _ASG
    RB = zeros.shape[0] // SUBLANE

    @pl.when(pl.program_id(0) == 0)
    def _():
        zeros[...] = jnp.zeros_like(zeros)

        def fill(row0):
            return pltpu.make_async_copy(
                zeros, xs_hbm.at[pl.ds(pl.multiple_of(row0 * SUBLANE, SUBLANE), RB * SUBLANE)], zsem)

        def experts(act):
            def body(e, carry):
                @pl.when(pstart_ref[e + 1] > pstart_ref[e])
                def _():
                    act(fill(pstart_ref[e + 1] - RB))
                return carry
            lax.fori_loop(0, N_EXPERTS, body, 0)

        def trailing(act):
            def body(b, carry):
                act(fill(b * RB))
                return carry
            lax.fori_loop(pstart_ref[N_EXPERTS] // RB, n_blocks, body, 0)

        experts(lambda c: c.start())
        trailing(lambda c: c.start())
        experts(lambda c: c.wait())
        trailing(lambda c: c.wait())

    def issue(jj, carry):
        for u in range(ISSUE_UNROLL):
            j = jj * ISSUE_UNROLL + u
            for k in range(2):
                _row_copy(h2_ref, j, xs_hbm, _dest_row(pstart_ref, asg_ref, j, k), sem).start(priority=k)
        return carry

    lax.fori_loop(0, n // ISSUE_UNROLL, issue, 0)
    whole = xs_hbm.at[pl.ds(0, 2 * n * SUBLANE)]
    pltpu.make_async_copy(whole, whole, sem).wait()


def _dispatch(pstart, asg3, h2t, n_blocks):
    nt, _, width = asg3.shape
    n = width // _ASG
    return pl.pallas_call(
        functools.partial(_dispatch_kernel, n_blocks=n_blocks),
        grid=(nt,),
        in_specs=[pl.BlockSpec(memory_space=pltpu.SMEM),
                  pl.BlockSpec((1, 1, width), lambda i: (i, 0, 0), memory_space=pltpu.SMEM),
                  pl.BlockSpec((n * SUBLANE, LANE), lambda i: (i, 0))],
        out_specs=pl.BlockSpec(memory_space=pl.ANY),
        out_shape=jax.ShapeDtypeStruct((n_blocks * EXPERT_ROWS * SUBLANE, LANE), F32),
        scratch_shapes=[pltpu.VMEM((EXPERT_ROWS * SUBLANE, LANE), F32), pltpu.SemaphoreType.DMA(()),
                        pltpu.SemaphoreType.DMA(())],
        compiler_params=_params(1),
        name="dispatch",
    )(pstart, asg3, h2t)


def _expert_kernel(be_ref, nb_ref, xs_ref, w1_ref, w3_ref, w2_ref, ys_ref, w1b, w3b, w2b):
    RB = xs_ref.shape[0] // SUBLANE
    i = pl.program_id(0)

    @pl.when(i < nb_ref[0])
    def _():
        @pl.when((i == 0) | (be_ref[i] != be_ref[jnp.maximum(i - 1, 0)]))
        def _():
            w1b[...] = w1_ref[0].astype(BF16)
            w3b[...] = w3_ref[0].astype(BF16)
            w2b[...] = w2_ref[0].astype(BF16)

        x = jnp.concatenate([xs_ref[pl.ds(s, RB, stride=SUBLANE), :] for s in range(SUBLANE)], axis=1).astype(BF16)
        hb = _silu(_dot(x, w1b[...])) * _dot(x, w3b[...])
        y = _dot(hb.astype(BF16), w2b[...])
        for s in range(SUBLANE):
            ys_ref[pl.ds(s, RB, stride=SUBLANE), :] = y[:, s * LANE:(s + 1) * LANE]

    @pl.when(i >= nb_ref[0])
    def _():
        ys_ref[...] = jnp.zeros_like(ys_ref)


def _experts(block_expert, n_used, xs, w1, w3, w2):
    n_blocks = block_expert.shape[0]
    RB = EXPERT_ROWS
    D, F = w1.shape[1], w1.shape[2]
    cur = lambda i, nb: jnp.minimum(i, nb[0] - 1)
    rows = pl.BlockSpec((RB * SUBLANE, LANE), lambda i, be, nb: (cur(i, nb), 0))
    wspec = lambda a, b: pl.BlockSpec((1, a, b), lambda i, be, nb: (be[cur(i, nb)], 0, 0))
    return pl.pallas_call(
        _expert_kernel,
        grid_spec=pltpu.PrefetchScalarGridSpec(
            num_scalar_prefetch=2, grid=(n_blocks,),
            in_specs=[rows, wspec(D, F), wspec(D, F), wspec(F, D)],
            out_specs=pl.BlockSpec((RB * SUBLANE, LANE), lambda i, be, nb: (i, 0)),
            scratch_shapes=[pltpu.VMEM((D, F), BF16), pltpu.VMEM((D, F), BF16), pltpu.VMEM((F, D), BF16)]),
        out_shape=jax.ShapeDtypeStruct(xs.shape, F32),
        compiler_params=_params(1),
        name="experts",
    )(block_expert, n_used, xs, w1, w3, w2)


def _combine_kernel(pstart_ref, asg_ref, asg_next_ref, ys_hbm, x1_ref, route_ref, mod_ref, nw_ref, o_ref, buf, sems):
    n = x1_ref.shape[0]
    i = pl.program_id(0)
    slot = i % 2

    def gather(dref, into):
        def issue(jj, carry):
            for u in range(ISSUE_UNROLL):
                j = jj * ISSUE_UNROLL + u
                for k in range(2):
                    _row_copy(ys_hbm, _dest_row(pstart_ref, dref, j, k), buf.at[into], k * n + j,
                              sems.at[into]).start(priority=k)
            return carry

        lax.fori_loop(0, n // ISSUE_UNROLL, issue, 0)

    @pl.when(i == 0)
    def _():
        gather(asg_ref, 0)

    @pl.when(i + 1 < pl.num_programs(0))
    def _():
        gather(asg_next_ref, 1 - slot)

    pltpu.make_async_copy(ys_hbm.at[pl.ds(0, 2 * n * SUBLANE)], buf.at[slot], sems.at[slot]).wait()

    route = route_ref[...]
    g0, g1 = route[:, 2:3], route[:, 3:4]
    y = jnp.concatenate(
        [g0 * buf[slot, pl.ds(s, n, stride=SUBLANE), :] + g1 * buf[slot, pl.ds(n * SUBLANE + s, n, stride=SUBLANE), :]
         for s in range(SUBLANE)], axis=1)
    o_ref[...] = x1_ref[...] + mod_ref[0][5:6] * _rms(y, nw_ref[...])


def _combine(pstart, asg3, ys, x1, route, mod3, norm_w, S):
    T, D = x1.shape
    nt, _, width = asg3.shape
    n = width // _ASG
    tiles_per_seq = S // n
    return pl.pallas_call(
        _combine_kernel,
        grid=(nt,),
        in_specs=[pl.BlockSpec(memory_space=pltpu.SMEM),
                  pl.BlockSpec((1, 1, width), lambda i: (i, 0, 0), memory_space=pltpu.SMEM),
                  pl.BlockSpec((1, 1, width), lambda i: (jnp.minimum(i + 1, nt - 1), 0, 0), memory_space=pltpu.SMEM),
                  pl.BlockSpec(memory_space=pl.ANY), pl.BlockSpec((n, D), lambda i: (i, 0)),
                  pl.BlockSpec((n, LANE), lambda i: (i, 0)),
                  pl.BlockSpec((1, 6, D), lambda i: (i // tiles_per_seq, 0, 0)), _full((1, D))],
        out_specs=pl.BlockSpec((n, D), lambda i: (i, 0)),
        out_shape=jax.ShapeDtypeStruct((T, D), F32),
        scratch_shapes=[pltpu.VMEM((2, 2 * n * SUBLANE, LANE), F32), pltpu.SemaphoreType.DMA((2,))],
        compiler_params=_params(1),
        name="combine",
    )(pstart, asg3, asg3, ys, x1, route, mod3, norm_w[None, :])


def _layer(x, c, positions, ada_w, ada_b, norm_mix_pre, norm_mix_post, norm_ffn_pre, norm_ffn_post, w_in,
           conv_w, gate_b, head_norm, cmp_k, cmp_v, w_out, wg, bg, we, be, w1, w3, w2):
    B, S, D = x.shape
    T = B * S
    x2 = x.reshape(T, D)
    mod3 = _adaln(c, ada_w, ada_b).reshape(B, 6, D)

    o_mi = 4 * M_WIDTH
    o_nq = o_mi + 2 * M_HEADS
    o_kv = o_nq + N_WIDTH
    kv = lambda i: w_in[:, o_kv + i * KV_WIDTH:o_kv + (i + 1) * KV_WIDTH]
    o_ng = o_kv + 6 * KV_WIDTH
    w_main = jnp.concatenate([w_in[:, :2 * M_WIDTH], w_in[:, o_nq:o_kv], kv(0), kv(2), kv(4), kv(1)],
                             axis=1).astype(BF16)
    w_t = jnp.concatenate([kv(3), kv(5), w_in[:, o_ng:], jnp.zeros((D, _R_MV - _R_NG - 3 * N_HEADS), F32),
                           w_in[:, 2 * M_WIDTH:o_mi]], axis=1).T.astype(BF16)
    w_gate = w_in[:, o_mi:o_nq].T
    wg_t = w_gate.astype(BF16)
    w_t = jnp.concatenate([w_t, wg_t, (w_gate - wg_t.astype(F32)).astype(BF16)], axis=0)
    half = N_HEAD_DIM // 2
    inv = ROPE_THETA ** (-jnp.arange(half, dtype=F32) / half)
    inv_row = jnp.tile(inv, LANE // half)[None, :]
    pos_col = positions.astype(F32).reshape(T, 1)

    zqk, q, kc, ks, kw, vc, vs_t, vw_t, ng_t, mv_t, mo_t, gt = _inproj(
        x2, mod3, norm_mix_pre, w_main, w_t, wg_t, gate_b, pos_col, inv_row, S)
    ym = _mlstm(zqk, mv_t, mo_t, gt, conv_w, head_norm, B, S)
    kcmp, vcmp_t = _compress(kc, vc, _compress_weights(*cmp_k), _compress_weights(*cmp_v), B, S)
    yn = _nsa(q, kcmp, vcmp_t, ks, vs_t, kw, vw_t, ng_t, B, S)

    wr = jnp.concatenate([wg, we, jnp.zeros((D, LANE - MOE_GROUPS - N_EXPERTS), F32)], axis=1)
    br = jnp.concatenate([bg, be, jnp.zeros((LANE - MOE_GROUPS - N_EXPERTS,), F32)])[None, :]
    x1, h2t, route, cnt = _post(ym, yn, w_out, x2, mod3, norm_mix_post, norm_ffn_pre, wr, br, S)

    n_blocks = (2 * T) // EXPERT_ROWS + N_EXPERTS
    pstart, block_expert, n_used = _sort_plan(cnt, n_blocks)
    n_tok = min(GATHER_TILE, S)
    asg = jnp.concatenate([route[:, 0:2], route[:, 4:6]], axis=1).astype(jnp.int32)
    asg3 = asg.reshape(T // n_tok, 1, _ASG * n_tok)
    xs = _dispatch(pstart, asg3, h2t, n_blocks)
    ys = _experts(block_expert, n_used, xs, w1, w3, w2)
    out = _combine(pstart, asg3, ys, x1, route, mod3, norm_ffn_post, S)
    return out.reshape(B, S, D)


def kernel(x, c, positions, ada_w, ada_b, norm_mix_pre, norm_mix_post, norm_ffn_pre, norm_ffn_post, w_in, mlstm_conv_w, mlstm_gate_b, mlstm_head_norm, cmp_pe_k, cmp_w1_k, cmp_w2_k, cmp_pe_v, cmp_w1_v, cmp_w2_v, w_out, router_grp_w, router_grp_b, router_exp_w, router_exp_b, expert_w1, expert_w3, expert_w2):
    for l in range(ada_w.shape[0]):
        x = _layer(x, c, positions, ada_w[l], ada_b[l], norm_mix_pre[l], norm_mix_post[l], norm_ffn_pre[l],
                   norm_ffn_post[l], w_in[l], mlstm_conv_w[l], mlstm_gate_b[l], mlstm_head_norm[l],
                   (cmp_pe_k[l], cmp_w1_k[l], cmp_w2_k[l]), (cmp_pe_v[l], cmp_w1_v[l], cmp_w2_v[l]), w_out[l],
                   router_grp_w[l], router_grp_b[l], router_exp_w[l], router_exp_b[l],
                   expert_w1[l], expert_w3[l], expert_w2[l])
    return x
```

```python
import functools

import jax
import jax.numpy as jnp
import numpy as np
from jax import lax
from jax.experimental import pallas as pl
from jax.experimental.pallas import tpu as pltpu

M_HEADS = 4
M_HEAD_DIM = 128
M_WIDTH = M_HEADS * M_HEAD_DIM
M_CONV = 4
N_HEADS = 8
N_KV_GROUPS = 2
HEADS_PER_GROUP = N_HEADS // N_KV_GROUPS
N_HEAD_DIM = 64
N_WIDTH = N_HEADS * N_HEAD_DIM
KV_WIDTH = N_KV_GROUPS * N_HEAD_DIM
CMP_BLOCK = 32
CMP_STRIDE = 16
CMP_HIDDEN = 2 * N_HEAD_DIM
SLC_BLOCK = 64
SLC_TOPK = 16
WINDOW = 512
ROPE_THETA = 10000.0
MOE_GROUPS = 4
EXPERTS_PER_GROUP = 8
N_EXPERTS = MOE_GROUPS * EXPERTS_PER_GROUP
RMS_EPS = 1e-6
LOG2_E = 1.4426950408889634

LANE = 128
SUBLANE = 8
VMEM_LIMIT_BYTES = 56 * 1024 * 1024

TOKEN_TILE = 512
MLSTM_CHUNK = 256
NSA_Q_TILE = 256
NSA_K_TILE = 512
NSA_COL_BLOCK = 256
EXPERT_ROWS = 512
GATHER_TILE = 256
ISSUE_UNROLL = 8

F32 = jnp.float32
BF16 = jnp.bfloat16
NEG = -1e30
HIGHEST = lax.Precision.HIGHEST
NT_DIMS = (((1,), (1,)), ((), ()))
TN_DIMS = (((0,), (0,)), ((), ()))


def _params(n_grid):
    return pltpu.CompilerParams(
        dimension_semantics=("arbitrary",) * n_grid, vmem_limit_bytes=VMEM_LIMIT_BYTES)


def _dot(a, b, **kw):
    return jnp.dot(a, b, preferred_element_type=F32, **kw)


def _dot_nt(a, b, **kw):
    return lax.dot_general(a, b, NT_DIMS, preferred_element_type=F32, **kw)


def _dot_tn(a, b, **kw):
    return lax.dot_general(a, b, TN_DIMS, preferred_element_type=F32, **kw)


def _sigmoid(x):
    return 1.0 / (1.0 + jnp.exp(-x))


def _silu(x):
    return x * _sigmoid(x)


def _rms(x, w):
    return x * lax.rsqrt(jnp.mean(x * x, axis=-1, keepdims=True) + RMS_EPS) * w


def _full(shape):
    return pl.BlockSpec(shape, lambda *_: (0,) * len(shape))


def _adaln_kernel(c_ref, w_ref, b_ref, o_ref):
    o_ref[...] = _dot(_silu(c_ref[...]), w_ref[...], precision=HIGHEST) + b_ref[...]


def _adaln(c, ada_w, ada_b):
    B, D = c.shape
    n = ada_w.shape[1] // D
    return pl.pallas_call(
        _adaln_kernel,
        grid=(n,),
        in_specs=[_full((B, D)), pl.BlockSpec((D, D), lambda j: (0, j)), pl.BlockSpec((1, D), lambda j: (0, j))],
        out_specs=pl.BlockSpec((B, D), lambda j: (0, j)),
        out_shape=jax.ShapeDtypeStruct((B, ada_w.shape[1]), F32),
        compiler_params=_params(1),
        name="adaln",
    )(c, ada_w, ada_b[None, :])


_C_M = 0
_C_Q = 2 * M_WIDTH
_C_KC = _C_Q + N_WIDTH
_C_KS = _C_KC + KV_WIDTH
_C_KW = _C_KS + KV_WIDTH
_C_VC = _C_KW + KV_WIDTH
_C_END = _C_VC + KV_WIDTH
_R_VS = 0
_R_VW = KV_WIDTH
_R_NG = 2 * KV_WIDTH
_R_MV = _R_NG + 4 * SUBLANE
_R_MO = _R_MV + M_WIDTH
_R_GH = _R_MO + M_WIDTH
_R_GL = _R_GH + 2 * M_HEADS
_R_END = _R_GL + 2 * M_HEADS


def _inproj_kernel(x_ref, mod_ref, nw_ref, w_ref, wt_ref, wg_ref, gb_ref, pos_ref, inv_ref,
                   zm_ref, q_ref, kc_ref, ks_ref, kw_ref, vc_ref, vst_ref, vwt_ref, ngt_ref, mvt_ref, mot_ref, gt_ref):
    mod = mod_ref[0]
    h = _rms(x_ref[...], nw_ref[...]) * (1.0 + mod[1:2]) + mod[0:1]
    hb = h.astype(BF16)
    z = _dot(hb, w_ref[...])
    zm_ref[...] = z[:, _C_M:_C_Q]
    zt = _dot_nt(wt_ref[...], hb)
    h_lo = (h - hb.astype(F32)).astype(BF16)
    gt_ref[...] = (zt[_R_GH:_R_GL] + zt[_R_GL:_R_END] + _dot_nt(wg_ref[...], h_lo)) + gb_ref[...]
    vst_ref[...] = zt[_R_VS:_R_VW].astype(BF16)
    vwt_ref[...] = zt[_R_VW:_R_NG].astype(BF16)
    ngt_ref[...] = _sigmoid(zt[_R_NG:_R_MV])
    mvt_ref[...] = zt[_R_MV:_R_MO].astype(BF16)
    mot_ref[...] = zt[_R_MO:_R_GH]

    ang = pos_ref[...] * inv_ref[...]
    cos = jnp.cos(ang)
    sin = jnp.sin(ang)
    lane = lax.broadcasted_iota(jnp.int32, (1, LANE), 1)
    first = (lane % N_HEAD_DIM) < (N_HEAD_DIM // 2)
    sin_signed = jnp.where(first, -sin, sin)

    def rope(slab):
        rot = jnp.where(first, pltpu.roll(slab, LANE - N_HEAD_DIM // 2, 1), pltpu.roll(slab, N_HEAD_DIM // 2, 1))
        return slab * cos + rot * sin_signed

    scale = N_HEAD_DIM ** -0.5 * LOG2_E
    for j in range(N_WIDTH // LANE):
        q_ref[:, j * LANE:(j + 1) * LANE] = (rope(z[:, _C_Q + j * LANE:_C_Q + (j + 1) * LANE]) * scale).astype(BF16)
    kc_ref[...] = rope(z[:, _C_KC:_C_KS])
    ks_ref[...] = rope(z[:, _C_KS:_C_KW]).astype(BF16)
    kw_ref[...] = rope(z[:, _C_KW:_C_VC]).astype(BF16)
    vc_ref[...] = z[:, _C_VC:_C_END]


def _inproj(x2, mod3, norm_w, w_main, w_t, wg_t, gate_b, pos_col, inv_row, S):
    T, D = x2.shape
    TM = min(TOKEN_TILE, S)
    tiles_per_seq = S // TM
    row = lambda w: pl.BlockSpec((TM, w), lambda i: (i, 0))
    col = lambda r: pl.BlockSpec((r, TM), lambda i: (0, i))
    outs = [(2 * M_WIDTH, F32), (N_WIDTH, BF16), (KV_WIDTH, F32), (KV_WIDTH, BF16), (KV_WIDTH, BF16), (KV_WIDTH, F32)]
    outs_t = [(KV_WIDTH, BF16), (KV_WIDTH, BF16), (_R_MV - _R_NG, F32), (M_WIDTH, BF16), (M_WIDTH, F32),
              (2 * M_HEADS, F32)]
    return pl.pallas_call(
        _inproj_kernel,
        grid=(T // TM,),
        in_specs=[row(D), pl.BlockSpec((1, 6, D), lambda i: (i // tiles_per_seq, 0, 0)), _full((1, D)),
                  _full(w_main.shape), _full(w_t.shape), _full(wg_t.shape), _full((2 * M_HEADS, 1)), row(1),
                  _full((1, LANE))],
        out_specs=[row(w) for w, _ in outs] + [col(r) for r, _ in outs_t],
        out_shape=[jax.ShapeDtypeStruct((T, w), dt) for w, dt in outs]
        + [jax.ShapeDtypeStruct((r, T), dt) for r, dt in outs_t],
        compiler_params=_params(1),
        name="inproj",
    )(x2, mod3, norm_w[None, :], w_main, w_t, wg_t, gate_b[:, None], pos_col, inv_row)


def _mlstm_kernel(zm_ref, vt_ref, ot_ref, gt_ref, cw_ref, hn_ref, ltri_ref, eye_ref, y_ref, buf, c_s, n_s, m_s):
    L = zm_ref.shape[0]
    QK = 2 * M_WIDTH
    DH = M_HEAD_DIM

    @pl.when(pl.program_id(1) == 0)
    def _():
        buf[0:SUBLANE, :] = jnp.zeros((SUBLANE, QK), F32)
        c_s[...] = jnp.zeros_like(c_s)
        n_s[...] = jnp.zeros_like(n_s)
        m_s[...] = jnp.zeros_like(m_s)

    buf[SUBLANE:SUBLANE + L, :] = zm_ref[:, 0:QK]
    cw = cw_ref[...]
    conv = cw[M_CONV - 1:M_CONV] * buf[SUBLANE:SUBLANE + L, :]
    for j in range(M_CONV - 1):
        off = SUBLANE - (M_CONV - 1) + j
        conv = conv + cw[j:j + 1] * buf[off:off + L, :]
    buf[0:SUBLANE, :] = buf[L:L + SUBLANE, :]
    qk = _silu(conv)

    g = gt_ref[...]
    fp = g[M_HEADS:]
    lf = jnp.minimum(fp, 0.0) - jnp.log(1.0 + jnp.exp(-jnp.abs(fp)))
    lf8 = jnp.concatenate([lf, jnp.zeros_like(lf)], axis=0)
    ltri = ltri_ref[...]
    b_rows = _dot_nt(lf8, ltri, precision=HIGHEST)
    b_cols = _dot_nt(ltri, lf8, precision=HIGHEST)
    i_cols = _dot_nt(eye_ref[...], g, precision=HIGHEST)

    causal = lax.broadcasted_iota(jnp.int32, (L, L), 0) <= lax.broadcasted_iota(jnp.int32, (L, L), 1)
    ones_rows = jnp.ones((2 * SUBLANE, L), F32)
    outs = []
    for h in range(M_HEADS):
        hs = slice(h * DH, (h + 1) * DH)
        b_r, i_r = b_rows[h:h + 1], g[h:h + 1]
        src_c = b_cols[:, h:h + 1] - i_cols[:, h:h + 1]
        m_prev = m_s[h:h + 1, 0:1]
        g_tot = b_r[:, L - 1:L]
        a_r = g_tot - b_r + i_r
        m_new = jnp.maximum(g_tot + m_prev, jnp.max(a_r, axis=1, keepdims=True))

        qb = qk[:, hs].astype(BF16)
        kb = (qk[:, M_WIDTH + h * DH:M_WIDTH + (h + 1) * DH] * (DH ** -0.5)).astype(BF16)
        vt = vt_ref[hs, :]

        dlog = jnp.where(causal, b_r - src_c, -jnp.inf)
        inter = b_r + m_prev
        m_t = jnp.maximum(inter, jnp.max(dlog, axis=0, keepdims=True))
        wts = jnp.exp(dlog - m_t) * _dot_nt(kb, qb)
        dec = jnp.exp(inter - m_t)
        ct_prev = c_s[h]
        n_prev = n_s[...]
        num = _dot(vt, wts.astype(BF16)) + dec * _dot_nt(ct_prev.astype(BF16), qb)
        qn = _dot_nt(n_prev.astype(BF16), qb)[h:h + 1]
        den = jnp.sum(wts, axis=0, keepdims=True) + dec * qn
        hh = num * (1.0 / jnp.maximum(jnp.abs(den), jnp.exp(-m_t)))

        w_r = jnp.exp(a_r - m_new)
        keep = jnp.exp(g_tot + m_prev - m_new)
        lhs = jnp.concatenate([vt.astype(F32), ones_rows], axis=0) * w_r
        upd = _dot(lhs.astype(BF16), kb)
        c_s[h] = keep * ct_prev + upd[:DH]
        n_s[h:h + 1] = keep * n_prev[h:h + 1] + upd[DH:DH + 1]
        m_s[h:h + 1] = jnp.broadcast_to(m_new, (1, LANE))

        hn = hh * lax.rsqrt(jnp.mean(hh * hh, axis=0, keepdims=True) + RMS_EPS)
        hn = hn * jnp.concatenate([hn_ref[hs, :]] * (L // LANE), axis=1)
        outs.append(_sigmoid(ot_ref[hs, :]) * hn)
    y_ref[...] = jnp.concatenate(outs, axis=0).T.astype(BF16)


def _mlstm(zqk, v_t, o_t, gt, conv_w, head_norm, B, S):
    T = zqk.shape[0]
    L = min(MLSTM_CHUNK, S)
    assert L % LANE == 0
    nc = S // L
    ltri = jnp.asarray(np.tril(np.ones((L, L), np.float32)))
    eye = jnp.asarray(np.eye(L, dtype=np.float32))
    col = lambda r: pl.BlockSpec((r, L), lambda b, c: (0, b * nc + c))
    return pl.pallas_call(
        _mlstm_kernel,
        grid=(B, nc),
        in_specs=[pl.BlockSpec((L, 2 * M_WIDTH), lambda b, c: (b * nc + c, 0)), col(M_WIDTH), col(M_WIDTH),
                  col(2 * M_HEADS), _full((M_CONV, 2 * M_WIDTH)), _full((M_WIDTH, LANE)), _full((L, L)), _full((L, L))],
        out_specs=pl.BlockSpec((L, M_WIDTH), lambda b, c: (b * nc + c, 0)),
        out_shape=jax.ShapeDtypeStruct((T, M_WIDTH), BF16),
        scratch_shapes=[pltpu.VMEM((L + SUBLANE, 2 * M_WIDTH), F32),
                        pltpu.VMEM((M_HEADS, M_HEAD_DIM, M_HEAD_DIM), F32),
                        pltpu.VMEM((SUBLANE, M_HEAD_DIM), F32),
                        pltpu.VMEM((SUBLANE, LANE), F32)],
        compiler_params=_params(2),
        name="mlstm",
    )(zqk, v_t, o_t, gt, conv_w, jnp.broadcast_to(head_norm[:, None], (M_WIDTH, LANE)), ltri, eye)


def _compress_kernel(k_ref, v_ref, pak, pbk, wak, wbk, w2k, pav, pbv, wav, wbv, w2v, ko_ref, vo_ref):
    def one(x_ref, pa, pb, wa, wb, w2, o_ref, transposed):
        n = x_ref.shape[0] // CMP_STRIDE
        x = jnp.concatenate([x_ref[pl.ds(t, n, stride=CMP_STRIDE), :] for t in range(CMP_STRIDE)], axis=1)
        first = _dot((x + pa[...]).astype(BF16), wa[...])
        second = _dot((x + pb[...]).astype(BF16), wb[...])
        hid = _silu(first + pltpu.roll(second, n - 1, 0)).astype(BF16)
        o_ref[0] = (_dot_nt(w2[...], hid) if transposed else _dot(hid, w2[...])).astype(BF16)

    one(k_ref, pak, pbk, wak, wbk, w2k, ko_ref, False)
    one(v_ref, pav, pbv, wav, wbv, w2v, vo_ref, True)


def _compress_weights(pe, w1, w2):
    half = CMP_BLOCK // 2
    eye = jnp.eye(N_KV_GROUPS, dtype=F32)
    w1r = w1.reshape(CMP_BLOCK, N_HEAD_DIM, CMP_HIDDEN)
    big = lambda w: jnp.einsum("ldc,gh->lgdhc", w, eye).reshape(half * KV_WIDTH, N_KV_GROUPS * CMP_HIDDEN).astype(BF16)
    pe_row = lambda p: jnp.broadcast_to(p[:, None, :], (half, N_KV_GROUPS, N_HEAD_DIM)).reshape(1, half * KV_WIDTH)
    w2bd = jnp.einsum("cd,gh->gchd", w2, eye).reshape(N_KV_GROUPS * CMP_HIDDEN, KV_WIDTH).astype(BF16)
    return pe_row(pe[:half]), pe_row(pe[half:]), big(w1r[:half]), big(w1r[half:]), w2bd


def _compress(kc, vc, wk, wv, B, S):
    nc = S // CMP_STRIDE
    blk = pl.BlockSpec((S, KV_WIDTH), lambda b: (b, 0))
    wv = wv[:4] + (wv[4].T,)
    return pl.pallas_call(
        _compress_kernel,
        grid=(B,),
        in_specs=[blk, blk] + [_full(w.shape) for w in wk] + [_full(w.shape) for w in wv],
        out_specs=[pl.BlockSpec((1, nc, KV_WIDTH), lambda b: (b, 0, 0)),
                   pl.BlockSpec((1, KV_WIDTH, nc), lambda b: (b, 0, 0))],
        out_shape=[jax.ShapeDtypeStruct((B, nc, KV_WIDTH), BF16), jax.ShapeDtypeStruct((B, KV_WIDTH, nc), BF16)],
        compiler_params=_params(1),
        name="compress",
    )(kc, vc, *wk, *wv)


def _masked_softmax_keys(s, mask):
    s = jnp.where(mask, s, -jnp.inf)
    m = jnp.max(s, axis=0, keepdims=True)
    m = jnp.where(jnp.isfinite(m), m, 0.0)
    e = jnp.exp2(s - m)
    return e, 1.0 / jnp.maximum(jnp.sum(e, axis=0, keepdims=True), 1e-30)


def _nsa_kernel(q_ref, kc_ref, vct_ref, ks_ref, vst_ref, kw_ref, vwt_ref, ngt_ref, ov_ref, ek_ref, eye_ref,
                y_ref, m_s, l_s, acc_s, *, n_top, TK):
    TQ = q_ref.shape[0]
    NS = ov_ref.shape[0]
    G, HPG, DH = N_KV_GROUPS, HEADS_PER_GROUP, N_HEAD_DIM
    R = G * HPG * TQ
    q0 = pl.program_id(1) * TQ

    q = q_ref[...]
    zero = jnp.zeros((TQ, DH), BF16)
    parts = []
    for hd in range(N_HEADS):
        qh = q[:, hd * DH:(hd + 1) * DH]
        parts.append(jnp.concatenate([qh, zero] if hd < HPG else [zero, qh], axis=1))
    qp = jnp.concatenate(parts, axis=0)
    t_q = q0 + lax.broadcasted_iota(jnp.int32, (1, R), 1) % TQ

    kc = kc_ref[0]
    NC = kc.shape[0]
    cmp_end = lax.broadcasted_iota(jnp.int32, (NC, 1), 0) * CMP_STRIDE + (CMP_BLOCK - 1)
    e_c, inv_c = _masked_softmax_keys(_dot_nt(kc, qp), cmp_end <= t_q)
    p_c = e_c * inv_c
    o_c = _dot(vct_ref[0], p_c.astype(BF16))
    p_grp = jnp.concatenate(
        [sum(p_c[:, (g * HPG + h) * TQ:(g * HPG + h + 1) * TQ] for h in range(HPG)) for g in range(G)], axis=1)
    imp = _dot(ov_ref[...], p_grp, precision=HIGHEST)

    j_io = lax.broadcasted_iota(jnp.int32, (NS, G * TQ), 0)
    t_row = q0 + lax.broadcasted_iota(jnp.int32, (NS, G * TQ), 1) % TQ
    cur = t_row // SLC_BLOCK
    forced = (j_io == 0) | (j_io == cur) | (j_io == cur - 1)
    future = j_io * SLC_BLOCK > t_row
    score = jnp.where(forced, jnp.inf, jnp.where(future, -jnp.inf, imp))
    rank = jnp.zeros((NS, G * TQ), F32)
    for jp in range(NS):
        other = score[jp:jp + 1, :]
        rank = rank + jnp.where(j_io > jp, jnp.where(other >= score, 1.0, 0.0), jnp.where(other > score, 1.0, 0.0))
    sel_bias = jnp.where(future, NEG, jnp.where(rank < n_top, 0.0, NEG)).astype(BF16)
    sel_rows = _dot_tn(sel_bias, eye_ref[...]).astype(BF16)
    sel_rows = jnp.broadcast_to(sel_rows.reshape(G, 1, TQ, LANE), (G, HPG, TQ, LANE)).reshape(R, LANE)
    qa = jnp.concatenate([qp, sel_rows], axis=1)

    m_s[...] = jnp.full_like(m_s, NEG)
    l_s[...] = jnp.zeros_like(l_s)
    acc_s[...] = jnp.zeros_like(acc_s)

    CB = NSA_COL_BLOCK

    def step(kt, causal):
        k0 = pl.multiple_of(kt * TK, TK)
        ka = jnp.concatenate([ks_ref[0, pl.ds(k0, TK), :], ek_ref[pl.ds(k0, TK), :]], axis=1)
        vt = vst_ref[:, pl.ds(k0, TK)]
        kpos = k0 + lax.broadcasted_iota(jnp.int32, (TK, 1), 0)
        scores = [_dot_nt(ka, qa[j * CB:(j + 1) * CB]) for j in range(R // CB)]
        probs, alphas = [], []
        for j in range(R // CB):
            cols = slice(j * CB, (j + 1) * CB)
            s = scores[j]
            if causal:
                s = jnp.where(kpos <= t_q[:, cols], s, NEG)
            m_old = m_s[:, cols]
            m_new = jnp.maximum(m_old, jnp.max(s, axis=0, keepdims=True))
            alpha = jnp.exp2(m_old - m_new)
            p = jnp.exp2(s - m_new)
            l_s[:, cols] = alpha * l_s[:, cols] + jnp.sum(p, axis=0, keepdims=True)
            m_s[:, cols] = m_new
            probs.append(p.astype(BF16))
            alphas.append(alpha)
        for j in range(R // CB):
            cols = slice(j * CB, (j + 1) * CB)
            acc_s[:, cols] = alphas[j] * acc_s[:, cols] + _dot(vt, probs[j])

    def full_tile(kt, carry):
        step(kt, False)
        return carry

    last = q0 // TK
    lax.fori_loop(0, last, full_tile, 0)
    step(last, True)
    o_s = acc_s[...] * (1.0 / l_s[...])

    WK = WINDOW + TQ
    start = pl.multiple_of(jnp.maximum(q0 - WINDOW, 0), TQ)
    wpos = start + lax.broadcasted_iota(jnp.int32, (WK, 1), 0)
    e_w, inv_w = _masked_softmax_keys(_dot_nt(kw_ref[0, pl.ds(start, WK), :], qp),
                                      (t_q - wpos).astype(jnp.uint32) < WINDOW)
    o_w =_dot(vwt_ref[:, pl.ds(start, WK)], e_w.astype(BF16)) * inv_w

    gates = ngt_ref[...]
    outs = []
    for hd in range(N_HEADS):
        cols = slice(hd * TQ, (hd + 1) * TQ)
        rows = slice((hd // HPG) * DH, (hd // HPG + 1) * DH)
        outs.append(gates[3 * hd:3 * hd + 1] * o_c[rows, cols]
                    + gates[3 * hd + 1:3 * hd + 2] * o_s[rows, cols]
                    + gates[3 * hd + 2:3 * hd + 3] * o_w[rows, cols])
    y_ref[...] = jnp.concatenate(outs, axis=0).T.astype(BF16)


def _nsa(q, kcmp, vcmp_t, ks, vs_t, kw, vw_t, ng_t, B, S):
    T = q.shape[0]
    TQ = min(NSA_Q_TILE, S)
    TK = min(NSA_K_TILE, S)
    nq = S // TQ
    NS = S // SLC_BLOCK
    NC = S // CMP_STRIDE
    n_cmp = (S - CMP_BLOCK) // CMP_STRIDE + 1
    js = np.arange(NS)[:, None] * SLC_BLOCK
    cs = np.arange(NC)[None, :] * CMP_STRIDE
    ov = np.clip(np.minimum(js + SLC_BLOCK, cs + CMP_BLOCK) - np.maximum(js, cs), 0, None) / CMP_STRIDE
    ov[:, n_cmp:] = 0.0
    assert NS <= LANE and TK % TQ == 0
    block_of_key = (np.arange(S)[:, None] // SLC_BLOCK == np.arange(LANE)[None, :]).astype(np.float32)
    eye_pad = np.eye(NS, LANE, dtype=np.float32)
    R = N_HEADS * TQ
    assert R % NSA_COL_BLOCK == 0
    seq = lambda: pl.BlockSpec((1, S, KV_WIDTH), lambda b, i: (b, 0, 0))
    seq_t = lambda: pl.BlockSpec((KV_WIDTH, S), lambda b, i: (0, b))
    return pl.pallas_call(
        functools.partial(_nsa_kernel, n_top=min(SLC_TOPK, NS), TK=TK),
        grid=(B, nq),
        in_specs=[pl.BlockSpec((TQ, N_WIDTH), lambda b, i: (b * nq + i, 0)),
                  pl.BlockSpec((1, NC, KV_WIDTH), lambda b, i: (b, 0, 0)),
                  pl.BlockSpec((1, KV_WIDTH, NC), lambda b, i: (b, 0, 0)),
                  seq(), seq_t(), seq(), seq_t(), pl.BlockSpec((ng_t.shape[0], TQ), lambda b, i: (0, b * nq + i)),
                  _full((NS, NC)), _full((S, LANE)), _full((NS, LANE))],
        out_specs=pl.BlockSpec((TQ, N_WIDTH), lambda b, i: (b * nq + i, 0)),
        out_shape=jax.ShapeDtypeStruct((T, N_WIDTH), BF16),
        scratch_shapes=[pltpu.VMEM((1, R), F32), pltpu.VMEM((1, R), F32), pltpu.VMEM((KV_WIDTH, R), F32)],
        compiler_params=_params(2),
        name="nsa",
    )(q, kcmp, vcmp_t, ks.reshape(B, S, KV_WIDTH), vs_t, kw.reshape(B, S, KV_WIDTH), vw_t, ng_t,
      jnp.asarray(ov.astype(np.float32)), jnp.asarray(block_of_key, dtype=BF16), jnp.asarray(eye_pad, dtype=BF16))


_ROUTER_ROWS = 40


def _post_kernel(ym_ref, yn_ref, wm_ref, wn_ref, x_ref, mod_ref, npost_ref, npre_ref, wr2_ref, br_ref,
                 utri_ref, lanes_ref, x1_ref, h2_ref, route_ref, cnt_ref, carry):
    TM = x_ref.shape[0]
    NR = _ROUTER_ROWS

    @pl.when(pl.program_id(0) == 0)
    def _():
        carry[...] = jnp.zeros_like(carry)

    mod = mod_ref[0]
    y = _dot(ym_ref[...], wm_ref[...]) + _dot(yn_ref[...], wn_ref[...])
    x1 = x_ref[...] + mod[2:3] * _rms(y, npost_ref[...])
    x1_ref[...] = x1
    h2 = _rms(x1, npre_ref[...]) * (1.0 + mod[4:5]) + mod[3:4]
    for s in range(SUBLANE):
        h2_ref[pl.ds(s, TM, stride=SUBLANE), :] = h2[:, s * LANE:(s + 1) * LANE]

    h_hi = h2.astype(BF16)
    h_lo = (h2 - h_hi.astype(F32)).astype(BF16)
    z2 = _dot_nt(wr2_ref[...], h_hi)
    logits = z2[:NR] + z2[NR:] + _dot_nt(wr2_ref[0:NR, :], h_lo) + br_ref[...]
    row = lax.broadcasted_iota(jnp.int32, (NR, TM), 0)
    is_grp = row < MOE_GROUPS
    lg = jnp.where(is_grp, logits, -jnp.inf)
    eg = jnp.exp(lg - jnp.max(lg, axis=0, keepdims=True))
    pg = eg / jnp.sum(eg, axis=0, keepdims=True)
    pg_top = jnp.max(pg, axis=0, keepdims=True)
    grp = jnp.min(jnp.where(is_grp & (pg == pg_top), row, NR), axis=0, keepdims=True)
    lo = MOE_GROUPS + EXPERTS_PER_GROUP * grp
    in_grp = (row >= lo) & (row < lo + EXPERTS_PER_GROUP)
    le = jnp.where(in_grp, logits, -jnp.inf)
    ee = jnp.exp(le - jnp.max(le, axis=0, keepdims=True))
    pe = jnp.where(in_grp, ee / jnp.sum(ee, axis=0, keepdims=True), -1.0)
    p1 = jnp.max(pe, axis=0, keepdims=True)
    i1 = jnp.min(jnp.where(pe == p1, row, NR), axis=0, keepdims=True)
    pe2 = jnp.where(row == i1, -1.0, pe)
    p2 = jnp.max(pe2, axis=0, keepdims=True)
    i2 = jnp.min(jnp.where((pe2 == p2) & in_grp & (row != i1), row, NR), axis=0, keepdims=True)
    den = p1 + p2
    oh1, oh2 = row == i1, row == i2
    oh = jnp.where(oh1, 1.0, 0.0) + jnp.where(oh2, 1.0, 0.0)
    before = _dot(oh.astype(BF16), utri_ref[...]) + carry[...]
    r1 = jnp.sum(jnp.where(oh1, before, 0.0), axis=0, keepdims=True)
    r2 = jnp.sum(jnp.where(oh2, before, 0.0), axis=0, keepdims=True)
    carry[...] = carry[...] + jnp.sum(oh, axis=1, keepdims=True)
    cnt_ref[...] = _dot_tn(jnp.broadcast_to(carry[...], (NR, SUBLANE)), lanes_ref[...], precision=HIGHEST)
    rows = ((i1 - MOE_GROUPS).astype(F32), (i2 - MOE_GROUPS).astype(F32), pg_top * p1 / den, pg_top * p2 / den, r1, r2)
    route_ref[...] = jnp.concatenate(rows + (jnp.zeros((SUBLANE - len(rows), TM), F32),), axis=0)


def _post(ym, yn, w_out, x2, mod3, norm_post, norm_pre, wr, br, S):
    T, D = x2.shape
    TM = min(TOKEN_TILE, S)
    tiles_per_seq = S // TM
    row = lambda w: pl.BlockSpec((TM, w), lambda i: (i, 0))
    small = pl.BlockSpec((SUBLANE, LANE), lambda i: (0, 0))
    NR = _ROUTER_ROWS
    utri = jnp.asarray(np.triu(np.ones((TM, TM), np.float32), 1), dtype=BF16)
    lanes = jnp.asarray(np.eye(NR, LANE, k=-MOE_GROUPS, dtype=np.float32))
    wr_hi = wr.astype(BF16)
    wr2 = jnp.concatenate([wr_hi, (wr - wr_hi.astype(F32)).astype(BF16)], axis=0)
    return pl.pallas_call(
        _post_kernel,
        grid=(T // TM,),
        in_specs=[row(M_WIDTH), row(N_WIDTH), _full((M_WIDTH, D)), _full((N_WIDTH, D)), row(D),
                  pl.BlockSpec((1, 6, D), lambda i: (i // tiles_per_seq, 0, 0)), _full((1, D)), _full((1, D)),
                  _full((2 * NR, D)), _full((NR, 1)), _full((TM, TM)), _full((NR, LANE))],
        out_specs=[row(D), pl.BlockSpec((TM * SUBLANE, LANE), lambda i: (i, 0)),
                   pl.BlockSpec((SUBLANE, TM), lambda i: (0, i)), small],
        out_shape=[jax.ShapeDtypeStruct((T, D), F32), jax.ShapeDtypeStruct((T * SUBLANE, LANE), F32),
                   jax.ShapeDtypeStruct((SUBLANE, T), F32), jax.ShapeDtypeStruct((SUBLANE, LANE), F32)],
        scratch_shapes=[pltpu.VMEM((NR, 1), F32)],
        compiler_params=_params(1),
        name="post_mix_router",
    )(ym, yn, w_out[:M_WIDTH].astype(BF16), w_out[M_WIDTH:].astype(BF16), x2, mod3,
      norm_post[None, :], norm_pre[None, :], wr2, br, utri, lanes)


def _plan_kernel(cnt_ref, pstart_ref, blk_ref, tot_ref, *, rows_per_block):
    lane = lax.broadcasted_iota(jnp.int32, (SUBLANE, LANE), 1)
    nblk = (cnt_ref[...].astype(jnp.int32) + (rows_per_block - 1)) // rows_per_block
    end = nblk
    sh = 1
    while sh < 2 * N_EXPERTS:
        end = end + jnp.where(lane >= sh, pltpu.roll(end, sh, 1), 0)
        sh *= 2
    pstart_ref[...] = (end - nblk) * rows_per_block
    nbp = blk_ref.shape[0]
    blk_io = lax.broadcasted_iota(jnp.int32, (nbp, LANE), 0)
    lane_b = lax.broadcasted_iota(jnp.int32, (nbp, LANE), 1)
    passed = jnp.where((lane_b < N_EXPERTS) & (blk_io >= end[0:1, :]), 1, 0)
    blk_ref[...] = jnp.broadcast_to(jnp.minimum(jnp.sum(passed, axis=1, keepdims=True), N_EXPERTS - 1), (nbp, LANE))
    tot_ref[...] = jnp.broadcast_to(
        jnp.sum(jnp.where(lane == N_EXPERTS - 1, end, 0), axis=1, keepdims=True), (SUBLANE, LANE))


def _sort_plan(cnt, n_blocks):
    small = pl.BlockSpec((SUBLANE, LANE), lambda i: (0, 0))
    nbp = -(-n_blocks // SUBLANE) * SUBLANE
    pstart, blk, tot = pl.pallas_call(
        functools.partial(_plan_kernel, rows_per_block=EXPERT_ROWS),
        grid=(1,),
        in_specs=[small],
        out_specs=[small, _full((nbp, LANE)), small],
        out_shape=[jax.ShapeDtypeStruct((SUBLANE, LANE), jnp.int32), jax.ShapeDtypeStruct((nbp, LANE), jnp.int32),
                   jax.ShapeDtypeStruct((SUBLANE, LANE), jnp.int32)],
        compiler_params=_params(1),
        name="expert_plan",
    )(cnt)
    return pstart[0, :N_EXPERTS + 1], blk[:n_blocks, 0], tot[0, :1]


_ASG = 4


def _dest_row(pstart_ref, asg_ref, j, k):
    n = asg_ref.shape[2] // _ASG
    return pstart_ref[asg_ref[0, 0, k * n + j]] + asg_ref[0, 0, (2 + k) * n + j]


def _row_copy(src, src_row, dst, dst_row, sem):
    return pltpu.make_async_copy(src.at[pl.ds(pl.multiple_of(src_row * SUBLANE, SUBLANE), SUBLANE)],
                                 dst.at[pl.ds(pl.multiple_of(dst_row * SUBLANE, SUBLANE), SUBLANE)], sem)


def _dispatch_kernel(pstart_ref, asg_ref, h2_ref, xs_hbm, zeros, sem, zsem, *, n_blocks):
    n = asg_ref.shape[2] // _ASG
    RB = zeros.shape[0] // SUBLANE

    @pl.when(pl.program_id(0) == 0)
    def _():
        zeros[...] = jnp.zeros_like(zeros)

        def fill(row0):
            return pltpu.make_async_copy(
                zeros, xs_hbm.at[pl.ds(pl.multiple_of(row0 * SUBLANE, SUBLANE), RB * SUBLANE)], zsem)

        def experts(act):
            def body(e, carry):
                @pl.when(pstart_ref[e + 1] > pstart_ref[e])
                def _():
                    act(fill(pstart_ref[e + 1] - RB))
                return carry
            lax.fori_loop(0, N_EXPERTS, body, 0)

        def trailing(act):
            def body(b, carry):
                act(fill(b * RB))
                return carry
            lax.fori_loop(pstart_ref[N_EXPERTS] // RB, n_blocks, body, 0)

        experts(lambda c: c.start())
        trailing(lambda c: c.start())
        experts(lambda c: c.wait())
        trailing(lambda c: c.wait())

    def issue(jj, carry):
        for u in range(ISSUE_UNROLL):
            j = jj * ISSUE_UNROLL + u
            for k in range(2):
                _row_copy(h2_ref, j, xs_hbm, _dest_row(pstart_ref, asg_ref, j, k), sem).start(priority=k)
        return carry

    lax.fori_loop(0, n // ISSUE_UNROLL, issue, 0)
    whole = xs_hbm.at[pl.ds(0, 2 * n * SUBLANE)]
    pltpu.make_async_copy(whole, whole, sem).wait()


def _dispatch(pstart, asg3, h2t, n_blocks):
    nt, _, width = asg3.shape
    n = width // _ASG
    return pl.pallas_call(
        functools.partial(_dispatch_kernel, n_blocks=n_blocks),
        grid=(nt,),
        in_specs=[pl.BlockSpec(memory_space=pltpu.SMEM),
                  pl.BlockSpec((1, 1, width), lambda i: (i, 0, 0), memory_space=pltpu.SMEM),
                  pl.BlockSpec((n * SUBLANE, LANE), lambda i: (i, 0))],
        out_specs=pl.BlockSpec(memory_space=pl.ANY),
        out_shape=jax.ShapeDtypeStruct((n_blocks * EXPERT_ROWS * SUBLANE, LANE), F32),
        scratch_shapes=[pltpu.VMEM((EXPERT_ROWS * SUBLANE, LANE), F32), pltpu.SemaphoreType.DMA(()),
                        pltpu.SemaphoreType.DMA(())],
        compiler_params=_params(1),
        name="dispatch",
    )(pstart, asg3, h2t)


def _expert_kernel(be_ref, nb_ref, xs_ref, w1_ref, w3_ref, w2_ref, ys_ref, w1b, w3b, w2b):
    RB = xs_ref.shape[0] // SUBLANE
    i = pl.program_id(0)

    @pl.when(i < nb_ref[0])
    def _():
        @pl.when((i == 0) | (be_ref[i] != be_ref[jnp.maximum(i - 1, 0)]))
        def _():
            w1b[...] = w1_ref[0].astype(BF16)
            w3b[...] = w3_ref[0].astype(BF16)
            w2b[...] = w2_ref[0].astype(BF16)

        x = jnp.concatenate([xs_ref[pl.ds(s, RB, stride=SUBLANE), :] for s in range(SUBLANE)], axis=1).astype(BF16)
        hb = _silu(_dot(x, w1b[...])) * _dot(x, w3b[...])
        y = _dot(hb.astype(BF16), w2b[...])
        for s in range(SUBLANE):
            ys_ref[pl.ds(s, RB, stride=SUBLANE), :] = y[:, s * LANE:(s + 1) * LANE]

    @pl.when(i >= nb_ref[0])
    def _():
        ys_ref[...] = jnp.zeros_like(ys_ref)


def _experts(block_expert, n_used, xs, w1, w3, w2):
    n_blocks = block_expert.shape[0]
    RB = EXPERT_ROWS
    D, F = w1.shape[1], w1.shape[2]
    cur = lambda i, nb: jnp.minimum(i, nb[0] - 1)
    rows = pl.BlockSpec((RB * SUBLANE, LANE), lambda i, be, nb: (cur(i, nb), 0))
    wspec = lambda a, b: pl.BlockSpec((1, a, b), lambda i, be, nb: (be[cur(i, nb)], 0, 0))
    return pl.pallas_call(
        _expert_kernel,
        grid_spec=pltpu.PrefetchScalarGridSpec(
            num_scalar_prefetch=2, grid=(n_blocks,),
            in_specs=[rows, wspec(D, F), wspec(D, F), wspec(F, D)],
            out_specs=pl.BlockSpec((RB * SUBLANE, LANE), lambda i, be, nb: (i, 0)),
            scratch_shapes=[pltpu.VMEM((D, F), BF16), pltpu.VMEM((D, F), BF16), pltpu.VMEM((F, D), BF16)]),
        out_shape=jax.ShapeDtypeStruct(xs.shape, F32),
        compiler_params=_params(1),
        name="experts",
    )(block_expert, n_used, xs, w1, w3, w2)


def _combine_kernel(pstart_ref, asg_ref, asg_next_ref, ys_hbm, x1_ref, route_ref, mod_ref, nw_ref, o_ref, buf, sems):
    n = x1_ref.shape[0]
    i = pl.program_id(0)
    slot = i % 2

    def gather(dref, into):
        def issue(jj, carry):
            for u in range(ISSUE_UNROLL):
                j = jj * ISSUE_UNROLL + u
                for k in range(2):
                    _row_copy(ys_hbm, _dest_row(pstart_ref, dref, j, k), buf.at[into], k * n + j,
                              sems.at[into]).start(priority=k)
            return carry

        lax.fori_loop(0, n // ISSUE_UNROLL, issue, 0)

    @pl.when(i == 0)
    def _():
        gather(asg_ref, 0)

    @pl.when(i + 1 < pl.num_programs(0))
    def _():
        gather(asg_next_ref, 1 - slot)

    pltpu.make_async_copy(ys_hbm.at[pl.ds(0, 2 * n * SUBLANE)], buf.at[slot], sems.at[slot]).wait()

    route = jnp.concatenate([route_ref[...], jnp.zeros((LANE - SUBLANE, n), F32)], axis=0).T
    g0, g1 = route[:, 2:3], route[:, 3:4]
    y = jnp.concatenate(
        [g0 * buf[slot, pl.ds(s, n, stride=SUBLANE), :] + g1 * buf[slot, pl.ds(n * SUBLANE + s, n, stride=SUBLANE), :]
         for s in range(SUBLANE)], axis=1)
    o_ref[...] = x1_ref[...] + mod_ref[0][5:6] * _rms(y, nw_ref[...])


def _combine(pstart, asg3, ys, x1, route, mod3, norm_w, S):
    T, D = x1.shape
    nt, _, width = asg3.shape
    n = width // _ASG
    tiles_per_seq = S // n
    return pl.pallas_call(
        _combine_kernel,
        grid=(nt,),
        in_specs=[pl.BlockSpec(memory_space=pltpu.SMEM),
                  pl.BlockSpec((1, 1, width), lambda i: (i, 0, 0), memory_space=pltpu.SMEM),
                  pl.BlockSpec((1, 1, width), lambda i: (jnp.minimum(i + 1, nt - 1), 0, 0), memory_space=pltpu.SMEM),
                  pl.BlockSpec(memory_space=pl.ANY), pl.BlockSpec((n, D), lambda i: (i, 0)),
                  pl.BlockSpec((SUBLANE, n), lambda i: (0, i)),
                  pl.BlockSpec((1, 6, D), lambda i: (i // tiles_per_seq, 0, 0)), _full((1, D))],
        out_specs=pl.BlockSpec((n, D), lambda i: (i, 0)),
        out_shape=jax.ShapeDtypeStruct((T, D), F32),
        scratch_shapes=[pltpu.VMEM((2, 2 * n * SUBLANE, LANE), F32), pltpu.SemaphoreType.DMA((2,))],
        compiler_params=_params(1),
        name="combine",
    )(pstart, asg3, asg3, ys, x1, route, mod3, norm_w[None, :])


def _layer(x, c, positions, ada_w, ada_b, norm_mix_pre, norm_mix_post, norm_ffn_pre, norm_ffn_post, w_in,
           conv_w, gate_b, head_norm, cmp_k, cmp_v, w_out, wg, bg, we, be, w1, w3, w2):
    B, S, D = x.shape
    T = B * S
    x2 = x.reshape(T, D)
    mod3 = _adaln(c, ada_w, ada_b).reshape(B, 6, D)

    o_mi = 4 * M_WIDTH
    o_nq = o_mi + 2 * M_HEADS
    o_kv = o_nq + N_WIDTH
    kv = lambda i: w_in[:, o_kv + i * KV_WIDTH:o_kv + (i + 1) * KV_WIDTH]
    o_ng = o_kv + 6 * KV_WIDTH
    w_main = jnp.concatenate([w_in[:, :2 * M_WIDTH], w_in[:, o_nq:o_kv], kv(0), kv(2), kv(4), kv(1)],
                             axis=1).astype(BF16)
    w_t = jnp.concatenate([kv(3), kv(5), w_in[:, o_ng:], jnp.zeros((D, _R_MV - _R_NG - 3 * N_HEADS), F32),
                           w_in[:, 2 * M_WIDTH:o_mi]], axis=1).T.astype(BF16)
    w_gate = w_in[:, o_mi:o_nq].T
    wg_t = w_gate.astype(BF16)
    w_t = jnp.concatenate([w_t, wg_t, (w_gate - wg_t.astype(F32)).astype(BF16)], axis=0)
    half = N_HEAD_DIM // 2
    inv = ROPE_THETA ** (-jnp.arange(half, dtype=F32) / half)
    inv_row = jnp.tile(inv, LANE // half)[None, :]
    pos_col = positions.astype(F32).reshape(T, 1)

    zqk, q, kc, ks, kw, vc, vs_t, vw_t, ng_t, mv_t, mo_t, gt = _inproj(
        x2, mod3, norm_mix_pre, w_main, w_t, wg_t, gate_b, pos_col, inv_row, S)
    ym = _mlstm(zqk, mv_t, mo_t, gt, conv_w, head_norm, B, S)
    kcmp, vcmp_t = _compress(kc, vc, _compress_weights(*cmp_k), _compress_weights(*cmp_v), B, S)
    yn = _nsa(q, kcmp, vcmp_t, ks, vs_t, kw, vw_t, ng_t, B, S)

    pad = _ROUTER_ROWS - MOE_GROUPS - N_EXPERTS
    wr = jnp.concatenate([wg, we, jnp.zeros((D, pad), F32)], axis=1).T
    br = jnp.concatenate([bg, be, jnp.zeros((pad,), F32)])[:, None]
    x1, h2t, route, cnt = _post(ym, yn, w_out, x2, mod3, norm_mix_post, norm_ffn_pre, wr, br, S)

    n_blocks = (2 * T) // EXPERT_ROWS + N_EXPERTS
    pstart, block_expert, n_used = _sort_plan(cnt, n_blocks)
    n_tok = min(GATHER_TILE, S)
    asg = jnp.concatenate([route[0:2], route[4:6]], axis=0).astype(jnp.int32)
    asg3 = asg.reshape(_ASG, T // n_tok, n_tok).transpose(1, 0, 2).reshape(T // n_tok, 1, _ASG * n_tok)
    xs = _dispatch(pstart, asg3, h2t, n_blocks)
    ys = _experts(block_expert, n_used, xs, w1, w3, w2)
    out = _combine(pstart, asg3, ys, x1, route, mod3, norm_ffn_post, S)
    return out.reshape(B, S, D)


def kernel(x, c, positions, ada_w, ada_b, norm_mix_pre, norm_mix_post, norm_ffn_pre, norm_ffn_post, w_in, mlstm_conv_w, mlstm_gate_b, mlstm_head_norm, cmp_pe_k, cmp_w1_k, cmp_w2_k, cmp_pe_v, cmp_w1_v, cmp_w2_v, w_out, router_grp_w, router_grp_b, router_exp_w, router_exp_b, expert_w1, expert_w3, expert_w2):
    for l in range(ada_w.shape[0]):
        x = _layer(x, c, positions, ada_w[l], ada_b[l], norm_mix_pre[l], norm_mix_post[l], norm_ffn_pre[l],
                   norm_ffn_post[l], w_in[l], mlstm_conv_w[l], mlstm_gate_b[l], mlstm_head_norm[l],
                   (cmp_pe_k[l], cmp_w1_k[l], cmp_w2_k[l]), (cmp_pe_v[l], cmp_w1_v[l], cmp_w2_v[l]), w_out[l],
                   router_grp_w[l], router_grp_b[l], router_exp_w[l], router_exp_b[l],
                   expert_w1[l], expert_w3[l], expert_w2[l])
    return x
```

```python
import functools

import jax
import jax.numpy as jnp
import numpy as np
from jax import lax
from jax.experimental import pallas as pl
from jax.experimental.pallas import tpu as pltpu

M_HEADS = 4
M_HEAD_DIM = 128
M_WIDTH = M_HEADS * M_HEAD_DIM
M_CONV = 4
N_HEADS = 8
N_KV_GROUPS = 2
HEADS_PER_GROUP = N_HEADS // N_KV_GROUPS
N_HEAD_DIM = 64
N_WIDTH = N_HEADS * N_HEAD_DIM
KV_WIDTH = N_KV_GROUPS * N_HEAD_DIM
CMP_BLOCK = 32
CMP_STRIDE = 16
CMP_HIDDEN = 2 * N_HEAD_DIM
SLC_BLOCK = 64
SLC_TOPK = 16
WINDOW = 512
ROPE_THETA = 10000.0
MOE_GROUPS = 4
EXPERTS_PER_GROUP = 8
N_EXPERTS = MOE_GROUPS * EXPERTS_PER_GROUP
RMS_EPS = 1e-6
LOG2_E = 1.4426950408889634

LANE = 128
SUBLANE = 8
VMEM_LIMIT_BYTES = 56 * 1024 * 1024

TOKEN_TILE = 512
MLSTM_CHUNK = 256
NSA_Q_TILE = 256
NSA_K_TILE = 512
NSA_COL_BLOCK = 256
EXPERT_ROWS = 512
GATHER_TILE = 256
ISSUE_UNROLL = 8

F32 = jnp.float32
BF16 = jnp.bfloat16
NEG = -1e30
HIGHEST = lax.Precision.HIGHEST
NT_DIMS = (((1,), (1,)), ((), ()))
TN_DIMS = (((0,), (0,)), ((), ()))


def _params(n_grid):
    return pltpu.CompilerParams(
        dimension_semantics=("arbitrary",) * n_grid, vmem_limit_bytes=VMEM_LIMIT_BYTES)


def _dot(a, b, **kw):
    return jnp.dot(a, b, preferred_element_type=F32, **kw)


def _dot_nt(a, b, **kw):
    return lax.dot_general(a, b, NT_DIMS, preferred_element_type=F32, **kw)


def _dot_tn(a, b, **kw):
    return lax.dot_general(a, b, TN_DIMS, preferred_element_type=F32, **kw)


def _sigmoid(x):
    return 1.0 / (1.0 + jnp.exp(-x))


def _silu(x):
    return x * _sigmoid(x)


def _rms(x, w):
    return x * lax.rsqrt(jnp.mean(x * x, axis=-1, keepdims=True) + RMS_EPS) * w


def _full(shape):
    return pl.BlockSpec(shape, lambda *_: (0,) * len(shape))


def _adaln_kernel(c_ref, w_ref, b_ref, o_ref):
    o_ref[...] = _dot(_silu(c_ref[...]), w_ref[...], precision=HIGHEST) + b_ref[...]


def _adaln(c, ada_w, ada_b):
    B, D = c.shape
    n = ada_w.shape[1] // D
    return pl.pallas_call(
        _adaln_kernel,
        grid=(n,),
        in_specs=[_full((B, D)), pl.BlockSpec((D, D), lambda j: (0, j)), pl.BlockSpec((1, D), lambda j: (0, j))],
        out_specs=pl.BlockSpec((B, D), lambda j: (0, j)),
        out_shape=jax.ShapeDtypeStruct((B, ada_w.shape[1]), F32),
        compiler_params=_params(1),
        name="adaln",
    )(c, ada_w, ada_b[None, :])


_C_M = 0
_C_Q = 2 * M_WIDTH
_C_KC = _C_Q + N_WIDTH
_C_KS = _C_KC + KV_WIDTH
_C_KW = _C_KS + KV_WIDTH
_C_VC = _C_KW + KV_WIDTH
_C_END = _C_VC + KV_WIDTH
_R_VS = 0
_R_VW = KV_WIDTH
_R_NG = 2 * KV_WIDTH
_R_MV = _R_NG + 4 * SUBLANE
_R_MO = _R_MV + M_WIDTH
_R_GH = _R_MO + M_WIDTH
_R_GL = _R_GH + 2 * M_HEADS
_R_END = _R_GL + 2 * M_HEADS


def _inproj_kernel(x_ref, mod_ref, nw_ref, w_ref, wt_ref, wg_ref, gb_ref, pos_ref, inv_ref,
                   zm_ref, q_ref, kc_ref, ks_ref, kw_ref, vc_ref, vst_ref, vwt_ref, ngt_ref, mvt_ref, mot_ref, gt_ref):
    mod = mod_ref[0]
    h = _rms(x_ref[...], nw_ref[...]) * (1.0 + mod[1:2]) + mod[0:1]
    hb = h.astype(BF16)
    z = _dot(hb, w_ref[...])
    zm_ref[...] = z[:, _C_M:_C_Q]
    zt = _dot_nt(wt_ref[...], hb)
    h_lo = (h - hb.astype(F32)).astype(BF16)
    gt_ref[...] = (zt[_R_GH:_R_GL] + zt[_R_GL:_R_END] + _dot_nt(wg_ref[...], h_lo)) + gb_ref[...]
    vst_ref[...] = zt[_R_VS:_R_VW].astype(BF16)
    vwt_ref[...] = zt[_R_VW:_R_NG].astype(BF16)
    ngt_ref[...] = _sigmoid(zt[_R_NG:_R_MV])
    mvt_ref[...] = zt[_R_MV:_R_MO].astype(BF16)
    mot_ref[...] = zt[_R_MO:_R_GH]

    ang = pos_ref[...] * inv_ref[...]
    cos = jnp.cos(ang)
    sin = jnp.sin(ang)
    lane = lax.broadcasted_iota(jnp.int32, (1, LANE), 1)
    first = (lane % N_HEAD_DIM) < (N_HEAD_DIM // 2)
    sin_signed = jnp.where(first, -sin, sin)

    def rope(slab):
        rot = jnp.where(first, pltpu.roll(slab, LANE - N_HEAD_DIM // 2, 1), pltpu.roll(slab, N_HEAD_DIM // 2, 1))
        return slab * cos + rot * sin_signed

    scale = N_HEAD_DIM ** -0.5 * LOG2_E
    for j in range(N_WIDTH // LANE):
        q_ref[:, j * LANE:(j + 1) * LANE] = (rope(z[:, _C_Q + j * LANE:_C_Q + (j + 1) * LANE]) * scale).astype(BF16)
    kc_ref[...] = rope(z[:, _C_KC:_C_KS])
    ks_ref[...] = rope(z[:, _C_KS:_C_KW]).astype(BF16)
    kw_ref[...] = rope(z[:, _C_KW:_C_VC]).astype(BF16)
    vc_ref[...] = z[:, _C_VC:_C_END]


def _inproj(x2, mod3, norm_w, w_main, w_t, wg_t, gate_b, pos_col, inv_row, S):
    T, D = x2.shape
    TM = min(TOKEN_TILE, S)
    tiles_per_seq = S // TM
    row = lambda w: pl.BlockSpec((TM, w), lambda i: (i, 0))
    col = lambda r: pl.BlockSpec((r, TM), lambda i: (0, i))
    outs = [(2 * M_WIDTH, F32), (N_WIDTH, BF16), (KV_WIDTH, F32), (KV_WIDTH, BF16), (KV_WIDTH, BF16), (KV_WIDTH, F32)]
    outs_t = [(KV_WIDTH, BF16), (KV_WIDTH, BF16), (_R_MV - _R_NG, F32), (M_WIDTH, BF16), (M_WIDTH, F32),
              (2 * M_HEADS, F32)]
    return pl.pallas_call(
        _inproj_kernel,
        grid=(T // TM,),
        in_specs=[row(D), pl.BlockSpec((1, 6, D), lambda i: (i // tiles_per_seq, 0, 0)), _full((1, D)),
                  _full(w_main.shape), _full(w_t.shape), _full(wg_t.shape), _full((2 * M_HEADS, 1)), row(1),
                  _full((1, LANE))],
        out_specs=[row(w) for w, _ in outs] + [col(r) for r, _ in outs_t],
        out_shape=[jax.ShapeDtypeStruct((T, w), dt) for w, dt in outs]
        + [jax.ShapeDtypeStruct((r, T), dt) for r, dt in outs_t],
        compiler_params=_params(1),
        name="inproj",
    )(x2, mod3, norm_w[None, :], w_main, w_t, wg_t, gate_b[:, None], pos_col, inv_row)


def _mlstm_kernel(zm_ref, vt_ref, ot_ref, gt_ref, cw_ref, hn_ref, ltri_ref, eye_ref, y_ref, buf, c_s, n_s, m_s):
    L = zm_ref.shape[0]
    QK = 2 * M_WIDTH
    DH = M_HEAD_DIM

    @pl.when(pl.program_id(1) == 0)
    def _():
        buf[0:SUBLANE, :] = jnp.zeros((SUBLANE, QK), F32)
        c_s[...] = jnp.zeros_like(c_s)
        n_s[...] = jnp.zeros_like(n_s)
        m_s[...] = jnp.zeros_like(m_s)

    buf[SUBLANE:SUBLANE + L, :] = zm_ref[:, 0:QK]
    cw = cw_ref[...]
    conv = cw[M_CONV - 1:M_CONV] * buf[SUBLANE:SUBLANE + L, :]
    for j in range(M_CONV - 1):
        off = SUBLANE - (M_CONV - 1) + j
        conv = conv + cw[j:j + 1] * buf[off:off + L, :]
    buf[0:SUBLANE, :] = buf[L:L + SUBLANE, :]
    qk = _silu(conv)

    g = gt_ref[...]
    fp = g[M_HEADS:]
    lf = jnp.minimum(fp, 0.0) - jnp.log(1.0 + jnp.exp(-jnp.abs(fp)))
    lf8 = jnp.concatenate([lf, jnp.zeros_like(lf)], axis=0)
    ltri = ltri_ref[...]
    b_rows = _dot_nt(lf8, ltri, precision=HIGHEST)
    b_cols = _dot_nt(ltri, lf8, precision=HIGHEST)
    i_cols = _dot_nt(eye_ref[...], g, precision=HIGHEST)

    causal = lax.broadcasted_iota(jnp.int32, (L, L), 0) <= lax.broadcasted_iota(jnp.int32, (L, L), 1)
    ones_rows = jnp.ones((2 * SUBLANE, L), F32)
    outs = []
    for h in range(M_HEADS):
        hs = slice(h * DH, (h + 1) * DH)
        b_r, i_r = b_rows[h:h + 1], g[h:h + 1]
        src_c = b_cols[:, h:h + 1] - i_cols[:, h:h + 1]
        m_prev = m_s[h:h + 1, 0:1]
        g_tot = b_r[:, L - 1:L]
        a_r = g_tot - b_r + i_r
        m_new = jnp.maximum(g_tot + m_prev, jnp.max(a_r, axis=1, keepdims=True))

        qb = qk[:, hs].astype(BF16)
        kb = (qk[:, M_WIDTH + h * DH:M_WIDTH + (h + 1) * DH] * (DH ** -0.5)).astype(BF16)
        vt = vt_ref[hs, :]

        dlog = jnp.where(causal, b_r - src_c, -jnp.inf)
        inter = b_r + m_prev
        m_t = jnp.maximum(inter, jnp.max(dlog, axis=0, keepdims=True))
        wts = jnp.exp(dlog - m_t) * _dot_nt(kb, qb)
        dec = jnp.exp(inter - m_t)
        ct_prev = c_s[h]
        n_prev = n_s[...]
        num = _dot(vt, wts.astype(BF16)) + dec * _dot_nt(ct_prev.astype(BF16), qb)
        qn = _dot_nt(n_prev.astype(BF16), qb)[h:h + 1]
        den = jnp.sum(wts, axis=0, keepdims=True) + dec * qn
        hh = num * (1.0 / jnp.maximum(jnp.abs(den), jnp.exp(-m_t)))

        w_r = jnp.exp(a_r - m_new)
        keep = jnp.exp(g_tot + m_prev - m_new)
        lhs = jnp.concatenate([vt.astype(F32), ones_rows], axis=0) * w_r
        upd = _dot(lhs.astype(BF16), kb)
        c_s[h] = keep * ct_prev + upd[:DH]
        n_s[h:h + 1] = keep * n_prev[h:h + 1] + upd[DH:DH + 1]
        m_s[h:h + 1] = jnp.broadcast_to(m_new, (1, LANE))

        hn = hh * lax.rsqrt(jnp.mean(hh * hh, axis=0, keepdims=True) + RMS_EPS)
        hn = hn * jnp.concatenate([hn_ref[hs, :]] * (L // LANE), axis=1)
        outs.append(_sigmoid(ot_ref[hs, :]) * hn)
    y_ref[...] = jnp.concatenate(outs, axis=0).T.astype(BF16)


def _mlstm(zqk, v_t, o_t, gt, conv_w, head_norm, B, S):
    T = zqk.shape[0]
    L = min(MLSTM_CHUNK, S)
    assert L % LANE == 0
    nc = S // L
    ltri = jnp.asarray(np.tril(np.ones((L, L), np.float32)))
    eye = jnp.asarray(np.eye(L, dtype=np.float32))
    col = lambda r: pl.BlockSpec((r, L), lambda b, c: (0, b * nc + c))
    return pl.pallas_call(
        _mlstm_kernel,
        grid=(B, nc),
        in_specs=[pl.BlockSpec((L, 2 * M_WIDTH), lambda b, c: (b * nc + c, 0)), col(M_WIDTH), col(M_WIDTH),
                  col(2 * M_HEADS), _full((M_CONV, 2 * M_WIDTH)), _full((M_WIDTH, LANE)), _full((L, L)), _full((L, L))],
        out_specs=pl.BlockSpec((L, M_WIDTH), lambda b, c: (b * nc + c, 0)),
        out_shape=jax.ShapeDtypeStruct((T, M_WIDTH), BF16),
        scratch_shapes=[pltpu.VMEM((L + SUBLANE, 2 * M_WIDTH), F32),
                        pltpu.VMEM((M_HEADS, M_HEAD_DIM, M_HEAD_DIM), F32),
                        pltpu.VMEM((SUBLANE, M_HEAD_DIM), F32),
                        pltpu.VMEM((SUBLANE, LANE), F32)],
        compiler_params=_params(2),
        name="mlstm",
    )(zqk, v_t, o_t, gt, conv_w, jnp.broadcast_to(head_norm[:, None], (M_WIDTH, LANE)), ltri, eye)


def _compress_kernel(k_ref, v_ref, pak, pbk, wak, wbk, w2k, pav, pbv, wav, wbv, w2v, ko_ref, vo_ref):
    def one(x_ref, pa, pb, wa, wb, w2, o_ref, transposed):
        n = x_ref.shape[0] // CMP_STRIDE
        x = jnp.concatenate([x_ref[pl.ds(t, n, stride=CMP_STRIDE), :] for t in range(CMP_STRIDE)], axis=1)
        first = _dot((x + pa[...]).astype(BF16), wa[...])
        second = _dot((x + pb[...]).astype(BF16), wb[...])
        hid = _silu(first + pltpu.roll(second, n - 1, 0)).astype(BF16)
        o_ref[0] = (_dot_nt(w2[...], hid) if transposed else _dot(hid, w2[...])).astype(BF16)

    one(k_ref, pak, pbk, wak, wbk, w2k, ko_ref, False)
    one(v_ref, pav, pbv, wav, wbv, w2v, vo_ref, True)


def _compress_weights(pe, w1, w2):
    half = CMP_BLOCK // 2
    eye = jnp.eye(N_KV_GROUPS, dtype=F32)
    w1r = w1.reshape(CMP_BLOCK, N_HEAD_DIM, CMP_HIDDEN)
    big = lambda w: jnp.einsum("ldc,gh->lgdhc", w, eye).reshape(half * KV_WIDTH, N_KV_GROUPS * CMP_HIDDEN).astype(BF16)
    pe_row = lambda p: jnp.broadcast_to(p[:, None, :], (half, N_KV_GROUPS, N_HEAD_DIM)).reshape(1, half * KV_WIDTH)
    w2bd = jnp.einsum("cd,gh->gchd", w2, eye).reshape(N_KV_GROUPS * CMP_HIDDEN, KV_WIDTH).astype(BF16)
    return pe_row(pe[:half]), pe_row(pe[half:]), big(w1r[:half]), big(w1r[half:]), w2bd


def _compress(kc, vc, wk, wv, B, S):
    nc = S // CMP_STRIDE
    blk = pl.BlockSpec((S, KV_WIDTH), lambda b: (b, 0))
    wv = wv[:4] + (wv[4].T,)
    return pl.pallas_call(
        _compress_kernel,
        grid=(B,),
        in_specs=[blk, blk] + [_full(w.shape) for w in wk] + [_full(w.shape) for w in wv],
        out_specs=[pl.BlockSpec((1, nc, KV_WIDTH), lambda b: (b, 0, 0)),
                   pl.BlockSpec((1, KV_WIDTH, nc), lambda b: (b, 0, 0))],
        out_shape=[jax.ShapeDtypeStruct((B, nc, KV_WIDTH), BF16), jax.ShapeDtypeStruct((B, KV_WIDTH, nc), BF16)],
        compiler_params=_params(1),
        name="compress",
    )(kc, vc, *wk, *wv)


def _masked_softmax_keys(s, mask):
    s = jnp.where(mask, s, -jnp.inf)
    m = jnp.max(s, axis=0, keepdims=True)
    m = jnp.where(jnp.isfinite(m), m, 0.0)
    e = jnp.exp2(s - m)
    return e, 1.0 / jnp.maximum(jnp.sum(e, axis=0, keepdims=True), 1e-30)


def _nsa_kernel(q_ref, kc_ref, vct_ref, ks_ref, vst_ref, kw_ref, vwt_ref, ngt_ref, ov_ref, ek_ref, eye_ref,
                y_ref, m_s, l_s, acc_s, rank_s, *, n_top, TK):
    TQ = q_ref.shape[0]
    NS = ov_ref.shape[0]
    G, HPG, DH = N_KV_GROUPS, HEADS_PER_GROUP, N_HEAD_DIM
    R = G * HPG * TQ
    q0 = pl.program_id(1) * TQ

    q = q_ref[...]
    zero = jnp.zeros((TQ, DH), BF16)
    parts = []
    for hd in range(N_HEADS):
        qh = q[:, hd * DH:(hd + 1) * DH]
        parts.append(jnp.concatenate([qh, zero] if hd < HPG else [zero, qh], axis=1))
    qp = jnp.concatenate(parts, axis=0)
    t_q = q0 + lax.broadcasted_iota(jnp.int32, (1, R), 1) % TQ

    kc = kc_ref[0]
    NC = kc.shape[0]
    cmp_end = lax.broadcasted_iota(jnp.int32, (NC, 1), 0) * CMP_STRIDE + (CMP_BLOCK - 1)
    e_c, inv_c = _masked_softmax_keys(_dot_nt(kc, qp), cmp_end <= t_q)
    p_c = e_c * inv_c
    o_c = _dot(vct_ref[0], p_c.astype(BF16))
    p_grp = jnp.concatenate(
        [sum(p_c[:, (g * HPG + h) * TQ:(g * HPG + h + 1) * TQ] for h in range(HPG)) for g in range(G)], axis=1)
    imp = _dot(ov_ref[...], p_grp, precision=HIGHEST)

    j_io = lax.broadcasted_iota(jnp.int32, (NS, G * TQ), 0)
    t_row = q0 + lax.broadcasted_iota(jnp.int32, (NS, G * TQ), 1) % TQ
    cur = t_row // SLC_BLOCK
    forced = (j_io == 0) | (j_io == cur) | (j_io == cur - 1)
    future = j_io * SLC_BLOCK > t_row
    score = jnp.where(forced, jnp.inf, jnp.where(future, -jnp.inf, imp))
    rank_s[...] = jnp.zeros_like(rank_s)
    n_rival = (q0 + TQ - 1) // SLC_BLOCK + 1
    for c0 in range(0, NS, SUBLANE):
        @pl.when(c0 < n_rival)
        def _():
            part = jnp.zeros((NS, G * TQ), F32)
            for jp in range(c0, c0 + SUBLANE):
                other = score[jp:jp + 1, :]
                part = part + jnp.where(j_io > jp, jnp.where(other >= score, 1.0, 0.0),
                                        jnp.where(other > score, 1.0, 0.0))
            rank_s[...] = rank_s[...] + part
    rank = rank_s[...]
    sel_bias = jnp.where(future, NEG, jnp.where(rank < n_top, 0.0, NEG)).astype(BF16)
    sel_rows = _dot_tn(sel_bias, eye_ref[...]).astype(BF16)
    sel_rows = jnp.broadcast_to(sel_rows.reshape(G, 1, TQ, LANE), (G, HPG, TQ, LANE)).reshape(R, LANE)
    qa = jnp.concatenate([qp, sel_rows], axis=1)

    m_s[...] = jnp.full_like(m_s, NEG)
    l_s[...] = jnp.zeros_like(l_s)
    acc_s[...] = jnp.zeros_like(acc_s)

    CB = NSA_COL_BLOCK

    def step(kt, causal):
        k0 = pl.multiple_of(kt * TK, TK)
        ka = jnp.concatenate([ks_ref[0, pl.ds(k0, TK), :], ek_ref[pl.ds(k0, TK), :]], axis=1)
        vt = vst_ref[:, pl.ds(k0, TK)]
        kpos = k0 + lax.broadcasted_iota(jnp.int32, (TK, 1), 0)
        scores = [_dot_nt(ka, qa[j * CB:(j + 1) * CB]) for j in range(R // CB)]
        probs, alphas = [], []
        for j in range(R // CB):
            cols = slice(j * CB, (j + 1) * CB)
            s = scores[j]
            if causal:
                s = jnp.where(kpos <= t_q[:, cols], s, NEG)
            m_old = m_s[:, cols]
            m_new = jnp.maximum(m_old, jnp.max(s, axis=0, keepdims=True))
            alpha = jnp.exp2(m_old - m_new)
            p = jnp.exp2(s - m_new)
            l_s[:, cols] = alpha * l_s[:, cols] + jnp.sum(p, axis=0, keepdims=True)
            m_s[:, cols] = m_new
            probs.append(p.astype(BF16))
            alphas.append(alpha)
        for j in range(R // CB):
            cols = slice(j * CB, (j + 1) * CB)
            acc_s[:, cols] = alphas[j] * acc_s[:, cols] + _dot(vt, probs[j])

    def full_tile(kt, carry):
        step(kt, False)
        return carry

    last = q0 // TK
    lax.fori_loop(0, last, full_tile, 0)
    step(last, True)
    o_s = acc_s[...] * (1.0 / l_s[...])

    WK = WINDOW + TQ
    start = pl.multiple_of(jnp.maximum(q0 - WINDOW, 0), TQ)
    wpos = start + lax.broadcasted_iota(jnp.int32, (WK, 1), 0)
    e_w, inv_w = _masked_softmax_keys(_dot_nt(kw_ref[0, pl.ds(start, WK), :], qp),
                                      (t_q - wpos).astype(jnp.uint32) < WINDOW)
    o_w =_dot(vwt_ref[:, pl.ds(start, WK)], e_w.astype(BF16)) * inv_w

    gates = ngt_ref[...]
    outs = []
    for hd in range(N_HEADS):
        cols = slice(hd * TQ, (hd + 1) * TQ)
        rows = slice((hd // HPG) * DH, (hd // HPG + 1) * DH)
        outs.append(gates[3 * hd:3 * hd + 1] * o_c[rows, cols]
                    + gates[3 * hd + 1:3 * hd + 2] * o_s[rows, cols]
                    + gates[3 * hd + 2:3 * hd + 3] * o_w[rows, cols])
    y_ref[...] = jnp.concatenate(outs, axis=0).T.astype(BF16)


def _nsa(q, kcmp, vcmp_t, ks, vs_t, kw, vw_t, ng_t, B, S):
    T = q.shape[0]
    TQ = min(NSA_Q_TILE, S)
    TK = min(NSA_K_TILE, S)
    nq = S // TQ
    NS = S // SLC_BLOCK
    NC = S // CMP_STRIDE
    n_cmp = (S - CMP_BLOCK) // CMP_STRIDE + 1
    js = np.arange(NS)[:, None] * SLC_BLOCK
    cs = np.arange(NC)[None, :] * CMP_STRIDE
    ov = np.clip(np.minimum(js + SLC_BLOCK, cs + CMP_BLOCK) - np.maximum(js, cs), 0, None) / CMP_STRIDE
    ov[:, n_cmp:] = 0.0
    assert NS <= LANE and TK % TQ == 0
    block_of_key = (np.arange(S)[:, None] // SLC_BLOCK == np.arange(LANE)[None, :]).astype(np.float32)
    eye_pad = np.eye(NS, LANE, dtype=np.float32)
    R = N_HEADS * TQ
    assert R % NSA_COL_BLOCK == 0
    seq = lambda: pl.BlockSpec((1, S, KV_WIDTH), lambda b, i: (b, 0, 0))
    seq_t = lambda: pl.BlockSpec((KV_WIDTH, S), lambda b, i: (0, b))
    return pl.pallas_call(
        functools.partial(_nsa_kernel, n_top=min(SLC_TOPK, NS), TK=TK),
        grid=(B, nq),
        in_specs=[pl.BlockSpec((TQ, N_WIDTH), lambda b, i: (b * nq + i, 0)),
                  pl.BlockSpec((1, NC, KV_WIDTH), lambda b, i: (b, 0, 0)),
                  pl.BlockSpec((1, KV_WIDTH, NC), lambda b, i: (b, 0, 0)),
                  seq(), seq_t(), seq(), seq_t(), pl.BlockSpec((ng_t.shape[0], TQ), lambda b, i: (0, b * nq + i)),
                  _full((NS, NC)), _full((S, LANE)), _full((NS, LANE))],
        out_specs=pl.BlockSpec((TQ, N_WIDTH), lambda b, i: (b * nq + i, 0)),
        out_shape=jax.ShapeDtypeStruct((T, N_WIDTH), BF16),
        scratch_shapes=[pltpu.VMEM((1, R), F32), pltpu.VMEM((1, R), F32), pltpu.VMEM((KV_WIDTH, R), F32),
                        pltpu.VMEM((NS, N_KV_GROUPS * TQ), F32)],
        compiler_params=_params(2),
        name="nsa",
    )(q, kcmp, vcmp_t, ks.reshape(B, S, KV_WIDTH), vs_t, kw.reshape(B, S, KV_WIDTH), vw_t, ng_t,
      jnp.asarray(ov.astype(np.float32)), jnp.asarray(block_of_key, dtype=BF16), jnp.asarray(eye_pad, dtype=BF16))


_ROUTER_ROWS = 40


def _post_kernel(ym_ref, yn_ref, wm_ref, wn_ref, x_ref, mod_ref, npost_ref, npre_ref, wr2_ref, br_ref,
                 utri_ref, lanes_ref, x1_ref, h2_ref, route_ref, cnt_ref, carry):
    TM = x_ref.shape[0]
    NR = _ROUTER_ROWS

    @pl.when(pl.program_id(0) == 0)
    def _():
        carry[...] = jnp.zeros_like(carry)

    mod = mod_ref[0]
    y = _dot(ym_ref[...], wm_ref[...]) + _dot(yn_ref[...], wn_ref[...])
    x1 = x_ref[...] + mod[2:3] * _rms(y, npost_ref[...])
    x1_ref[...] = x1
    h2 = _rms(x1, npre_ref[...]) * (1.0 + mod[4:5]) + mod[3:4]
    for s in range(SUBLANE):
        h2_ref[pl.ds(s, TM, stride=SUBLANE), :] = h2[:, s * LANE:(s + 1) * LANE]

    h_hi = h2.astype(BF16)
    h_lo = (h2 - h_hi.astype(F32)).astype(BF16)
    z2 = _dot_nt(wr2_ref[...], h_hi)
    logits = z2[:NR] + z2[NR:] + _dot_nt(wr2_ref[0:NR, :], h_lo) + br_ref[...]
    row = lax.broadcasted_iota(jnp.int32, (NR, TM), 0)
    is_grp = row < MOE_GROUPS
    lg = jnp.where(is_grp, logits, -jnp.inf)
    eg = jnp.exp(lg - jnp.max(lg, axis=0, keepdims=True))
    pg = eg / jnp.sum(eg, axis=0, keepdims=True)
    pg_top = jnp.max(pg, axis=0, keepdims=True)
    grp = jnp.min(jnp.where(is_grp & (pg == pg_top), row, NR), axis=0, keepdims=True)
    lo = MOE_GROUPS + EXPERTS_PER_GROUP * grp
    in_grp = (row >= lo) & (row < lo + EXPERTS_PER_GROUP)
    le = jnp.where(in_grp, logits, -jnp.inf)
    ee = jnp.exp(le - jnp.max(le, axis=0, keepdims=True))
    pe = jnp.where(in_grp, ee / jnp.sum(ee, axis=0, keepdims=True), -1.0)
    p1 = jnp.max(pe, axis=0, keepdims=True)
    i1 = jnp.min(jnp.where(pe == p1, row, NR), axis=0, keepdims=True)
    pe2 = jnp.where(row == i1, -1.0, pe)
    p2 = jnp.max(pe2, axis=0, keepdims=True)
    i2 = jnp.min(jnp.where((pe2 == p2) & in_grp & (row != i1), row, NR), axis=0, keepdims=True)
    den = p1 + p2
    oh1, oh2 = row == i1, row == i2
    oh = jnp.where(oh1, 1.0, 0.0) + jnp.where(oh2, 1.0, 0.0)
    before = _dot(oh.astype(BF16), utri_ref[...]) + carry[...]
    r1 = jnp.sum(jnp.where(oh1, before, 0.0), axis=0, keepdims=True)
    r2 = jnp.sum(jnp.where(oh2, before, 0.0), axis=0, keepdims=True)
    carry[...] = carry[...] + jnp.sum(oh, axis=1, keepdims=True)
    cnt_ref[...] = _dot_tn(jnp.broadcast_to(carry[...], (NR, SUBLANE)), lanes_ref[...], precision=HIGHEST)
    rows = ((i1 - MOE_GROUPS).astype(F32), (i2 - MOE_GROUPS).astype(F32), pg_top * p1 / den, pg_top * p2 / den, r1, r2)
    route_ref[...] = jnp.concatenate(rows + (jnp.zeros((SUBLANE - len(rows), TM), F32),), axis=0)


def _post(ym, yn, w_out, x2, mod3, norm_post, norm_pre, wr, br, S):
    T, D = x2.shape
    TM = min(TOKEN_TILE, S)
    tiles_per_seq = S // TM
    row = lambda w: pl.BlockSpec((TM, w), lambda i: (i, 0))
    small = pl.BlockSpec((SUBLANE, LANE), lambda i: (0, 0))
    NR = _ROUTER_ROWS
    utri = jnp.asarray(np.triu(np.ones((TM, TM), np.float32), 1), dtype=BF16)
    lanes = jnp.asarray(np.eye(NR, LANE, k=-MOE_GROUPS, dtype=np.float32))
    wr_hi = wr.astype(BF16)
    wr2 = jnp.concatenate([wr_hi, (wr - wr_hi.astype(F32)).astype(BF16)], axis=0)
    return pl.pallas_call(
        _post_kernel,
        grid=(T // TM,),
        in_specs=[row(M_WIDTH), row(N_WIDTH), _full((M_WIDTH, D)), _full((N_WIDTH, D)), row(D),
                  pl.BlockSpec((1, 6, D), lambda i: (i // tiles_per_seq, 0, 0)), _full((1, D)), _full((1, D)),
                  _full((2 * NR, D)), _full((NR, 1)), _full((TM, TM)), _full((NR, LANE))],
        out_specs=[row(D), pl.BlockSpec((TM * SUBLANE, LANE), lambda i: (i, 0)),
                   pl.BlockSpec((SUBLANE, TM), lambda i: (0, i)), small],
        out_shape=[jax.ShapeDtypeStruct((T, D), F32), jax.ShapeDtypeStruct((T * SUBLANE, LANE), F32),
                   jax.ShapeDtypeStruct((SUBLANE, T), F32), jax.ShapeDtypeStruct((SUBLANE, LANE), F32)],
        scratch_shapes=[pltpu.VMEM((NR, 1), F32)],
        compiler_params=_params(1),
        name="post_mix_router",
    )(ym, yn, w_out[:M_WIDTH].astype(BF16), w_out[M_WIDTH:].astype(BF16), x2, mod3,
      norm_post[None, :], norm_pre[None, :], wr2, br, utri, lanes)


def _plan_kernel(cnt_ref, pstart_ref, blk_ref, tot_ref, *, rows_per_block):
    lane = lax.broadcasted_iota(jnp.int32, (SUBLANE, LANE), 1)
    nblk = (cnt_ref[...].astype(jnp.int32) + (rows_per_block - 1)) // rows_per_block
    end = nblk
    sh = 1
    while sh < 2 * N_EXPERTS:
        end = end + jnp.where(lane >= sh, pltpu.roll(end, sh, 1), 0)
        sh *= 2
    pstart_ref[...] = (end - nblk) * rows_per_block
    nbp = blk_ref.shape[0]
    blk_io = lax.broadcasted_iota(jnp.int32, (nbp, LANE), 0)
    lane_b = lax.broadcasted_iota(jnp.int32, (nbp, LANE), 1)
    passed = jnp.where((lane_b < N_EXPERTS) & (blk_io >= end[0:1, :]), 1, 0)
    blk_ref[...] = jnp.broadcast_to(jnp.minimum(jnp.sum(passed, axis=1, keepdims=True), N_EXPERTS - 1), (nbp, LANE))
    tot_ref[...] = jnp.broadcast_to(
        jnp.sum(jnp.where(lane == N_EXPERTS - 1, end, 0), axis=1, keepdims=True), (SUBLANE, LANE))


def _sort_plan(cnt, n_blocks):
    small = pl.BlockSpec((SUBLANE, LANE), lambda i: (0, 0))
    nbp = -(-n_blocks // SUBLANE) * SUBLANE
    pstart, blk, tot = pl.pallas_call(
        functools.partial(_plan_kernel, rows_per_block=EXPERT_ROWS),
        grid=(1,),
        in_specs=[small],
        out_specs=[small, _full((nbp, LANE)), small],
        out_shape=[jax.ShapeDtypeStruct((SUBLANE, LANE), jnp.int32), jax.ShapeDtypeStruct((nbp, LANE), jnp.int32),
                   jax.ShapeDtypeStruct((SUBLANE, LANE), jnp.int32)],
        compiler_params=_params(1),
        name="expert_plan",
    )(cnt)
    return pstart[0, :N_EXPERTS + 1], blk[:n_blocks, 0], tot[0, :1]


_ASG = 4


def _dest_row(pstart_ref, asg_ref, j, k):
    n = asg_ref.shape[2] // _ASG
    return pstart_ref[asg_ref[0, 0, k * n + j]] + asg_ref[0, 0, (2 + k) * n + j]


def _row_copy(src, src_row, dst, dst_row, sem):
    return pltpu.make_async_copy(src.at[pl.ds(pl.multiple_of(src_row * SUBLANE, SUBLANE), SUBLANE)],
                                 dst.at[pl.ds(pl.multiple_of(dst_row * SUBLANE, SUBLANE), SUBLANE)], sem)


def _dispatch_kernel(pstart_ref, asg_ref, h2_ref, xs_hbm, zeros, sem, zsem, *, n_blocks):
    n = asg_ref.shape[2] // _ASG
    RB = zeros.shape[0] // SUBLANE

    @pl.when(pl.program_id(0) == 0)
    def _():
        zeros[...] = jnp.zeros_like(zeros)

        def fill(row0):
            return pltpu.make_async_copy(
                zeros, xs_hbm.at[pl.ds(pl.multiple_of(row0 * SUBLANE, SUBLANE), RB * SUBLANE)], zsem)

        def experts(act):
            def body(e, carry):
                @pl.when(pstart_ref[e + 1] > pstart_ref[e])
                def _():
                    act(fill(pstart_ref[e + 1] - RB))
                return carry
            lax.fori_loop(0, N_EXPERTS, body, 0)

        def trailing(act):
            def body(b, carry):
                act(fill(b * RB))
                return carry
            lax.fori_loop(pstart_ref[N_EXPERTS] // RB, n_blocks, body, 0)

        experts(lambda c: c.start())
        trailing(lambda c: c.start())
        experts(lambda c: c.wait())
        trailing(lambda c: c.wait())

    def issue(jj, carry):
        for u in range(ISSUE_UNROLL):
            j = jj * ISSUE_UNROLL + u
            for k in range(2):
                _row_copy(h2_ref, j, xs_hbm, _dest_row(pstart_ref, asg_ref, j, k), sem).start(priority=k)
        return carry

    lax.fori_loop(0, n // ISSUE_UNROLL, issue, 0)
    whole = xs_hbm.at[pl.ds(0, 2 * n * SUBLANE)]
    pltpu.make_async_copy(whole, whole, sem).wait()


def _dispatch(pstart, asg3, h2t, n_blocks):
    nt, _, width = asg3.shape
    n = width // _ASG
    return pl.pallas_call(
        functools.partial(_dispatch_kernel, n_blocks=n_blocks),
        grid=(nt,),
        in_specs=[pl.BlockSpec(memory_space=pltpu.SMEM),
                  pl.BlockSpec((1, 1, width), lambda i: (i, 0, 0), memory_space=pltpu.SMEM),
                  pl.BlockSpec((n * SUBLANE, LANE), lambda i: (i, 0))],
        out_specs=pl.BlockSpec(memory_space=pl.ANY),
        out_shape=jax.ShapeDtypeStruct((n_blocks * EXPERT_ROWS * SUBLANE, LANE), F32),
        scratch_shapes=[pltpu.VMEM((EXPERT_ROWS * SUBLANE, LANE), F32), pltpu.SemaphoreType.DMA(()),
                        pltpu.SemaphoreType.DMA(())],
        compiler_params=_params(1),
        name="dispatch",
    )(pstart, asg3, h2t)


def _expert_kernel(be_ref, nb_ref, xs_ref, w1_hbm, w3_hbm, w2_hbm, ys_ref, w1f, w3f, w2f, w1b, w3b, w2b, sems, turn):
    RB = xs_ref.shape[0] // SUBLANE
    i = pl.program_id(0)
    nb = nb_ref[0]

    def fetch(e, slot):
        return [pltpu.make_async_copy(src.at[e], dst.at[slot], sems.at[slot])
                for src, dst in ((w1_hbm, w1f), (w3_hbm, w3f), (w2_hbm, w2f))]

    @pl.when(i < nb)
    def _():
        e = be_ref[i]

        @pl.when(i == 0)
        def _():
            turn[0] = 0
            for c in fetch(e, 0):
                c.start()

        @pl.when((i == 0) | (e != be_ref[jnp.maximum(i - 1, 0)]))
        def _():
            slot = turn[0] % 2
            for c in fetch(e, slot):
                c.wait()
            w1b[...] = w1f[slot].astype(BF16)
            w3b[...] = w3f[slot].astype(BF16)
            w2b[...] = w2f[slot].astype(BF16)
            nxt = lax.while_loop(lambda j: (j < nb) & (be_ref[jnp.minimum(j, nb - 1)] == e), lambda j: j + 1, i + 1)

            @pl.when(nxt < nb)
            def _():
                for c in fetch(be_ref[jnp.minimum(nxt, nb - 1)], 1 - slot):
                    c.start()

            turn[0] = turn[0] + 1

        x = jnp.concatenate([xs_ref[pl.ds(s, RB, stride=SUBLANE), :] for s in range(SUBLANE)], axis=1).astype(BF16)
        hb = _silu(_dot(x, w1b[...])) * _dot(x, w3b[...])
        y = _dot(hb.astype(BF16), w2b[...])
        for s in range(SUBLANE):
            ys_ref[pl.ds(s, RB, stride=SUBLANE), :] = y[:, s * LANE:(s + 1) * LANE]

    @pl.when(i >= nb_ref[0])
    def _():
        ys_ref[...] = jnp.zeros_like(ys_ref)


def _experts(block_expert, n_used, xs, w1, w3, w2):
    n_blocks = block_expert.shape[0]
    RB = EXPERT_ROWS
    D, F = w1.shape[1], w1.shape[2]
    cur = lambda i, nb: jnp.minimum(i, nb[0] - 1)
    rows = pl.BlockSpec((RB * SUBLANE, LANE), lambda i, be, nb: (cur(i, nb), 0))
    hbm = pl.BlockSpec(memory_space=pl.ANY)
    return pl.pallas_call(
        _expert_kernel,
        grid_spec=pltpu.PrefetchScalarGridSpec(
            num_scalar_prefetch=2, grid=(n_blocks,),
            in_specs=[rows, hbm, hbm, hbm],
            out_specs=pl.BlockSpec((RB * SUBLANE, LANE), lambda i, be, nb: (i, 0)),
            scratch_shapes=[pltpu.VMEM((2, D, F), F32), pltpu.VMEM((2, D, F), F32), pltpu.VMEM((2, F, D), F32),
                            pltpu.VMEM((D, F), BF16), pltpu.VMEM((D, F), BF16), pltpu.VMEM((F, D), BF16),
                            pltpu.SemaphoreType.DMA((2,)), pltpu.SMEM((1,), jnp.int32)]),
        out_shape=jax.ShapeDtypeStruct(xs.shape, F32),
        compiler_params=_params(1),
        name="experts",
    )(block_expert, n_used, xs, w1, w3, w2)


def _combine_kernel(pstart_ref, asg_ref, asg_next_ref, ys_hbm, x1_ref, route_ref, mod_ref, nw_ref, o_ref, buf, sems):
    n = x1_ref.shape[0]
    i = pl.program_id(0)
    slot = i % 2

    def gather(dref, into):
        def issue(jj, carry):
            for u in range(ISSUE_UNROLL):
                j = jj * ISSUE_UNROLL + u
                for k in range(2):
                    _row_copy(ys_hbm, _dest_row(pstart_ref, dref, j, k), buf.at[into], k * n + j,
                              sems.at[into]).start(priority=k)
            return carry

        lax.fori_loop(0, n // ISSUE_UNROLL, issue, 0)

    @pl.when(i == 0)
    def _():
        gather(asg_ref, 0)

    @pl.when(i + 1 < pl.num_programs(0))
    def _():
        gather(asg_next_ref, 1 - slot)

    pltpu.make_async_copy(ys_hbm.at[pl.ds(0, 2 * n * SUBLANE)], buf.at[slot], sems.at[slot]).wait()

    route = jnp.concatenate([route_ref[...], jnp.zeros((LANE - SUBLANE, n), F32)], axis=0).T
    g0, g1 = route[:, 2:3], route[:, 3:4]
    y = jnp.concatenate(
        [g0 * buf[slot, pl.ds(s, n, stride=SUBLANE), :] + g1 * buf[slot, pl.ds(n * SUBLANE + s, n, stride=SUBLANE), :]
         for s in range(SUBLANE)], axis=1)
    o_ref[...] = x1_ref[...] + mod_ref[0][5:6] * _rms(y, nw_ref[...])


def _combine(pstart, asg3, ys, x1, route, mod3, norm_w, S):
    T, D = x1.shape
    nt, _, width = asg3.shape
    n = width // _ASG
    tiles_per_seq = S // n
    return pl.pallas_call(
        _combine_kernel,
        grid=(nt,),
        in_specs=[pl.BlockSpec(memory_space=pltpu.SMEM),
                  pl.BlockSpec((1, 1, width), lambda i: (i, 0, 0), memory_space=pltpu.SMEM),
                  pl.BlockSpec((1, 1, width), lambda i: (jnp.minimum(i + 1, nt - 1), 0, 0), memory_space=pltpu.SMEM),
                  pl.BlockSpec(memory_space=pl.ANY), pl.BlockSpec((n, D), lambda i: (i, 0)),
                  pl.BlockSpec((SUBLANE, n), lambda i: (0, i)),
                  pl.BlockSpec((1, 6, D), lambda i: (i // tiles_per_seq, 0, 0)), _full((1, D))],
        out_specs=pl.BlockSpec((n, D), lambda i: (i, 0)),
        out_shape=jax.ShapeDtypeStruct((T, D), F32),
        scratch_shapes=[pltpu.VMEM((2, 2 * n * SUBLANE, LANE), F32), pltpu.SemaphoreType.DMA((2,))],
        compiler_params=_params(1),
        name="combine",
    )(pstart, asg3, asg3, ys, x1, route, mod3, norm_w[None, :])


def _layer(x, c, positions, ada_w, ada_b, norm_mix_pre, norm_mix_post, norm_ffn_pre, norm_ffn_post, w_in,
           conv_w, gate_b, head_norm, cmp_k, cmp_v, w_out, wg, bg, we, be, w1, w3, w2):
    B, S, D = x.shape
    T = B * S
    x2 = x.reshape(T, D)
    mod3 = _adaln(c, ada_w, ada_b).reshape(B, 6, D)

    o_mi = 4 * M_WIDTH
    o_nq = o_mi + 2 * M_HEADS
    o_kv = o_nq + N_WIDTH
    kv = lambda i: w_in[:, o_kv + i * KV_WIDTH:o_kv + (i + 1) * KV_WIDTH]
    o_ng = o_kv + 6 * KV_WIDTH
    w_main = jnp.concatenate([w_in[:, :2 * M_WIDTH], w_in[:, o_nq:o_kv], kv(0), kv(2), kv(4), kv(1)],
                             axis=1).astype(BF16)
    w_t = jnp.concatenate([kv(3), kv(5), w_in[:, o_ng:], jnp.zeros((D, _R_MV - _R_NG - 3 * N_HEADS), F32),
                           w_in[:, 2 * M_WIDTH:o_mi]], axis=1).T.astype(BF16)
    w_gate = w_in[:, o_mi:o_nq].T
    wg_t = w_gate.astype(BF16)
    w_t = jnp.concatenate([w_t, wg_t, (w_gate - wg_t.astype(F32)).astype(BF16)], axis=0)
    half = N_HEAD_DIM // 2
    inv = ROPE_THETA ** (-jnp.arange(half, dtype=F32) / half)
    inv_row = jnp.tile(inv, LANE // half)[None, :]
    pos_col = positions.astype(F32).reshape(T, 1)

    zqk, q, kc, ks, kw, vc, vs_t, vw_t, ng_t, mv_t, mo_t, gt = _inproj(
        x2, mod3, norm_mix_pre, w_main, w_t, wg_t, gate_b, pos_col, inv_row, S)
    ym = _mlstm(zqk, mv_t, mo_t, gt, conv_w, head_norm, B, S)
    kcmp, vcmp_t = _compress(kc, vc, _compress_weights(*cmp_k), _compress_weights(*cmp_v), B, S)
    yn = _nsa(q, kcmp, vcmp_t, ks, vs_t, kw, vw_t, ng_t, B, S)

    pad = _ROUTER_ROWS - MOE_GROUPS - N_EXPERTS
    wr = jnp.concatenate([wg, we, jnp.zeros((D, pad), F32)], axis=1).T
    br = jnp.concatenate([bg, be, jnp.zeros((pad,), F32)])[:, None]
    x1, h2t, route, cnt = _post(ym, yn, w_out, x2, mod3, norm_mix_post, norm_ffn_pre, wr, br, S)

    n_blocks = (2 * T) // EXPERT_ROWS + N_EXPERTS
    pstart, block_expert, n_used = _sort_plan(cnt, n_blocks)
    n_tok = min(GATHER_TILE, S)
    asg = jnp.concatenate([route[0:2], route[4:6]], axis=0).astype(jnp.int32)
    asg3 = asg.reshape(_ASG, T // n_tok, n_tok).transpose(1, 0, 2).reshape(T // n_tok, 1, _ASG * n_tok)
    xs = _dispatch(pstart, asg3, h2t, n_blocks)
    ys = _experts(block_expert, n_used, xs, w1, w3, w2)
    out = _combine(pstart, asg3, ys, x1, route, mod3, norm_ffn_post, S)
    return out.reshape(B, S, D)


def kernel(x, c, positions, ada_w, ada_b, norm_mix_pre, norm_mix_post, norm_ffn_pre, norm_ffn_post, w_in, mlstm_conv_w, mlstm_gate_b, mlstm_head_norm, cmp_pe_k, cmp_w1_k, cmp_w2_k, cmp_pe_v, cmp_w1_v, cmp_w2_v, w_out, router_grp_w, router_grp_b, router_exp_w, router_exp_b, expert_w1, expert_w3, expert_w2):
    for l in range(ada_w.shape[0]):
        x = _layer(x, c, positions, ada_w[l], ada_b[l], norm_mix_pre[l], norm_mix_post[l], norm_ffn_pre[l],
                   norm_ffn_post[l], w_in[l], mlstm_conv_w[l], mlstm_gate_b[l], mlstm_head_norm[l],
                   (cmp_pe_k[l], cmp_w1_k[l], cmp_w2_k[l]), (cmp_pe_v[l], cmp_w1_v[l], cmp_w2_v[l]), w_out[l],
                   router_grp_w[l], router_grp_b[l], router_exp_w[l], router_exp_b[l],
                   expert_w1[l], expert_w3[l], expert_w2[l])
    return x
```

```python
import functools

import jax
import jax.numpy as jnp
import numpy as np
from jax import lax
from jax.experimental import pallas as pl
from jax.experimental.pallas import tpu as pltpu

M_HEADS = 4
M_HEAD_DIM = 128
M_WIDTH = M_HEADS * M_HEAD_DIM
M_CONV = 4
N_HEADS = 8
N_KV_GROUPS = 2
HEADS_PER_GROUP = N_HEADS // N_KV_GROUPS
N_HEAD_DIM = 64
N_WIDTH = N_HEADS * N_HEAD_DIM
KV_WIDTH = N_KV_GROUPS * N_HEAD_DIM
CMP_BLOCK = 32
CMP_STRIDE = 16
CMP_HIDDEN = 2 * N_HEAD_DIM
SLC_BLOCK = 64
SLC_TOPK = 16
WINDOW = 512
ROPE_THETA = 10000.0
MOE_GROUPS = 4
EXPERTS_PER_GROUP = 8
N_EXPERTS = MOE_GROUPS * EXPERTS_PER_GROUP
RMS_EPS = 1e-6
LOG2_E = 1.4426950408889634

LANE = 128
SUBLANE = 8
VMEM_LIMIT_BYTES = 56 * 1024 * 1024

TOKEN_TILE = 512
MLSTM_CHUNK = 256
NSA_Q_TILE = 256
NSA_K_TILE = 512
NSA_COL_BLOCK = 256
EXPERT_ROWS = 512
GATHER_TILE = 256
ISSUE_UNROLL = 8

F32 = jnp.float32
BF16 = jnp.bfloat16
NEG = -1e30
HIGHEST = lax.Precision.HIGHEST
NT_DIMS = (((1,), (1,)), ((), ()))
TN_DIMS = (((0,), (0,)), ((), ()))


def _params(n_grid):
    return pltpu.CompilerParams(
        dimension_semantics=("arbitrary",) * n_grid, vmem_limit_bytes=VMEM_LIMIT_BYTES)


def _dot(a, b, **kw):
    return jnp.dot(a, b, preferred_element_type=F32, **kw)


def _dot_nt(a, b, **kw):
    return lax.dot_general(a, b, NT_DIMS, preferred_element_type=F32, **kw)


def _dot_tn(a, b, **kw):
    return lax.dot_general(a, b, TN_DIMS, preferred_element_type=F32, **kw)


def _sigmoid(x):
    return 1.0 / (1.0 + jnp.exp(-x))


def _silu(x):
    return x * _sigmoid(x)


def _split3(x):
    hi = x.astype(BF16)
    r1 = x - hi.astype(F32)
    mid = r1.astype(BF16)
    return [hi, mid, (r1 - mid.astype(F32)).astype(BF16)]


def _rms(x, w):
    return x * lax.rsqrt(jnp.mean(x * x, axis=-1, keepdims=True) + RMS_EPS) * w


def _full(shape):
    return pl.BlockSpec(shape, lambda *_: (0,) * len(shape))


def _adaln_kernel(c_ref, w_ref, b_ref, o_ref):
    o_ref[...] = _dot(_silu(c_ref[...]), w_ref[...], precision=HIGHEST) + b_ref[...]


def _adaln(c, ada_w, ada_b):
    B, D = c.shape
    n = ada_w.shape[1] // D
    return pl.pallas_call(
        _adaln_kernel,
        grid=(n,),
        in_specs=[_full((B, D)), pl.BlockSpec((D, D), lambda j: (0, j)), pl.BlockSpec((1, D), lambda j: (0, j))],
        out_specs=pl.BlockSpec((B, D), lambda j: (0, j)),
        out_shape=jax.ShapeDtypeStruct((B, ada_w.shape[1]), F32),
        compiler_params=_params(1),
        name="adaln",
    )(c, ada_w, ada_b[None, :])


_C_M = 0
_C_Q = 2 * M_WIDTH
_C_KC = _C_Q + N_WIDTH
_C_KS = _C_KC + KV_WIDTH
_C_KW = _C_KS + KV_WIDTH
_C_VC = _C_KW + KV_WIDTH
_C_END = _C_VC + KV_WIDTH
_R_VS = 0
_R_VW = KV_WIDTH
_R_NG = 2 * KV_WIDTH
_R_MV = _R_NG + 4 * SUBLANE
_R_MO = _R_MV + M_WIDTH
_R_GH = _R_MO + M_WIDTH
_R_GL = _R_GH + 2 * M_HEADS
_R_END = _R_GL + 2 * M_HEADS


def _inproj_kernel(x_ref, mod_ref, nw_ref, w_ref, wt_ref, wg_ref, gb_ref, pos_ref, inv_ref,
                   zm_ref, q_ref, kc_ref, ks_ref, kw_ref, vc_ref, vst_ref, vwt_ref, ngt_ref, mvt_ref, mot_ref, gt_ref):
    mod = mod_ref[0]
    h = _rms(x_ref[...], nw_ref[...]) * (1.0 + mod[1:2]) + mod[0:1]
    hb = h.astype(BF16)
    z = _dot(hb, w_ref[...])
    zm_ref[...] = z[:, _C_M:_C_Q]
    zt = _dot_nt(wt_ref[...], hb)
    h_lo = (h - hb.astype(F32)).astype(BF16)
    gt_ref[...] = (zt[_R_GH:_R_GL] + zt[_R_GL:_R_END] + _dot_nt(wg_ref[...], h_lo)) + gb_ref[...]
    vst_ref[...] = zt[_R_VS:_R_VW].astype(BF16)
    vwt_ref[...] = zt[_R_VW:_R_NG].astype(BF16)
    ngt_ref[...] = _sigmoid(zt[_R_NG:_R_MV])
    mvt_ref[...] = zt[_R_MV:_R_MO].astype(BF16)
    mot_ref[...] = zt[_R_MO:_R_GH]

    ang = pos_ref[...] * inv_ref[...]
    cos = jnp.cos(ang)
    sin = jnp.sin(ang)
    lane = lax.broadcasted_iota(jnp.int32, (1, LANE), 1)
    first = (lane % N_HEAD_DIM) < (N_HEAD_DIM // 2)
    sin_signed = jnp.where(first, -sin, sin)

    def rope(slab):
        rot = jnp.where(first, pltpu.roll(slab, LANE - N_HEAD_DIM // 2, 1), pltpu.roll(slab, N_HEAD_DIM // 2, 1))
        return slab * cos + rot * sin_signed

    scale = N_HEAD_DIM ** -0.5 * LOG2_E
    for j in range(N_WIDTH // LANE):
        q_ref[:, j * LANE:(j + 1) * LANE] = (rope(z[:, _C_Q + j * LANE:_C_Q + (j + 1) * LANE]) * scale).astype(BF16)
    kc_ref[...] = rope(z[:, _C_KC:_C_KS])
    ks_ref[...] = rope(z[:, _C_KS:_C_KW]).astype(BF16)
    kw_ref[...] = rope(z[:, _C_KW:_C_VC]).astype(BF16)
    vc_ref[...] = z[:, _C_VC:_C_END]


def _inproj(x2, mod3, norm_w, w_main, w_t, wg_t, gate_b, pos_col, inv_row, S):
    T, D = x2.shape
    TM = min(TOKEN_TILE, S)
    tiles_per_seq = S // TM
    row = lambda w: pl.BlockSpec((TM, w), lambda i: (i, 0))
    col = lambda r: pl.BlockSpec((r, TM), lambda i: (0, i))
    outs = [(2 * M_WIDTH, F32), (N_WIDTH, BF16), (KV_WIDTH, F32), (KV_WIDTH, BF16), (KV_WIDTH, BF16), (KV_WIDTH, F32)]
    outs_t = [(KV_WIDTH, BF16), (KV_WIDTH, BF16), (_R_MV - _R_NG, F32), (M_WIDTH, BF16), (M_WIDTH, F32),
              (2 * M_HEADS, F32)]
    return pl.pallas_call(
        _inproj_kernel,
        grid=(T // TM,),
        in_specs=[row(D), pl.BlockSpec((1, 6, D), lambda i: (i // tiles_per_seq, 0, 0)), _full((1, D)),
                  _full(w_main.shape), _full(w_t.shape), _full(wg_t.shape), _full((2 * M_HEADS, 1)), row(1),
                  _full((1, LANE))],
        out_specs=[row(w) for w, _ in outs] + [col(r) for r, _ in outs_t],
        out_shape=[jax.ShapeDtypeStruct((T, w), dt) for w, dt in outs]
        + [jax.ShapeDtypeStruct((r, T), dt) for r, dt in outs_t],
        compiler_params=_params(1),
        name="inproj",
    )(x2, mod3, norm_w[None, :], w_main, w_t, wg_t, gate_b[:, None], pos_col, inv_row)


def _mlstm_kernel(zm_ref, vt_ref, ot_ref, gt_ref, cw_ref, hn_ref, ltri_ref, y_ref, buf, c_s, n_s, m_s):
    L = zm_ref.shape[0]
    QK = 2 * M_WIDTH
    DH = M_HEAD_DIM

    @pl.when(pl.program_id(1) == 0)
    def _():
        buf[0:SUBLANE, :] = jnp.zeros((SUBLANE, QK), F32)
        c_s[...] = jnp.zeros_like(c_s)
        n_s[...] = jnp.zeros_like(n_s)
        m_s[...] = jnp.zeros_like(m_s)

    buf[SUBLANE:SUBLANE + L, :] = zm_ref[:, 0:QK]
    cw = cw_ref[...]
    conv = cw[M_CONV - 1:M_CONV] * buf[SUBLANE:SUBLANE + L, :]
    for j in range(M_CONV - 1):
        off = SUBLANE - (M_CONV - 1) + j
        conv = conv + cw[j:j + 1] * buf[off:off + L, :]
    buf[0:SUBLANE, :] = buf[L:L + SUBLANE, :]
    qk = _silu(conv)

    g = gt_ref[...]
    fp = g[M_HEADS:]
    lf = jnp.minimum(fp, 0.0) - jnp.log(1.0 + jnp.exp(-jnp.abs(fp)))
    cs = _dot_nt(jnp.concatenate(_split3(lf) + [jnp.zeros((M_HEADS, L), BF16)], axis=0), ltri_ref[...])
    b_rows = cs[0:M_HEADS] + cs[M_HEADS:2 * M_HEADS] + cs[2 * M_HEADS:3 * M_HEADS]
    src_rows = b_rows - g[:M_HEADS]

    causal = lax.broadcasted_iota(jnp.int32, (L, L), 0) <= lax.broadcasted_iota(jnp.int32, (L, L), 1)
    ones_rows = jnp.ones((2 * SUBLANE, L), F32)
    ones3 = jnp.ones((3, L), BF16)
    zeros10 = jnp.zeros((2 * SUBLANE - 6, L), BF16)
    outs = []
    for h in range(M_HEADS):
        hs = slice(h * DH, (h + 1) * DH)
        b_r, i_r = b_rows[h:h + 1], g[h:h + 1]
        m_prev = m_s[h:h + 1, 0:1]
        g_tot = b_r[:, L - 1:L]
        a_r = g_tot - b_r + i_r
        m_new = jnp.maximum(g_tot + m_prev, jnp.max(a_r, axis=1, keepdims=True))

        qb = qk[:, hs].astype(BF16)
        kb = (qk[:, M_WIDTH + h * DH:M_WIDTH + (h + 1) * DH] * (DH ** -0.5)).astype(BF16)
        vt = vt_ref[hs, :]

        lhs_t = jnp.concatenate(_split3(-src_rows[h:h + 1]) + [ones3, zeros10], axis=0)
        rhs_t = jnp.concatenate([ones3] + _split3(b_r) + [zeros10], axis=0)
        dlog = jnp.where(causal, _dot_tn(lhs_t, rhs_t), -jnp.inf)
        inter = b_r + m_prev
        m_t = jnp.maximum(inter, jnp.max(dlog, axis=0, keepdims=True))
        wts = jnp.exp(dlog - m_t) * _dot_nt(kb, qb)
        dec = jnp.exp(inter - m_t)
        ct_prev = c_s[h]
        n_prev = n_s[...]
        num = _dot(vt, wts.astype(BF16)) + dec * _dot_nt(ct_prev.astype(BF16), qb)
        qn = _dot_nt(n_prev.astype(BF16), qb)[h:h + 1]
        den = jnp.sum(wts, axis=0, keepdims=True) + dec * qn
        hh = num * (1.0 / jnp.maximum(jnp.abs(den), jnp.exp(-m_t)))

        w_r = jnp.exp(a_r - m_new)
        keep = jnp.exp(g_tot + m_prev - m_new)
        lhs = jnp.concatenate([vt.astype(F32), ones_rows], axis=0) * w_r
        upd = _dot(lhs.astype(BF16), kb)
        c_s[h] = keep * ct_prev + upd[:DH]
        n_s[h:h + 1] = keep * n_prev[h:h + 1] + upd[DH:DH + 1]
        m_s[h:h + 1] = jnp.broadcast_to(m_new, (1, LANE))

        hn = hh * lax.rsqrt(jnp.mean(hh * hh, axis=0, keepdims=True) + RMS_EPS)
        hn = hn * jnp.concatenate([hn_ref[hs, :]] * (L // LANE), axis=1)
        outs.append(_sigmoid(ot_ref[hs, :]) * hn)
    y_ref[...] = jnp.concatenate(outs, axis=0).T.astype(BF16)


def _mlstm(zqk, v_t, o_t, gt, conv_w, head_norm, B, S):
    T = zqk.shape[0]
    L = min(MLSTM_CHUNK, S)
    assert L % LANE == 0
    nc = S // L
    ltri = jnp.asarray(np.tril(np.ones((L, L), np.float32)), dtype=BF16)
    col = lambda r: pl.BlockSpec((r, L), lambda b, c: (0, b * nc + c))
    return pl.pallas_call(
        _mlstm_kernel,
        grid=(B, nc),
        in_specs=[pl.BlockSpec((L, 2 * M_WIDTH), lambda b, c: (b * nc + c, 0)), col(M_WIDTH), col(M_WIDTH),
                  col(2 * M_HEADS), _full((M_CONV, 2 * M_WIDTH)), _full((M_WIDTH, LANE)), _full((L, L))],
        out_specs=pl.BlockSpec((L, M_WIDTH), lambda b, c: (b * nc + c, 0)),
        out_shape=jax.ShapeDtypeStruct((T, M_WIDTH), BF16),
        scratch_shapes=[pltpu.VMEM((L + SUBLANE, 2 * M_WIDTH), F32),
                        pltpu.VMEM((M_HEADS, M_HEAD_DIM, M_HEAD_DIM), F32),
                        pltpu.VMEM((SUBLANE, M_HEAD_DIM), F32),
                        pltpu.VMEM((SUBLANE, LANE), F32)],
        compiler_params=_params(2),
        name="mlstm",
    )(zqk, v_t, o_t, gt, conv_w, jnp.broadcast_to(head_norm[:, None], (M_WIDTH, LANE)), ltri)


def _compress_kernel(k_ref, v_ref, pak, pbk, wak, wbk, w2k, pav, pbv, wav, wbv, w2v, ko_ref, vo_ref):
    def one(x_ref, pa, pb, wa, wb, w2, o_ref, transposed):
        n = x_ref.shape[0] // CMP_STRIDE
        x = jnp.concatenate([x_ref[pl.ds(t, n, stride=CMP_STRIDE), :] for t in range(CMP_STRIDE)], axis=1)
        first = _dot((x + pa[...]).astype(BF16), wa[...])
        second = _dot((x + pb[...]).astype(BF16), wb[...])
        hid = _silu(first + pltpu.roll(second, n - 1, 0)).astype(BF16)
        o_ref[0] = (_dot_nt(w2[...], hid) if transposed else _dot(hid, w2[...])).astype(BF16)

    one(k_ref, pak, pbk, wak, wbk, w2k, ko_ref, False)
    one(v_ref, pav, pbv, wav, wbv, w2v, vo_ref, True)


def _compress_weights(pe, w1, w2):
    half = CMP_BLOCK // 2
    eye = jnp.eye(N_KV_GROUPS, dtype=F32)
    w1r = w1.reshape(CMP_BLOCK, N_HEAD_DIM, CMP_HIDDEN)
    big = lambda w: jnp.einsum("ldc,gh->lgdhc", w, eye).reshape(half * KV_WIDTH, N_KV_GROUPS * CMP_HIDDEN).astype(BF16)
    pe_row = lambda p: jnp.broadcast_to(p[:, None, :], (half, N_KV_GROUPS, N_HEAD_DIM)).reshape(1, half * KV_WIDTH)
    w2bd = jnp.einsum("cd,gh->gchd", w2, eye).reshape(N_KV_GROUPS * CMP_HIDDEN, KV_WIDTH).astype(BF16)
    return pe_row(pe[:half]), pe_row(pe[half:]), big(w1r[:half]), big(w1r[half:]), w2bd


def _compress(kc, vc, wk, wv, B, S):
    nc = S // CMP_STRIDE
    blk = pl.BlockSpec((S, KV_WIDTH), lambda b: (b, 0))
    wv = wv[:4] + (wv[4].T,)
    return pl.pallas_call(
        _compress_kernel,
        grid=(B,),
        in_specs=[blk, blk] + [_full(w.shape) for w in wk] + [_full(w.shape) for w in wv],
        out_specs=[pl.BlockSpec((1, nc, KV_WIDTH), lambda b: (b, 0, 0)),
                   pl.BlockSpec((1, KV_WIDTH, nc), lambda b: (b, 0, 0))],
        out_shape=[jax.ShapeDtypeStruct((B, nc, KV_WIDTH), BF16), jax.ShapeDtypeStruct((B, KV_WIDTH, nc), BF16)],
        compiler_params=_params(1),
        name="compress",
    )(kc, vc, *wk, *wv)


def _masked_softmax_keys(s, mask):
    s = jnp.where(mask, s, -jnp.inf)
    m = jnp.max(s, axis=0, keepdims=True)
    m = jnp.where(jnp.isfinite(m), m, 0.0)
    e = jnp.exp2(s - m)
    return e, 1.0 / jnp.maximum(jnp.sum(e, axis=0, keepdims=True), 1e-30)


def _nsa_kernel(q_ref, kc_ref, vct_ref, ks_ref, vst_ref, kw_ref, vwt_ref, ngt_ref, ov_ref, ek_ref, eye_ref,
                y_ref, m_s, l_s, acc_s, rank_s, *, n_top, TK):
    TQ = q_ref.shape[0]
    NS = ov_ref.shape[0]
    G, HPG, DH = N_KV_GROUPS, HEADS_PER_GROUP, N_HEAD_DIM
    R = G * HPG * TQ
    q0 = pl.program_id(1) * TQ

    q = q_ref[...]
    zero = jnp.zeros((TQ, DH), BF16)
    parts = []
    for hd in range(N_HEADS):
        qh = q[:, hd * DH:(hd + 1) * DH]
        parts.append(jnp.concatenate([qh, zero] if hd < HPG else [zero, qh], axis=1))
    qp = jnp.concatenate(parts, axis=0)
    t_q = q0 + lax.broadcasted_iota(jnp.int32, (1, R), 1) % TQ

    kc = kc_ref[0]
    NC = kc.shape[0]
    cmp_end = lax.broadcasted_iota(jnp.int32, (NC, 1), 0) * CMP_STRIDE + (CMP_BLOCK - 1)
    e_c, inv_c = _masked_softmax_keys(_dot_nt(kc, qp), cmp_end <= t_q)
    p_c = e_c * inv_c
    o_c = _dot(vct_ref[0], p_c.astype(BF16))
    p_grp = jnp.concatenate(
        [sum(p_c[:, (g * HPG + h) * TQ:(g * HPG + h + 1) * TQ] for h in range(HPG)) for g in range(G)], axis=1)
    imp = _dot(ov_ref[...], p_grp, precision=HIGHEST)

    j_io = lax.broadcasted_iota(jnp.int32, (NS, G * TQ), 0)
    t_row = q0 + lax.broadcasted_iota(jnp.int32, (NS, G * TQ), 1) % TQ
    cur = t_row // SLC_BLOCK
    forced = (j_io == 0) | (j_io == cur) | (j_io == cur - 1)
    future = j_io * SLC_BLOCK > t_row
    score = jnp.where(forced, jnp.inf, jnp.where(future, -jnp.inf, imp))
    rank_s[...] = jnp.zeros_like(rank_s)
    n_rival = (q0 + TQ - 1) // SLC_BLOCK + 1
    for c0 in range(0, NS, SUBLANE):
        @pl.when(c0 < n_rival)
        def _():
            part = jnp.zeros((NS, G * TQ), F32)
            for jp in range(c0, c0 + SUBLANE):
                other = score[jp:jp + 1, :]
                part = part + jnp.where(j_io > jp, jnp.where(other >= score, 1.0, 0.0),
                                        jnp.where(other > score, 1.0, 0.0))
            rank_s[...] = rank_s[...] + part
    rank = rank_s[...]
    sel_bias = jnp.where(future, NEG, jnp.where(rank < n_top, 0.0, NEG)).astype(BF16)
    sel_rows = _dot_tn(sel_bias, eye_ref[...]).astype(BF16)
    sel_rows = jnp.broadcast_to(sel_rows.reshape(G, 1, TQ, LANE), (G, HPG, TQ, LANE)).reshape(R, LANE)
    qa = jnp.concatenate([qp, sel_rows], axis=1)

    m_s[...] = jnp.full_like(m_s, NEG)
    l_s[...] = jnp.zeros_like(l_s)
    acc_s[...] = jnp.zeros_like(acc_s)

    CB = NSA_COL_BLOCK

    def step(kt, causal):
        k0 = pl.multiple_of(kt * TK, TK)
        ka = jnp.concatenate([ks_ref[0, pl.ds(k0, TK), :], ek_ref[pl.ds(k0, TK), :]], axis=1)
        vt = vst_ref[:, pl.ds(k0, TK)]
        kpos = k0 + lax.broadcasted_iota(jnp.int32, (TK, 1), 0)
        scores = [_dot_nt(ka, qa[j * CB:(j + 1) * CB]) for j in range(R // CB)]
        probs, alphas = [], []
        for j in range(R // CB):
            cols = slice(j * CB, (j + 1) * CB)
            s = scores[j]
            if causal:
                s = jnp.where(kpos <= t_q[:, cols], s, NEG)
            m_old = m_s[:, cols]
            m_new = jnp.maximum(m_old, jnp.max(s, axis=0, keepdims=True))
            alpha = jnp.exp2(m_old - m_new)
            p = jnp.exp2(s - m_new)
            l_s[:, cols] = alpha * l_s[:, cols] + jnp.sum(p, axis=0, keepdims=True)
            m_s[:, cols] = m_new
            probs.append(p.astype(BF16))
            alphas.append(alpha)
        for j in range(R // CB):
            cols = slice(j * CB, (j + 1) * CB)
            acc_s[:, cols] = alphas[j] * acc_s[:, cols] + _dot(vt, probs[j])

    def full_tile(kt, carry):
        step(kt, False)
        return carry

    last = q0 // TK
    lax.fori_loop(0, last, full_tile, 0)
    step(last, True)
    o_s = acc_s[...] * (1.0 / l_s[...])

    WK = WINDOW + TQ
    start = pl.multiple_of(jnp.maximum(q0 - WINDOW, 0), TQ)
    wpos = start + lax.broadcasted_iota(jnp.int32, (WK, 1), 0)
    e_w, inv_w = _masked_softmax_keys(_dot_nt(kw_ref[0, pl.ds(start, WK), :], qp),
                                      (t_q - wpos).astype(jnp.uint32) < WINDOW)
    o_w =_dot(vwt_ref[:, pl.ds(start, WK)], e_w.astype(BF16)) * inv_w

    gates = ngt_ref[...]
    outs = []
    for hd in range(N_HEADS):
        cols = slice(hd * TQ, (hd + 1) * TQ)
        rows = slice((hd // HPG) * DH, (hd // HPG + 1) * DH)
        outs.append(gates[3 * hd:3 * hd + 1] * o_c[rows, cols]
                    + gates[3 * hd + 1:3 * hd + 2] * o_s[rows, cols]
                    + gates[3 * hd + 2:3 * hd + 3] * o_w[rows, cols])
    y_ref[...] = jnp.concatenate(outs, axis=0).T.astype(BF16)


def _nsa(q, kcmp, vcmp_t, ks, vs_t, kw, vw_t, ng_t, B, S):
    T = q.shape[0]
    TQ = min(NSA_Q_TILE, S)
    TK = min(NSA_K_TILE, S)
    nq = S // TQ
    NS = S // SLC_BLOCK
    NC = S // CMP_STRIDE
    n_cmp = (S - CMP_BLOCK) // CMP_STRIDE + 1
    js = np.arange(NS)[:, None] * SLC_BLOCK
    cs = np.arange(NC)[None, :] * CMP_STRIDE
    ov = np.clip(np.minimum(js + SLC_BLOCK, cs + CMP_BLOCK) - np.maximum(js, cs), 0, None) / CMP_STRIDE
    ov[:, n_cmp:] = 0.0
    assert NS <= LANE and TK % TQ == 0
    block_of_key = (np.arange(S)[:, None] // SLC_BLOCK == np.arange(LANE)[None, :]).astype(np.float32)
    eye_pad = np.eye(NS, LANE, dtype=np.float32)
    R = N_HEADS * TQ
    assert R % NSA_COL_BLOCK == 0
    seq = lambda: pl.BlockSpec((1, S, KV_WIDTH), lambda b, i: (b, 0, 0))
    seq_t = lambda: pl.BlockSpec((KV_WIDTH, S), lambda b, i: (0, b))
    return pl.pallas_call(
        functools.partial(_nsa_kernel, n_top=min(SLC_TOPK, NS), TK=TK),
        grid=(B, nq),
        in_specs=[pl.BlockSpec((TQ, N_WIDTH), lambda b, i: (b * nq + i, 0)),
                  pl.BlockSpec((1, NC, KV_WIDTH), lambda b, i: (b, 0, 0)),
                  pl.BlockSpec((1, KV_WIDTH, NC), lambda b, i: (b, 0, 0)),
                  seq(), seq_t(), seq(), seq_t(), pl.BlockSpec((ng_t.shape[0], TQ), lambda b, i: (0, b * nq + i)),
                  _full((NS, NC)), _full((S, LANE)), _full((NS, LANE))],
        out_specs=pl.BlockSpec((TQ, N_WIDTH), lambda b, i: (b * nq + i, 0)),
        out_shape=jax.ShapeDtypeStruct((T, N_WIDTH), BF16),
        scratch_shapes=[pltpu.VMEM((1, R), F32), pltpu.VMEM((1, R), F32), pltpu.VMEM((KV_WIDTH, R), F32),
                        pltpu.VMEM((NS, N_KV_GROUPS * TQ), F32)],
        compiler_params=_params(2),
        name="nsa",
    )(q, kcmp, vcmp_t, ks.reshape(B, S, KV_WIDTH), vs_t, kw.reshape(B, S, KV_WIDTH), vw_t, ng_t,
      jnp.asarray(ov.astype(np.float32)), jnp.asarray(block_of_key, dtype=BF16), jnp.asarray(eye_pad, dtype=BF16))


_ROUTER_ROWS = 40


def _post_kernel(ym_ref, yn_ref, wm_ref, wn_ref, x_ref, mod_ref, npost_ref, npre_ref, wr2_ref, br_ref,
                 utri_ref, lanes_ref, x1_ref, h2_ref, route_ref, cnt_ref, carry):
    TM = x_ref.shape[0]
    NR = _ROUTER_ROWS

    @pl.when(pl.program_id(0) == 0)
    def _():
        carry[...] = jnp.zeros_like(carry)

    mod = mod_ref[0]
    y = _dot(ym_ref[...], wm_ref[...]) + _dot(yn_ref[...], wn_ref[...])
    x1 = x_ref[...] + mod[2:3] * _rms(y, npost_ref[...])
    x1_ref[...] = x1
    h2 = _rms(x1, npre_ref[...]) * (1.0 + mod[4:5]) + mod[3:4]
    for s in range(SUBLANE):
        h2_ref[pl.ds(s, TM, stride=SUBLANE), :] = h2[:, s * LANE:(s + 1) * LANE]

    h_hi = h2.astype(BF16)
    h_lo = (h2 - h_hi.astype(F32)).astype(BF16)
    z2 = _dot_nt(wr2_ref[...], h_hi)
    logits = z2[:NR] + z2[NR:] + _dot_nt(wr2_ref[0:NR, :], h_lo) + br_ref[...]
    row = lax.broadcasted_iota(jnp.int32, (NR, TM), 0)
    is_grp = row < MOE_GROUPS
    lg = jnp.where(is_grp, logits, -jnp.inf)
    eg = jnp.exp(lg - jnp.max(lg, axis=0, keepdims=True))
    pg = eg / jnp.sum(eg, axis=0, keepdims=True)
    pg_top = jnp.max(pg, axis=0, keepdims=True)
    grp = jnp.min(jnp.where(is_grp & (pg == pg_top), row, NR), axis=0, keepdims=True)
    lo = MOE_GROUPS + EXPERTS_PER_GROUP * grp
    in_grp = (row >= lo) & (row < lo + EXPERTS_PER_GROUP)
    le = jnp.where(in_grp, logits, -jnp.inf)
    ee = jnp.exp(le - jnp.max(le, axis=0, keepdims=True))
    pe = jnp.where(in_grp, ee / jnp.sum(ee, axis=0, keepdims=True), -1.0)
    p1 = jnp.max(pe, axis=0, keepdims=True)
    i1 = jnp.min(jnp.where(pe == p1, row, NR), axis=0, keepdims=True)
    pe2 = jnp.where(row == i1, -1.0, pe)
    p2 = jnp.max(pe2, axis=0, keepdims=True)
    i2 = jnp.min(jnp.where((pe2 == p2) & in_grp & (row != i1), row, NR), axis=0, keepdims=True)
    den = p1 + p2
    oh1, oh2 = row == i1, row == i2
    oh = jnp.where(oh1, 1.0, 0.0) + jnp.where(oh2, 1.0, 0.0)
    before = _dot(oh.astype(BF16), utri_ref[...]) + carry[...]
    r1 = jnp.sum(jnp.where(oh1, before, 0.0), axis=0, keepdims=True)
    r2 = jnp.sum(jnp.where(oh2, before, 0.0), axis=0, keepdims=True)
    carry[...] = carry[...] + jnp.sum(oh, axis=1, keepdims=True)
    cnt_ref[...] = _dot_tn(jnp.broadcast_to(carry[...], (NR, SUBLANE)), lanes_ref[...], precision=HIGHEST)
    rows = ((i1 - MOE_GROUPS).astype(F32), (i2 - MOE_GROUPS).astype(F32), pg_top * p1 / den, pg_top * p2 / den, r1, r2)
    route_ref[...] = jnp.concatenate(rows + (jnp.zeros((SUBLANE - len(rows), TM), F32),), axis=0)


def _post(ym, yn, w_out, x2, mod3, norm_post, norm_pre, wr, br, S):
    T, D = x2.shape
    TM = min(TOKEN_TILE, S)
    tiles_per_seq = S // TM
    row = lambda w: pl.BlockSpec((TM, w), lambda i: (i, 0))
    small = pl.BlockSpec((SUBLANE, LANE), lambda i: (0, 0))
    NR = _ROUTER_ROWS
    utri = jnp.asarray(np.triu(np.ones((TM, TM), np.float32), 1), dtype=BF16)
    lanes = jnp.asarray(np.eye(NR, LANE, k=-MOE_GROUPS, dtype=np.float32))
    wr_hi = wr.astype(BF16)
    wr2 = jnp.concatenate([wr_hi, (wr - wr_hi.astype(F32)).astype(BF16)], axis=0)
    return pl.pallas_call(
        _post_kernel,
        grid=(T // TM,),
        in_specs=[row(M_WIDTH), row(N_WIDTH), _full((M_WIDTH, D)), _full((N_WIDTH, D)), row(D),
                  pl.BlockSpec((1, 6, D), lambda i: (i // tiles_per_seq, 0, 0)), _full((1, D)), _full((1, D)),
                  _full((2 * NR, D)), _full((NR, 1)), _full((TM, TM)), _full((NR, LANE))],
        out_specs=[row(D), pl.BlockSpec((TM * SUBLANE, LANE), lambda i: (i, 0)),
                   pl.BlockSpec((SUBLANE, TM), lambda i: (0, i)), small],
        out_shape=[jax.ShapeDtypeStruct((T, D), F32), jax.ShapeDtypeStruct((T * SUBLANE, LANE), F32),
                   jax.ShapeDtypeStruct((SUBLANE, T), F32), jax.ShapeDtypeStruct((SUBLANE, LANE), F32)],
        scratch_shapes=[pltpu.VMEM((NR, 1), F32)],
        compiler_params=_params(1),
        name="post_mix_router",
    )(ym, yn, w_out[:M_WIDTH].astype(BF16), w_out[M_WIDTH:].astype(BF16), x2, mod3,
      norm_post[None, :], norm_pre[None, :], wr2, br, utri, lanes)


def _plan_kernel(cnt_ref, pstart_ref, blk_ref, tot_ref, *, rows_per_block):
    lane = lax.broadcasted_iota(jnp.int32, (SUBLANE, LANE), 1)
    nblk = (cnt_ref[...].astype(jnp.int32) + (rows_per_block - 1)) // rows_per_block
    end = nblk
    sh = 1
    while sh < 2 * N_EXPERTS:
        end = end + jnp.where(lane >= sh, pltpu.roll(end, sh, 1), 0)
        sh *= 2
    pstart_ref[...] = (end - nblk) * rows_per_block
    nbp = blk_ref.shape[0]
    blk_io = lax.broadcasted_iota(jnp.int32, (nbp, LANE), 0)
    lane_b = lax.broadcasted_iota(jnp.int32, (nbp, LANE), 1)
    passed = jnp.where((lane_b < N_EXPERTS) & (blk_io >= end[0:1, :]), 1, 0)
    blk_ref[...] = jnp.broadcast_to(jnp.minimum(jnp.sum(passed, axis=1, keepdims=True), N_EXPERTS - 1), (nbp, LANE))
    tot_ref[...] = jnp.broadcast_to(
        jnp.sum(jnp.where(lane == N_EXPERTS - 1, end, 0), axis=1, keepdims=True), (SUBLANE, LANE))


def _sort_plan(cnt, n_blocks):
    small = pl.BlockSpec((SUBLANE, LANE), lambda i: (0, 0))
    nbp = -(-n_blocks // SUBLANE) * SUBLANE
    pstart, blk, tot = pl.pallas_call(
        functools.partial(_plan_kernel, rows_per_block=EXPERT_ROWS),
        grid=(1,),
        in_specs=[small],
        out_specs=[small, _full((nbp, LANE)), small],
        out_shape=[jax.ShapeDtypeStruct((SUBLANE, LANE), jnp.int32), jax.ShapeDtypeStruct((nbp, LANE), jnp.int32),
                   jax.ShapeDtypeStruct((SUBLANE, LANE), jnp.int32)],
        compiler_params=_params(1),
        name="expert_plan",
    )(cnt)
    return pstart[0, :N_EXPERTS + 1], blk[:n_blocks, 0], tot[0, :1]


_ASG = 4


def _dest_row(pstart_ref, asg_ref, j, k):
    n = asg_ref.shape[2] // _ASG
    return pstart_ref[asg_ref[0, 0, k * n + j]] + asg_ref[0, 0, (2 + k) * n + j]


def _row_copy(src, src_row, dst, dst_row, sem):
    return pltpu.make_async_copy(src.at[pl.ds(pl.multiple_of(src_row * SUBLANE, SUBLANE), SUBLANE)],
                                 dst.at[pl.ds(pl.multiple_of(dst_row * SUBLANE, SUBLANE), SUBLANE)], sem)


def _dispatch_kernel(pstart_ref, asg_ref, h2_ref, xs_hbm, zeros, sem, zsem, *, n_blocks):
    n = asg_ref.shape[2] // _ASG
    RB = zeros.shape[0] // SUBLANE

    @pl.when(pl.program_id(0) == 0)
    def _():
        zeros[...] = jnp.zeros_like(zeros)

        def fill(row0):
            return pltpu.make_async_copy(
                zeros, xs_hbm.at[pl.ds(pl.multiple_of(row0 * SUBLANE, SUBLANE), RB * SUBLANE)], zsem)

        def experts(act):
            def body(e, carry):
                @pl.when(pstart_ref[e + 1] > pstart_ref[e])
                def _():
                    act(fill(pstart_ref[e + 1] - RB))
                return carry
            lax.fori_loop(0, N_EXPERTS, body, 0)

        def trailing(act):
            def body(b, carry):
                act(fill(b * RB))
                return carry
            lax.fori_loop(pstart_ref[N_EXPERTS] // RB, n_blocks, body, 0)

        experts(lambda c: c.start())
        trailing(lambda c: c.start())
        experts(lambda c: c.wait())
        trailing(lambda c: c.wait())

    def issue(jj, carry):
        for u in range(ISSUE_UNROLL):
            j = jj * ISSUE_UNROLL + u
            for k in range(2):
                _row_copy(h2_ref, j, xs_hbm, _dest_row(pstart_ref, asg_ref, j, k), sem).start(priority=k)
        return carry

    lax.fori_loop(0, n // ISSUE_UNROLL, issue, 0)
    whole = xs_hbm.at[pl.ds(0, 2 * n * SUBLANE)]
    pltpu.make_async_copy(whole, whole, sem).wait()


def _dispatch(pstart, asg3, h2t, n_blocks):
    nt, _, width = asg3.shape
    n = width // _ASG
    return pl.pallas_call(
        functools.partial(_dispatch_kernel, n_blocks=n_blocks),
        grid=(nt,),
        in_specs=[pl.BlockSpec(memory_space=pltpu.SMEM),
                  pl.BlockSpec((1, 1, width), lambda i: (i, 0, 0), memory_space=pltpu.SMEM),
                  pl.BlockSpec((n * SUBLANE, LANE), lambda i: (i, 0))],
        out_specs=pl.BlockSpec(memory_space=pl.ANY),
        out_shape=jax.ShapeDtypeStruct((n_blocks * EXPERT_ROWS * SUBLANE, LANE), F32),
        scratch_shapes=[pltpu.VMEM((EXPERT_ROWS * SUBLANE, LANE), F32), pltpu.SemaphoreType.DMA(()),
                        pltpu.SemaphoreType.DMA(())],
        compiler_params=_params(1),
        name="dispatch",
    )(pstart, asg3, h2t)


def _expert_kernel(be_ref, nb_ref, xs_ref, w1_hbm, w3_hbm, w2_hbm, ys_ref, w1f, w3f, w2f, w1b, w3b, w2b, sems, turn):
    RB = xs_ref.shape[0] // SUBLANE
    i = pl.program_id(0)
    nb = nb_ref[0]

    def fetch(e, slot):
        return [pltpu.make_async_copy(src.at[e], dst.at[slot], sems.at[slot])
                for src, dst in ((w1_hbm, w1f), (w3_hbm, w3f), (w2_hbm, w2f))]

    @pl.when(i < nb)
    def _():
        e = be_ref[i]

        @pl.when(i == 0)
        def _():
            turn[0] = 0
            for c in fetch(e, 0):
                c.start()

        @pl.when((i == 0) | (e != be_ref[jnp.maximum(i - 1, 0)]))
        def _():
            slot = turn[0] % 2
            for c in fetch(e, slot):
                c.wait()
            w1b[...] = w1f[slot].astype(BF16)
            w3b[...] = w3f[slot].astype(BF16)
            w2b[...] = w2f[slot].astype(BF16)
            nxt = lax.while_loop(lambda j: (j < nb) & (be_ref[jnp.minimum(j, nb - 1)] == e), lambda j: j + 1, i + 1)

            @pl.when(nxt < nb)
            def _():
                for c in fetch(be_ref[jnp.minimum(nxt, nb - 1)], 1 - slot):
                    c.start()

            turn[0] = turn[0] + 1

        x = jnp.concatenate([xs_ref[pl.ds(s, RB, stride=SUBLANE), :] for s in range(SUBLANE)], axis=1).astype(BF16)
        hb = _silu(_dot(x, w1b[...])) * _dot(x, w3b[...])
        y = _dot(hb.astype(BF16), w2b[...])
        for s in range(SUBLANE):
            ys_ref[pl.ds(s, RB, stride=SUBLANE), :] = y[:, s * LANE:(s + 1) * LANE]

    @pl.when(i >= nb_ref[0])
    def _():
        ys_ref[...] = jnp.zeros_like(ys_ref)


def _experts(block_expert, n_used, xs, w1, w3, w2):
    n_blocks = block_expert.shape[0]
    RB = EXPERT_ROWS
    D, F = w1.shape[1], w1.shape[2]
    cur = lambda i, nb: jnp.minimum(i, nb[0] - 1)
    rows = pl.BlockSpec((RB * SUBLANE, LANE), lambda i, be, nb: (cur(i, nb), 0))
    hbm = pl.BlockSpec(memory_space=pl.ANY)
    return pl.pallas_call(
        _expert_kernel,
        grid_spec=pltpu.PrefetchScalarGridSpec(
            num_scalar_prefetch=2, grid=(n_blocks,),
            in_specs=[rows, hbm, hbm, hbm],
            out_specs=pl.BlockSpec((RB * SUBLANE, LANE), lambda i, be, nb: (i, 0)),
            scratch_shapes=[pltpu.VMEM((2, D, F), F32), pltpu.VMEM((2, D, F), F32), pltpu.VMEM((2, F, D), F32),
                            pltpu.VMEM((D, F), BF16), pltpu.VMEM((D, F), BF16), pltpu.VMEM((F, D), BF16),
                            pltpu.SemaphoreType.DMA((2,)), pltpu.SMEM((1,), jnp.int32)]),
        out_shape=jax.ShapeDtypeStruct(xs.shape, F32),
        compiler_params=_params(1),
        name="experts",
    )(block_expert, n_used, xs, w1, w3, w2)


def _combine_kernel(pstart_ref, asg_ref, asg_next_ref, ys_hbm, x1_ref, route_ref, mod_ref, nw_ref, o_ref, buf, sems):
    n = x1_ref.shape[0]
    i = pl.program_id(0)
    slot = i % 2

    def gather(dref, into):
        def issue(jj, carry):
            for u in range(ISSUE_UNROLL):
                j = jj * ISSUE_UNROLL + u
                for k in range(2):
                    _row_copy(ys_hbm, _dest_row(pstart_ref, dref, j, k), buf.at[into], k * n + j,
                              sems.at[into]).start(priority=k)
            return carry

        lax.fori_loop(0, n // ISSUE_UNROLL, issue, 0)

    @pl.when(i == 0)
    def _():
        gather(asg_ref, 0)

    @pl.when(i + 1 < pl.num_programs(0))
    def _():
        gather(asg_next_ref, 1 - slot)

    pltpu.make_async_copy(ys_hbm.at[pl.ds(0, 2 * n * SUBLANE)], buf.at[slot], sems.at[slot]).wait()

    route = jnp.concatenate([route_ref[...], jnp.zeros((LANE - SUBLANE, n), F32)], axis=0).T
    g0, g1 = route[:, 2:3], route[:, 3:4]
    y = jnp.concatenate(
        [g0 * buf[slot, pl.ds(s, n, stride=SUBLANE), :] + g1 * buf[slot, pl.ds(n * SUBLANE + s, n, stride=SUBLANE), :]
         for s in range(SUBLANE)], axis=1)
    o_ref[...] = x1_ref[...] + mod_ref[0][5:6] * _rms(y, nw_ref[...])


def _combine(pstart, asg3, ys, x1, route, mod3, norm_w, S):
    T, D = x1.shape
    nt, _, width = asg3.shape
    n = width // _ASG
    tiles_per_seq = S // n
    return pl.pallas_call(
        _combine_kernel,
        grid=(nt,),
        in_specs=[pl.BlockSpec(memory_space=pltpu.SMEM),
                  pl.BlockSpec((1, 1, width), lambda i: (i, 0, 0), memory_space=pltpu.SMEM),
                  pl.BlockSpec((1, 1, width), lambda i: (jnp.minimum(i + 1, nt - 1), 0, 0), memory_space=pltpu.SMEM),
                  pl.BlockSpec(memory_space=pl.ANY), pl.BlockSpec((n, D), lambda i: (i, 0)),
                  pl.BlockSpec((SUBLANE, n), lambda i: (0, i)),
                  pl.BlockSpec((1, 6, D), lambda i: (i // tiles_per_seq, 0, 0)), _full((1, D))],
        out_specs=pl.BlockSpec((n, D), lambda i: (i, 0)),
        out_shape=jax.ShapeDtypeStruct((T, D), F32),
        scratch_shapes=[pltpu.VMEM((2, 2 * n * SUBLANE, LANE), F32), pltpu.SemaphoreType.DMA((2,))],
        compiler_params=_params(1),
        name="combine",
    )(pstart, asg3, asg3, ys, x1, route, mod3, norm_w[None, :])


def _layer(x, c, positions, ada_w, ada_b, norm_mix_pre, norm_mix_post, norm_ffn_pre, norm_ffn_post, w_in,
           conv_w, gate_b, head_norm, cmp_k, cmp_v, w_out, wg, bg, we, be, w1, w3, w2):
    B, S, D = x.shape
    T = B * S
    x2 = x.reshape(T, D)
    mod3 = _adaln(c, ada_w, ada_b).reshape(B, 6, D)

    o_mi = 4 * M_WIDTH
    o_nq = o_mi + 2 * M_HEADS
    o_kv = o_nq + N_WIDTH
    kv = lambda i: w_in[:, o_kv + i * KV_WIDTH:o_kv + (i + 1) * KV_WIDTH]
    o_ng = o_kv + 6 * KV_WIDTH
    w_main = jnp.concatenate([w_in[:, :2 * M_WIDTH], w_in[:, o_nq:o_kv], kv(0), kv(2), kv(4), kv(1)],
                             axis=1).astype(BF16)
    w_t = jnp.concatenate([kv(3), kv(5), w_in[:, o_ng:], jnp.zeros((D, _R_MV - _R_NG - 3 * N_HEADS), F32),
                           w_in[:, 2 * M_WIDTH:o_mi]], axis=1).T.astype(BF16)
    w_gate = w_in[:, o_mi:o_nq].T
    wg_t = w_gate.astype(BF16)
    w_t = jnp.concatenate([w_t, wg_t, (w_gate - wg_t.astype(F32)).astype(BF16)], axis=0)
    half = N_HEAD_DIM // 2
    inv = ROPE_THETA ** (-jnp.arange(half, dtype=F32) / half)
    inv_row = jnp.tile(inv, LANE // half)[None, :]
    pos_col = positions.astype(F32).reshape(T, 1)

    zqk, q, kc, ks, kw, vc, vs_t, vw_t, ng_t, mv_t, mo_t, gt = _inproj(
        x2, mod3, norm_mix_pre, w_main, w_t, wg_t, gate_b, pos_col, inv_row, S)
    ym = _mlstm(zqk, mv_t, mo_t, gt, conv_w, head_norm, B, S)
    kcmp, vcmp_t = _compress(kc, vc, _compress_weights(*cmp_k), _compress_weights(*cmp_v), B, S)
    yn = _nsa(q, kcmp, vcmp_t, ks, vs_t, kw, vw_t, ng_t, B, S)

    pad = _ROUTER_ROWS - MOE_GROUPS - N_EXPERTS
    wr = jnp.concatenate([wg, we, jnp.zeros((D, pad), F32)], axis=1).T
    br = jnp.concatenate([bg, be, jnp.zeros((pad,), F32)])[:, None]
    x1, h2t, route, cnt = _post(ym, yn, w_out, x2, mod3, norm_mix_post, norm_ffn_pre, wr, br, S)

    n_blocks = (2 * T) // EXPERT_ROWS + N_EXPERTS
    pstart, block_expert, n_used = _sort_plan(cnt, n_blocks)
    n_tok = min(GATHER_TILE, S)
    asg = jnp.concatenate([route[0:2], route[4:6]], axis=0).astype(jnp.int32)
    asg3 = asg.reshape(_ASG, T // n_tok, n_tok).transpose(1, 0, 2).reshape(T // n_tok, 1, _ASG * n_tok)
    xs = _dispatch(pstart, asg3, h2t, n_blocks)
    ys = _experts(block_expert, n_used, xs, w1, w3, w2)
    out = _combine(pstart, asg3, ys, x1, route, mod3, norm_ffn_post, S)
    return out.reshape(B, S, D)


def kernel(x, c, positions, ada_w, ada_b, norm_mix_pre, norm_mix_post, norm_ffn_pre, norm_ffn_post, w_in, mlstm_conv_w, mlstm_gate_b, mlstm_head_norm, cmp_pe_k, cmp_w1_k, cmp_w2_k, cmp_pe_v, cmp_w1_v, cmp_w2_v, w_out, router_grp_w, router_grp_b, router_exp_w, router_exp_b, expert_w1, expert_w3, expert_w2):
    for l in range(ada_w.shape[0]):
        x = _layer(x, c, positions, ada_w[l], ada_b[l], norm_mix_pre[l], norm_mix_post[l], norm_ffn_pre[l],
                   norm_ffn_post[l], w_in[l], mlstm_conv_w[l], mlstm_gate_b[l], mlstm_head_norm[l],
                   (cmp_pe_k[l], cmp_w1_k[l], cmp_w2_k[l]), (cmp_pe_v[l], cmp_w1_v[l], cmp_w2_v[l]), w_out[l],
                   router_grp_w[l], router_grp_b[l], router_exp_w[l], router_exp_b[l],
                   expert_w1[l], expert_w3[l], expert_w2[l])
    return x
```

```python
import functools

import jax
import jax.numpy as jnp
import numpy as np
from jax import lax
from jax.experimental import pallas as pl
from jax.experimental.pallas import tpu as pltpu

M_HEADS = 4
M_HEAD_DIM = 128
M_WIDTH = M_HEADS * M_HEAD_DIM
M_CONV = 4
N_HEADS = 8
N_KV_GROUPS = 2
HEADS_PER_GROUP = N_HEADS // N_KV_GROUPS
N_HEAD_DIM = 64
N_WIDTH = N_HEADS * N_HEAD_DIM
KV_WIDTH = N_KV_GROUPS * N_HEAD_DIM
CMP_BLOCK = 32
CMP_STRIDE = 16
CMP_HIDDEN = 2 * N_HEAD_DIM
SLC_BLOCK = 64
SLC_TOPK = 16
WINDOW = 512
ROPE_THETA = 10000.0
MOE_GROUPS = 4
EXPERTS_PER_GROUP = 8
N_EXPERTS = MOE_GROUPS * EXPERTS_PER_GROUP
RMS_EPS = 1e-6
LOG2_E = 1.4426950408889634

LANE = 128
SUBLANE = 8
VMEM_LIMIT_BYTES = 56 * 1024 * 1024

TOKEN_TILE = 512
MLSTM_CHUNK = 256
NSA_Q_TILE = 256
NSA_K_TILE = 512
NSA_COL_BLOCK = 256
EXPERT_ROWS = 512
GATHER_TILE = 256
ISSUE_UNROLL = 8

F32 = jnp.float32
BF16 = jnp.bfloat16
NEG = -1e30
HIGHEST = lax.Precision.HIGHEST
NT_DIMS = (((1,), (1,)), ((), ()))
TN_DIMS = (((0,), (0,)), ((), ()))


def _params(n_grid):
    return pltpu.CompilerParams(
        dimension_semantics=("arbitrary",) * n_grid, vmem_limit_bytes=VMEM_LIMIT_BYTES)


def _dot(a, b, **kw):
    return jnp.dot(a, b, preferred_element_type=F32, **kw)


def _dot_nt(a, b, **kw):
    return lax.dot_general(a, b, NT_DIMS, preferred_element_type=F32, **kw)


def _dot_tn(a, b, **kw):
    return lax.dot_general(a, b, TN_DIMS, preferred_element_type=F32, **kw)


def _sigmoid(x):
    return 1.0 / (1.0 + jnp.exp(-x))


def _silu(x):
    return x * _sigmoid(x)


def _split3(x):
    hi = x.astype(BF16)
    r1 = x - hi.astype(F32)
    mid = r1.astype(BF16)
    return [hi, mid, (r1 - mid.astype(F32)).astype(BF16)]


def _rms(x, w):
    return x * lax.rsqrt(jnp.mean(x * x, axis=-1, keepdims=True) + RMS_EPS) * w


def _full(shape):
    return pl.BlockSpec(shape, lambda *_: (0,) * len(shape))


def _adaln_kernel(c_ref, w_ref, b_ref, o_ref):
    o_ref[...] = _dot(_silu(c_ref[...]), w_ref[...], precision=HIGHEST) + b_ref[...]


def _adaln(c, ada_w, ada_b):
    B, D = c.shape
    n = ada_w.shape[1] // D
    return pl.pallas_call(
        _adaln_kernel,
        grid=(n,),
        in_specs=[_full((B, D)), pl.BlockSpec((D, D), lambda j: (0, j)), pl.BlockSpec((1, D), lambda j: (0, j))],
        out_specs=pl.BlockSpec((B, D), lambda j: (0, j)),
        out_shape=jax.ShapeDtypeStruct((B, ada_w.shape[1]), F32),
        compiler_params=_params(1),
        name="adaln",
    )(c, ada_w, ada_b[None, :])


_C_M = 0
_C_Q = 2 * M_WIDTH
_C_KC = _C_Q + N_WIDTH
_C_KS = _C_KC + KV_WIDTH
_C_KW = _C_KS + KV_WIDTH
_C_VC = _C_KW + KV_WIDTH
_C_END = _C_VC + KV_WIDTH
_R_VS = 0
_R_VW = KV_WIDTH
_R_NG = 2 * KV_WIDTH
_R_MV = _R_NG + 4 * SUBLANE
_R_MO = _R_MV + M_WIDTH
_R_GH = _R_MO + M_WIDTH
_R_GL = _R_GH + 2 * M_HEADS
_R_END = _R_GL + 2 * M_HEADS


def _inproj_kernel(x_ref, mod_ref, nw_ref, w_ref, wt_ref, wg_ref, gb_ref, pos_ref, inv_ref,
                   zm_ref, q_ref, kc_ref, ks_ref, kw_ref, vc_ref, vst_ref, vwt_ref, ngt_ref, mvt_ref, mot_ref, gt_ref):
    mod = mod_ref[0]
    h = _rms(x_ref[...], nw_ref[...]) * (1.0 + mod[1:2]) + mod[0:1]
    hb = h.astype(BF16)
    z = _dot(hb, w_ref[...])
    zm_ref[...] = z[:, _C_M:_C_Q]
    zt = _dot_nt(wt_ref[...], hb)
    h_lo = (h - hb.astype(F32)).astype(BF16)
    gt_ref[...] = (zt[_R_GH:_R_GL] + zt[_R_GL:_R_END] + _dot_nt(wg_ref[...], h_lo)) + gb_ref[...]
    vst_ref[...] = zt[_R_VS:_R_VW].astype(BF16)
    vwt_ref[...] = zt[_R_VW:_R_NG].astype(BF16)
    ngt_ref[...] = _sigmoid(zt[_R_NG:_R_MV])
    mvt_ref[...] = zt[_R_MV:_R_MO].astype(BF16)
    mot_ref[...] = zt[_R_MO:_R_GH]

    ang = pos_ref[...] * inv_ref[...]
    cos = jnp.cos(ang)
    sin = jnp.sin(ang)
    lane = lax.broadcasted_iota(jnp.int32, (1, LANE), 1)
    first = (lane % N_HEAD_DIM) < (N_HEAD_DIM // 2)
    sin_signed = jnp.where(first, -sin, sin)

    def rope(slab):
        rot = jnp.where(first, pltpu.roll(slab, LANE - N_HEAD_DIM // 2, 1), pltpu.roll(slab, N_HEAD_DIM // 2, 1))
        return slab * cos + rot * sin_signed

    scale = N_HEAD_DIM ** -0.5 * LOG2_E
    for j in range(N_WIDTH // LANE):
        q_ref[:, j * LANE:(j + 1) * LANE] = (rope(z[:, _C_Q + j * LANE:_C_Q + (j + 1) * LANE]) * scale).astype(BF16)
    kc_ref[...] = rope(z[:, _C_KC:_C_KS])
    ks_ref[...] = rope(z[:, _C_KS:_C_KW]).astype(BF16)
    kw_ref[...] = rope(z[:, _C_KW:_C_VC]).astype(BF16)
    vc_ref[...] = z[:, _C_VC:_C_END]


def _inproj(x2, mod3, norm_w, w_main, w_t, wg_t, gate_b, pos_col, inv_row, S):
    T, D = x2.shape
    TM = min(TOKEN_TILE, S)
    tiles_per_seq = S // TM
    row = lambda w: pl.BlockSpec((TM, w), lambda i: (i, 0))
    col = lambda r: pl.BlockSpec((r, TM), lambda i: (0, i))
    outs = [(2 * M_WIDTH, F32), (N_WIDTH, BF16), (KV_WIDTH, F32), (KV_WIDTH, BF16), (KV_WIDTH, BF16), (KV_WIDTH, F32)]
    outs_t = [(KV_WIDTH, BF16), (KV_WIDTH, BF16), (_R_MV - _R_NG, F32), (M_WIDTH, BF16), (M_WIDTH, F32),
              (2 * M_HEADS, F32)]
    return pl.pallas_call(
        _inproj_kernel,
        grid=(T // TM,),
        in_specs=[row(D), pl.BlockSpec((1, 6, D), lambda i: (i // tiles_per_seq, 0, 0)), _full((1, D)),
                  _full(w_main.shape), _full(w_t.shape), _full(wg_t.shape), _full((2 * M_HEADS, 1)), row(1),
                  _full((1, LANE))],
        out_specs=[row(w) for w, _ in outs] + [col(r) for r, _ in outs_t],
        out_shape=[jax.ShapeDtypeStruct((T, w), dt) for w, dt in outs]
        + [jax.ShapeDtypeStruct((r, T), dt) for r, dt in outs_t],
        compiler_params=_params(1),
        name="inproj",
    )(x2, mod3, norm_w[None, :], w_main, w_t, wg_t, gate_b[:, None], pos_col, inv_row)


def _mlstm_kernel(zm_ref, vt_ref, ot_ref, gt_ref, cw_ref, hn_ref, ltri_ref, y_ref, buf, c_s, n_s, m_s):
    L = zm_ref.shape[0]
    QK = 2 * M_WIDTH
    DH = M_HEAD_DIM

    @pl.when(pl.program_id(1) == 0)
    def _():
        buf[0:SUBLANE, :] = jnp.zeros((SUBLANE, QK), F32)
        c_s[...] = jnp.zeros_like(c_s)
        n_s[...] = jnp.zeros_like(n_s)
        m_s[...] = jnp.zeros_like(m_s)

    buf[SUBLANE:SUBLANE + L, :] = zm_ref[:, 0:QK]
    cw = cw_ref[...]
    conv = cw[M_CONV - 1:M_CONV] * buf[SUBLANE:SUBLANE + L, :]
    for j in range(M_CONV - 1):
        off = SUBLANE - (M_CONV - 1) + j
        conv = conv + cw[j:j + 1] * buf[off:off + L, :]
    buf[0:SUBLANE, :] = buf[L:L + SUBLANE, :]
    qk = _silu(conv)

    g = gt_ref[...]
    fp = g[M_HEADS:]
    lf = jnp.minimum(fp, 0.0) - jnp.log(1.0 + jnp.exp(-jnp.abs(fp)))
    cs = _dot_nt(jnp.concatenate(_split3(lf) + [jnp.zeros((M_HEADS, L), BF16)], axis=0), ltri_ref[...])
    b_rows = cs[0:M_HEADS] + cs[M_HEADS:2 * M_HEADS] + cs[2 * M_HEADS:3 * M_HEADS]
    src_rows = b_rows - g[:M_HEADS]

    causal = lax.broadcasted_iota(jnp.int32, (L, L), 0) <= lax.broadcasted_iota(jnp.int32, (L, L), 1)
    ones_rows = jnp.ones((2 * SUBLANE, L), F32)
    ones3 = jnp.ones((3, L), BF16)
    zeros10 = jnp.zeros((2 * SUBLANE - 6, L), BF16)
    outs = []
    for h in range(M_HEADS):
        hs = slice(h * DH, (h + 1) * DH)
        b_r, i_r = b_rows[h:h + 1], g[h:h + 1]
        m_prev = m_s[h:h + 1, 0:1]
        g_tot = b_r[:, L - 1:L]
        a_r = g_tot - b_r + i_r
        m_new = jnp.maximum(g_tot + m_prev, jnp.max(a_r, axis=1, keepdims=True))

        qb = qk[:, hs].astype(BF16)
        kb = (qk[:, M_WIDTH + h * DH:M_WIDTH + (h + 1) * DH] * (DH ** -0.5)).astype(BF16)
        vt = vt_ref[hs, :]

        lhs_t = jnp.concatenate(_split3(-src_rows[h:h + 1]) + [ones3, zeros10], axis=0)
        rhs_t = jnp.concatenate([ones3] + _split3(b_r) + [zeros10], axis=0)
        dlog = jnp.where(causal, _dot_tn(lhs_t, rhs_t), -jnp.inf)
        inter = b_r + m_prev
        m_t = jnp.maximum(inter, jnp.max(dlog, axis=0, keepdims=True))
        wts = jnp.exp(dlog - m_t) * _dot_nt(kb, qb)
        dec = jnp.exp(inter - m_t)
        ct_prev = c_s[h]
        n_prev = n_s[...]
        num = _dot(vt, wts.astype(BF16)) + dec * _dot_nt(ct_prev.astype(BF16), qb)
        qn = _dot_nt(n_prev.astype(BF16), qb)[h:h + 1]
        den = jnp.sum(wts, axis=0, keepdims=True) + dec * qn
        hh = num * (1.0 / jnp.maximum(jnp.abs(den), jnp.exp(-m_t)))

        w_r = jnp.exp(a_r - m_new)
        keep = jnp.exp(g_tot + m_prev - m_new)
        lhs = jnp.concatenate([vt.astype(F32), ones_rows], axis=0) * w_r
        upd = _dot(lhs.astype(BF16), kb)
        c_s[h] = keep * ct_prev + upd[:DH]
        n_s[h:h + 1] = keep * n_prev[h:h + 1] + upd[DH:DH + 1]
        m_s[h:h + 1] = jnp.broadcast_to(m_new, (1, LANE))

        hn = hh * lax.rsqrt(jnp.mean(hh * hh, axis=0, keepdims=True) + RMS_EPS)
        hn = hn * jnp.concatenate([hn_ref[hs, :]] * (L // LANE), axis=1)
        outs.append(_sigmoid(ot_ref[hs, :]) * hn)
    y_ref[...] = jnp.concatenate(outs, axis=0).T.astype(BF16)


def _mlstm(zqk, v_t, o_t, gt, conv_w, head_norm, B, S):
    T = zqk.shape[0]
    L = min(MLSTM_CHUNK, S)
    assert L % LANE == 0
    nc = S // L
    ltri = jnp.asarray(np.tril(np.ones((L, L), np.float32)), dtype=BF16)
    col = lambda r: pl.BlockSpec((r, L), lambda b, c: (0, b * nc + c))
    return pl.pallas_call(
        _mlstm_kernel,
        grid=(B, nc),
        in_specs=[pl.BlockSpec((L, 2 * M_WIDTH), lambda b, c: (b * nc + c, 0)), col(M_WIDTH), col(M_WIDTH),
                  col(2 * M_HEADS), _full((M_CONV, 2 * M_WIDTH)), _full((M_WIDTH, LANE)), _full((L, L))],
        out_specs=pl.BlockSpec((L, M_WIDTH), lambda b, c: (b * nc + c, 0)),
        out_shape=jax.ShapeDtypeStruct((T, M_WIDTH), BF16),
        scratch_shapes=[pltpu.VMEM((L + SUBLANE, 2 * M_WIDTH), F32),
                        pltpu.VMEM((M_HEADS, M_HEAD_DIM, M_HEAD_DIM), F32),
                        pltpu.VMEM((SUBLANE, M_HEAD_DIM), F32),
                        pltpu.VMEM((SUBLANE, LANE), F32)],
        compiler_params=_params(2),
        name="mlstm",
    )(zqk, v_t, o_t, gt, conv_w, jnp.broadcast_to(head_norm[:, None], (M_WIDTH, LANE)), ltri)


def _compress_kernel(k_ref, v_ref, pak, pbk, wak, wbk, w2k, pav, pbv, wav, wbv, w2v, ko_ref, vo_ref):
    def one(x_ref, pa, pb, wa, wb, w2, o_ref, transposed):
        n = x_ref.shape[0] // CMP_STRIDE
        x = jnp.concatenate([x_ref[pl.ds(t, n, stride=CMP_STRIDE), :] for t in range(CMP_STRIDE)], axis=1)
        first = _dot((x + pa[...]).astype(BF16), wa[...])
        second = _dot((x + pb[...]).astype(BF16), wb[...])
        hid = _silu(first + pltpu.roll(second, n - 1, 0)).astype(BF16)
        o_ref[0] = (_dot_nt(w2[...], hid) if transposed else _dot(hid, w2[...])).astype(BF16)

    one(k_ref, pak, pbk, wak, wbk, w2k, ko_ref, False)
    one(v_ref, pav, pbv, wav, wbv, w2v, vo_ref, True)


def _compress_weights(pe, w1, w2):
    half = CMP_BLOCK // 2
    eye = jnp.eye(N_KV_GROUPS, dtype=F32)
    w1r = w1.reshape(CMP_BLOCK, N_HEAD_DIM, CMP_HIDDEN)
    big = lambda w: jnp.einsum("ldc,gh->lgdhc", w, eye).reshape(half * KV_WIDTH, N_KV_GROUPS * CMP_HIDDEN).astype(BF16)
    pe_row = lambda p: jnp.broadcast_to(p[:, None, :], (half, N_KV_GROUPS, N_HEAD_DIM)).reshape(1, half * KV_WIDTH)
    w2bd = jnp.einsum("cd,gh->gchd", w2, eye).reshape(N_KV_GROUPS * CMP_HIDDEN, KV_WIDTH).astype(BF16)
    return pe_row(pe[:half]), pe_row(pe[half:]), big(w1r[:half]), big(w1r[half:]), w2bd


def _compress(kc, vc, wk, wv, B, S):
    nc = S // CMP_STRIDE
    blk = pl.BlockSpec((S, KV_WIDTH), lambda b: (b, 0))
    wv = wv[:4] + (wv[4].T,)
    return pl.pallas_call(
        _compress_kernel,
        grid=(B,),
        in_specs=[blk, blk] + [_full(w.shape) for w in wk] + [_full(w.shape) for w in wv],
        out_specs=[pl.BlockSpec((1, nc, KV_WIDTH), lambda b: (b, 0, 0)),
                   pl.BlockSpec((1, KV_WIDTH, nc), lambda b: (b, 0, 0))],
        out_shape=[jax.ShapeDtypeStruct((B, nc, KV_WIDTH), BF16), jax.ShapeDtypeStruct((B, KV_WIDTH, nc), BF16)],
        compiler_params=_params(1),
        name="compress",
    )(kc, vc, *wk, *wv)


def _masked_softmax_keys(s, mask):
    s = jnp.where(mask, s, -jnp.inf)
    m = jnp.max(s, axis=0, keepdims=True)
    m = jnp.where(jnp.isfinite(m), m, 0.0)
    e = jnp.exp2(s - m)
    return e, 1.0 / jnp.maximum(jnp.sum(e, axis=0, keepdims=True), 1e-30)


def _nsa_kernel(q_ref, kc_ref, vct_ref, ks_ref, vst_ref, kw_ref, vwt_ref, ngt_ref, ov_ref, ek_ref, eye_ref,
                y_ref, m_s, l_s, acc_s, rank_s, *, n_top, TK):
    TQ = q_ref.shape[0]
    NS = ov_ref.shape[0]
    G, HPG, DH = N_KV_GROUPS, HEADS_PER_GROUP, N_HEAD_DIM
    R = G * HPG * TQ
    q0 = pl.program_id(1) * TQ

    q = q_ref[...]
    zero = jnp.zeros((TQ, DH), BF16)
    parts = []
    for hd in range(N_HEADS):
        qh = q[:, hd * DH:(hd + 1) * DH]
        parts.append(jnp.concatenate([qh, zero] if hd < HPG else [zero, qh], axis=1))
    qp = jnp.concatenate(parts, axis=0)
    t_q = q0 + lax.broadcasted_iota(jnp.int32, (1, R), 1) % TQ

    kc = kc_ref[0]
    NC = kc.shape[0]
    cmp_end = lax.broadcasted_iota(jnp.int32, (NC, 1), 0) * CMP_STRIDE + (CMP_BLOCK - 1)
    e_c, inv_c = _masked_softmax_keys(_dot_nt(kc, qp), cmp_end <= t_q)
    p_c = e_c * inv_c
    o_c = _dot(vct_ref[0], p_c.astype(BF16))
    p_grp = jnp.concatenate(
        [sum(p_c[:, (g * HPG + h) * TQ:(g * HPG + h + 1) * TQ] for h in range(HPG)) for g in range(G)], axis=1)
    imp3 = _dot(ov_ref[...], jnp.concatenate(_split3(p_grp), axis=1))
    imp = imp3[:, :G * TQ] + imp3[:, G * TQ:2 * G * TQ] + imp3[:, 2 * G * TQ:]

    j_io = lax.broadcasted_iota(jnp.int32, (NS, G * TQ), 0)
    t_row = q0 + lax.broadcasted_iota(jnp.int32, (NS, G * TQ), 1) % TQ
    cur = t_row // SLC_BLOCK
    forced = (j_io == 0) | (j_io == cur) | (j_io == cur - 1)
    future = j_io * SLC_BLOCK > t_row
    score = jnp.where(forced, jnp.inf, jnp.where(future, -jnp.inf, imp))
    rank_s[...] = jnp.zeros_like(rank_s)
    n_rival = (q0 + TQ - 1) // SLC_BLOCK + 1
    for c0 in range(0, NS, SUBLANE):
        @pl.when(c0 < n_rival)
        def _():
            part = jnp.zeros((NS, G * TQ), F32)
            for jp in range(c0, c0 + SUBLANE):
                other = score[jp:jp + 1, :]
                part = part + jnp.where(j_io > jp, jnp.where(other >= score, 1.0, 0.0),
                                        jnp.where(other > score, 1.0, 0.0))
            rank_s[...] = rank_s[...] + part
    rank = rank_s[...]
    sel_bias = jnp.where(future, NEG, jnp.where(rank < n_top, 0.0, NEG)).astype(BF16)
    sel_rows = _dot_tn(sel_bias, eye_ref[...]).astype(BF16)
    sel_rows = jnp.broadcast_to(sel_rows.reshape(G, 1, TQ, LANE), (G, HPG, TQ, LANE)).reshape(R, LANE)
    qa = jnp.concatenate([qp, sel_rows], axis=1)

    m_s[...] = jnp.full_like(m_s, NEG)
    l_s[...] = jnp.zeros_like(l_s)
    acc_s[...] = jnp.zeros_like(acc_s)

    CB = NSA_COL_BLOCK

    def step(kt, causal):
        k0 = pl.multiple_of(kt * TK, TK)
        ka = jnp.concatenate([ks_ref[0, pl.ds(k0, TK), :], ek_ref[pl.ds(k0, TK), :]], axis=1)
        vt = vst_ref[:, pl.ds(k0, TK)]
        kpos = k0 + lax.broadcasted_iota(jnp.int32, (TK, 1), 0)
        scores = [_dot_nt(ka, qa[j * CB:(j + 1) * CB]) for j in range(R // CB)]
        probs, alphas = [], []
        for j in range(R // CB):
            cols = slice(j * CB, (j + 1) * CB)
            s = scores[j]
            if causal:
                s = jnp.where(kpos <= t_q[:, cols], s, NEG)
            m_old = m_s[:, cols]
            m_new = jnp.maximum(m_old, jnp.max(s, axis=0, keepdims=True))
            alpha = jnp.exp2(m_old - m_new)
            p = jnp.exp2(s - m_new)
            l_s[:, cols] = alpha * l_s[:, cols] + jnp.sum(p, axis=0, keepdims=True)
            m_s[:, cols] = m_new
            probs.append(p.astype(BF16))
            alphas.append(alpha)
        for j in range(R // CB):
            cols = slice(j * CB, (j + 1) * CB)
            acc_s[:, cols] = alphas[j] * acc_s[:, cols] + _dot(vt, probs[j])

    def full_tile(kt, carry):
        step(kt, False)
        return carry

    last = q0 // TK
    lax.fori_loop(0, last, full_tile, 0)
    step(last, True)
    o_s = acc_s[...] * (1.0 / l_s[...])

    WK = WINDOW + TQ
    start = pl.multiple_of(jnp.maximum(q0 - WINDOW, 0), TQ)
    wpos = start + lax.broadcasted_iota(jnp.int32, (WK, 1), 0)
    e_w, inv_w = _masked_softmax_keys(_dot_nt(kw_ref[0, pl.ds(start, WK), :], qp),
                                      (t_q - wpos).astype(jnp.uint32) < WINDOW)
    o_w =_dot(vwt_ref[:, pl.ds(start, WK)], e_w.astype(BF16)) * inv_w

    gates = ngt_ref[...]
    outs = []
    for hd in range(N_HEADS):
        cols = slice(hd * TQ, (hd + 1) * TQ)
        rows = slice((hd // HPG) * DH, (hd // HPG + 1) * DH)
        outs.append(gates[3 * hd:3 * hd + 1] * o_c[rows, cols]
                    + gates[3 * hd + 1:3 * hd + 2] * o_s[rows, cols]
                    + gates[3 * hd + 2:3 * hd + 3] * o_w[rows, cols])
    y_ref[...] = jnp.concatenate(outs, axis=0).T.astype(BF16)


def _nsa(q, kcmp, vcmp_t, ks, vs_t, kw, vw_t, ng_t, B, S):
    T = q.shape[0]
    TQ = min(NSA_Q_TILE, S)
    TK = min(NSA_K_TILE, S)
    nq = S // TQ
    NS = S // SLC_BLOCK
    NC = S // CMP_STRIDE
    n_cmp = (S - CMP_BLOCK) // CMP_STRIDE + 1
    js = np.arange(NS)[:, None] * SLC_BLOCK
    cs = np.arange(NC)[None, :] * CMP_STRIDE
    ov = np.clip(np.minimum(js + SLC_BLOCK, cs + CMP_BLOCK) - np.maximum(js, cs), 0, None) / CMP_STRIDE
    ov[:, n_cmp:] = 0.0
    assert NS <= LANE and TK % TQ == 0
    block_of_key = (np.arange(S)[:, None] // SLC_BLOCK == np.arange(LANE)[None, :]).astype(np.float32)
    eye_pad = np.eye(NS, LANE, dtype=np.float32)
    R = N_HEADS * TQ
    assert R % NSA_COL_BLOCK == 0
    seq = lambda: pl.BlockSpec((1, S, KV_WIDTH), lambda b, i: (b, 0, 0))
    seq_t = lambda: pl.BlockSpec((KV_WIDTH, S), lambda b, i: (0, b))
    return pl.pallas_call(
        functools.partial(_nsa_kernel, n_top=min(SLC_TOPK, NS), TK=TK),
        grid=(B, nq),
        in_specs=[pl.BlockSpec((TQ, N_WIDTH), lambda b, i: (b * nq + i, 0)),
                  pl.BlockSpec((1, NC, KV_WIDTH), lambda b, i: (b, 0, 0)),
                  pl.BlockSpec((1, KV_WIDTH, NC), lambda b, i: (b, 0, 0)),
                  seq(), seq_t(), seq(), seq_t(), pl.BlockSpec((ng_t.shape[0], TQ), lambda b, i: (0, b * nq + i)),
                  _full((NS, NC)), _full((S, LANE)), _full((NS, LANE))],
        out_specs=pl.BlockSpec((TQ, N_WIDTH), lambda b, i: (b * nq + i, 0)),
        out_shape=jax.ShapeDtypeStruct((T, N_WIDTH), BF16),
        scratch_shapes=[pltpu.VMEM((1, R), F32), pltpu.VMEM((1, R), F32), pltpu.VMEM((KV_WIDTH, R), F32),
                        pltpu.VMEM((NS, N_KV_GROUPS * TQ), F32)],
        compiler_params=_params(2),
        name="nsa",
    )(q, kcmp, vcmp_t, ks.reshape(B, S, KV_WIDTH), vs_t, kw.reshape(B, S, KV_WIDTH), vw_t, ng_t,
      jnp.asarray(ov, dtype=BF16), jnp.asarray(block_of_key, dtype=BF16), jnp.asarray(eye_pad, dtype=BF16))


_ROUTER_ROWS = 40


def _post_kernel(ym_ref, yn_ref, wm_ref, wn_ref, x_ref, mod_ref, npost_ref, npre_ref, wr2_ref, br_ref,
                 utri_ref, lanes_ref, x1_ref, h2_ref, route_ref, cnt_ref, carry):
    TM = x_ref.shape[0]
    NR = _ROUTER_ROWS

    @pl.when(pl.program_id(0) == 0)
    def _():
        carry[...] = jnp.zeros_like(carry)

    mod = mod_ref[0]
    y = _dot(ym_ref[...], wm_ref[...]) + _dot(yn_ref[...], wn_ref[...])
    x1 = x_ref[...] + mod[2:3] * _rms(y, npost_ref[...])
    x1_ref[...] = x1
    h2 = _rms(x1, npre_ref[...]) * (1.0 + mod[4:5]) + mod[3:4]
    for s in range(SUBLANE):
        h2_ref[pl.ds(s, TM, stride=SUBLANE), :] = h2[:, s * LANE:(s + 1) * LANE]

    h_hi = h2.astype(BF16)
    h_lo = (h2 - h_hi.astype(F32)).astype(BF16)
    z2 = _dot_nt(wr2_ref[...], h_hi)
    logits = z2[:NR] + z2[NR:] + _dot_nt(wr2_ref[0:NR, :], h_lo) + br_ref[...]
    row = lax.broadcasted_iota(jnp.int32, (NR, TM), 0)
    is_grp = row < MOE_GROUPS
    lg = jnp.where(is_grp, logits, -jnp.inf)
    eg = jnp.exp(lg - jnp.max(lg, axis=0, keepdims=True))
    pg = eg / jnp.sum(eg, axis=0, keepdims=True)
    pg_top = jnp.max(pg, axis=0, keepdims=True)
    grp = jnp.min(jnp.where(is_grp & (pg == pg_top), row, NR), axis=0, keepdims=True)
    lo = MOE_GROUPS + EXPERTS_PER_GROUP * grp
    in_grp = (row >= lo) & (row < lo + EXPERTS_PER_GROUP)
    le = jnp.where(in_grp, logits, -jnp.inf)
    ee = jnp.exp(le - jnp.max(le, axis=0, keepdims=True))
    pe = jnp.where(in_grp, ee / jnp.sum(ee, axis=0, keepdims=True), -1.0)
    p1 = jnp.max(pe, axis=0, keepdims=True)
    i1 = jnp.min(jnp.where(pe == p1, row, NR), axis=0, keepdims=True)
    pe2 = jnp.where(row == i1, -1.0, pe)
    p2 = jnp.max(pe2, axis=0, keepdims=True)
    i2 = jnp.min(jnp.where((pe2 == p2) & in_grp & (row != i1), row, NR), axis=0, keepdims=True)
    den = p1 + p2
    oh1, oh2 = row == i1, row == i2
    oh = jnp.where(oh1, 1.0, 0.0) + jnp.where(oh2, 1.0, 0.0)
    before = _dot(oh.astype(BF16), utri_ref[...]) + carry[...]
    r1 = jnp.sum(jnp.where(oh1, before, 0.0), axis=0, keepdims=True)
    r2 = jnp.sum(jnp.where(oh2, before, 0.0), axis=0, keepdims=True)
    carry[...] = carry[...] + jnp.sum(oh, axis=1, keepdims=True)
    cnt_ref[...] = _dot_tn(jnp.broadcast_to(carry[...], (NR, SUBLANE)), lanes_ref[...], precision=HIGHEST)
    rows = ((i1 - MOE_GROUPS).astype(F32), (i2 - MOE_GROUPS).astype(F32), pg_top * p1 / den, pg_top * p2 / den, r1, r2)
    route_ref[...] = jnp.concatenate(rows + (jnp.zeros((SUBLANE - len(rows), TM), F32),), axis=0)


def _post(ym, yn, w_out, x2, mod3, norm_post, norm_pre, wr, br, S):
    T, D = x2.shape
    TM = min(TOKEN_TILE, S)
    tiles_per_seq = S // TM
    row = lambda w: pl.BlockSpec((TM, w), lambda i: (i, 0))
    small = pl.BlockSpec((SUBLANE, LANE), lambda i: (0, 0))
    NR = _ROUTER_ROWS
    utri = jnp.asarray(np.triu(np.ones((TM, TM), np.float32), 1), dtype=BF16)
    lanes = jnp.asarray(np.eye(NR, LANE, k=-MOE_GROUPS, dtype=np.float32))
    wr_hi = wr.astype(BF16)
    wr2 = jnp.concatenate([wr_hi, (wr - wr_hi.astype(F32)).astype(BF16)], axis=0)
    return pl.pallas_call(
        _post_kernel,
        grid=(T // TM,),
        in_specs=[row(M_WIDTH), row(N_WIDTH), _full((M_WIDTH, D)), _full((N_WIDTH, D)), row(D),
                  pl.BlockSpec((1, 6, D), lambda i: (i // tiles_per_seq, 0, 0)), _full((1, D)), _full((1, D)),
                  _full((2 * NR, D)), _full((NR, 1)), _full((TM, TM)), _full((NR, LANE))],
        out_specs=[row(D), pl.BlockSpec((TM * SUBLANE, LANE), lambda i: (i, 0)),
                   pl.BlockSpec((SUBLANE, TM), lambda i: (0, i)), small],
        out_shape=[jax.ShapeDtypeStruct((T, D), F32), jax.ShapeDtypeStruct((T * SUBLANE, LANE), F32),
                   jax.ShapeDtypeStruct((SUBLANE, T), F32), jax.ShapeDtypeStruct((SUBLANE, LANE), F32)],
        scratch_shapes=[pltpu.VMEM((NR, 1), F32)],
        compiler_params=_params(1),
        name="post_mix_router",
    )(ym, yn, w_out[:M_WIDTH].astype(BF16), w_out[M_WIDTH:].astype(BF16), x2, mod3,
      norm_post[None, :], norm_pre[None, :], wr2, br, utri, lanes)


def _plan_kernel(cnt_ref, pstart_ref, blk_ref, tot_ref, *, rows_per_block):
    lane = lax.broadcasted_iota(jnp.int32, (SUBLANE, LANE), 1)
    nblk = (cnt_ref[...].astype(jnp.int32) + (rows_per_block - 1)) // rows_per_block
    end = nblk
    sh = 1
    while sh < 2 * N_EXPERTS:
        end = end + jnp.where(lane >= sh, pltpu.roll(end, sh, 1), 0)
        sh *= 2
    pstart_ref[...] = (end - nblk) * rows_per_block
    nbp = blk_ref.shape[0]
    blk_io = lax.broadcasted_iota(jnp.int32, (nbp, LANE), 0)
    lane_b = lax.broadcasted_iota(jnp.int32, (nbp, LANE), 1)
    passed = jnp.where((lane_b < N_EXPERTS) & (blk_io >= end[0:1, :]), 1, 0)
    blk_ref[...] = jnp.broadcast_to(jnp.minimum(jnp.sum(passed, axis=1, keepdims=True), N_EXPERTS - 1), (nbp, LANE))
    tot_ref[...] = jnp.broadcast_to(
        jnp.sum(jnp.where(lane == N_EXPERTS - 1, end, 0), axis=1, keepdims=True), (SUBLANE, LANE))


def _sort_plan(cnt, n_blocks):
    small = pl.BlockSpec((SUBLANE, LANE), lambda i: (0, 0))
    nbp = -(-n_blocks // SUBLANE) * SUBLANE
    pstart, blk, tot = pl.pallas_call(
        functools.partial(_plan_kernel, rows_per_block=EXPERT_ROWS),
        grid=(1,),
        in_specs=[small],
        out_specs=[small, _full((nbp, LANE)), small],
        out_shape=[jax.ShapeDtypeStruct((SUBLANE, LANE), jnp.int32), jax.ShapeDtypeStruct((nbp, LANE), jnp.int32),
                   jax.ShapeDtypeStruct((SUBLANE, LANE), jnp.int32)],
        compiler_params=_params(1),
        name="expert_plan",
    )(cnt)
    return pstart[0, :N_EXPERTS + 1], blk[:n_blocks, 0], tot[0, :1]


_ASG = 4


def _dest_row(pstart_ref, asg_ref, j, k):
    n = asg_ref.shape[2] // _ASG
    return pstart_ref[asg_ref[0, 0, k * n + j]] + asg_ref[0, 0, (2 + k) * n + j]


def _row_copy(src, src_row, dst, dst_row, sem):
    return pltpu.make_async_copy(src.at[pl.ds(pl.multiple_of(src_row * SUBLANE, SUBLANE), SUBLANE)],
                                 dst.at[pl.ds(pl.multiple_of(dst_row * SUBLANE, SUBLANE), SUBLANE)], sem)


def _dispatch_kernel(pstart_ref, asg_ref, h2_ref, xs_hbm, zeros, sem, zsem, *, n_blocks):
    n = asg_ref.shape[2] // _ASG
    RB = zeros.shape[0] // SUBLANE

    @pl.when(pl.program_id(0) == 0)
    def _():
        zeros[...] = jnp.zeros_like(zeros)

        def fill(row0):
            return pltpu.make_async_copy(
                zeros, xs_hbm.at[pl.ds(pl.multiple_of(row0 * SUBLANE, SUBLANE), RB * SUBLANE)], zsem)

        def experts(act):
            def body(e, carry):
                @pl.when(pstart_ref[e + 1] > pstart_ref[e])
                def _():
                    act(fill(pstart_ref[e + 1] - RB))
                return carry
            lax.fori_loop(0, N_EXPERTS, body, 0)

        def trailing(act):
            def body(b, carry):
                act(fill(b * RB))
                return carry
            lax.fori_loop(pstart_ref[N_EXPERTS] // RB, n_blocks, body, 0)

        experts(lambda c: c.start())
        trailing(lambda c: c.start())
        experts(lambda c: c.wait())
        trailing(lambda c: c.wait())

    def issue(jj, carry):
        for u in range(ISSUE_UNROLL):
            j = jj * ISSUE_UNROLL + u
            for k in range(2):
                _row_copy(h2_ref, j, xs_hbm, _dest_row(pstart_ref, asg_ref, j, k), sem).start(priority=k)
        return carry

    lax.fori_loop(0, n // ISSUE_UNROLL, issue, 0)
    whole = xs_hbm.at[pl.ds(0, 2 * n * SUBLANE)]
    pltpu.make_async_copy(whole, whole, sem).wait()


def _dispatch(pstart, asg3, h2t, n_blocks):
    nt, _, width = asg3.shape
    n = width // _ASG
    return pl.pallas_call(
        functools.partial(_dispatch_kernel, n_blocks=n_blocks),
        grid=(nt,),
        in_specs=[pl.BlockSpec(memory_space=pltpu.SMEM),
                  pl.BlockSpec((1, 1, width), lambda i: (i, 0, 0), memory_space=pltpu.SMEM),
                  pl.BlockSpec((n * SUBLANE, LANE), lambda i: (i, 0))],
        out_specs=pl.BlockSpec(memory_space=pl.ANY),
        out_shape=jax.ShapeDtypeStruct((n_blocks * EXPERT_ROWS * SUBLANE, LANE), F32),
        scratch_shapes=[pltpu.VMEM((EXPERT_ROWS * SUBLANE, LANE), F32), pltpu.SemaphoreType.DMA(()),
                        pltpu.SemaphoreType.DMA(())],
        compiler_params=_params(1),
        name="dispatch",
    )(pstart, asg3, h2t)


def _expert_kernel(be_ref, nb_ref, xs_ref, w1_hbm, w3_hbm, w2_hbm, ys_ref, w1f, w3f, w2f, w1b, w3b, w2b, sems, turn):
    RB = xs_ref.shape[0] // SUBLANE
    i = pl.program_id(0)
    nb = nb_ref[0]

    def fetch(e, slot):
        return [pltpu.make_async_copy(src.at[e], dst.at[slot], sems.at[slot])
                for src, dst in ((w1_hbm, w1f), (w3_hbm, w3f), (w2_hbm, w2f))]

    @pl.when(i < nb)
    def _():
        e = be_ref[i]

        @pl.when(i == 0)
        def _():
            turn[0] = 0
            for c in fetch(e, 0):
                c.start()

        @pl.when((i == 0) | (e != be_ref[jnp.maximum(i - 1, 0)]))
        def _():
            slot = turn[0] % 2
            for c in fetch(e, slot):
                c.wait()
            w1b[...] = w1f[slot].astype(BF16)
            w3b[...] = w3f[slot].astype(BF16)
            w2b[...] = w2f[slot].astype(BF16)
            nxt = lax.while_loop(lambda j: (j < nb) & (be_ref[jnp.minimum(j, nb - 1)] == e), lambda j: j + 1, i + 1)

            @pl.when(nxt < nb)
            def _():
                for c in fetch(be_ref[jnp.minimum(nxt, nb - 1)], 1 - slot):
                    c.start()

            turn[0] = turn[0] + 1

        x = jnp.concatenate([xs_ref[pl.ds(s, RB, stride=SUBLANE), :] for s in range(SUBLANE)], axis=1).astype(BF16)
        hb = _silu(_dot(x, w1b[...])) * _dot(x, w3b[...])
        y = _dot(hb.astype(BF16), w2b[...])
        for s in range(SUBLANE):
            ys_ref[pl.ds(s, RB, stride=SUBLANE), :] = y[:, s * LANE:(s + 1) * LANE]

    @pl.when(i >= nb_ref[0])
    def _():
        ys_ref[...] = jnp.zeros_like(ys_ref)


def _experts(block_expert, n_used, xs, w1, w3, w2):
    n_blocks = block_expert.shape[0]
    RB = EXPERT_ROWS
    D, F = w1.shape[1], w1.shape[2]
    cur = lambda i, nb: jnp.minimum(i, nb[0] - 1)
    rows = pl.BlockSpec((RB * SUBLANE, LANE), lambda i, be, nb: (cur(i, nb), 0))
    hbm = pl.BlockSpec(memory_space=pl.ANY)
    return pl.pallas_call(
        _expert_kernel,
        grid_spec=pltpu.PrefetchScalarGridSpec(
            num_scalar_prefetch=2, grid=(n_blocks,),
            in_specs=[rows, hbm, hbm, hbm],
            out_specs=pl.BlockSpec((RB * SUBLANE, LANE), lambda i, be, nb: (i, 0)),
            scratch_shapes=[pltpu.VMEM((2, D, F), F32), pltpu.VMEM((2, D, F), F32), pltpu.VMEM((2, F, D), F32),
                            pltpu.VMEM((D, F), BF16), pltpu.VMEM((D, F), BF16), pltpu.VMEM((F, D), BF16),
                            pltpu.SemaphoreType.DMA((2,)), pltpu.SMEM((1,), jnp.int32)]),
        out_shape=jax.ShapeDtypeStruct(xs.shape, F32),
        compiler_params=_params(1),
        name="experts",
    )(block_expert, n_used, xs, w1, w3, w2)


def _combine_kernel(pstart_ref, asg_ref, asg_next_ref, ys_hbm, x1_ref, route_ref, mod_ref, nw_ref, o_ref, buf, sems):
    n = x1_ref.shape[0]
    i = pl.program_id(0)
    slot = i % 2

    def gather(dref, into):
        def issue(jj, carry):
            for u in range(ISSUE_UNROLL):
                j = jj * ISSUE_UNROLL + u
                for k in range(2):
                    _row_copy(ys_hbm, _dest_row(pstart_ref, dref, j, k), buf.at[into], k * n + j,
                              sems.at[into]).start(priority=k)
            return carry

        lax.fori_loop(0, n // ISSUE_UNROLL, issue, 0)

    @pl.when(i == 0)
    def _():
        gather(asg_ref, 0)

    @pl.when(i + 1 < pl.num_programs(0))
    def _():
        gather(asg_next_ref, 1 - slot)

    pltpu.make_async_copy(ys_hbm.at[pl.ds(0, 2 * n * SUBLANE)], buf.at[slot], sems.at[slot]).wait()

    route = jnp.concatenate([route_ref[...], jnp.zeros((LANE - SUBLANE, n), F32)], axis=0).T
    g0, g1 = route[:, 2:3], route[:, 3:4]
    y = jnp.concatenate(
        [g0 * buf[slot, pl.ds(s, n, stride=SUBLANE), :] + g1 * buf[slot, pl.ds(n * SUBLANE + s, n, stride=SUBLANE), :]
         for s in range(SUBLANE)], axis=1)
    o_ref[...] = x1_ref[...] + mod_ref[0][5:6] * _rms(y, nw_ref[...])


def _combine(pstart, asg3, ys, x1, route, mod3, norm_w, S):
    T, D = x1.shape
    nt, _, width = asg3.shape
    n = width // _ASG
    tiles_per_seq = S // n
    return pl.pallas_call(
        _combine_kernel,
        grid=(nt,),
        in_specs=[pl.BlockSpec(memory_space=pltpu.SMEM),
                  pl.BlockSpec((1, 1, width), lambda i: (i, 0, 0), memory_space=pltpu.SMEM),
                  pl.BlockSpec((1, 1, width), lambda i: (jnp.minimum(i + 1, nt - 1), 0, 0), memory_space=pltpu.SMEM),
                  pl.BlockSpec(memory_space=pl.ANY), pl.BlockSpec((n, D), lambda i: (i, 0)),
                  pl.BlockSpec((SUBLANE, n), lambda i: (0, i)),
                  pl.BlockSpec((1, 6, D), lambda i: (i // tiles_per_seq, 0, 0)), _full((1, D))],
        out_specs=pl.BlockSpec((n, D), lambda i: (i, 0)),
        out_shape=jax.ShapeDtypeStruct((T, D), F32),
        scratch_shapes=[pltpu.VMEM((2, 2 * n * SUBLANE, LANE), F32), pltpu.SemaphoreType.DMA((2,))],
        compiler_params=_params(1),
        name="combine",
    )(pstart, asg3, asg3, ys, x1, route, mod3, norm_w[None, :])


def _layer(x, c, positions, ada_w, ada_b, norm_mix_pre, norm_mix_post, norm_ffn_pre, norm_ffn_post, w_in,
           conv_w, gate_b, head_norm, cmp_k, cmp_v, w_out, wg, bg, we, be, w1, w3, w2):
    B, S, D = x.shape
    T = B * S
    x2 = x.reshape(T, D)
    mod3 = _adaln(c, ada_w, ada_b).reshape(B, 6, D)

    o_mi = 4 * M_WIDTH
    o_nq = o_mi + 2 * M_HEADS
    o_kv = o_nq + N_WIDTH
    kv = lambda i: w_in[:, o_kv + i * KV_WIDTH:o_kv + (i + 1) * KV_WIDTH]
    o_ng = o_kv + 6 * KV_WIDTH
    w_main = jnp.concatenate([w_in[:, :2 * M_WIDTH], w_in[:, o_nq:o_kv], kv(0), kv(2), kv(4), kv(1)],
                             axis=1).astype(BF16)
    w_t = jnp.concatenate([kv(3), kv(5), w_in[:, o_ng:], jnp.zeros((D, _R_MV - _R_NG - 3 * N_HEADS), F32),
                           w_in[:, 2 * M_WIDTH:o_mi]], axis=1).T.astype(BF16)
    w_gate = w_in[:, o_mi:o_nq].T
    wg_t = w_gate.astype(BF16)
    w_t = jnp.concatenate([w_t, wg_t, (w_gate - wg_t.astype(F32)).astype(BF16)], axis=0)
    half = N_HEAD_DIM // 2
    inv = ROPE_THETA ** (-jnp.arange(half, dtype=F32) / half)
    inv_row = jnp.tile(inv, LANE // half)[None, :]
    pos_col = positions.astype(F32).reshape(T, 1)

    zqk, q, kc, ks, kw, vc, vs_t, vw_t, ng_t, mv_t, mo_t, gt = _inproj(
        x2, mod3, norm_mix_pre, w_main, w_t, wg_t, gate_b, pos_col, inv_row, S)
    ym = _mlstm(zqk, mv_t, mo_t, gt, conv_w, head_norm, B, S)
    kcmp, vcmp_t = _compress(kc, vc, _compress_weights(*cmp_k), _compress_weights(*cmp_v), B, S)
    yn = _nsa(q, kcmp, vcmp_t, ks, vs_t, kw, vw_t, ng_t, B, S)

    pad = _ROUTER_ROWS - MOE_GROUPS - N_EXPERTS
    wr = jnp.concatenate([wg, we, jnp.zeros((D, pad), F32)], axis=1).T
    br = jnp.concatenate([bg, be, jnp.zeros((pad,), F32)])[:, None]
    x1, h2t, route, cnt = _post(ym, yn, w_out, x2, mod3, norm_mix_post, norm_ffn_pre, wr, br, S)

    n_blocks = (2 * T) // EXPERT_ROWS + N_EXPERTS
    pstart, block_expert, n_used = _sort_plan(cnt, n_blocks)
    n_tok = min(GATHER_TILE, S)
    asg = jnp.concatenate([route[0:2], route[4:6]], axis=0).astype(jnp.int32)
    asg3 = asg.reshape(_ASG, T // n_tok, n_tok).transpose(1, 0, 2).reshape(T // n_tok, 1, _ASG * n_tok)
    xs = _dispatch(pstart, asg3, h2t, n_blocks)
    ys = _experts(block_expert, n_used, xs, w1, w3, w2)
    out = _combine(pstart, asg3, ys, x1, route, mod3, norm_ffn_post, S)
    return out.reshape(B, S, D)


def kernel(x, c, positions, ada_w, ada_b, norm_mix_pre, norm_mix_post, norm_ffn_pre, norm_ffn_post, w_in, mlstm_conv_w, mlstm_gate_b, mlstm_head_norm, cmp_pe_k, cmp_w1_k, cmp_w2_k, cmp_pe_v, cmp_w1_v, cmp_w2_v, w_out, router_grp_w, router_grp_b, router_exp_w, router_exp_b, expert_w1, expert_w3, expert_w2):
    for l in range(ada_w.shape[0]):
        x = _layer(x, c, positions, ada_w[l], ada_b[l], norm_mix_pre[l], norm_mix_post[l], norm_ffn_pre[l],
                   norm_ffn_post[l], w_in[l], mlstm_conv_w[l], mlstm_gate_b[l], mlstm_head_norm[l],
                   (cmp_pe_k[l], cmp_w1_k[l], cmp_w2_k[l]), (cmp_pe_v[l], cmp_w1_v[l], cmp_w2_v[l]), w_out[l],
                   router_grp_w[l], router_grp_b[l], router_exp_w[l], router_exp_b[l],
                   expert_w1[l], expert_w3[l], expert_w2[l])
    return x
```

```python
import functools

import jax
import jax.numpy as jnp
import numpy as np
from jax import lax
from jax.experimental import pallas as pl
from jax.experimental.pallas import tpu as pltpu

M_HEADS = 4
M_HEAD_DIM = 128
M_WIDTH = M_HEADS * M_HEAD_DIM
M_CONV = 4
N_HEADS = 8
N_KV_GROUPS = 2
HEADS_PER_GROUP = N_HEADS // N_KV_GROUPS
N_HEAD_DIM = 64
N_WIDTH = N_HEADS * N_HEAD_DIM
KV_WIDTH = N_KV_GROUPS * N_HEAD_DIM
CMP_BLOCK = 32
CMP_STRIDE = 16
CMP_HIDDEN = 2 * N_HEAD_DIM
SLC_BLOCK = 64
SLC_TOPK = 16
WINDOW = 512
ROPE_THETA = 10000.0
MOE_GROUPS = 4
EXPERTS_PER_GROUP = 8
N_EXPERTS = MOE_GROUPS * EXPERTS_PER_GROUP
RMS_EPS = 1e-6
LOG2_E = 1.4426950408889634

LANE = 128
SUBLANE = 8
VMEM_LIMIT_BYTES = 56 * 1024 * 1024

TOKEN_TILE = 512
MLSTM_CHUNK = 256
NSA_Q_TILE = 256
NSA_K_TILE = 512
NSA_COL_BLOCK = 256
EXPERT_ROWS = 512
GATHER_TILE = 512
ISSUE_UNROLL = 8

F32 = jnp.float32
BF16 = jnp.bfloat16
NEG = -1e30
HIGHEST = lax.Precision.HIGHEST
NT_DIMS = (((1,), (1,)), ((), ()))
TN_DIMS = (((0,), (0,)), ((), ()))


def _params(n_grid):
    return pltpu.CompilerParams(
        dimension_semantics=("arbitrary",) * n_grid, vmem_limit_bytes=VMEM_LIMIT_BYTES)


def _dot(a, b, **kw):
    return jnp.dot(a, b, preferred_element_type=F32, **kw)


def _dot_nt(a, b, **kw):
    return lax.dot_general(a, b, NT_DIMS, preferred_element_type=F32, **kw)


def _dot_tn(a, b, **kw):
    return lax.dot_general(a, b, TN_DIMS, preferred_element_type=F32, **kw)


def _sigmoid(x):
    return 1.0 / (1.0 + jnp.exp(-x))


def _silu(x):
    return x * _sigmoid(x)


def _split3(x):
    hi = x.astype(BF16)
    r1 = x - hi.astype(F32)
    mid = r1.astype(BF16)
    return [hi, mid, (r1 - mid.astype(F32)).astype(BF16)]


def _rms(x, w):
    return x * lax.rsqrt(jnp.mean(x * x, axis=-1, keepdims=True) + RMS_EPS) * w


def _full(shape):
    return pl.BlockSpec(shape, lambda *_: (0,) * len(shape))


def _adaln_kernel(c_ref, w_ref, b_ref, o_ref):
    o_ref[...] = _dot(_silu(c_ref[...]), w_ref[...], precision=HIGHEST) + b_ref[...]


def _adaln(c, ada_w, ada_b):
    B, D = c.shape
    n = ada_w.shape[1] // D
    return pl.pallas_call(
        _adaln_kernel,
        grid=(n,),
        in_specs=[_full((B, D)), pl.BlockSpec((D, D), lambda j: (0, j)), pl.BlockSpec((1, D), lambda j: (0, j))],
        out_specs=pl.BlockSpec((B, D), lambda j: (0, j)),
        out_shape=jax.ShapeDtypeStruct((B, ada_w.shape[1]), F32),
        compiler_params=_params(1),
        name="adaln",
    )(c, ada_w, ada_b[None, :])


_C_M = 0
_C_Q = 2 * M_WIDTH
_C_KC = _C_Q + N_WIDTH
_C_KS = _C_KC + KV_WIDTH
_C_KW = _C_KS + KV_WIDTH
_C_VC = _C_KW + KV_WIDTH
_C_END = _C_VC + KV_WIDTH
_R_VS = 0
_R_VW = KV_WIDTH
_R_NG = 2 * KV_WIDTH
_R_MV = _R_NG + 4 * SUBLANE
_R_MO = _R_MV + M_WIDTH
_R_GH = _R_MO + M_WIDTH
_R_GL = _R_GH + 2 * M_HEADS
_R_END = _R_GL + 2 * M_HEADS


def _inproj_kernel(x_ref, mod_ref, nw_ref, w_ref, wt_ref, wg_ref, gb_ref, pos_ref, inv_ref,
                   zm_ref, q_ref, kc_ref, ks_ref, kw_ref, vc_ref, vst_ref, vwt_ref, ngt_ref, mvt_ref, mot_ref, gt_ref):
    mod = mod_ref[0]
    h = _rms(x_ref[...], nw_ref[...]) * (1.0 + mod[1:2]) + mod[0:1]
    hb = h.astype(BF16)
    z = _dot(hb, w_ref[...])
    zm_ref[...] = z[:, _C_M:_C_Q]
    zt = _dot_nt(wt_ref[...], hb)
    h_lo = (h - hb.astype(F32)).astype(BF16)
    gt_ref[...] = (zt[_R_GH:_R_GL] + zt[_R_GL:_R_END] + _dot_nt(wg_ref[...], h_lo)) + gb_ref[...]
    vst_ref[...] = zt[_R_VS:_R_VW].astype(BF16)
    vwt_ref[...] = zt[_R_VW:_R_NG].astype(BF16)
    ngt_ref[...] = _sigmoid(zt[_R_NG:_R_MV])
    mvt_ref[...] = zt[_R_MV:_R_MO].astype(BF16)
    mot_ref[...] = zt[_R_MO:_R_GH]

    ang = pos_ref[...] * inv_ref[...]
    cos = jnp.cos(ang)
    sin = jnp.sin(ang)
    lane = lax.broadcasted_iota(jnp.int32, (1, LANE), 1)
    first = (lane % N_HEAD_DIM) < (N_HEAD_DIM // 2)
    sin_signed = jnp.where(first, -sin, sin)

    def rope(slab):
        rot = jnp.where(first, pltpu.roll(slab, LANE - N_HEAD_DIM // 2, 1), pltpu.roll(slab, N_HEAD_DIM // 2, 1))
        return slab * cos + rot * sin_signed

    scale = N_HEAD_DIM ** -0.5 * LOG2_E
    for j in range(N_WIDTH // LANE):
        q_ref[:, j * LANE:(j + 1) * LANE] = (rope(z[:, _C_Q + j * LANE:_C_Q + (j + 1) * LANE]) * scale).astype(BF16)
    kc_ref[...] = rope(z[:, _C_KC:_C_KS])
    ks_ref[...] = rope(z[:, _C_KS:_C_KW]).astype(BF16)
    kw_ref[...] = rope(z[:, _C_KW:_C_VC]).astype(BF16)
    vc_ref[...] = z[:, _C_VC:_C_END]


def _inproj(x2, mod3, norm_w, w_main, w_t, wg_t, gate_b, pos_col, inv_row, S):
    T, D = x2.shape
    TM = min(TOKEN_TILE, S)
    tiles_per_seq = S // TM
    row = lambda w: pl.BlockSpec((TM, w), lambda i: (i, 0))
    col = lambda r: pl.BlockSpec((r, TM), lambda i: (0, i))
    outs = [(2 * M_WIDTH, F32), (N_WIDTH, BF16), (KV_WIDTH, F32), (KV_WIDTH, BF16), (KV_WIDTH, BF16), (KV_WIDTH, F32)]
    outs_t = [(KV_WIDTH, BF16), (KV_WIDTH, BF16), (_R_MV - _R_NG, F32), (M_WIDTH, BF16), (M_WIDTH, F32),
              (2 * M_HEADS, F32)]
    return pl.pallas_call(
        _inproj_kernel,
        grid=(T // TM,),
        in_specs=[row(D), pl.BlockSpec((1, 6, D), lambda i: (i // tiles_per_seq, 0, 0)), _full((1, D)),
                  _full(w_main.shape), _full(w_t.shape), _full(wg_t.shape), _full((2 * M_HEADS, 1)), row(1),
                  _full((1, LANE))],
        out_specs=[row(w) for w, _ in outs] + [col(r) for r, _ in outs_t],
        out_shape=[jax.ShapeDtypeStruct((T, w), dt) for w, dt in outs]
        + [jax.ShapeDtypeStruct((r, T), dt) for r, dt in outs_t],
        compiler_params=_params(1),
        name="inproj",
    )(x2, mod3, norm_w[None, :], w_main, w_t, wg_t, gate_b[:, None], pos_col, inv_row)


def _mlstm_kernel(zm_ref, vt_ref, ot_ref, gt_ref, cw_ref, hn_ref, ltri_ref, y_ref, buf, c_s, n_s, m_s):
    L = zm_ref.shape[0]
    QK = 2 * M_WIDTH
    DH = M_HEAD_DIM

    @pl.when(pl.program_id(1) == 0)
    def _():
        buf[0:SUBLANE, :] = jnp.zeros((SUBLANE, QK), F32)
        c_s[...] = jnp.zeros_like(c_s)
        n_s[...] = jnp.zeros_like(n_s)
        m_s[...] = jnp.zeros_like(m_s)

    buf[SUBLANE:SUBLANE + L, :] = zm_ref[:, 0:QK]
    cw = cw_ref[...]
    conv = cw[M_CONV - 1:M_CONV] * buf[SUBLANE:SUBLANE + L, :]
    for j in range(M_CONV - 1):
        off = SUBLANE - (M_CONV - 1) + j
        conv = conv + cw[j:j + 1] * buf[off:off + L, :]
    buf[0:SUBLANE, :] = buf[L:L + SUBLANE, :]
    qk = _silu(conv)

    g = gt_ref[...]
    fp = g[M_HEADS:]
    lf = jnp.minimum(fp, 0.0) - jnp.log(1.0 + jnp.exp(-jnp.abs(fp)))
    cs = _dot_nt(jnp.concatenate(_split3(lf) + [jnp.zeros((M_HEADS, L), BF16)], axis=0), ltri_ref[...])
    b_rows = cs[0:M_HEADS] + cs[M_HEADS:2 * M_HEADS] + cs[2 * M_HEADS:3 * M_HEADS]
    src_rows = b_rows - g[:M_HEADS]

    causal = lax.broadcasted_iota(jnp.int32, (L, L), 0) <= lax.broadcasted_iota(jnp.int32, (L, L), 1)
    ones_rows = jnp.ones((2 * SUBLANE, L), F32)
    ones3 = jnp.ones((3, L), BF16)
    zeros10 = jnp.zeros((2 * SUBLANE - 6, L), BF16)
    outs = []
    for h in range(M_HEADS):
        hs = slice(h * DH, (h + 1) * DH)
        b_r, i_r = b_rows[h:h + 1], g[h:h + 1]
        m_prev = m_s[h:h + 1, 0:1]
        g_tot = b_r[:, L - 1:L]
        a_r = g_tot - b_r + i_r
        m_new = jnp.maximum(g_tot + m_prev, jnp.max(a_r, axis=1, keepdims=True))

        qb = qk[:, hs].astype(BF16)
        kb = (qk[:, M_WIDTH + h * DH:M_WIDTH + (h + 1) * DH] * (DH ** -0.5)).astype(BF16)
        vt = vt_ref[hs, :]

        lhs_t = jnp.concatenate(_split3(-src_rows[h:h + 1]) + [ones3, zeros10], axis=0)
        rhs_t = jnp.concatenate([ones3] + _split3(b_r) + [zeros10], axis=0)
        dlog = jnp.where(causal, _dot_tn(lhs_t, rhs_t), -jnp.inf)
        inter = b_r + m_prev
        m_t = jnp.maximum(inter, jnp.max(dlog, axis=0, keepdims=True))
        wts = jnp.exp(dlog - m_t) * _dot_nt(kb, qb)
        dec = jnp.exp(inter - m_t)
        ct_prev = c_s[h]
        n_prev = n_s[...]
        num = _dot(vt, wts.astype(BF16)) + dec * _dot_nt(ct_prev.astype(BF16), qb)
        qn = _dot_nt(n_prev.astype(BF16), qb)[h:h + 1]
        den = jnp.sum(wts, axis=0, keepdims=True) + dec * qn
        hh = num * (1.0 / jnp.maximum(jnp.abs(den), jnp.exp(-m_t)))

        w_r = jnp.exp(a_r - m_new)
        keep = jnp.exp(g_tot + m_prev - m_new)
        lhs = jnp.concatenate([vt.astype(F32), ones_rows], axis=0) * w_r
        upd = _dot(lhs.astype(BF16), kb)
        c_s[h] = keep * ct_prev + upd[:DH]
        n_s[h:h + 1] = keep * n_prev[h:h + 1] + upd[DH:DH + 1]
        m_s[h:h + 1] = jnp.broadcast_to(m_new, (1, LANE))

        hn = hh * lax.rsqrt(jnp.mean(hh * hh, axis=0, keepdims=True) + RMS_EPS)
        hn = hn * jnp.concatenate([hn_ref[hs, :]] * (L // LANE), axis=1)
        outs.append(_sigmoid(ot_ref[hs, :]) * hn)
    y_ref[...] = jnp.concatenate(outs, axis=0).T.astype(BF16)


def _mlstm(zqk, v_t, o_t, gt, conv_w, head_norm, B, S):
    T = zqk.shape[0]
    L = min(MLSTM_CHUNK, S)
    assert L % LANE == 0
    nc = S // L
    ltri = jnp.asarray(np.tril(np.ones((L, L), np.float32)), dtype=BF16)
    col = lambda r: pl.BlockSpec((r, L), lambda b, c: (0, b * nc + c))
    return pl.pallas_call(
        _mlstm_kernel,
        grid=(B, nc),
        in_specs=[pl.BlockSpec((L, 2 * M_WIDTH), lambda b, c: (b * nc + c, 0)), col(M_WIDTH), col(M_WIDTH),
                  col(2 * M_HEADS), _full((M_CONV, 2 * M_WIDTH)), _full((M_WIDTH, LANE)), _full((L, L))],
        out_specs=pl.BlockSpec((L, M_WIDTH), lambda b, c: (b * nc + c, 0)),
        out_shape=jax.ShapeDtypeStruct((T, M_WIDTH), BF16),
        scratch_shapes=[pltpu.VMEM((L + SUBLANE, 2 * M_WIDTH), F32),
                        pltpu.VMEM((M_HEADS, M_HEAD_DIM, M_HEAD_DIM), F32),
                        pltpu.VMEM((SUBLANE, M_HEAD_DIM), F32),
                        pltpu.VMEM((SUBLANE, LANE), F32)],
        compiler_params=_params(2),
        name="mlstm",
    )(zqk, v_t, o_t, gt, conv_w, jnp.broadcast_to(head_norm[:, None], (M_WIDTH, LANE)), ltri)


def _compress_kernel(k_ref, v_ref, pak, pbk, wak, wbk, w2k, pav, pbv, wav, wbv, w2v, ko_ref, vo_ref):
    def one(x_ref, pa, pb, wa, wb, w2, o_ref, transposed):
        n = x_ref.shape[0] // CMP_STRIDE
        x = jnp.concatenate([x_ref[pl.ds(t, n, stride=CMP_STRIDE), :] for t in range(CMP_STRIDE)], axis=1)
        first = _dot((x + pa[...]).astype(BF16), wa[...])
        second = _dot((x + pb[...]).astype(BF16), wb[...])
        hid = _silu(first + pltpu.roll(second, n - 1, 0)).astype(BF16)
        o_ref[0] = (_dot_nt(w2[...], hid) if transposed else _dot(hid, w2[...])).astype(BF16)

    one(k_ref, pak, pbk, wak, wbk, w2k, ko_ref, False)
    one(v_ref, pav, pbv, wav, wbv, w2v, vo_ref, True)


def _compress_weights(pe, w1, w2):
    half = CMP_BLOCK // 2
    eye = jnp.eye(N_KV_GROUPS, dtype=F32)
    w1r = w1.reshape(CMP_BLOCK, N_HEAD_DIM, CMP_HIDDEN)
    big = lambda w: jnp.einsum("ldc,gh->lgdhc", w, eye).reshape(half * KV_WIDTH, N_KV_GROUPS * CMP_HIDDEN).astype(BF16)
    pe_row = lambda p: jnp.broadcast_to(p[:, None, :], (half, N_KV_GROUPS, N_HEAD_DIM)).reshape(1, half * KV_WIDTH)
    w2bd = jnp.einsum("cd,gh->gchd", w2, eye).reshape(N_KV_GROUPS * CMP_HIDDEN, KV_WIDTH).astype(BF16)
    return pe_row(pe[:half]), pe_row(pe[half:]), big(w1r[:half]), big(w1r[half:]), w2bd


def _compress(kc, vc, wk, wv, B, S):
    nc = S // CMP_STRIDE
    blk = pl.BlockSpec((S, KV_WIDTH), lambda b: (b, 0))
    wv = wv[:4] + (wv[4].T,)
    return pl.pallas_call(
        _compress_kernel,
        grid=(B,),
        in_specs=[blk, blk] + [_full(w.shape) for w in wk] + [_full(w.shape) for w in wv],
        out_specs=[pl.BlockSpec((1, nc, KV_WIDTH), lambda b: (b, 0, 0)),
                   pl.BlockSpec((1, KV_WIDTH, nc), lambda b: (b, 0, 0))],
        out_shape=[jax.ShapeDtypeStruct((B, nc, KV_WIDTH), BF16), jax.ShapeDtypeStruct((B, KV_WIDTH, nc), BF16)],
        compiler_params=_params(1),
        name="compress",
    )(kc, vc, *wk, *wv)


def _masked_softmax_keys(s, mask):
    s = jnp.where(mask, s, -jnp.inf)
    m = jnp.max(s, axis=0, keepdims=True)
    m = jnp.where(jnp.isfinite(m), m, 0.0)
    e = jnp.exp2(s - m)
    return e, 1.0 / jnp.maximum(jnp.sum(e, axis=0, keepdims=True), 1e-30)


def _nsa_kernel(q_ref, kc_ref, vct_ref, ks_ref, vst_ref, kw_ref, vwt_ref, ngt_ref, ov_ref, ek_ref, eye_ref,
                y_ref, m_s, l_s, acc_s, rank_s, *, n_top, TK):
    TQ = q_ref.shape[0]
    NS = ov_ref.shape[0]
    G, HPG, DH = N_KV_GROUPS, HEADS_PER_GROUP, N_HEAD_DIM
    R = G * HPG * TQ
    q0 = pl.program_id(1) * TQ

    q = q_ref[...]
    zero = jnp.zeros((TQ, DH), BF16)
    parts = []
    for hd in range(N_HEADS):
        qh = q[:, hd * DH:(hd + 1) * DH]
        parts.append(jnp.concatenate([qh, zero] if hd < HPG else [zero, qh], axis=1))
    qp = jnp.concatenate(parts, axis=0)
    t_q = q0 + lax.broadcasted_iota(jnp.int32, (1, R), 1) % TQ

    kc = kc_ref[0]
    NC = kc.shape[0]
    cmp_end = lax.broadcasted_iota(jnp.int32, (NC, 1), 0) * CMP_STRIDE + (CMP_BLOCK - 1)
    e_c, inv_c = _masked_softmax_keys(_dot_nt(kc, qp), cmp_end <= t_q)
    p_c = e_c * inv_c
    o_c = _dot(vct_ref[0], p_c.astype(BF16))
    p_grp = jnp.concatenate(
        [sum(p_c[:, (g * HPG + h) * TQ:(g * HPG + h + 1) * TQ] for h in range(HPG)) for g in range(G)], axis=1)
    imp3 = _dot(ov_ref[...], jnp.concatenate(_split3(p_grp), axis=1))
    imp = imp3[:, :G * TQ] + imp3[:, G * TQ:2 * G * TQ] + imp3[:, 2 * G * TQ:]

    j_io = lax.broadcasted_iota(jnp.int32, (NS, G * TQ), 0)
    t_row = q0 + lax.broadcasted_iota(jnp.int32, (NS, G * TQ), 1) % TQ
    cur = t_row // SLC_BLOCK
    forced = (j_io == 0) | (j_io == cur) | (j_io == cur - 1)
    future = j_io * SLC_BLOCK > t_row
    score = jnp.where(forced, jnp.inf, jnp.where(future, -jnp.inf, imp))
    rank_s[...] = jnp.zeros_like(rank_s)
    n_rival = (q0 + TQ - 1) // SLC_BLOCK + 1
    for c0 in range(0, NS, SUBLANE):
        @pl.when(c0 < n_rival)
        def _():
            part = jnp.zeros((NS, G * TQ), F32)
            for jp in range(c0, c0 + SUBLANE):
                other = score[jp:jp + 1, :]
                part = part + jnp.where(j_io > jp, jnp.where(other >= score, 1.0, 0.0),
                                        jnp.where(other > score, 1.0, 0.0))
            rank_s[...] = rank_s[...] + part
    rank = rank_s[...]
    sel_bias = jnp.where(future, NEG, jnp.where(rank < n_top, 0.0, NEG)).astype(BF16)
    sel_rows = _dot_tn(sel_bias, eye_ref[...]).astype(BF16)
    sel_rows = jnp.broadcast_to(sel_rows.reshape(G, 1, TQ, LANE), (G, HPG, TQ, LANE)).reshape(R, LANE)
    qa = jnp.concatenate([qp, sel_rows], axis=1)

    m_s[...] = jnp.full_like(m_s, NEG)
    l_s[...] = jnp.zeros_like(l_s)
    acc_s[...] = jnp.zeros_like(acc_s)

    CB = NSA_COL_BLOCK

    def step(kt, causal):
        k0 = pl.multiple_of(kt * TK, TK)
        ka = jnp.concatenate([ks_ref[0, pl.ds(k0, TK), :], ek_ref[pl.ds(k0, TK), :]], axis=1)
        vt = vst_ref[:, pl.ds(k0, TK)]
        kpos = k0 + lax.broadcasted_iota(jnp.int32, (TK, 1), 0)
        scores = [_dot_nt(ka, qa[j * CB:(j + 1) * CB]) for j in range(R // CB)]
        probs, alphas = [], []
        for j in range(R // CB):
            cols = slice(j * CB, (j + 1) * CB)
            s = scores[j]
            if causal:
                s = jnp.where(kpos <= t_q[:, cols], s, NEG)
            m_old = m_s[:, cols]
            m_new = jnp.maximum(m_old, jnp.max(s, axis=0, keepdims=True))
            alpha = jnp.exp2(m_old - m_new)
            p = jnp.exp2(s - m_new)
            l_s[:, cols] = alpha * l_s[:, cols] + jnp.sum(p, axis=0, keepdims=True)
            m_s[:, cols] = m_new
            probs.append(p.astype(BF16))
            alphas.append(alpha)
        for j in range(R // CB):
            cols = slice(j * CB, (j + 1) * CB)
            acc_s[:, cols] = alphas[j] * acc_s[:, cols] + _dot(vt, probs[j])

    def full_tile(kt, carry):
        step(kt, False)
        return carry

    last = q0 // TK
    lax.fori_loop(0, last, full_tile, 0)
    step(last, True)
    o_s = acc_s[...] * (1.0 / l_s[...])

    WK = WINDOW + TQ
    start = pl.multiple_of(jnp.maximum(q0 - WINDOW, 0), TQ)
    wpos = start + lax.broadcasted_iota(jnp.int32, (WK, 1), 0)
    e_w, inv_w = _masked_softmax_keys(_dot_nt(kw_ref[0, pl.ds(start, WK), :], qp),
                                      (t_q - wpos).astype(jnp.uint32) < WINDOW)
    o_w =_dot(vwt_ref[:, pl.ds(start, WK)], e_w.astype(BF16)) * inv_w

    gates = ngt_ref[...]
    outs = []
    for hd in range(N_HEADS):
        cols = slice(hd * TQ, (hd + 1) * TQ)
        rows = slice((hd // HPG) * DH, (hd // HPG + 1) * DH)
        outs.append(gates[3 * hd:3 * hd + 1] * o_c[rows, cols]
                    + gates[3 * hd + 1:3 * hd + 2] * o_s[rows, cols]
                    + gates[3 * hd + 2:3 * hd + 3] * o_w[rows, cols])
    y_ref[...] = jnp.concatenate(outs, axis=0).T.astype(BF16)


def _nsa(q, kcmp, vcmp_t, ks, vs_t, kw, vw_t, ng_t, B, S):
    T = q.shape[0]
    TQ = min(NSA_Q_TILE, S)
    TK = min(NSA_K_TILE, S)
    nq = S // TQ
    NS = S // SLC_BLOCK
    NC = S // CMP_STRIDE
    n_cmp = (S - CMP_BLOCK) // CMP_STRIDE + 1
    js = np.arange(NS)[:, None] * SLC_BLOCK
    cs = np.arange(NC)[None, :] * CMP_STRIDE
    ov = np.clip(np.minimum(js + SLC_BLOCK, cs + CMP_BLOCK) - np.maximum(js, cs), 0, None) / CMP_STRIDE
    ov[:, n_cmp:] = 0.0
    assert NS <= LANE and TK % TQ == 0
    block_of_key = (np.arange(S)[:, None] // SLC_BLOCK == np.arange(LANE)[None, :]).astype(np.float32)
    eye_pad = np.eye(NS, LANE, dtype=np.float32)
    R = N_HEADS * TQ
    assert R % NSA_COL_BLOCK == 0
    seq = lambda: pl.BlockSpec((1, S, KV_WIDTH), lambda b, i: (b, 0, 0))
    seq_t = lambda: pl.BlockSpec((KV_WIDTH, S), lambda b, i: (0, b))
    return pl.pallas_call(
        functools.partial(_nsa_kernel, n_top=min(SLC_TOPK, NS), TK=TK),
        grid=(B, nq),
        in_specs=[pl.BlockSpec((TQ, N_WIDTH), lambda b, i: (b * nq + i, 0)),
                  pl.BlockSpec((1, NC, KV_WIDTH), lambda b, i: (b, 0, 0)),
                  pl.BlockSpec((1, KV_WIDTH, NC), lambda b, i: (b, 0, 0)),
                  seq(), seq_t(), seq(), seq_t(), pl.BlockSpec((ng_t.shape[0], TQ), lambda b, i: (0, b * nq + i)),
                  _full((NS, NC)), _full((S, LANE)), _full((NS, LANE))],
        out_specs=pl.BlockSpec((TQ, N_WIDTH), lambda b, i: (b * nq + i, 0)),
        out_shape=jax.ShapeDtypeStruct((T, N_WIDTH), BF16),
        scratch_shapes=[pltpu.VMEM((1, R), F32), pltpu.VMEM((1, R), F32), pltpu.VMEM((KV_WIDTH, R), F32),
                        pltpu.VMEM((NS, N_KV_GROUPS * TQ), F32)],
        compiler_params=_params(2),
        name="nsa",
    )(q, kcmp, vcmp_t, ks.reshape(B, S, KV_WIDTH), vs_t, kw.reshape(B, S, KV_WIDTH), vw_t, ng_t,
      jnp.asarray(ov, dtype=BF16), jnp.asarray(block_of_key, dtype=BF16), jnp.asarray(eye_pad, dtype=BF16))


_ROUTER_ROWS = 40


def _post_kernel(ym_ref, yn_ref, wm_ref, wn_ref, x_ref, mod_ref, npost_ref, npre_ref, wr2_ref, br_ref,
                 utri_ref, lanes_ref, x1_ref, h2_ref, route_ref, cnt_ref, carry):
    TM = x_ref.shape[0]
    NR = _ROUTER_ROWS

    @pl.when(pl.program_id(0) == 0)
    def _():
        carry[...] = jnp.zeros_like(carry)

    mod = mod_ref[0]
    y = _dot(ym_ref[...], wm_ref[...]) + _dot(yn_ref[...], wn_ref[...])
    x1 = x_ref[...] + mod[2:3] * _rms(y, npost_ref[...])
    x1_ref[...] = x1
    h2 = _rms(x1, npre_ref[...]) * (1.0 + mod[4:5]) + mod[3:4]
    for s in range(SUBLANE):
        h2_ref[pl.ds(s, TM, stride=SUBLANE), :] = h2[:, s * LANE:(s + 1) * LANE]

    h_hi = h2.astype(BF16)
    h_lo = (h2 - h_hi.astype(F32)).astype(BF16)
    z2 = _dot_nt(wr2_ref[...], h_hi)
    logits = z2[:NR] + z2[NR:] + _dot_nt(wr2_ref[0:NR, :], h_lo) + br_ref[...]
    row = lax.broadcasted_iota(jnp.int32, (NR, TM), 0)
    is_grp = row < MOE_GROUPS
    lg = jnp.where(is_grp, logits, -jnp.inf)
    eg = jnp.exp(lg - jnp.max(lg, axis=0, keepdims=True))
    pg = eg / jnp.sum(eg, axis=0, keepdims=True)
    pg_top = jnp.max(pg, axis=0, keepdims=True)
    grp = jnp.min(jnp.where(is_grp & (pg == pg_top), row, NR), axis=0, keepdims=True)
    lo = MOE_GROUPS + EXPERTS_PER_GROUP * grp
    in_grp = (row >= lo) & (row < lo + EXPERTS_PER_GROUP)
    le = jnp.where(in_grp, logits, -jnp.inf)
    ee = jnp.exp(le - jnp.max(le, axis=0, keepdims=True))
    pe = jnp.where(in_grp, ee / jnp.sum(ee, axis=0, keepdims=True), -1.0)
    p1 = jnp.max(pe, axis=0, keepdims=True)
    i1 = jnp.min(jnp.where(pe == p1, row, NR), axis=0, keepdims=True)
    pe2 = jnp.where(row == i1, -1.0, pe)
    p2 = jnp.max(pe2, axis=0, keepdims=True)
    i2 = jnp.min(jnp.where((pe2 == p2) & in_grp & (row != i1), row, NR), axis=0, keepdims=True)
    den = p1 + p2
    oh1, oh2 = row == i1, row == i2
    oh = jnp.where(oh1, 1.0, 0.0) + jnp.where(oh2, 1.0, 0.0)
    before = _dot(oh.astype(BF16), utri_ref[...]) + carry[...]
    r1 = jnp.sum(jnp.where(oh1, before, 0.0), axis=0, keepdims=True)
    r2 = jnp.sum(jnp.where(oh2, before, 0.0), axis=0, keepdims=True)
    carry[...] = carry[...] + jnp.sum(oh, axis=1, keepdims=True)
    cnt_ref[...] = _dot_tn(jnp.broadcast_to(carry[...], (NR, SUBLANE)), lanes_ref[...], precision=HIGHEST)
    rows = ((i1 - MOE_GROUPS).astype(F32), (i2 - MOE_GROUPS).astype(F32), pg_top * p1 / den, pg_top * p2 / den, r1, r2)
    route_ref[...] = jnp.concatenate(rows + (jnp.zeros((SUBLANE - len(rows), TM), F32),), axis=0)


def _post(ym, yn, w_out, x2, mod3, norm_post, norm_pre, wr, br, S):
    T, D = x2.shape
    TM = min(TOKEN_TILE, S)
    tiles_per_seq = S // TM
    row = lambda w: pl.BlockSpec((TM, w), lambda i: (i, 0))
    small = pl.BlockSpec((SUBLANE, LANE), lambda i: (0, 0))
    NR = _ROUTER_ROWS
    utri = jnp.asarray(np.triu(np.ones((TM, TM), np.float32), 1), dtype=BF16)
    lanes = jnp.asarray(np.eye(NR, LANE, k=-MOE_GROUPS, dtype=np.float32))
    wr_hi = wr.astype(BF16)
    wr2 = jnp.concatenate([wr_hi, (wr - wr_hi.astype(F32)).astype(BF16)], axis=0)
    return pl.pallas_call(
        _post_kernel,
        grid=(T // TM,),
        in_specs=[row(M_WIDTH), row(N_WIDTH), _full((M_WIDTH, D)), _full((N_WIDTH, D)), row(D),
                  pl.BlockSpec((1, 6, D), lambda i: (i // tiles_per_seq, 0, 0)), _full((1, D)), _full((1, D)),
                  _full((2 * NR, D)), _full((NR, 1)), _full((TM, TM)), _full((NR, LANE))],
        out_specs=[row(D), pl.BlockSpec((TM * SUBLANE, LANE), lambda i: (i, 0)),
                   pl.BlockSpec((SUBLANE, TM), lambda i: (0, i)), small],
        out_shape=[jax.ShapeDtypeStruct((T, D), F32), jax.ShapeDtypeStruct((T * SUBLANE, LANE), F32),
                   jax.ShapeDtypeStruct((SUBLANE, T), F32), jax.ShapeDtypeStruct((SUBLANE, LANE), F32)],
        scratch_shapes=[pltpu.VMEM((NR, 1), F32)],
        compiler_params=_params(1),
        name="post_mix_router",
    )(ym, yn, w_out[:M_WIDTH].astype(BF16), w_out[M_WIDTH:].astype(BF16), x2, mod3,
      norm_post[None, :], norm_pre[None, :], wr2, br, utri, lanes)


def _plan_kernel(cnt_ref, pstart_ref, blk_ref, tot_ref, *, rows_per_block):
    lane = lax.broadcasted_iota(jnp.int32, (SUBLANE, LANE), 1)
    nblk = (cnt_ref[...].astype(jnp.int32) + (rows_per_block - 1)) // rows_per_block
    end = nblk
    sh = 1
    while sh < 2 * N_EXPERTS:
        end = end + jnp.where(lane >= sh, pltpu.roll(end, sh, 1), 0)
        sh *= 2
    pstart_ref[...] = (end - nblk) * rows_per_block
    nbp = blk_ref.shape[0]
    blk_io = lax.broadcasted_iota(jnp.int32, (nbp, LANE), 0)
    lane_b = lax.broadcasted_iota(jnp.int32, (nbp, LANE), 1)
    passed = jnp.where((lane_b < N_EXPERTS) & (blk_io >= end[0:1, :]), 1, 0)
    blk_ref[...] = jnp.broadcast_to(jnp.minimum(jnp.sum(passed, axis=1, keepdims=True), N_EXPERTS - 1), (nbp, LANE))
    tot_ref[...] = jnp.broadcast_to(
        jnp.sum(jnp.where(lane == N_EXPERTS - 1, end, 0), axis=1, keepdims=True), (SUBLANE, LANE))


def _sort_plan(cnt, n_blocks):
    small = pl.BlockSpec((SUBLANE, LANE), lambda i: (0, 0))
    nbp = -(-n_blocks // SUBLANE) * SUBLANE
    pstart, blk, tot = pl.pallas_call(
        functools.partial(_plan_kernel, rows_per_block=EXPERT_ROWS),
        grid=(1,),
        in_specs=[small],
        out_specs=[small, _full((nbp, LANE)), small],
        out_shape=[jax.ShapeDtypeStruct((SUBLANE, LANE), jnp.int32), jax.ShapeDtypeStruct((nbp, LANE), jnp.int32),
                   jax.ShapeDtypeStruct((SUBLANE, LANE), jnp.int32)],
        compiler_params=_params(1),
        name="expert_plan",
    )(cnt)
    return pstart[0, :N_EXPERTS + 1], blk[:n_blocks, 0], tot[0, :1]


_ASG = 4


def _dest_row(pstart_ref, asg_ref, j, k):
    n = asg_ref.shape[2] // _ASG
    return pstart_ref[asg_ref[0, 0, k * n + j]] + asg_ref[0, 0, (2 + k) * n + j]


def _row_copy(src, src_row, dst, dst_row, sem):
    return pltpu.make_async_copy(src.at[pl.ds(pl.multiple_of(src_row * SUBLANE, SUBLANE), SUBLANE)],
                                 dst.at[pl.ds(pl.multiple_of(dst_row * SUBLANE, SUBLANE), SUBLANE)], sem)


def _dispatch_kernel(pstart_ref, asg_ref, h2_ref, xs_hbm, zeros, sem, zsem, *, n_blocks):
    n = asg_ref.shape[2] // _ASG
    RB = zeros.shape[0] // SUBLANE

    @pl.when(pl.program_id(0) == 0)
    def _():
        zeros[...] = jnp.zeros_like(zeros)

        def fill(row0):
            return pltpu.make_async_copy(
                zeros, xs_hbm.at[pl.ds(pl.multiple_of(row0 * SUBLANE, SUBLANE), RB * SUBLANE)], zsem)

        def experts(act):
            def body(e, carry):
                @pl.when(pstart_ref[e + 1] > pstart_ref[e])
                def _():
                    act(fill(pstart_ref[e + 1] - RB))
                return carry
            lax.fori_loop(0, N_EXPERTS, body, 0)

        def trailing(act):
            def body(b, carry):
                act(fill(b * RB))
                return carry
            lax.fori_loop(pstart_ref[N_EXPERTS] // RB, n_blocks, body, 0)

        experts(lambda c: c.start())
        trailing(lambda c: c.start())
        experts(lambda c: c.wait())
        trailing(lambda c: c.wait())

    def issue(jj, carry):
        for u in range(ISSUE_UNROLL):
            j = jj * ISSUE_UNROLL + u
            for k in range(2):
                _row_copy(h2_ref, j, xs_hbm, _dest_row(pstart_ref, asg_ref, j, k), sem).start(priority=k)
        return carry

    lax.fori_loop(0, n // ISSUE_UNROLL, issue, 0)
    whole = xs_hbm.at[pl.ds(0, 2 * n * SUBLANE)]
    pltpu.make_async_copy(whole, whole, sem).wait()


def _dispatch(pstart, asg3, h2t, n_blocks):
    nt, _, width = asg3.shape
    n = width // _ASG
    return pl.pallas_call(
        functools.partial(_dispatch_kernel, n_blocks=n_blocks),
        grid=(nt,),
        in_specs=[pl.BlockSpec(memory_space=pltpu.SMEM),
                  pl.BlockSpec((1, 1, width), lambda i: (i, 0, 0), memory_space=pltpu.SMEM),
                  pl.BlockSpec((n * SUBLANE, LANE), lambda i: (i, 0))],
        out_specs=pl.BlockSpec(memory_space=pl.ANY),
        out_shape=jax.ShapeDtypeStruct((n_blocks * EXPERT_ROWS * SUBLANE, LANE), F32),
        scratch_shapes=[pltpu.VMEM((EXPERT_ROWS * SUBLANE, LANE), F32), pltpu.SemaphoreType.DMA(()),
                        pltpu.SemaphoreType.DMA(())],
        compiler_params=_params(1),
        name="dispatch",
    )(pstart, asg3, h2t)


def _expert_kernel(be_ref, nb_ref, xs_ref, w1_hbm, w3_hbm, w2_hbm, ys_ref, w1f, w3f, w2f, w1b, w3b, w2b, sems, turn):
    RB = xs_ref.shape[0] // SUBLANE
    i = pl.program_id(0)
    nb = nb_ref[0]

    def fetch(e, slot):
        return [pltpu.make_async_copy(src.at[e], dst.at[slot], sems.at[slot])
                for src, dst in ((w1_hbm, w1f), (w3_hbm, w3f), (w2_hbm, w2f))]

    @pl.when(i < nb)
    def _():
        e = be_ref[i]

        @pl.when(i == 0)
        def _():
            turn[0] = 0
            for c in fetch(e, 0):
                c.start()

        @pl.when((i == 0) | (e != be_ref[jnp.maximum(i - 1, 0)]))
        def _():
            slot = turn[0] % 2
            for c in fetch(e, slot):
                c.wait()
            w1b[...] = w1f[slot].astype(BF16)
            w3b[...] = w3f[slot].astype(BF16)
            w2b[...] = w2f[slot].astype(BF16)
            nxt = lax.while_loop(lambda j: (j < nb) & (be_ref[jnp.minimum(j, nb - 1)] == e), lambda j: j + 1, i + 1)

            @pl.when(nxt < nb)
            def _():
                for c in fetch(be_ref[jnp.minimum(nxt, nb - 1)], 1 - slot):
                    c.start()

            turn[0] = turn[0] + 1

        x = jnp.concatenate([xs_ref[pl.ds(s, RB, stride=SUBLANE), :] for s in range(SUBLANE)], axis=1).astype(BF16)
        hb = _silu(_dot(x, w1b[...])) * _dot(x, w3b[...])
        y = _dot(hb.astype(BF16), w2b[...])
        for s in range(SUBLANE):
            ys_ref[pl.ds(s, RB, stride=SUBLANE), :] = y[:, s * LANE:(s + 1) * LANE]

    @pl.when(i >= nb_ref[0])
    def _():
        ys_ref[...] = jnp.zeros_like(ys_ref)


def _experts(block_expert, n_used, xs, w1, w3, w2):
    n_blocks = block_expert.shape[0]
    RB = EXPERT_ROWS
    D, F = w1.shape[1], w1.shape[2]
    cur = lambda i, nb: jnp.minimum(i, nb[0] - 1)
    rows = pl.BlockSpec((RB * SUBLANE, LANE), lambda i, be, nb: (cur(i, nb), 0))
    hbm = pl.BlockSpec(memory_space=pl.ANY)
    return pl.pallas_call(
        _expert_kernel,
        grid_spec=pltpu.PrefetchScalarGridSpec(
            num_scalar_prefetch=2, grid=(n_blocks,),
            in_specs=[rows, hbm, hbm, hbm],
            out_specs=pl.BlockSpec((RB * SUBLANE, LANE), lambda i, be, nb: (i, 0)),
            scratch_shapes=[pltpu.VMEM((2, D, F), F32), pltpu.VMEM((2, D, F), F32), pltpu.VMEM((2, F, D), F32),
                            pltpu.VMEM((D, F), BF16), pltpu.VMEM((D, F), BF16), pltpu.VMEM((F, D), BF16),
                            pltpu.SemaphoreType.DMA((2,)), pltpu.SMEM((1,), jnp.int32)]),
        out_shape=jax.ShapeDtypeStruct(xs.shape, F32),
        compiler_params=_params(1),
        name="experts",
    )(block_expert, n_used, xs, w1, w3, w2)


def _combine_kernel(pstart_ref, asg_ref, asg_next_ref, ys_hbm, x1_ref, route_ref, mod_ref, nw_ref, o_ref, buf, sems):
    n = x1_ref.shape[0]
    i = pl.program_id(0)
    slot = i % 2

    def gather(dref, into):
        def issue(jj, carry):
            for u in range(ISSUE_UNROLL):
                j = jj * ISSUE_UNROLL + u
                for k in range(2):
                    _row_copy(ys_hbm, _dest_row(pstart_ref, dref, j, k), buf.at[into], k * n + j,
                              sems.at[into]).start(priority=k)
            return carry

        lax.fori_loop(0, n // ISSUE_UNROLL, issue, 0)

    @pl.when(i == 0)
    def _():
        gather(asg_ref, 0)

    @pl.when(i + 1 < pl.num_programs(0))
    def _():
        gather(asg_next_ref, 1 - slot)

    pltpu.make_async_copy(ys_hbm.at[pl.ds(0, 2 * n * SUBLANE)], buf.at[slot], sems.at[slot]).wait()

    route = jnp.concatenate([route_ref[...], jnp.zeros((LANE - SUBLANE, n), F32)], axis=0).T
    g0, g1 = route[:, 2:3], route[:, 3:4]
    y = jnp.concatenate(
        [g0 * buf[slot, pl.ds(s, n, stride=SUBLANE), :] + g1 * buf[slot, pl.ds(n * SUBLANE + s, n, stride=SUBLANE), :]
         for s in range(SUBLANE)], axis=1)
    o_ref[...] = x1_ref[...] + mod_ref[0][5:6] * _rms(y, nw_ref[...])


def _combine(pstart, asg3, ys, x1, route, mod3, norm_w, S):
    T, D = x1.shape
    nt, _, width = asg3.shape
    n = width // _ASG
    tiles_per_seq = S // n
    return pl.pallas_call(
        _combine_kernel,
        grid=(nt,),
        in_specs=[pl.BlockSpec(memory_space=pltpu.SMEM),
                  pl.BlockSpec((1, 1, width), lambda i: (i, 0, 0), memory_space=pltpu.SMEM),
                  pl.BlockSpec((1, 1, width), lambda i: (jnp.minimum(i + 1, nt - 1), 0, 0), memory_space=pltpu.SMEM),
                  pl.BlockSpec(memory_space=pl.ANY), pl.BlockSpec((n, D), lambda i: (i, 0)),
                  pl.BlockSpec((SUBLANE, n), lambda i: (0, i)),
                  pl.BlockSpec((1, 6, D), lambda i: (i // tiles_per_seq, 0, 0)), _full((1, D))],
        out_specs=pl.BlockSpec((n, D), lambda i: (i, 0)),
        out_shape=jax.ShapeDtypeStruct((T, D), F32),
        scratch_shapes=[pltpu.VMEM((2, 2 * n * SUBLANE, LANE), F32), pltpu.SemaphoreType.DMA((2,))],
        compiler_params=_params(1),
        name="combine",
    )(pstart, asg3, asg3, ys, x1, route, mod3, norm_w[None, :])


def _layer(x, c, positions, ada_w, ada_b, norm_mix_pre, norm_mix_post, norm_ffn_pre, norm_ffn_post, w_in,
           conv_w, gate_b, head_norm, cmp_k, cmp_v, w_out, wg, bg, we, be, w1, w3, w2):
    B, S, D = x.shape
    T = B * S
    x2 = x.reshape(T, D)
    mod3 = _adaln(c, ada_w, ada_b).reshape(B, 6, D)

    o_mi = 4 * M_WIDTH
    o_nq = o_mi + 2 * M_HEADS
    o_kv = o_nq + N_WIDTH
    kv = lambda i: w_in[:, o_kv + i * KV_WIDTH:o_kv + (i + 1) * KV_WIDTH]
    o_ng = o_kv + 6 * KV_WIDTH
    w_main = jnp.concatenate([w_in[:, :2 * M_WIDTH], w_in[:, o_nq:o_kv], kv(0), kv(2), kv(4), kv(1)],
                             axis=1).astype(BF16)
    w_t = jnp.concatenate([kv(3), kv(5), w_in[:, o_ng:], jnp.zeros((D, _R_MV - _R_NG - 3 * N_HEADS), F32),
                           w_in[:, 2 * M_WIDTH:o_mi]], axis=1).T.astype(BF16)
    w_gate = w_in[:, o_mi:o_nq].T
    wg_t = w_gate.astype(BF16)
    w_t = jnp.concatenate([w_t, wg_t, (w_gate - wg_t.astype(F32)).astype(BF16)], axis=0)
    half = N_HEAD_DIM // 2
    inv = ROPE_THETA ** (-jnp.arange(half, dtype=F32) / half)
    inv_row = jnp.tile(inv, LANE // half)[None, :]
    pos_col = positions.astype(F32).reshape(T, 1)

    zqk, q, kc, ks, kw, vc, vs_t, vw_t, ng_t, mv_t, mo_t, gt = _inproj(
        x2, mod3, norm_mix_pre, w_main, w_t, wg_t, gate_b, pos_col, inv_row, S)
    ym = _mlstm(zqk, mv_t, mo_t, gt, conv_w, head_norm, B, S)
    kcmp, vcmp_t = _compress(kc, vc, _compress_weights(*cmp_k), _compress_weights(*cmp_v), B, S)
    yn = _nsa(q, kcmp, vcmp_t, ks, vs_t, kw, vw_t, ng_t, B, S)

    pad = _ROUTER_ROWS - MOE_GROUPS - N_EXPERTS
    wr = jnp.concatenate([wg, we, jnp.zeros((D, pad), F32)], axis=1).T
    br = jnp.concatenate([bg, be, jnp.zeros((pad,), F32)])[:, None]
    x1, h2t, route, cnt = _post(ym, yn, w_out, x2, mod3, norm_mix_post, norm_ffn_pre, wr, br, S)

    n_blocks = (2 * T) // EXPERT_ROWS + N_EXPERTS
    pstart, block_expert, n_used = _sort_plan(cnt, n_blocks)
    n_tok = min(GATHER_TILE, S)
    asg = jnp.concatenate([route[0:2], route[4:6]], axis=0).astype(jnp.int32)
    asg3 = asg.reshape(_ASG, T // n_tok, n_tok).transpose(1, 0, 2).reshape(T // n_tok, 1, _ASG * n_tok)
    xs = _dispatch(pstart, asg3, h2t, n_blocks)
    ys = _experts(block_expert, n_used, xs, w1, w3, w2)
    out = _combine(pstart, asg3, ys, x1, route, mod3, norm_ffn_post, S)
    return out.reshape(B, S, D)


def kernel(x, c, positions, ada_w, ada_b, norm_mix_pre, norm_mix_post, norm_ffn_pre, norm_ffn_post, w_in, mlstm_conv_w, mlstm_gate_b, mlstm_head_norm, cmp_pe_k, cmp_w1_k, cmp_w2_k, cmp_pe_v, cmp_w1_v, cmp_w2_v, w_out, router_grp_w, router_grp_b, router_exp_w, router_exp_b, expert_w1, expert_w3, expert_w2):
    for l in range(ada_w.shape[0]):
        x = _layer(x, c, positions, ada_w[l], ada_b[l], norm_mix_pre[l], norm_mix_post[l], norm_ffn_pre[l],
                   norm_ffn_post[l], w_in[l], mlstm_conv_w[l], mlstm_gate_b[l], mlstm_head_norm[l],
                   (cmp_pe_k[l], cmp_w1_k[l], cmp_w2_k[l]), (cmp_pe_v[l], cmp_w1_v[l], cmp_w2_v[l]), w_out[l],
                   router_grp_w[l], router_grp_b[l], router_exp_w[l], router_exp_b[l],
                   expert_w1[l], expert_w3[l], expert_w2[l])
    return x
```
